```python
import math
import jax, jax.numpy as jnp
from jax import lax
import numpy as np

D_MODEL = 1024
BATCH = 4
SEQ = 4096
DEPTH = 2

N_EVEN = (DEPTH + 1) // 2
N_ODD = DEPTH // 2

RMS_EPS = 1e-6

S5_WIDTH = D_MODEL // 2
S5_GROUP = 16
S5_GROUPS = S5_WIDTH // S5_GROUP
S5_STATE = 64
DT_MIN = 1e-3
DT_MAX = 1e-1

DIFF_WIDTH = D_MODEL - S5_WIDTH
DIFF_HEAD_DIM = 64
DIFF_VALUE_DIM = 2 * DIFF_HEAD_DIM
DIFF_HEADS = DIFF_WIDTH // DIFF_VALUE_DIM
ROT_DIM = DIFF_HEAD_DIM // 4
ROPE_THETA = 500000.0
Q_BLOCK = 128
EVEN_IN_COLS = S5_WIDTH + 3 * DIFF_WIDTH

RWKV_HEAD = 64
RWKV_HEADS = D_MODEL // RWKV_HEAD
DECAY_LORA = 64
AAA_LORA = 64
GATE_LORA = 128
GN_EPS = 64e-5

FFN_HIDDEN = ((8 * D_MODEL // 3 + 255) // 256) * 256

kernel_name = "hybrid_s5_diffattn_rwkv7_adaln_block"


def rms_norm(x, g, eps=RMS_EPS):
    xf = x.astype(jnp.float32)
    y = xf * lax.rsqrt(jnp.mean(xf * xf, axis=-1, keepdims=True) + eps)
    return (y * g.astype(jnp.float32)).astype(x.dtype)


def apply_partial_rope(x, cos, sin):
    half = ROT_DIM // 2
    xf = x[..., :ROT_DIM].astype(jnp.float32)
    x1, x2 = xf[..., :half], xf[..., half:]
    rot = jnp.concatenate([x1 * cos - x2 * sin, x2 * cos + x1 * sin], axis=-1).astype(x.dtype)
    return jnp.concatenate([rot, x[..., ROT_DIM:]], axis=-1)


def s5_ssm(u, lam_re, lam_im, log_dt, b_re, b_im, c_re, c_im, d_skip):
    uf = u.astype(jnp.float32)
    lr = lam_re.astype(jnp.float32)
    li = lam_im.astype(jnp.float32)
    dt = jnp.exp(log_dt.astype(jnp.float32))[:, None]
    mag = jnp.exp(lr * dt)
    abar_r = mag * jnp.cos(li * dt)
    abar_i = mag * jnp.sin(li * dt)
    den = lr * lr + li * li
    num_r = abar_r - 1.0
    q_r = (num_r * lr + abar_i * li) / den
    q_i = (abar_i * lr - num_r * li) / den
    br = b_re.astype(jnp.float32)
    bi = b_im.astype(jnp.float32)
    bb_r = q_r[..., None] * br - q_i[..., None] * bi
    bb_i = q_r[..., None] * bi + q_i[..., None] * br
    bu_r = jnp.einsum('bsgh,gph->bsgp', uf, bb_r)
    bu_i = jnp.einsum('bsgh,gph->bsgp', uf, bb_i)
    a_r = jnp.broadcast_to(abar_r, bu_r.shape)
    a_i = jnp.broadcast_to(abar_i, bu_i.shape)

    def combine(e1, e2):
        a1r, a1i, b1r, b1i = e1
        a2r, a2i, b2r, b2i = e2
        return (a2r * a1r - a2i * a1i,
                a2r * a1i + a2i * a1r,
                a2r * b1r - a2i * b1i + b2r,
                a2r * b1i + a2i * b1r + b2i)

    _, _, x_r, x_i = lax.associative_scan(combine, (a_r, a_i, bu_r, bu_i), axis=1)
    y = (jnp.einsum('bsgp,ghp->bsgh', x_r, c_re.astype(jnp.float32))
         - jnp.einsum('bsgp,ghp->bsgh', x_i, c_im.astype(jnp.float32))
         + d_skip.astype(jnp.float32) * uf)
    return y.astype(u.dtype)


def diff_attention(q, k, v, lam, lam_init, subln_g):
    seq = q.shape[1]
    scale = DIFF_HEAD_DIM ** -0.5
    qf = q.astype(jnp.float32)
    kf = k.astype(jnp.float32)
    vf = v.astype(jnp.float32)
    outs = []
    for i in range(seq // Q_BLOCK):
        kv_len = (i + 1) * Q_BLOCK
        q_blk = qf[:, i * Q_BLOCK:kv_len]
        s = jnp.einsum('bqhcd,bkhcd->bhcqk', q_blk, kf[:, :kv_len]) * scale
        q_pos = i * Q_BLOCK + jnp.arange(Q_BLOCK)
        k_pos = jnp.arange(kv_len)
        mask = k_pos[None, :] <= q_pos[:, None]
        p = jax.nn.softmax(jnp.where(mask, s, -jnp.inf), axis=-1)
        w = p[:, :, 0] - lam * p[:, :, 1]
        outs.append(jnp.einsum('bhqk,bkhe->bqhe', w, vf[:, :kv_len]))
    o = jnp.concatenate(outs, axis=1)
    o = rms_norm(o, subln_g) * (1.0 - lam_init)
    return o.astype(q.dtype)


def even_mixer(h, cos, sin, w_in, lam_re, lam_im, log_dt, b_re, b_im, c_re, c_im,
               d_skip, w_glu, q_norm, k_norm, lq1, lk1, lq2, lk2, subln, w_out, lam_init):
    bsz, seq, _ = h.shape
    proj = h @ w_in
    u, q, k, v = jnp.split(proj, [S5_WIDTH, S5_WIDTH + DIFF_WIDTH, S5_WIDTH + 2 * DIFF_WIDTH], axis=-1)
    y = s5_ssm(u.reshape(bsz, seq, S5_GROUPS, S5_GROUP), lam_re, lam_im, log_dt,
               b_re, b_im, c_re, c_im, d_skip).reshape(bsz, seq, S5_WIDTH)
    y = jax.nn.gelu(y)
    y = y * jax.nn.sigmoid(y @ w_glu)
    q = q.reshape(bsz, seq, DIFF_HEADS, 2, DIFF_HEAD_DIM)
    k = k.reshape(bsz, seq, DIFF_HEADS, 2, DIFF_HEAD_DIM)
    v = v.reshape(bsz, seq, DIFF_HEADS, DIFF_VALUE_DIM)
    q = apply_partial_rope(rms_norm(q, q_norm), cos, sin)
    k = apply_partial_rope(rms_norm(k, k_norm), cos, sin)
    lam = (jnp.exp(jnp.sum(lq1.astype(jnp.float32) * lk1.astype(jnp.float32)))
           - jnp.exp(jnp.sum(lq2.astype(jnp.float32) * lk2.astype(jnp.float32))) + lam_init)
    o = diff_attention(q, k, v, lam, lam_init, subln).reshape(bsz, seq, DIFF_WIDTH)
    return jnp.concatenate([y, o], axis=-1) @ w_out


def rwkv7_recurrence(r, decay, k, v, kk, a):
    bsz, _, nh, n = r.shape

    def step(state, inp):
        r_t, w_t, k_t, v_t, kk_t, b_t = inp
        sa = jnp.einsum('bhvk,bhk->bhv', state, -kk_t)
        state = (state * w_t[:, :, None, :] + sa[..., None] * b_t[:, :, None, :]
                 + v_t[..., None] * k_t[:, :, None, :])
        return state, jnp.einsum('bhvk,bhk->bhv', state, r_t)

    xs = tuple(jnp.moveaxis(t.astype(jnp.float32), 1, 0) for t in (r, decay, k, v, kk, kk * a))
    s0 = jnp.zeros((bsz, nh, n, n), jnp.float32)
    _, ys = lax.scan(step, s0, xs)
    return jnp.moveaxis(ys, 0, 1)


def rwkv7_mixer(h, mu, w_r, w_k, w_v, w_o, w0, w1, w2, a0, a1, a2, g1, g2,
                k_k, k_a, r_k, ln_g, ln_b):
    bsz, seq, dm = h.shape
    h_prev = jnp.pad(h, ((0, 0), (1, 0), (0, 0)))[:, :-1]
    dx = h_prev - h
    xr = h + dx * mu[0]
    xw = h + dx * mu[1]
    xk = h + dx * mu[2]
    xv = h + dx * mu[3]
    xa = h + dx * mu[4]
    xg = h + dx * mu[5]
    r = xr @ w_r
    w = -jax.nn.softplus(-(w0 + jnp.tanh(xw @ w1) @ w2)) - 0.5
    decay = jnp.exp(-jnp.exp(w.astype(jnp.float32)))
    k = xk @ w_k
    v = xv @ w_v
    a = jax.nn.sigmoid(a0 + (xa @ a1) @ a2)
    g = jax.nn.sigmoid(xg @ g1) @ g2
    hs = (bsz, seq, RWKV_HEADS, RWKV_HEAD)
    kk = (k * k_k).reshape(hs).astype(jnp.float32)
    kk = kk / jnp.maximum(jnp.sqrt(jnp.sum(kk * kk, axis=-1, keepdims=True)), 1e-12)
    k = k * (1.0 + (a - 1.0) * k_a)
    rh = r.reshape(hs).astype(jnp.float32)
    kh = k.reshape(hs).astype(jnp.float32)
    vh = v.reshape(hs).astype(jnp.float32)
    y = rwkv7_recurrence(rh, decay.reshape(hs), kh, vh, kk, a.reshape(hs))
    mean = jnp.mean(y, axis=-1, keepdims=True)
    var = jnp.mean((y - mean) ** 2, axis=-1, keepdims=True)
    y = ((y - mean) * lax.rsqrt(var + GN_EPS)).reshape(bsz, seq, dm)
    y = y * ln_g.astype(jnp.float32) + ln_b.astype(jnp.float32)
    bonus = jnp.sum(rh * kh * r_k.astype(jnp.float32), axis=-1, keepdims=True) * vh
    y = (y + bonus.reshape(bsz, seq, dm)).astype(h.dtype)
    return (y * g) @ w_o


def swiglu(h, w_gate, w_up, w_down):
    return (jax.nn.silu(h @ w_gate) * (h @ w_up)) @ w_down


def setup_inputs(seed: int = 0) -> dict:
    key = jax.random.key(seed)
    ks = iter(jax.random.split(key, 64))
    f32 = jnp.float32

    def nrm(shape, s):
        return jax.random.normal(next(ks), shape, f32) * s

    def unif(shape, lo, hi):
        return jax.random.uniform(next(ks), shape, f32, lo, hi)

    D = D_MODEL
    G, P, H16 = S5_GROUPS, S5_STATE, S5_GROUP
    d = DIFF_HEAD_DIM
    inp = {}
    inp["x"] = nrm((BATCH, SEQ, D), 1.0)
    inp["c"] = nrm((BATCH, D), 1.0)
    inp["positions"] = jnp.broadcast_to(jnp.arange(SEQ, dtype=jnp.int32), (BATCH, SEQ))
    inp["w_ada"] = nrm((DEPTH, D, 6 * D), 0.5 * D ** -0.5)
    inp["b_ada"] = nrm((DEPTH, 6 * D), 0.02)
    inp["norm_mix"] = 1.0 + nrm((DEPTH, D), 0.02)
    inp["norm_ffn"] = 1.0 + nrm((DEPTH, D), 0.02)
    inp["ffn_w_gate"] = nrm((DEPTH, D, FFN_HIDDEN), D ** -0.5)
    inp["ffn_w_up"] = nrm((DEPTH, D, FFN_HIDDEN), D ** -0.5)
    inp["ffn_w_down"] = nrm((DEPTH, FFN_HIDDEN, D), FFN_HIDDEN ** -0.5)
    inp["ev_w_in"] = nrm((N_EVEN, D, EVEN_IN_COLS), D ** -0.5)
    inp["ev_s5_lam_re"] = -0.5 + nrm((N_EVEN, G, P), 0.01)
    inp["ev_s5_lam_im"] = math.pi * jnp.arange(P, dtype=f32) + nrm((N_EVEN, G, P), 0.01)
    inp["ev_s5_log_dt"] = unif((N_EVEN, G), math.log(DT_MIN), math.log(DT_MAX))
    inp["ev_s5_b_re"] = nrm((N_EVEN, G, P, H16), (2 * H16) ** -0.5)
    inp["ev_s5_b_im"] = nrm((N_EVEN, G, P, H16), (2 * H16) ** -0.5)
    inp["ev_s5_c_re"] = nrm((N_EVEN, G, H16, P), (2 * P) ** -0.5)
    inp["ev_s5_c_im"] = nrm((N_EVEN, G, H16, P), (2 * P) ** -0.5)
    inp["ev_s5_d"] = nrm((N_EVEN, G, H16), 1.0)
    inp["ev_s5_w_glu"] = nrm((N_EVEN, S5_WIDTH, S5_WIDTH), S5_WIDTH ** -0.5)
    inp["ev_q_norm"] = 1.0 + nrm((N_EVEN, d), 0.02)
    inp["ev_k_norm"] = 1.0 + nrm((N_EVEN, d), 0.02)
    inp["ev_lambda_q1"] = nrm((N_EVEN, d), 0.1)
    inp["ev_lambda_k1"] = nrm((N_EVEN, d), 0.1)
    inp["ev_lambda_q2"] = nrm((N_EVEN, d), 0.1)
    inp["ev_lambda_k2"] = nrm((N_EVEN, d), 0.1)
    inp["ev_subln"] = 1.0 + nrm((N_EVEN, DIFF_VALUE_DIM), 0.02)
    inp["ev_w_out"] = nrm((N_EVEN, D, D), D ** -0.5)
    inp["od_mu"] = unif((N_ODD, 6, D), 0.0, 1.0)
    inp["od_w_r"] = nrm((N_ODD, D, D), D ** -0.5)
    inp["od_w_k"] = nrm((N_ODD, D, D), D ** -0.5)
    inp["od_w_v"] = nrm((N_ODD, D, D), D ** -0.5)
    inp["od_w_o"] = nrm((N_ODD, D, D), D ** -0.5)
    inp["od_w0"] = unif((N_ODD, D), -6.0, 0.0)
    inp["od_w1"] = nrm((N_ODD, D, DECAY_LORA), D ** -0.5)
    inp["od_w2"] = nrm((N_ODD, DECAY_LORA, D), 0.5 * DECAY_LORA ** -0.5)
    inp["od_a0"] = nrm((N_ODD, D), 0.1)
    inp["od_a1"] = nrm((N_ODD, D, AAA_LORA), D ** -0.5)
    inp["od_a2"] = nrm((N_ODD, AAA_LORA, D), 0.5 * AAA_LORA ** -0.5)
    inp["od_g1"] = nrm((N_ODD, D, GATE_LORA), D ** -0.5)
    inp["od_g2"] = nrm((N_ODD, GATE_LORA, D), GATE_LORA ** -0.5)
    inp["od_k_k"] = 0.85 + nrm((N_ODD, D), 0.02)
    inp["od_k_a"] = 1.0 + nrm((N_ODD, D), 0.02)
    inp["od_r_k"] = nrm((N_ODD, RWKV_HEADS, RWKV_HEAD), 0.1)
    inp["od_ln_g"] = 1.0 + nrm((N_ODD, D), 0.02)
    inp["od_ln_b"] = nrm((N_ODD, D), 0.02)
    return inp


def reference(x, c, positions, w_ada, b_ada, norm_mix, norm_ffn, ffn_w_gate, ffn_w_up,
              ffn_w_down, ev_w_in, ev_s5_lam_re, ev_s5_lam_im, ev_s5_log_dt, ev_s5_b_re,
              ev_s5_b_im, ev_s5_c_re, ev_s5_c_im, ev_s5_d, ev_s5_w_glu, ev_q_norm, ev_k_norm,
              ev_lambda_q1, ev_lambda_k1, ev_lambda_q2, ev_lambda_k2, ev_subln, ev_w_out,
              od_mu, od_w_r, od_w_k, od_w_v, od_w_o, od_w0, od_w1, od_w2, od_a0, od_a1,
              od_a2, od_g1, od_g2, od_k_k, od_k_a, od_r_k, od_ln_g, od_ln_b):
    inv_freq = ROPE_THETA ** (-jnp.arange(0, ROT_DIM, 2, dtype=jnp.float32) / ROT_DIM)
    ang = positions.astype(jnp.float32)[..., None] * inv_freq
    cos = jnp.cos(ang)[:, :, None, None, :]
    sin = jnp.sin(ang)[:, :, None, None, :]
    c_act = jax.nn.silu(c)
    for l in range(DEPTH):
        mod = (c_act @ w_ada[l] + b_ada[l])[:, None, :]
        shift_m, scale_m, gate_m, shift_f, scale_f, gate_f = jnp.split(mod, 6, axis=-1)
        h = rms_norm(x, norm_mix[l]) * (1.0 + scale_m) + shift_m
        if l % 2 == 0:
            e = l // 2
            lam_init = 0.8 - 0.6 * math.exp(-0.3 * l)
            mix = even_mixer(h, cos, sin, ev_w_in[e], ev_s5_lam_re[e], ev_s5_lam_im[e],
                             ev_s5_log_dt[e], ev_s5_b_re[e], ev_s5_b_im[e], ev_s5_c_re[e],
                             ev_s5_c_im[e], ev_s5_d[e], ev_s5_w_glu[e], ev_q_norm[e],
                             ev_k_norm[e], ev_lambda_q1[e], ev_lambda_k1[e], ev_lambda_q2[e],
                             ev_lambda_k2[e], ev_subln[e], ev_w_out[e], lam_init)
        else:
            o = l // 2
            mix = rwkv7_mixer(h, od_mu[o], od_w_r[o], od_w_k[o], od_w_v[o], od_w_o[o],
                              od_w0[o], od_w1[o], od_w2[o], od_a0[o], od_a1[o], od_a2[o],
                              od_g1[o], od_g2[o], od_k_k[o], od_k_a[o], od_r_k[o],
                              od_ln_g[o], od_ln_b[o])
        x = x + gate_m * mix
        h = rms_norm(x, norm_ffn[l]) * (1.0 + scale_f) + shift_f
        x = x + gate_f * swiglu(h, ffn_w_gate[l], ffn_w_up[l], ffn_w_down[l])
    return x
```

```python
import functools
import math

import jax
import jax.numpy as jnp
from jax import lax
from jax.experimental import pallas as pl
from jax.experimental.pallas import tpu as pltpu

F32 = jnp.float32
BF16 = jnp.bfloat16

RMS_EPS = 1e-6
GN_EPS = 64e-5
ROPE_THETA = 500000.0
DT_HEAD = 64
ROT_DIM = DT_HEAD // 4
S5_GROUP = 16
S5_STATE = 64
RWKV_HEAD = 64
LANES = 128
SUBLANES = 8
VMEM_LIMIT = 56 * 1024 * 1024


def _cparams(*sem):
    return pltpu.CompilerParams(dimension_semantics=sem, vmem_limit_bytes=VMEM_LIMIT)


def _const_spec(shape):
    nd = len(shape)
    return pl.BlockSpec(shape, lambda *_: (0,) * nd, pipeline_mode=pl.Buffered(1))


def _dot(a, b):
    return jnp.dot(a, b, preferred_element_type=F32)


def _dot_nt(a, b):
    return lax.dot_general(a, b, (((1,), (1,)), ((), ())), preferred_element_type=F32)


def _dot_tn(a, b):
    return lax.dot_general(a, b, (((0,), (0,)), ((), ())), preferred_element_type=F32)


def _split_bf16(x):
    hi = x.astype(BF16)
    lo = (x - hi.astype(F32)).astype(BF16)
    return hi, lo


def _dot_x2(x, w_bf16):
    hi, lo = _split_bf16(x)
    return _dot(hi, w_bf16) + _dot(lo, w_bf16)


def _modnorm(x, g, scale, shift):
    ms = jnp.mean(x * x, axis=-1, keepdims=True)
    return (x * lax.rsqrt(ms + RMS_EPS)) * g * (1.0 + scale) + shift


def _ada_kernel(c_ref, w_ref, b_ref, o_ref):
    c = c_ref[...]
    ca = c * jax.nn.sigmoid(c)
    hi, lo = _split_bf16(ca)
    w = w_ref[0]
    w_hi, w_lo = _split_bf16(w)
    o_ref[0] = _dot(hi, w_hi) + _dot(lo, w_hi) + _dot(hi, w_lo) + b_ref[0]


def _ada_mod(c, w_ada, b_ada):
    depth, d, n = w_ada.shape
    bsz = c.shape[0]
    rows = -(-bsz // SUBLANES) * SUBLANES
    c_pad = jnp.zeros((rows, d), F32).at[:bsz].set(c)
    tn = n // 4
    out = pl.pallas_call(
        _ada_kernel,
        grid=(depth, n // tn),
        in_specs=[
            pl.BlockSpec((rows, d), lambda l, j: (0, 0)),
            pl.BlockSpec((1, d, tn), lambda l, j: (l, 0, j)),
            pl.BlockSpec((1, 1, tn), lambda l, j: (l, 0, j)),
        ],
        out_specs=pl.BlockSpec((1, rows, tn), lambda l, j: (l, 0, j)),
        out_shape=jax.ShapeDtypeStruct((depth, rows, n), F32),
        compiler_params=_cparams("arbitrary", "arbitrary"),
        name="ada_mod",
    )(c_pad, w_ada, b_ada.reshape(depth, 1, n))
    mod = out[:, :bsz].reshape(depth, bsz, 6, d)
    return jnp.pad(mod, ((0, 0), (0, 0), (0, SUBLANES - 6), (0, 0)))


def _mul_trig_kernel(p_ref, f_ref, cos_ref, sin_ref):
    a = p_ref[...] * f_ref[...]
    cos_ref[...] = jnp.cos(a)
    sin_ref[...] = jnp.sin(a)


def _rope_tables(positions):
    bsz, seq = positions.shape
    half = ROT_DIM // 2
    inv_freq = ROPE_THETA ** (-jnp.arange(0, ROT_DIM, 2, dtype=F32) / ROT_DIM)
    n = bsz * seq * half
    pos_rep = jnp.broadcast_to(positions.astype(F32)[..., None], (bsz, seq, half))
    frq_rep = jnp.broadcast_to(inv_freq, (bsz, seq, half))
    rows = n // LANES
    tr = min(rows, 512)
    cos, sin = pl.pallas_call(
        _mul_trig_kernel,
        grid=(rows // tr,),
        in_specs=[pl.BlockSpec((tr, LANES), lambda i: (i, 0))] * 2,
        out_specs=[pl.BlockSpec((tr, LANES), lambda i: (i, 0))] * 2,
        out_shape=[jax.ShapeDtypeStruct((rows, LANES), F32)] * 2,
        compiler_params=_cparams("arbitrary"),
        name="rope_trig",
    )(pos_rep.reshape(rows, LANES), frq_rep.reshape(rows, LANES))
    cos = cos.reshape(bsz, seq, half)
    sin = sin.reshape(bsz, seq, half)
    one = jnp.ones((bsz, seq, DT_HEAD - ROT_DIM), F32)
    zero_h = jnp.zeros((bsz, seq, half), F32)
    zero_r = jnp.zeros((bsz, seq, DT_HEAD - ROT_DIM), F32)
    c64 = jnp.concatenate([cos, cos, one], axis=-1)
    sa64 = jnp.concatenate([-sin, zero_h, zero_r], axis=-1)
    sb64 = jnp.concatenate([zero_h, sin, zero_r], axis=-1)
    rep = LANES // DT_HEAD
    return (jnp.tile(c64, (1, 1, rep)), jnp.tile(sa64, (1, 1, rep)), jnp.tile(sb64, (1, 1, rep)))


def _group_mean_matrix(width, group):
    idx = jnp.arange(width) // group
    return jnp.where(idx[:, None] == idx[None, :], 1.0 / group, 0.0).astype(BF16)


def _even_in_kernel(x_ref, mod_ref, g_ref, w_ref, qn_ref, kn_ref, c_ref, sa_ref, sb_ref, bd_ref,
                    u_ref, q_ref, k_ref, v_ref, *, s5w, dfw):
    x = x_ref[0]
    h = _modnorm(x, g_ref[...], mod_ref[0, 1:2, :], mod_ref[0, 0:1, :])
    proj = _dot(h.astype(BF16), w_ref[...])
    u_ref[...] = proj[:, :s5w]
    cosv = c_ref[0]
    sav = sa_ref[0]
    sbv = sb_ref[0]
    half = ROT_DIM // 2

    def norm_rope(t, gn, out_ref, post_scale):
        ms = _dot((t * t).astype(BF16), bd_ref[...])
        t = t * lax.rsqrt(ms + RMS_EPS) * gn
        for j in range(dfw // LANES):
            tj = t[:, j * LANES:(j + 1) * LANES]
            up = pltpu.roll(tj, LANES - half, axis=1)
            dn = pltpu.roll(tj, half, axis=1)
            rj = tj * cosv + up * sav + dn * sbv
            out_ref[0, :, j * LANES:(j + 1) * LANES] = (rj * post_scale).astype(out_ref.dtype)

    norm_rope(proj[:, s5w:s5w + dfw], qn_ref[...], q_ref, DT_HEAD ** -0.5)
    norm_rope(proj[:, s5w + dfw:s5w + 2 * dfw], kn_ref[...], k_ref, 1.0)
    v_ref[0] = proj[:, s5w + 2 * dfw:].astype(v_ref.dtype)


def _even_in(x, mod, g, w_in, q_norm, k_norm, tables, s5w, dfw, tm):
    bsz, seq, d = x.shape
    ncol = w_in.shape[1]
    cos_t, sa_t, sb_t = tables
    rep = dfw // DT_HEAD
    kern = functools.partial(_even_in_kernel, s5w=s5w, dfw=dfw)
    tok = lambda w: pl.BlockSpec((1, tm, w), lambda b, i: (b, i, 0))
    return pl.pallas_call(
        kern,
        grid=(bsz, seq // tm),
        in_specs=[
            tok(d),
            pl.BlockSpec((1, SUBLANES, d), lambda b, i: (b, 0, 0)),
            _const_spec((1, d)),
            _const_spec((d, ncol)),
            _const_spec((1, dfw)),
            _const_spec((1, dfw)),
            tok(LANES), tok(LANES), tok(LANES),
            _const_spec((dfw, dfw)),
        ],
        out_specs=[
            pl.BlockSpec((tm, s5w), lambda b, i: (i, b)),
            tok(dfw), tok(dfw), tok(dfw),
        ],
        out_shape=[
            jax.ShapeDtypeStruct((seq, bsz * s5w), F32),
            jax.ShapeDtypeStruct((bsz, seq, dfw), BF16),
            jax.ShapeDtypeStruct((bsz, seq, dfw), BF16),
            jax.ShapeDtypeStruct((bsz, seq, dfw), BF16),
        ],
        compiler_params=_cparams("arbitrary", "arbitrary"),
        name="even_in_proj",
    )(x, mod, g.reshape(1, d), w_in.astype(BF16),
      jnp.tile(q_norm, rep).reshape(1, dfw), jnp.tile(k_norm, rep).reshape(1, dfw),
      cos_t, sa_t, sb_t, _group_mean_matrix(dfw, DT_HEAD))


def _s5_disc_kernel(lr_ref, li_ref, ldt_ref, br_ref, bi_ref, ar_ref, ai_ref, bbr_ref, bbi_ref,
                    *, bsz):
    lr = lr_ref[...]
    li = li_ref[...]
    dt = jnp.exp(ldt_ref[...])
    mag = jnp.exp(lr * dt)
    abar_r = mag * jnp.cos(li * dt)
    abar_i = mag * jnp.sin(li * dt)
    den = lr * lr + li * li
    num_r = abar_r - 1.0
    q_r = (num_r * lr + abar_i * li) / den
    q_i = (abar_i * lr - num_r * li) / den
    br = br_ref[...]
    bi = bi_ref[...]
    bbr_ref[...] = q_r * br - q_i * bi
    bbi_ref[...] = q_r * bi + q_i * br
    pr, pi = abar_r, abar_i
    for j in range(SUBLANES):
        if j and j % bsz == 0:
            pr, pi = pr * abar_r - pi * abar_i, pr * abar_i + pi * abar_r
        ar_ref[:, j:j + 1] = pr
        ai_ref[:, j:j + 1] = pi


def _s5_discretise(lam_re, lam_im, log_dt, b_re, b_im, bsz):
    g, p = lam_re.shape
    hh = b_re.shape[-1]
    n = g * p
    tab = jax.ShapeDtypeStruct((n, SUBLANES), F32)
    mat = jax.ShapeDtypeStruct((n, hh), F32)
    ldt = jnp.broadcast_to(log_dt[:, None], (g, p)).reshape(n, 1)
    return pl.pallas_call(
        functools.partial(_s5_disc_kernel, bsz=bsz),
        out_shape=[tab, tab, mat, mat],
        name="s5_discretise",
    )(lam_re.reshape(n, 1), lam_im.reshape(n, 1), ldt, b_re.reshape(n, hh), b_im.reshape(n, hh))


def _s5_scan_kernel(u_ref, bre_ref, bim_ref, cre_ref, cim_ref, dsk_ref, a1r_ref, a1i_ref,
                    pwr_ref, pwi_ref, wglu_ref, o_ref, xr_ref, xi_ref, cr_ref, ci_ref,
                    *, bsz, lane_chunk):
    rows, nstate = xr_ref.shape
    step = pl.program_id(0)

    @pl.when(step == 0)
    def _():
        cr_ref[...] = jnp.zeros_like(cr_ref)
        ci_ref[...] = jnp.zeros_like(ci_ref)

    u = u_ref[...]
    u_bf = u.astype(BF16)
    xr_ref[...] = _dot(u_bf, bre_ref[...])
    xi_ref[...] = _dot(u_bf, bim_ref[...])

    row = lax.broadcasted_iota(jnp.int32, (SUBLANES, lane_chunk), 0)
    steps_per_tile = SUBLANES // bsz
    ntiles = rows // SUBLANES

    for c in range(nstate // lane_chunk):
        cols = pl.ds(c * lane_chunk, lane_chunk)
        a1r = a1r_ref[:, cols]
        a1i = a1i_ref[:, cols]
        pwr = pwr_ref[:, cols]
        pwi = pwi_ref[:, cols]

        def tile_body(i, carry):
            pr, pi = carry
            base = pl.multiple_of(i * SUBLANES, SUBLANES)
            zr = xr_ref[pl.ds(base, SUBLANES), cols]
            zi = xi_ref[pl.ds(base, SUBLANES), cols]
            sh = bsz
            apr, api = a1r, a1i
            for _ in range(steps_per_tile.bit_length() - 1):
                sr = jnp.where(row >= sh, pltpu.roll(zr, sh, axis=0), 0.0)
                si = jnp.where(row >= sh, pltpu.roll(zi, sh, axis=0), 0.0)
                zr, zi = zr + apr * sr - api * si, zi + apr * si + api * sr
                apr, api = apr * apr - api * api, 2.0 * apr * api
                sh *= 2
            last = SUBLANES - bsz
            br_, bi_ = pr, pi
            sh = bsz
            while sh < SUBLANES:
                br_ = jnp.where(row >= last, br_, pltpu.roll(br_, SUBLANES - sh, axis=0))
                bi_ = jnp.where(row >= last, bi_, pltpu.roll(bi_, SUBLANES - sh, axis=0))
                last -= sh
                sh *= 2
            xr = zr + pwr * br_ - pwi * bi_
            xi = zi + pwr * bi_ + pwi * br_
            xr_ref[pl.ds(base, SUBLANES), cols] = xr
            xi_ref[pl.ds(base, SUBLANES), cols] = xi
            return xr, xi

        fr, fi = lax.fori_loop(0, ntiles, tile_body, (cr_ref[:, cols], ci_ref[:, cols]))
        cr_ref[:, cols] = fr
        ci_ref[:, cols] = fi

    y = (_dot(xr_ref[...].astype(BF16), cre_ref[...])
         - _dot(xi_ref[...].astype(BF16), cim_ref[...])
         + dsk_ref[...] * u)
    y = jax.nn.gelu(y)
    gate = jax.nn.sigmoid(_dot(y.astype(BF16), wglu_ref[...]))
    o_ref[...] = (y * gate).astype(o_ref.dtype)


def _s5_mixer(u2, bsz, lam_re, lam_im, log_dt, b_re, b_im, c_re, c_im, d_skip, w_glu, tchunk):
    total_rows, width = u2.shape
    g, p = lam_re.shape
    hh = b_re.shape[-1]
    nstate = g * p
    pw_r, pw_i, bbar_r, bbar_i = _s5_discretise(lam_re, lam_im, log_dt, b_re, b_im, bsz)
    eye = jnp.eye(g, dtype=F32)
    def in_map(bb):
        return jnp.einsum('gph,gk->ghkp', bb.reshape(g, p, hh), eye).reshape(width, nstate)
    def out_map(cc):
        return jnp.einsum('ghp,gk->gpkh', cc, eye).reshape(nstate, width)
    pw_r = pw_r.T
    pw_i = pw_i.T
    a1r = jnp.broadcast_to(pw_r[0:1], (SUBLANES, nstate))
    a1i = jnp.broadcast_to(pw_i[0:1], (SUBLANES, nstate))
    rows = tchunk * bsz
    kern = functools.partial(_s5_scan_kernel, bsz=bsz, lane_chunk=512)
    return pl.pallas_call(
        kern,
        grid=(total_rows // rows,),
        in_specs=[
            pl.BlockSpec((rows, width), lambda i: (i, 0)),
            _const_spec((width, nstate)), _const_spec((width, nstate)),
            _const_spec((nstate, width)), _const_spec((nstate, width)),
            _const_spec((1, width)),
            _const_spec((SUBLANES, nstate)), _const_spec((SUBLANES, nstate)),
            _const_spec((SUBLANES, nstate)), _const_spec((SUBLANES, nstate)),
            _const_spec((width, width)),
        ],
        out_specs=pl.BlockSpec((rows, width), lambda i: (i, 0)),
        out_shape=jax.ShapeDtypeStruct((total_rows, width), BF16),
        scratch_shapes=[
            pltpu.VMEM((rows, nstate), F32), pltpu.VMEM((rows, nstate), F32),
            pltpu.VMEM((SUBLANES, nstate), F32), pltpu.VMEM((SUBLANES, nstate), F32),
        ],
        compiler_params=_cparams("arbitrary"),
        name="s5_scan_glu",
    )(u2, in_map(bbar_r).astype(BF16), in_map(bbar_i).astype(BF16),
      out_map(c_re).astype(BF16), out_map(c_im).astype(BF16),
      d_skip.reshape(1, width), a1r, a1i, pw_r, pw_i, w_glu.astype(BF16))


def _diff_attn_kernel(lam_ref, sub_ref, q_ref, k_ref, v_ref, o_ref, acc1_ref, acc2_ref,
                      *, tq, lam_init):
    qi = pl.program_id(2)
    q = q_ref[0]
    lane = lax.broadcasted_iota(jnp.int32, q.shape, 1)
    zero = jnp.zeros_like(q)
    q_sub = (jnp.where(lane < DT_HEAD, q, zero), jnp.where(lane >= DT_HEAD, q, zero))
    acc_refs = (acc1_ref, acc2_ref)
    acc1_ref[...] = jnp.zeros_like(acc1_ref)
    acc2_ref[...] = jnp.zeros_like(acc2_ref)
    rowi = lax.broadcasted_iota(jnp.int32, (tq, tq), 0)
    coli = lax.broadcasted_iota(jnp.int32, (tq, tq), 1)
    causal = coli <= rowi

    def kv_block(j, carry, masked):
        start = pl.multiple_of(j * tq, tq)
        kb = k_ref[0, pl.ds(start, tq), :]
        vb = v_ref[0, pl.ds(start, tq), :]
        new = []
        for c in range(2):
            m_old, l_old = carry[c]
            s = _dot_nt(q_sub[c], kb)
            if masked:
                s = jnp.where(causal, s, -jnp.inf)
            m_new = jnp.maximum(m_old, jnp.max(s, axis=-1, keepdims=True))
            alpha = jnp.exp(m_old - m_new)
            p = jnp.exp(s - m_new)
            l_new = alpha * l_old + jnp.sum(p, axis=-1, keepdims=True)
            acc_refs[c][...] = alpha * acc_refs[c][...] + _dot(p.astype(BF16), vb)
            new.append((m_new, l_new))
        return tuple(new)

    init = tuple((jnp.full((tq, 1), -jnp.inf, F32), jnp.zeros((tq, 1), F32)) for _ in range(2))
    carry = lax.fori_loop(0, qi, lambda j, cr: kv_block(j, cr, False), init)
    (_, l1), (_, l2) = kv_block(qi, carry, True)

    lv = lam_ref[...]
    lam = (jnp.exp(jnp.sum(lv[0:1] * lv[1:2], axis=-1, keepdims=True))
           - jnp.exp(jnp.sum(lv[2:3] * lv[3:4], axis=-1, keepdims=True)) + lam_init)
    o = acc1_ref[...] / l1 - lam * (acc2_ref[...] / l2)
    ms = jnp.mean(o * o, axis=-1, keepdims=True)
    o = o * lax.rsqrt(ms + RMS_EPS) * sub_ref[...] * (1.0 - lam_init)
    o_ref[0] = o.astype(o_ref.dtype)


def _diff_attention(q, k, v, lq1, lk1, lq2, lk2, subln, lam_init, tq):
    bsz, seq, dfw = q.shape
    vdim = 2 * DT_HEAD
    heads = dfw // vdim
    lamv = jnp.zeros((SUBLANES, LANES), F32)
    for i, t in enumerate((lq1, lk1, lq2, lk2)):
        lamv = lamv.at[i, :t.shape[0]].set(t)
    kern = functools.partial(_diff_attn_kernel, tq=tq, lam_init=lam_init)
    kv_spec = pl.BlockSpec((1, seq, vdim), lambda b, h, i: (b, 0, h))
    return pl.pallas_call(
        kern,
        grid=(bsz, heads, seq // tq),
        in_specs=[
            _const_spec((SUBLANES, LANES)),
            _const_spec((1, vdim)),
            pl.BlockSpec((1, tq, vdim), lambda b, h, i: (b, i, h)),
            kv_spec, kv_spec,
        ],
        out_specs=pl.BlockSpec((1, tq, vdim), lambda b, h, i: (b, i, h)),
        out_shape=jax.ShapeDtypeStruct((bsz, seq, dfw), BF16),
        scratch_shapes=[pltpu.VMEM((tq, vdim), F32), pltpu.VMEM((tq, vdim), F32)],
        compiler_params=_cparams("arbitrary", "arbitrary", "arbitrary"),
        name="diff_attention",
    )(lamv, subln.reshape(1, vdim), q, k, v)


def _mix_ffn_kernel(*refs, n_in):
    x_ref, mod_ref, g_ref = refs[:3]
    act_refs = refs[3:3 + n_in]
    w_refs = refs[3 + n_in:3 + 2 * n_in]
    wg_ref, wu_ref, wd_ref, o_ref, acc_ref = refs[3 + 2 * n_in:]
    mix = None
    for a_ref, w_ref in zip(act_refs, w_refs):
        a = a_ref[0] if len(a_ref.shape) == 3 else a_ref[...]
        t = _dot(a, w_ref[...])
        mix = t if mix is None else mix + t
    x1 = x_ref[0] + mod_ref[0, 2:3, :] * mix
    h = _modnorm(x1, g_ref[...], mod_ref[0, 4:5, :], mod_ref[0, 3:4, :]).astype(BF16)
    acc_ref[...] = jnp.zeros_like(acc_ref)

    def body(j, _):
        gate = _dot(h, wg_ref[j])
        up = _dot(h, wu_ref[j])
        act = (gate * jax.nn.sigmoid(gate) * up).astype(BF16)
        acc_ref[...] += _dot(act, wd_ref[j])
        return 0

    lax.fori_loop(0, wg_ref.shape[0], body, 0)
    o_ref[0] = x1 + mod_ref[0, 5:6, :] * acc_ref[...]


def _mix_ffn(x, mod, g, acts, weights, w_gate, w_up, w_down, tm, hid_chunk):
    bsz, seq, d = x.shape
    hidden = w_gate.shape[1]
    nch = hidden // hid_chunk
    wg = w_gate.astype(BF16).reshape(d, nch, hid_chunk).transpose(1, 0, 2)
    wu = w_up.astype(BF16).reshape(d, nch, hid_chunk).transpose(1, 0, 2)
    wd = w_down.astype(BF16).reshape(nch, hid_chunk, d)
    tok = pl.BlockSpec((1, tm, d), lambda b, i: (b, i, 0))
    return pl.pallas_call(
        functools.partial(_mix_ffn_kernel, n_in=len(acts)),
        grid=(bsz, seq // tm),
        in_specs=[tok, pl.BlockSpec((1, SUBLANES, d), lambda b, i: (b, 0, 0)), _const_spec((1, d))]
                 + [spec for _, spec in acts]
                 + [_const_spec(w.shape) for w in weights]
                 + [_const_spec(wg.shape), _const_spec(wu.shape), _const_spec(wd.shape)],
        out_specs=tok,
        out_shape=jax.ShapeDtypeStruct((bsz, seq, d), F32),
        scratch_shapes=[pltpu.VMEM((tm, d), F32)],
        compiler_params=_cparams("arbitrary", "arbitrary"),
        name="mix_ffn",
    )(x, mod, g.reshape(1, d), *[a for a, _ in acts], *weights, wg, wu, wd)


def _rwkv_in_kernel(x_ref, xp_ref, mod_ref, g_ref, mu_ref, wr_ref, wk_ref, wv_ref, w1_ref, w2_ref,
                    a1_ref, a2_ref, g1_ref, g2_ref, w0_ref, a0_ref, kkw_ref, kaw_ref, bd_ref,
                    r_out, lw_out, k_out, v_out, kk_out, a_out, g_out):
    i = pl.program_id(1)
    g = g_ref[...]
    scale = mod_ref[0, 1:2, :]
    shift = mod_ref[0, 0:1, :]
    h = _modnorm(x_ref[0], g, scale, shift)
    hp = _modnorm(xp_ref[0][SUBLANES - 1:SUBLANES, :], g, scale, shift)
    hp = jnp.where(i == 0, 0.0, hp)
    row = lax.broadcasted_iota(jnp.int32, h.shape, 0)
    h_prev = jnp.where(row == 0, hp, pltpu.roll(h, 1, axis=0))
    dx = h_prev - h

    def lerp(j):
        return (h + dx * mu_ref[j:j + 1, :]).astype(BF16)

    r_out[0] = _dot(lerp(0), wr_ref[...])
    wl = jnp.tanh(_dot(lerp(1), w1_ref[...]))
    wdec = w0_ref[...] + _dot(wl.astype(BF16), w2_ref[...])
    w = -jax.nn.softplus(-wdec) - 0.5
    lw_out[0] = -jnp.exp(w)
    k = _dot(lerp(2), wk_ref[...])
    v_out[0] = _dot(lerp(3), wv_ref[...])
    al = _dot(lerp(4), a1_ref[...])
    a = jax.nn.sigmoid(a0_ref[...] + _dot(al.astype(BF16), a2_ref[...]))
    a_out[0] = a
    gl = jax.nn.sigmoid(_dot(lerp(5), g1_ref[...]))
    g_out[0] = _dot(gl.astype(BF16), g2_ref[...])
    kk = k * kkw_ref[...]
    ss = _dot_x2(kk * kk, bd_ref[...]) * float(RWKV_HEAD)
    kk_out[0] = kk / jnp.maximum(jnp.sqrt(ss), 1e-12)
    k_out[0] = k * (1.0 + (a - 1.0) * kaw_ref[...])


def _rwkv_in(x, mod, g, mu, w_r, w_k, w_v, w0, w1, w2, a0, a1, a2, g1, g2, k_k, k_a, tm):
    bsz, seq, d = x.shape
    tok = pl.BlockSpec((1, tm, d), lambda b, i: (b, i, 0))
    prev = pl.BlockSpec((1, SUBLANES, d),
                        lambda b, i: (b, jnp.maximum(i * (tm // SUBLANES) - 1, 0), 0))
    bf = lambda w: w.astype(BF16)
    vec = lambda t: t.reshape(1, d)
    consts = [vec(g), mu, bf(w_r), bf(w_k), bf(w_v), bf(w1), bf(w2), bf(a1), bf(a2), bf(g1), bf(g2),
              vec(w0), vec(a0), vec(k_k), vec(k_a), _group_mean_matrix(d, RWKV_HEAD)]
    out = jax.ShapeDtypeStruct((bsz, seq, d), F32)
    return pl.pallas_call(
        _rwkv_in_kernel,
        grid=(bsz, seq // tm),
        in_specs=[tok, prev, pl.BlockSpec((1, SUBLANES, d), lambda b, i: (b, 0, 0))]
                 + [_const_spec(t.shape) for t in consts],
        out_specs=[tok] * 7,
        out_shape=[out] * 7,
        compiler_params=_cparams("arbitrary", "arbitrary"),
        name="rwkv_in_proj",
    )(x, x, mod, *consts)


def _mm3(a, b, kind="nn"):
    f = {"nn": _dot, "nt": _dot_nt}[kind]
    a_hi, a_lo = _split_bf16(a)
    b_hi, b_lo = _split_bf16(b)
    return f(a_hi, b_hi) + f(a_hi, b_lo) + f(a_lo, b_hi)


def _rwkv_rec_kernel(r_ref, lw_ref, k_ref, v_ref, kk_ref, a_ref, g_ref, rk_ref, lng_ref, lnb_ref,
                     tri_ref, bd_ref, o_ref, m_ref, *, chunk):
    tstep = pl.program_id(2)

    @pl.when(tstep == 0)
    def _():
        m_ref[...] = jnp.zeros_like(m_ref)

    tb, width = r_ref.shape[1:]
    hd = RWKV_HEAD
    n2 = 2 * chunk
    lane = lax.broadcasted_iota(jnp.int32, (chunk, LANES), 1)
    ri = lax.broadcasted_iota(jnp.int32, (n2, n2), 0)
    ci = lax.broadcasted_iota(jnp.int32, (n2, n2), 1)
    strict = ri > ci
    incl = ri >= ci
    eye = jnp.where(ri == ci, 1.0, 0.0)
    rl = lax.broadcasted_iota(jnp.int32, (LANES, LANES), 0)
    cl_ = lax.broadcasted_iota(jnp.int32, (LANES, LANES), 1)
    tri = tri_ref[...]
    bd = bd_ref[...]

    def stack(t):
        return jnp.concatenate([jnp.where(lane < hd, t, 0.0), jnp.where(lane >= hd, t, 0.0)], axis=0)

    for c in range(tb // chunk):
        rows = slice(c * chunk, (c + 1) * chunk)
        for p in range(width // LANES):
            cols = slice(p * LANES, (p + 1) * LANES)
            r = r_ref[0, rows, cols]
            lw = lw_ref[0, rows, cols]
            k = k_ref[0, rows, cols]
            v = v_ref[0, rows, cols]
            kk = kk_ref[0, rows, cols]
            a = a_ref[0, rows, cols]
            b = kk * a
            hi = lw.astype(BF16)
            r1 = lw - hi.astype(F32)
            mid = r1.astype(BF16)
            lo = (r1 - mid.astype(F32)).astype(BF16)
            cl = _dot(tri, hi) + _dot(tri, mid) + _dot(tri, lo)
            cl_end = cl[chunk - 1:chunk, :]
            p_in = jnp.exp(cl)
            p_inv = jnp.exp(-cl)
            p_prev = jnp.exp(cl - lw)
            p_tail = jnp.exp(cl_end - cl)
            a_s = stack(-kk * p_prev)
            b_s = stack(b * p_inv)
            k_s = stack(k * p_inv)
            r_s = stack(r * p_in)
            v_s = stack(v)
            bh_s = stack(b * p_tail)
            kh_s = stack(k * p_tail)
            g_ab = _mm3(a_s, b_s, "nt")
            g_ak = _mm3(a_s, k_s, "nt")
            g_rb = _mm3(r_s, b_s, "nt")
            g_rk = _mm3(r_s, k_s, "nt")
            nm = jnp.where(strict, g_ab, 0.0)
            tinv = eye + nm
            pw = nm
            for _ in range(chunk.bit_length() - 2):
                pw = _mm3(pw, pw)
                tinv = tinv + _mm3(tinv, pw)
            m0 = m_ref[p]
            rhs = _mm3(a_s, m0) + _mm3(jnp.where(strict, g_ak, 0.0), v_s)
            u_s = _mm3(tinv, rhs)
            y_s = (_mm3(r_s, m0) + _mm3(jnp.where(incl, g_rb, 0.0), u_s)
                   + _mm3(jnp.where(incl, g_rk, 0.0), v_s))
            y = y_s[:chunk] + y_s[chunk:]
            pl_row = jnp.exp(cl_end)
            pl_col = jnp.sum(jnp.where(rl == cl_, jnp.broadcast_to(pl_row, (LANES, LANES)), 0.0),
                             axis=1, keepdims=True)
            m_ref[p] = pl_col * m0 + _mm3(bh_s.T, u_s) + _mm3(kh_s.T, v_s)
            mean = _dot_x2(y, bd)
            dlt = y - mean
            var = _dot_x2(dlt * dlt, bd)
            yn = dlt * lax.rsqrt(var + GN_EPS) * lng_ref[:, cols] + lnb_ref[:, cols]
            rk_sum = _dot_x2(r * k * rk_ref[:, cols], bd) * float(hd)
            out = (yn + rk_sum * v) * g_ref[0, rows, cols]
            o_ref[0, rows, cols] = out.astype(o_ref.dtype)


def _rwkv_recurrence(r, lw, k, v, kk, a, g, r_k, ln_g, ln_b, tb, chunk):
    bsz, seq, d = r.shape
    wblk = 2 * LANES
    tok = pl.BlockSpec((1, tb, wblk), lambda b, j, t: (b, t, j))
    vec = pl.BlockSpec((1, wblk), lambda b, j, t: (0, j))
    tri = jnp.tril(jnp.ones((chunk, chunk), F32)).astype(BF16)
    bd = _group_mean_matrix(LANES, RWKV_HEAD)
    return pl.pallas_call(
        functools.partial(_rwkv_rec_kernel, chunk=chunk),
        grid=(bsz, d // wblk, seq // tb),
        in_specs=[tok] * 7 + [vec] * 3 + [_const_spec(tri.shape), _const_spec(bd.shape)],
        out_specs=tok,
        out_shape=jax.ShapeDtypeStruct((bsz, seq, d), BF16),
        scratch_shapes=[pltpu.VMEM((wblk // LANES, LANES, LANES), F32)],
        compiler_params=_cparams("arbitrary", "arbitrary", "arbitrary"),
        name="rwkv_recurrence",
    )(r, lw, k, v, kk, a, g, r_k.reshape(1, d), ln_g.reshape(1, d), ln_b.reshape(1, d), tri, bd)


def _odd_layer(x, mod, norm_mix, norm_ffn, w_gate, w_up, w_down, mu, w_r, w_k, w_v, w_o, w0, w1,
               w2, a0, a1, a2, g1, g2, k_k, k_a, r_k, ln_g, ln_b):
    bsz, seq, d = x.shape
    r, lw, k, v, kk, a, g = _rwkv_in(x, mod, norm_mix, mu, w_r, w_k, w_v, w0, w1, w2, a0, a1, a2,
                                     g1, g2, k_k, k_a, _pick_tile(seq, 256))
    yg = _rwkv_recurrence(r, lw, k, v, kk, a, g, r_k, ln_g, ln_b, _pick_tile(seq, 256), 64)
    tm = _pick_tile(seq, 512)
    acts = [(yg, pl.BlockSpec((1, tm, d), lambda b, i: (b, i, 0)))]
    return _mix_ffn(x, mod, norm_ffn, acts, [w_o.astype(BF16)], w_gate, w_up, w_down, tm, 256)


def _pick_tile(n, pref):
    t = min(n, pref)
    assert n % t == 0, (n, t)
    return t


def _even_layer(x, mod, tables, lam_init, norm_mix, norm_ffn, w_gate, w_up, w_down, w_in,
                lam_re, lam_im, log_dt, b_re, b_im, c_re, c_im, d_skip, w_glu, q_norm, k_norm,
                lq1, lk1, lq2, lk2, subln, w_out):
    bsz, seq, d = x.shape
    s5w = lam_re.shape[0] * b_re.shape[-1]
    dfw = (w_in.shape[1] - s5w) // 3
    tm = _pick_tile(seq, 512)
    u2, q, k, v = _even_in(x, mod, norm_mix, w_in, q_norm, k_norm, tables, s5w, dfw, tm)
    ys = _s5_mixer(u2.reshape(seq * bsz, s5w), bsz, lam_re, lam_im, log_dt, b_re, b_im,
                   c_re, c_im, d_skip.reshape(-1), w_glu, _pick_tile(seq, 128))
    att = _diff_attention(q, k, v, lq1, lk1, lq2, lk2, subln, lam_init, _pick_tile(seq, 256))
    w_out_bf = w_out.astype(BF16)
    acts = [
        (ys.reshape(seq, bsz * s5w), pl.BlockSpec((tm, s5w), lambda b, i: (i, b))),
        (att, pl.BlockSpec((1, tm, dfw), lambda b, i: (b, i, 0))),
    ]
    return _mix_ffn(x, mod, norm_ffn, acts, [w_out_bf[:s5w], w_out_bf[s5w:]],
                    w_gate, w_up, w_down, tm, 256)


def kernel(x, c, positions, w_ada, b_ada, norm_mix, norm_ffn, ffn_w_gate, ffn_w_up, ffn_w_down,
           ev_w_in, ev_s5_lam_re, ev_s5_lam_im, ev_s5_log_dt, ev_s5_b_re, ev_s5_b_im, ev_s5_c_re,
           ev_s5_c_im, ev_s5_d, ev_s5_w_glu, ev_q_norm, ev_k_norm, ev_lambda_q1, ev_lambda_k1,
           ev_lambda_q2, ev_lambda_k2, ev_subln, ev_w_out, od_mu, od_w_r, od_w_k, od_w_v, od_w_o,
           od_w0, od_w1, od_w2, od_a0, od_a1, od_a2, od_g1, od_g2, od_k_k, od_k_a, od_r_k,
           od_ln_g, od_ln_b):
    depth = w_ada.shape[0]
    mod = _ada_mod(c, w_ada, b_ada)
    tables = _rope_tables(positions)
    for l in range(depth):
        if l % 2 == 0:
            e = l // 2
            lam_init = 0.8 - 0.6 * math.exp(-0.3 * l)
            x = _even_layer(x, mod[l], tables, lam_init, norm_mix[l], norm_ffn[l], ffn_w_gate[l],
                            ffn_w_up[l], ffn_w_down[l], ev_w_in[e], ev_s5_lam_re[e],
                            ev_s5_lam_im[e], ev_s5_log_dt[e], ev_s5_b_re[e], ev_s5_b_im[e],
                            ev_s5_c_re[e], ev_s5_c_im[e], ev_s5_d[e], ev_s5_w_glu[e], ev_q_norm[e],
                            ev_k_norm[e], ev_lambda_q1[e], ev_lambda_k1[e], ev_lambda_q2[e],
                            ev_lambda_k2[e], ev_subln[e], ev_w_out[e])
        else:
            o = l // 2
            x = _odd_layer(x, mod[l], norm_mix[l], norm_ffn[l], ffn_w_gate[l], ffn_w_up[l],
                           ffn_w_down[l], od_mu[o], od_w_r[o], od_w_k[o], od_w_v[o], od_w_o[o],
                           od_w0[o], od_w1[o], od_w2[o], od_a0[o], od_a1[o], od_a2[o], od_g1[o],
                           od_g2[o], od_k_k[o], od_k_a[o], od_r_k[o], od_ln_g[o], od_ln_b[o])
    return x
```

```python
import functools
import math

import jax
import jax.numpy as jnp
from jax import lax
from jax.experimental import pallas as pl
from jax.experimental.pallas import tpu as pltpu

F32 = jnp.float32
BF16 = jnp.bfloat16

RMS_EPS = 1e-6
GN_EPS = 64e-5
ROPE_THETA = 500000.0
DT_HEAD = 64
ROT_DIM = DT_HEAD // 4
S5_GROUP = 16
S5_STATE = 64
RWKV_HEAD = 64
LANES = 128
SUBLANES = 8
VMEM_LIMIT = 56 * 1024 * 1024


def _cparams(*sem):
    return pltpu.CompilerParams(dimension_semantics=sem, vmem_limit_bytes=VMEM_LIMIT)


def _const_spec(shape):
    nd = len(shape)
    return pl.BlockSpec(shape, lambda *_: (0,) * nd, pipeline_mode=pl.Buffered(1))


def _dot(a, b):
    return jnp.dot(a, b, preferred_element_type=F32)


def _dot_nt(a, b):
    return lax.dot_general(a, b, (((1,), (1,)), ((), ())), preferred_element_type=F32)


def _dot_tn(a, b):
    return lax.dot_general(a, b, (((0,), (0,)), ((), ())), preferred_element_type=F32)


def _split_bf16(x):
    hi = x.astype(BF16)
    lo = (x - hi.astype(F32)).astype(BF16)
    return hi, lo


def _dot_x2(x, w_bf16):
    hi, lo = _split_bf16(x)
    return _dot(hi, w_bf16) + _dot(lo, w_bf16)


def _modnorm(x, g, scale, shift):
    ms = jnp.mean(x * x, axis=-1, keepdims=True)
    return (x * lax.rsqrt(ms + RMS_EPS)) * g * (1.0 + scale) + shift


def _ada_kernel(c_ref, w_ref, b_ref, o_ref):
    c = c_ref[...]
    ca = c * jax.nn.sigmoid(c)
    hi, lo = _split_bf16(ca)
    w = w_ref[0]
    w_hi, w_lo = _split_bf16(w)
    o_ref[0] = _dot(hi, w_hi) + _dot(lo, w_hi) + _dot(hi, w_lo) + b_ref[0]


def _ada_mod(c, w_ada, b_ada):
    depth, d, n = w_ada.shape
    bsz = c.shape[0]
    rows = -(-bsz // SUBLANES) * SUBLANES
    c_pad = jnp.zeros((rows, d), F32).at[:bsz].set(c)
    tn = n // 4
    out = pl.pallas_call(
        _ada_kernel,
        grid=(depth, n // tn),
        in_specs=[
            pl.BlockSpec((rows, d), lambda l, j: (0, 0)),
            pl.BlockSpec((1, d, tn), lambda l, j: (l, 0, j)),
            pl.BlockSpec((1, 1, tn), lambda l, j: (l, 0, j)),
        ],
        out_specs=pl.BlockSpec((1, rows, tn), lambda l, j: (l, 0, j)),
        out_shape=jax.ShapeDtypeStruct((depth, rows, n), F32),
        compiler_params=_cparams("arbitrary", "arbitrary"),
        name="ada_mod",
    )(c_pad, w_ada, b_ada.reshape(depth, 1, n))
    mod = out[:, :bsz].reshape(depth, bsz, 6, d)
    return jnp.pad(mod, ((0, 0), (0, 0), (0, SUBLANES - 6), (0, 0)))


def _mul_trig_kernel(p_ref, f_ref, cos_ref, sin_ref):
    a = p_ref[...] * f_ref[...]
    cos_ref[...] = jnp.cos(a)
    sin_ref[...] = jnp.sin(a)


def _rope_tables(positions):
    bsz, seq = positions.shape
    half = ROT_DIM // 2
    inv_freq = ROPE_THETA ** (-jnp.arange(0, ROT_DIM, 2, dtype=F32) / ROT_DIM)
    n = bsz * seq * half
    pos_rep = jnp.broadcast_to(positions.astype(F32)[..., None], (bsz, seq, half))
    frq_rep = jnp.broadcast_to(inv_freq, (bsz, seq, half))
    rows = n // LANES
    tr = min(rows, 512)
    cos, sin = pl.pallas_call(
        _mul_trig_kernel,
        grid=(rows // tr,),
        in_specs=[pl.BlockSpec((tr, LANES), lambda i: (i, 0))] * 2,
        out_specs=[pl.BlockSpec((tr, LANES), lambda i: (i, 0))] * 2,
        out_shape=[jax.ShapeDtypeStruct((rows, LANES), F32)] * 2,
        compiler_params=_cparams("arbitrary"),
        name="rope_trig",
    )(pos_rep.reshape(rows, LANES), frq_rep.reshape(rows, LANES))
    cos = cos.reshape(bsz, seq, half)
    sin = sin.reshape(bsz, seq, half)
    one = jnp.ones((bsz, seq, DT_HEAD - ROT_DIM), F32)
    zero_h = jnp.zeros((bsz, seq, half), F32)
    zero_r = jnp.zeros((bsz, seq, DT_HEAD - ROT_DIM), F32)
    c64 = jnp.concatenate([cos, cos, one], axis=-1)
    sa64 = jnp.concatenate([-sin, zero_h, zero_r], axis=-1)
    sb64 = jnp.concatenate([zero_h, sin, zero_r], axis=-1)
    rep = LANES // DT_HEAD
    return (jnp.tile(c64, (1, 1, rep)), jnp.tile(sa64, (1, 1, rep)), jnp.tile(sb64, (1, 1, rep)))


def _group_mean_matrix(width, group):
    idx = jnp.arange(width) // group
    return jnp.where(idx[:, None] == idx[None, :], 1.0 / group, 0.0).astype(BF16)


def _even_in_kernel(x_ref, mod_ref, g_ref, w_ref, qn_ref, kn_ref, c_ref, sa_ref, sb_ref, bd_ref,
                    u_ref, q_ref, k_ref, v_ref, *, s5w, dfw):
    x = x_ref[0]
    h = _modnorm(x, g_ref[...], mod_ref[0, 1:2, :], mod_ref[0, 0:1, :])
    proj = _dot(h.astype(BF16), w_ref[...])
    u_ref[...] = proj[:, :s5w]
    cosv = c_ref[0]
    sav = sa_ref[0]
    sbv = sb_ref[0]
    half = ROT_DIM // 2

    def norm_rope(t, gn, out_ref, post_scale):
        ms = _dot((t * t).astype(BF16), bd_ref[...])
        t = t * lax.rsqrt(ms + RMS_EPS) * gn
        for j in range(dfw // LANES):
            tj = t[:, j * LANES:(j + 1) * LANES]
            up = pltpu.roll(tj, LANES - half, axis=1)
            dn = pltpu.roll(tj, half, axis=1)
            rj = tj * cosv + up * sav + dn * sbv
            out_ref[0, :, j * LANES:(j + 1) * LANES] = (rj * post_scale).astype(out_ref.dtype)

    norm_rope(proj[:, s5w:s5w + dfw], qn_ref[...], q_ref, DT_HEAD ** -0.5 * math.log2(math.e))
    norm_rope(proj[:, s5w + dfw:s5w + 2 * dfw], kn_ref[...], k_ref, 1.0)
    v_ref[0] = proj[:, s5w + 2 * dfw:].astype(v_ref.dtype)


def _even_in(x, mod, g, w_in, q_norm, k_norm, tables, s5w, dfw, tm):
    bsz, seq, d = x.shape
    ncol = w_in.shape[1]
    cos_t, sa_t, sb_t = tables
    rep = dfw // DT_HEAD
    kern = functools.partial(_even_in_kernel, s5w=s5w, dfw=dfw)
    tok = lambda w: pl.BlockSpec((1, tm, w), lambda b, i: (b, i, 0))
    return pl.pallas_call(
        kern,
        grid=(bsz, seq // tm),
        in_specs=[
            tok(d),
            pl.BlockSpec((1, SUBLANES, d), lambda b, i: (b, 0, 0)),
            _const_spec((1, d)),
            _const_spec((d, ncol)),
            _const_spec((1, dfw)),
            _const_spec((1, dfw)),
            tok(LANES), tok(LANES), tok(LANES),
            _const_spec((dfw, dfw)),
        ],
        out_specs=[
            pl.BlockSpec((tm, s5w), lambda b, i: (i, b)),
            tok(dfw), tok(dfw), tok(dfw),
        ],
        out_shape=[
            jax.ShapeDtypeStruct((seq, bsz * s5w), F32),
            jax.ShapeDtypeStruct((bsz, seq, dfw), BF16),
            jax.ShapeDtypeStruct((bsz, seq, dfw), BF16),
            jax.ShapeDtypeStruct((bsz, seq, dfw), BF16),
        ],
        compiler_params=_cparams("arbitrary", "arbitrary"),
        name="even_in_proj",
    )(x, mod, g.reshape(1, d), w_in.astype(BF16),
      jnp.tile(q_norm, rep).reshape(1, dfw), jnp.tile(k_norm, rep).reshape(1, dfw),
      cos_t, sa_t, sb_t, _group_mean_matrix(dfw, DT_HEAD))


def _s5_disc_kernel(lr_ref, li_ref, ldt_ref, br_ref, bi_ref, ar_ref, ai_ref, bbr_ref, bbi_ref,
                    *, bsz):
    lr = lr_ref[...]
    li = li_ref[...]
    dt = jnp.exp(ldt_ref[...])
    mag = jnp.exp(lr * dt)
    abar_r = mag * jnp.cos(li * dt)
    abar_i = mag * jnp.sin(li * dt)
    den = lr * lr + li * li
    num_r = abar_r - 1.0
    q_r = (num_r * lr + abar_i * li) / den
    q_i = (abar_i * lr - num_r * li) / den
    br = br_ref[...]
    bi = bi_ref[...]
    bbr_ref[...] = q_r * br - q_i * bi
    bbi_ref[...] = q_r * bi + q_i * br
    pr, pi = abar_r, abar_i
    for j in range(SUBLANES):
        if j and j % bsz == 0:
            pr, pi = pr * abar_r - pi * abar_i, pr * abar_i + pi * abar_r
        ar_ref[:, j:j + 1] = pr
        ai_ref[:, j:j + 1] = pi


def _s5_discretise(lam_re, lam_im, log_dt, b_re, b_im, bsz):
    g, p = lam_re.shape
    hh = b_re.shape[-1]
    n = g * p
    tab = jax.ShapeDtypeStruct((n, SUBLANES), F32)
    mat = jax.ShapeDtypeStruct((n, hh), F32)
    ldt = jnp.broadcast_to(log_dt[:, None], (g, p)).reshape(n, 1)
    return pl.pallas_call(
        functools.partial(_s5_disc_kernel, bsz=bsz),
        out_shape=[tab, tab, mat, mat],
        name="s5_discretise",
    )(lam_re.reshape(n, 1), lam_im.reshape(n, 1), ldt, b_re.reshape(n, hh), b_im.reshape(n, hh))


def _s5_scan_kernel(u_ref, bre_ref, bim_ref, cre_ref, cim_ref, dsk_ref, a1r_ref, a1i_ref,
                    pwr_ref, pwi_ref, wglu_ref, o_ref, xr_ref, xi_ref, cr_ref, ci_ref,
                    *, bsz, lane_chunk):
    rows, nstate = xr_ref.shape
    step = pl.program_id(0)

    @pl.when(step == 0)
    def _():
        cr_ref[...] = jnp.zeros_like(cr_ref)
        ci_ref[...] = jnp.zeros_like(ci_ref)

    u = u_ref[...]
    u_bf = u.astype(BF16)
    xr_ref[...] = _dot(u_bf, bre_ref[...])
    xi_ref[...] = _dot(u_bf, bim_ref[...])

    row = lax.broadcasted_iota(jnp.int32, (SUBLANES, lane_chunk), 0)
    steps_per_tile = SUBLANES // bsz
    ntiles = rows // SUBLANES

    for c in range(nstate // lane_chunk):
        cols = pl.ds(c * lane_chunk, lane_chunk)
        a1r = a1r_ref[:, cols]
        a1i = a1i_ref[:, cols]
        pwr = pwr_ref[:, cols]
        pwi = pwi_ref[:, cols]

        def tile_body(i, carry):
            pr, pi = carry
            base = pl.multiple_of(i * SUBLANES, SUBLANES)
            zr = xr_ref[pl.ds(base, SUBLANES), cols]
            zi = xi_ref[pl.ds(base, SUBLANES), cols]
            sh = bsz
            apr, api = a1r, a1i
            for _ in range(steps_per_tile.bit_length() - 1):
                sr = jnp.where(row >= sh, pltpu.roll(zr, sh, axis=0), 0.0)
                si = jnp.where(row >= sh, pltpu.roll(zi, sh, axis=0), 0.0)
                zr, zi = zr + apr * sr - api * si, zi + apr * si + api * sr
                apr, api = apr * apr - api * api, 2.0 * apr * api
                sh *= 2
            last = SUBLANES - bsz
            br_, bi_ = pr, pi
            sh = bsz
            while sh < SUBLANES:
                br_ = jnp.where(row >= last, br_, pltpu.roll(br_, SUBLANES - sh, axis=0))
                bi_ = jnp.where(row >= last, bi_, pltpu.roll(bi_, SUBLANES - sh, axis=0))
                last -= sh
                sh *= 2
            xr = zr + pwr * br_ - pwi * bi_
            xi = zi + pwr * bi_ + pwi * br_
            xr_ref[pl.ds(base, SUBLANES), cols] = xr
            xi_ref[pl.ds(base, SUBLANES), cols] = xi
            return xr, xi

        fr, fi = lax.fori_loop(0, ntiles, tile_body, (cr_ref[:, cols], ci_ref[:, cols]))
        cr_ref[:, cols] = fr
        ci_ref[:, cols] = fi

    y = (_dot(xr_ref[...].astype(BF16), cre_ref[...])
         - _dot(xi_ref[...].astype(BF16), cim_ref[...])
         + dsk_ref[...] * u)
    y = jax.nn.gelu(y)
    gate = jax.nn.sigmoid(_dot(y.astype(BF16), wglu_ref[...]))
    o_ref[...] = (y * gate).astype(o_ref.dtype)


def _s5_mixer(u2, bsz, lam_re, lam_im, log_dt, b_re, b_im, c_re, c_im, d_skip, w_glu, tchunk):
    total_rows, width = u2.shape
    g, p = lam_re.shape
    hh = b_re.shape[-1]
    nstate = g * p
    pw_r, pw_i, bbar_r, bbar_i = _s5_discretise(lam_re, lam_im, log_dt, b_re, b_im, bsz)
    eye = jnp.eye(g, dtype=F32)
    def in_map(bb):
        return jnp.einsum('gph,gk->ghkp', bb.reshape(g, p, hh), eye).reshape(width, nstate)
    def out_map(cc):
        return jnp.einsum('ghp,gk->gpkh', cc, eye).reshape(nstate, width)
    pw_r = pw_r.T
    pw_i = pw_i.T
    a1r = jnp.broadcast_to(pw_r[0:1], (SUBLANES, nstate))
    a1i = jnp.broadcast_to(pw_i[0:1], (SUBLANES, nstate))
    rows = tchunk * bsz
    kern = functools.partial(_s5_scan_kernel, bsz=bsz, lane_chunk=512)
    return pl.pallas_call(
        kern,
        grid=(total_rows // rows,),
        in_specs=[
            pl.BlockSpec((rows, width), lambda i: (i, 0)),
            _const_spec((width, nstate)), _const_spec((width, nstate)),
            _const_spec((nstate, width)), _const_spec((nstate, width)),
            _const_spec((1, width)),
            _const_spec((SUBLANES, nstate)), _const_spec((SUBLANES, nstate)),
            _const_spec((SUBLANES, nstate)), _const_spec((SUBLANES, nstate)),
            _const_spec((width, width)),
        ],
        out_specs=pl.BlockSpec((rows, width), lambda i: (i, 0)),
        out_shape=jax.ShapeDtypeStruct((total_rows, width), BF16),
        scratch_shapes=[
            pltpu.VMEM((rows, nstate), F32), pltpu.VMEM((rows, nstate), F32),
            pltpu.VMEM((SUBLANES, nstate), F32), pltpu.VMEM((SUBLANES, nstate), F32),
        ],
        compiler_params=_cparams("arbitrary"),
        name="s5_scan_glu",
    )(u2, in_map(bbar_r).astype(BF16), in_map(bbar_i).astype(BF16),
      out_map(c_re).astype(BF16), out_map(c_im).astype(BF16),
      d_skip.reshape(1, width), a1r, a1i, pw_r, pw_i, w_glu.astype(BF16))


def _diff_attn_kernel(lam_ref, sub_ref, q_ref, k_ref, v_ref, o_ref, q2_ref, m_ref, acc_ref,
                      *, tq, lam_init):
    qi = pl.program_id(2)
    vdim = v_ref.shape[-1]
    q = q_ref[0]
    lane = lax.broadcasted_iota(jnp.int32, q.shape, 1)
    zero = jnp.zeros_like(q)
    q2_ref[:tq, :] = jnp.where(lane < DT_HEAD, q, zero)
    q2_ref[tq:, :] = jnp.where(lane >= DT_HEAD, q, zero)
    m_ref[...] = jnp.full(m_ref.shape, -jnp.inf, F32)
    acc_ref[...] = jnp.zeros_like(acc_ref)
    ones = jnp.ones((tq, vdim), BF16)

    def kv_block(j, masked):
        start = pl.multiple_of(j * tq, tq)
        kb = k_ref[0, pl.ds(start, tq), :]
        v_ext = jnp.concatenate([v_ref[0, pl.ds(start, tq), :], ones], axis=1)
        s = _dot_nt(q2_ref[...], kb)
        if masked:
            rowi = lax.broadcasted_iota(jnp.int32, (tq, tq), 0)
            coli = lax.broadcasted_iota(jnp.int32, (tq, tq), 1)
            causal = coli <= rowi
            s = jnp.where(jnp.concatenate([causal, causal], axis=0), s, -jnp.inf)
        part = s[:, :LANES]
        for t in range(1, tq // LANES):
            part = jnp.maximum(part, s[:, t * LANES:(t + 1) * LANES])
        m_old = m_ref[...]
        m_new = jnp.maximum(m_old, jnp.max(part, axis=-1, keepdims=True))
        m_ref[...] = m_new
        alpha = jnp.exp2(m_old - m_new)
        p = jnp.concatenate([jnp.exp2(s[:, t * LANES:(t + 1) * LANES] - m_new)
                             for t in range(tq // LANES)], axis=1)
        acc_ref[...] = (jnp.concatenate([alpha] * (2 * vdim // LANES), axis=1) * acc_ref[...]
                        + _dot(p.astype(BF16), v_ext))

    lax.fori_loop(0, qi, lambda j, _: (kv_block(j, False), 0)[1], 0)
    kv_block(qi, True)

    lv = lam_ref[...]
    lam = (jnp.exp(jnp.sum(lv[0:1] * lv[1:2], axis=-1, keepdims=True))
           - jnp.exp(jnp.sum(lv[2:3] * lv[3:4], axis=-1, keepdims=True)) + lam_init)
    acc = acc_ref[...]
    o = (acc[:tq, :vdim] / acc[:tq, vdim:]) - lam * (acc[tq:, :vdim] / acc[tq:, vdim:])
    ms = jnp.mean(o * o, axis=-1, keepdims=True)
    o = o * lax.rsqrt(ms + RMS_EPS) * sub_ref[...] * (1.0 - lam_init)
    o_ref[0] = o.astype(o_ref.dtype)


def _diff_attention(q, k, v, lq1, lk1, lq2, lk2, subln, lam_init, tq):
    bsz, seq, dfw = q.shape
    vdim = 2 * DT_HEAD
    heads = dfw // vdim
    lamv = jnp.zeros((SUBLANES, LANES), F32)
    for i, t in enumerate((lq1, lk1, lq2, lk2)):
        lamv = lamv.at[i, :t.shape[0]].set(t)
    kern = functools.partial(_diff_attn_kernel, tq=tq, lam_init=lam_init)
    kv_spec = pl.BlockSpec((1, seq, vdim), lambda b, h, i: (b, 0, h))
    return pl.pallas_call(
        kern,
        grid=(bsz, heads, seq // tq),
        in_specs=[
            _const_spec((SUBLANES, LANES)),
            _const_spec((1, vdim)),
            pl.BlockSpec((1, tq, vdim), lambda b, h, i: (b, i, h)),
            kv_spec, kv_spec,
        ],
        out_specs=pl.BlockSpec((1, tq, vdim), lambda b, h, i: (b, i, h)),
        out_shape=jax.ShapeDtypeStruct((bsz, seq, dfw), BF16),
        scratch_shapes=[pltpu.VMEM((2 * tq, vdim), BF16), pltpu.VMEM((2 * tq, LANES), F32),
                        pltpu.VMEM((2 * tq, 2 * vdim), F32)],
        compiler_params=_cparams("arbitrary", "arbitrary", "arbitrary"),
        name="diff_attention",
    )(lamv, subln.reshape(1, vdim), q, k, v)


def _mix_ffn_kernel(*refs, n_in):
    x_ref, mod_ref, g_ref = refs[:3]
    act_refs = refs[3:3 + n_in]
    w_refs = refs[3 + n_in:3 + 2 * n_in]
    wg_ref, wu_ref, wd_ref, o_ref, acc_ref = refs[3 + 2 * n_in:]
    mix = None
    for a_ref, w_ref in zip(act_refs, w_refs):
        a = a_ref[0] if len(a_ref.shape) == 3 else a_ref[...]
        t = _dot(a, w_ref[...])
        mix = t if mix is None else mix + t
    x1 = x_ref[0] + mod_ref[0, 2:3, :] * mix
    h = _modnorm(x1, g_ref[...], mod_ref[0, 4:5, :], mod_ref[0, 3:4, :]).astype(BF16)
    acc_ref[...] = jnp.zeros_like(acc_ref)

    def body(j, _):
        gate = _dot(h, wg_ref[j])
        up = _dot(h, wu_ref[j])
        act = (gate * jax.nn.sigmoid(gate) * up).astype(BF16)
        acc_ref[...] += _dot(act, wd_ref[j])
        return 0

    lax.fori_loop(0, wg_ref.shape[0], body, 0)
    o_ref[0] = x1 + mod_ref[0, 5:6, :] * acc_ref[...]


def _mix_ffn(x, mod, g, acts, weights, w_gate, w_up, w_down, tm, hid_chunk):
    bsz, seq, d = x.shape
    hidden = w_gate.shape[1]
    nch = hidden // hid_chunk
    wg = w_gate.astype(BF16).reshape(d, nch, hid_chunk).transpose(1, 0, 2)
    wu = w_up.astype(BF16).reshape(d, nch, hid_chunk).transpose(1, 0, 2)
    wd = w_down.astype(BF16).reshape(nch, hid_chunk, d)
    tok = pl.BlockSpec((1, tm, d), lambda b, i: (b, i, 0))
    return pl.pallas_call(
        functools.partial(_mix_ffn_kernel, n_in=len(acts)),
        grid=(bsz, seq // tm),
        in_specs=[tok, pl.BlockSpec((1, SUBLANES, d), lambda b, i: (b, 0, 0)), _const_spec((1, d))]
                 + [spec for _, spec in acts]
                 + [_const_spec(w.shape) for w in weights]
                 + [_const_spec(wg.shape), _const_spec(wu.shape), _const_spec(wd.shape)],
        out_specs=tok,
        out_shape=jax.ShapeDtypeStruct((bsz, seq, d), F32),
        scratch_shapes=[pltpu.VMEM((tm, d), F32)],
        compiler_params=_cparams("arbitrary", "arbitrary"),
        name="mix_ffn",
    )(x, mod, g.reshape(1, d), *[a for a, _ in acts], *weights, wg, wu, wd)


def _rwkv_in_kernel(x_ref, xp_ref, mod_ref, g_ref, mu_ref, wr_ref, wk_ref, wv_ref, w1_ref, w2_ref,
                    a1_ref, a2_ref, g1_ref, g2_ref, w0_ref, a0_ref, kkw_ref, kaw_ref, bd_ref,
                    r_out, lw_out, k_out, v_out, kk_out, a_out, g_out):
    i = pl.program_id(1)
    g = g_ref[...]
    scale = mod_ref[0, 1:2, :]
    shift = mod_ref[0, 0:1, :]
    h = _modnorm(x_ref[0], g, scale, shift)
    hp = _modnorm(xp_ref[0][SUBLANES - 1:SUBLANES, :], g, scale, shift)
    hp = jnp.where(i == 0, 0.0, hp)
    row = lax.broadcasted_iota(jnp.int32, h.shape, 0)
    h_prev = jnp.where(row == 0, hp, pltpu.roll(h, 1, axis=0))
    dx = h_prev - h

    def lerp(j):
        return (h + dx * mu_ref[j:j + 1, :]).astype(BF16)

    r_out[0] = _dot(lerp(0), wr_ref[...])
    wl = jnp.tanh(_dot(lerp(1), w1_ref[...]))
    wdec = w0_ref[...] + _dot(wl.astype(BF16), w2_ref[...])
    w = -jax.nn.softplus(-wdec) - 0.5
    lw_out[0] = -jnp.exp(w)
    k = _dot(lerp(2), wk_ref[...])
    v_out[0] = _dot(lerp(3), wv_ref[...])
    al = _dot(lerp(4), a1_ref[...])
    a = jax.nn.sigmoid(a0_ref[...] + _dot(al.astype(BF16), a2_ref[...]))
    a_out[0] = a
    gl = jax.nn.sigmoid(_dot(lerp(5), g1_ref[...]))
    g_out[0] = _dot(gl.astype(BF16), g2_ref[...])
    kk = k * kkw_ref[...]
    ss = _dot_x2(kk * kk, bd_ref[...]) * float(RWKV_HEAD)
    kk_out[0] = kk / jnp.maximum(jnp.sqrt(ss), 1e-12)
    k_out[0] = k * (1.0 + (a - 1.0) * kaw_ref[...])


def _rwkv_in(x, mod, g, mu, w_r, w_k, w_v, w0, w1, w2, a0, a1, a2, g1, g2, k_k, k_a, tm):
    bsz, seq, d = x.shape
    tok = pl.BlockSpec((1, tm, d), lambda b, i: (b, i, 0))
    prev = pl.BlockSpec((1, SUBLANES, d),
                        lambda b, i: (b, jnp.maximum(i * (tm // SUBLANES) - 1, 0), 0))
    bf = lambda w: w.astype(BF16)
    vec = lambda t: t.reshape(1, d)
    consts = [vec(g), mu, bf(w_r), bf(w_k), bf(w_v), bf(w1), bf(w2), bf(a1), bf(a2), bf(g1), bf(g2),
              vec(w0), vec(a0), vec(k_k), vec(k_a), _group_mean_matrix(d, RWKV_HEAD)]
    out = jax.ShapeDtypeStruct((bsz, seq, d), F32)
    return pl.pallas_call(
        _rwkv_in_kernel,
        grid=(bsz, seq // tm),
        in_specs=[tok, prev, pl.BlockSpec((1, SUBLANES, d), lambda b, i: (b, 0, 0))]
                 + [_const_spec(t.shape) for t in consts],
        out_specs=[tok] * 7,
        out_shape=[out] * 7,
        compiler_params=_cparams("arbitrary", "arbitrary"),
        name="rwkv_in_proj",
    )(x, x, mod, *consts)


def _rwkv_rec_kernel(r_ref, lw_ref, k_ref, v_ref, kk_ref, a_ref, g_ref, rk_ref, lng_ref, lnb_ref,
                     tri_ref, bd_ref, o_ref, m_ref, y_ref, *, chunk):
    tstep = pl.program_id(2)

    @pl.when(tstep == 0)
    def _():
        m_ref[...] = jnp.zeros_like(m_ref)

    tb, width = r_ref.shape[1:]
    hd = RWKV_HEAD
    n2 = 2 * chunk
    npair = width // LANES
    lane = lax.broadcasted_iota(jnp.int32, (chunk, LANES), 1)
    ri = lax.broadcasted_iota(jnp.int32, (n2, n2), 0)
    ci = lax.broadcasted_iota(jnp.int32, (n2, n2), 1)
    strict = ri > ci
    incl = ri >= ci
    diag = ri == ci
    eye = jnp.where(diag, 1.0, 0.0)
    tri = tri_ref[...]
    bd = bd_ref[...]
    bf = lambda t: t.astype(BF16)

    def stack(t):
        return jnp.concatenate([jnp.where(lane < hd, t, 0.0), jnp.where(lane >= hd, t, 0.0)], axis=0)

    pairs = range(npair)
    colsl = [slice(p * LANES, (p + 1) * LANES) for p in pairs]

    def chunk_body(c, _):
        rows = pl.ds(pl.multiple_of(c * chunk, chunk), chunk)
        ops = []
        for p in pairs:
            cols = colsl[p]
            lw = lw_ref[0, rows, cols]
            hi = bf(lw)
            r1 = lw - hi.astype(F32)
            mid = bf(r1)
            lo = bf(r1 - mid.astype(F32))
            ops.append((lw, _dot(tri, hi) + _dot(tri, mid) + _dot(tri, lo)))
        stk = []
        for p in pairs:
            cols = colsl[p]
            lw, cl = ops[p]
            k = k_ref[0, rows, cols]
            kk = kk_ref[0, rows, cols]
            b = kk * a_ref[0, rows, cols]
            cl_end = cl[chunk - 1:chunk, :]
            p_inv = jnp.exp(-cl)
            p_tail = jnp.exp(cl_end - cl)
            a_s = bf(stack(-kk * jnp.exp(cl - lw)))
            b_s = bf(stack(b * p_inv))
            k_s = bf(stack(k * p_inv))
            r_s = bf(stack(r_ref[0, rows, cols] * jnp.exp(cl)))
            v_s = bf(stack(v_ref[0, rows, cols]))
            tail_t = jnp.concatenate([bf(stack(b * p_tail).T), bf(stack(k * p_tail).T)], axis=1)
            pl_full = jnp.broadcast_to(jnp.exp(cl_end), (LANES, LANES))
            pl_col = jnp.sum(jnp.where(diag, pl_full, 0.0), axis=1, keepdims=True)
            stk.append((a_s, b_s, k_s, r_s, v_s, tail_t, pl_col))
        grams = [_dot_nt(jnp.concatenate([s[0], s[3]], axis=0), jnp.concatenate([s[1], s[2]], axis=0))
                 for s in stk]
        nms = [jnp.where(strict, g[:n2, :n2], 0.0) for g in grams]
        tinv = [eye + nm for nm in nms]
        pw = [bf(nm) for nm in nms]
        pw = [bf(_dot(q, q)) for q in pw]
        for _ in range(chunk.bit_length() - 3):
            both = [_dot(jnp.concatenate([bf(t), q], axis=0), q) for t, q in zip(tinv, pw)]
            tinv = [t + bo[:n2] for t, bo in zip(tinv, both)]
            pw = [bf(bo[n2:]) for bo in both]
        tinv = [bf(t + _dot(bf(t), q)) for t, q in zip(tinv, pw)]
        g_ak = [bf(jnp.where(strict, g[:n2, n2:], 0.0)) for g in grams]
        g_rb = [bf(jnp.where(incl, g[n2:, :n2], 0.0)) for g in grams]
        g_rk = [bf(jnp.where(incl, g[n2:, n2:], 0.0)) for g in grams]
        m0 = [m_ref[p] for p in pairs]
        m0_bf = [bf(m) for m in m0]
        rhs = [bf(_dot(jnp.concatenate([stk[p][0], g_ak[p]], axis=1),
                       jnp.concatenate([m0_bf[p], stk[p][4]], axis=0))) for p in pairs]
        u_s = [bf(_dot(tinv[p], rhs[p])) for p in pairs]
        for p in pairs:
            y_s = _dot(jnp.concatenate([stk[p][3], g_rb[p], g_rk[p]], axis=1),
                       jnp.concatenate([m0_bf[p], u_s[p], stk[p][4]], axis=0))
            y_ref[rows, colsl[p]] = y_s[:chunk] + y_s[chunk:]
        for p in pairs:
            m_ref[p] = stk[p][6] * m0[p] + _dot(stk[p][5], jnp.concatenate([u_s[p], stk[p][4]], axis=0))
        return 0

    lax.fori_loop(0, tb // chunk, chunk_body, 0)

    for p in range(npair):
        cols = slice(p * LANES, (p + 1) * LANES)
        y = y_ref[:, cols]
        mean = _dot_x2(y, bd)
        dlt = y - mean
        var = _dot_x2(dlt * dlt, bd)
        yn = dlt * lax.rsqrt(var + GN_EPS) * lng_ref[:, cols] + lnb_ref[:, cols]
        rk_sum = _dot_x2(r_ref[0, :, cols] * k_ref[0, :, cols] * rk_ref[:, cols], bd) * float(hd)
        out = (yn + rk_sum * v_ref[0, :, cols]) * g_ref[0, :, cols]
        o_ref[0, :, cols] = out.astype(o_ref.dtype)


def _rwkv_recurrence(r, lw, k, v, kk, a, g, r_k, ln_g, ln_b, tb, chunk, wblk):
    bsz, seq, d = r.shape
    assert 2 * chunk == LANES and 2 * RWKV_HEAD == LANES
    tok = pl.BlockSpec((1, tb, wblk), lambda b, j, t: (b, t, j))
    vec = pl.BlockSpec((1, wblk), lambda b, j, t: (0, j))
    tri = jnp.tril(jnp.ones((chunk, chunk), F32)).astype(BF16)
    bd = _group_mean_matrix(LANES, RWKV_HEAD)
    return pl.pallas_call(
        functools.partial(_rwkv_rec_kernel, chunk=chunk),
        grid=(bsz, d // wblk, seq // tb),
        in_specs=[tok] * 7 + [vec] * 3 + [_const_spec(tri.shape), _const_spec(bd.shape)],
        out_specs=tok,
        out_shape=jax.ShapeDtypeStruct((bsz, seq, d), BF16),
        scratch_shapes=[pltpu.VMEM((wblk // LANES, LANES, LANES), F32),
                        pltpu.VMEM((tb, wblk), F32)],
        compiler_params=_cparams("arbitrary", "arbitrary", "arbitrary"),
        name="rwkv_recurrence",
    )(r, lw, k, v, kk, a, g, r_k.reshape(1, d), ln_g.reshape(1, d), ln_b.reshape(1, d), tri, bd)


def _odd_layer(x, mod, norm_mix, norm_ffn, w_gate, w_up, w_down, mu, w_r, w_k, w_v, w_o, w0, w1,
               w2, a0, a1, a2, g1, g2, k_k, k_a, r_k, ln_g, ln_b):
    bsz, seq, d = x.shape
    r, lw, k, v, kk, a, g = _rwkv_in(x, mod, norm_mix, mu, w_r, w_k, w_v, w0, w1, w2, a0, a1, a2,
                                     g1, g2, k_k, k_a, _pick_tile(seq, 256))
    yg = _rwkv_recurrence(r, lw, k, v, kk, a, g, r_k, ln_g, ln_b, _pick_tile(seq, 256), 64,
                          d)
    tm = _pick_tile(seq, 512)
    acts = [(yg, pl.BlockSpec((1, tm, d), lambda b, i: (b, i, 0)))]
    return _mix_ffn(x, mod, norm_ffn, acts, [w_o.astype(BF16)], w_gate, w_up, w_down, tm, 256)


def _pick_tile(n, pref):
    t = min(n, pref)
    assert n % t == 0, (n, t)
    return t


def _even_layer(x, mod, tables, lam_init, norm_mix, norm_ffn, w_gate, w_up, w_down, w_in,
                lam_re, lam_im, log_dt, b_re, b_im, c_re, c_im, d_skip, w_glu, q_norm, k_norm,
                lq1, lk1, lq2, lk2, subln, w_out):
    bsz, seq, d = x.shape
    s5w = lam_re.shape[0] * b_re.shape[-1]
    dfw = (w_in.shape[1] - s5w) // 3
    tm = _pick_tile(seq, 512)
    u2, q, k, v = _even_in(x, mod, norm_mix, w_in, q_norm, k_norm, tables, s5w, dfw, tm)
    ys = _s5_mixer(u2.reshape(seq * bsz, s5w), bsz, lam_re, lam_im, log_dt, b_re, b_im,
                   c_re, c_im, d_skip.reshape(-1), w_glu, _pick_tile(seq, 128))
    att = _diff_attention(q, k, v, lq1, lk1, lq2, lk2, subln, lam_init, _pick_tile(seq, 512))
    w_out_bf = w_out.astype(BF16)
    acts = [
        (ys.reshape(seq, bsz * s5w), pl.BlockSpec((tm, s5w), lambda b, i: (i, b))),
        (att, pl.BlockSpec((1, tm, dfw), lambda b, i: (b, i, 0))),
    ]
    return _mix_ffn(x, mod, norm_ffn, acts, [w_out_bf[:s5w], w_out_bf[s5w:]],
                    w_gate, w_up, w_down, tm, 256)


def kernel(x, c, positions, w_ada, b_ada, norm_mix, norm_ffn, ffn_w_gate, ffn_w_up, ffn_w_down,
           ev_w_in, ev_s5_lam_re, ev_s5_lam_im, ev_s5_log_dt, ev_s5_b_re, ev_s5_b_im, ev_s5_c_re,
           ev_s5_c_im, ev_s5_d, ev_s5_w_glu, ev_q_norm, ev_k_norm, ev_lambda_q1, ev_lambda_k1,
           ev_lambda_q2, ev_lambda_k2, ev_subln, ev_w_out, od_mu, od_w_r, od_w_k, od_w_v, od_w_o,
           od_w0, od_w1, od_w2, od_a0, od_a1, od_a2, od_g1, od_g2, od_k_k, od_k_a, od_r_k,
           od_ln_g, od_ln_b):
    depth = w_ada.shape[0]
    mod = _ada_mod(c, w_ada, b_ada)
    tables = _rope_tables(positions)
    for l in range(depth):
        if l % 2 == 0:
            e = l // 2
            lam_init = 0.8 - 0.6 * math.exp(-0.3 * l)
            x = _even_layer(x, mod[l], tables, lam_init, norm_mix[l], norm_ffn[l], ffn_w_gate[l],
                            ffn_w_up[l], ffn_w_down[l], ev_w_in[e], ev_s5_lam_re[e],
                            ev_s5_lam_im[e], ev_s5_log_dt[e], ev_s5_b_re[e], ev_s5_b_im[e],
                            ev_s5_c_re[e], ev_s5_c_im[e], ev_s5_d[e], ev_s5_w_glu[e], ev_q_norm[e],
                            ev_k_norm[e], ev_lambda_q1[e], ev_lambda_k1[e], ev_lambda_q2[e],
                            ev_lambda_k2[e], ev_subln[e], ev_w_out[e])
        else:
            o = l // 2
            x = _odd_layer(x, mod[l], norm_mix[l], norm_ffn[l], ffn_w_gate[l], ffn_w_up[l],
                           ffn_w_down[l], od_mu[o], od_w_r[o], od_w_k[o], od_w_v[o], od_w_o[o],
                           od_w0[o], od_w1[o], od_w2[o], od_a0[o], od_a1[o], od_a2[o], od_g1[o],
                           od_g2[o], od_k_k[o], od_k_a[o], od_r_k[o], od_ln_g[o], od_ln_b[o])
    return x
```

```python
import functools
import math

import jax
import jax.numpy as jnp
from jax import lax
from jax.experimental import pallas as pl
from jax.experimental.pallas import tpu as pltpu

F32 = jnp.float32
BF16 = jnp.bfloat16

RMS_EPS = 1e-6
GN_EPS = 64e-5
ROPE_THETA = 500000.0
DT_HEAD = 64
ROT_DIM = DT_HEAD // 4
S5_GROUP = 16
S5_STATE = 64
RWKV_HEAD = 64
LANES = 128
SUBLANES = 8
S5_MXU_DEPTH = 256
VMEM_LIMIT = 56 * 1024 * 1024


def _cparams(*sem):
    return pltpu.CompilerParams(dimension_semantics=sem, vmem_limit_bytes=VMEM_LIMIT)


def _const_spec(shape):
    nd = len(shape)
    return pl.BlockSpec(shape, lambda *_: (0,) * nd, pipeline_mode=pl.Buffered(1))


def _dot(a, b):
    return jnp.dot(a, b, preferred_element_type=F32)


def _dot_nt(a, b):
    return lax.dot_general(a, b, (((1,), (1,)), ((), ())), preferred_element_type=F32)


def _dot_tn(a, b):
    return lax.dot_general(a, b, (((0,), (0,)), ((), ())), preferred_element_type=F32)


def _split_bf16(x):
    hi = x.astype(BF16)
    lo = (x - hi.astype(F32)).astype(BF16)
    return hi, lo


def _dot_x2(x, w_bf16):
    hi, lo = _split_bf16(x)
    return _dot(hi, w_bf16) + _dot(lo, w_bf16)


def _modnorm(x, g, scale, shift):
    ms = jnp.mean(x * x, axis=-1, keepdims=True)
    return (x * lax.rsqrt(ms + RMS_EPS)) * g * (1.0 + scale) + shift


def _ada_kernel(c_ref, w_ref, b_ref, o_ref):
    c = c_ref[...]
    ca = c * jax.nn.sigmoid(c)
    hi, lo = _split_bf16(ca)
    w = w_ref[0]
    w_hi, w_lo = _split_bf16(w)
    o_ref[0] = _dot(hi, w_hi) + _dot(lo, w_hi) + _dot(hi, w_lo) + b_ref[0]


def _ada_mod(c, w_ada, b_ada):
    depth, d, n = w_ada.shape
    bsz = c.shape[0]
    rows = -(-bsz // SUBLANES) * SUBLANES
    c_pad = jnp.zeros((rows, d), F32).at[:bsz].set(c)
    tn = n // 4
    out = pl.pallas_call(
        _ada_kernel,
        grid=(depth, n // tn),
        in_specs=[
            pl.BlockSpec((rows, d), lambda l, j: (0, 0)),
            pl.BlockSpec((1, d, tn), lambda l, j: (l, 0, j)),
            pl.BlockSpec((1, 1, tn), lambda l, j: (l, 0, j)),
        ],
        out_specs=pl.BlockSpec((1, rows, tn), lambda l, j: (l, 0, j)),
        out_shape=jax.ShapeDtypeStruct((depth, rows, n), F32),
        compiler_params=_cparams("arbitrary", "arbitrary"),
        name="ada_mod",
    )(c_pad, w_ada, b_ada.reshape(depth, 1, n))
    mod = out[:, :bsz].reshape(depth, bsz, 6, d)
    return jnp.pad(mod, ((0, 0), (0, 0), (0, SUBLANES - 6), (0, 0)))


def _mul_trig_kernel(p_ref, f_ref, cos_ref, sin_ref):
    a = p_ref[...] * f_ref[...]
    cos_ref[...] = jnp.cos(a)
    sin_ref[...] = jnp.sin(a)


def _rope_tables(positions):
    bsz, seq = positions.shape
    half = ROT_DIM // 2
    inv_freq = ROPE_THETA ** (-jnp.arange(0, ROT_DIM, 2, dtype=F32) / ROT_DIM)
    n = bsz * seq * half
    pos_rep = jnp.broadcast_to(positions.astype(F32)[..., None], (bsz, seq, half))
    frq_rep = jnp.broadcast_to(inv_freq, (bsz, seq, half))
    rows = n // LANES
    tr = min(rows, 512)
    cos, sin = pl.pallas_call(
        _mul_trig_kernel,
        grid=(rows // tr,),
        in_specs=[pl.BlockSpec((tr, LANES), lambda i: (i, 0))] * 2,
        out_specs=[pl.BlockSpec((tr, LANES), lambda i: (i, 0))] * 2,
        out_shape=[jax.ShapeDtypeStruct((rows, LANES), F32)] * 2,
        compiler_params=_cparams("arbitrary"),
        name="rope_trig",
    )(pos_rep.reshape(rows, LANES), frq_rep.reshape(rows, LANES))
    cos = cos.reshape(bsz, seq, half)
    sin = sin.reshape(bsz, seq, half)
    one = jnp.ones((bsz, seq, DT_HEAD - ROT_DIM), F32)
    zero_h = jnp.zeros((bsz, seq, half), F32)
    zero_r = jnp.zeros((bsz, seq, DT_HEAD - ROT_DIM), F32)
    c64 = jnp.concatenate([cos, cos, one], axis=-1)
    sa64 = jnp.concatenate([-sin, zero_h, zero_r], axis=-1)
    sb64 = jnp.concatenate([zero_h, sin, zero_r], axis=-1)
    rep = LANES // DT_HEAD
    return (jnp.tile(c64, (1, 1, rep)), jnp.tile(sa64, (1, 1, rep)), jnp.tile(sb64, (1, 1, rep)))


def _group_mean_matrix(width, group):
    idx = jnp.arange(width) // group
    return jnp.where(idx[:, None] == idx[None, :], 1.0 / group, 0.0).astype(BF16)


def _even_in_kernel(x_ref, mod_ref, g_ref, w_ref, qn_ref, kn_ref, c_ref, sa_ref, sb_ref, bd_ref,
                    u_ref, q_ref, k_ref, v_ref, *, s5w, dfw):
    x = x_ref[0]
    h = _modnorm(x, g_ref[...], mod_ref[0, 1:2, :], mod_ref[0, 0:1, :])
    proj = _dot(h.astype(BF16), w_ref[...])
    u_ref[...] = proj[:, :s5w]
    cosv = c_ref[0]
    sav = sa_ref[0]
    sbv = sb_ref[0]
    half = ROT_DIM // 2

    def norm_rope(t, gn, out_ref, post_scale):
        ms = _dot((t * t).astype(BF16), bd_ref[...])
        t = t * lax.rsqrt(ms + RMS_EPS) * gn
        for j in range(dfw // LANES):
            tj = t[:, j * LANES:(j + 1) * LANES]
            up = pltpu.roll(tj, LANES - half, axis=1)
            dn = pltpu.roll(tj, half, axis=1)
            rj = tj * cosv + up * sav + dn * sbv
            out_ref[0, :, j * LANES:(j + 1) * LANES] = (rj * post_scale).astype(out_ref.dtype)

    norm_rope(proj[:, s5w:s5w + dfw], qn_ref[...], q_ref, DT_HEAD ** -0.5 * math.log2(math.e))
    norm_rope(proj[:, s5w + dfw:s5w + 2 * dfw], kn_ref[...], k_ref, 1.0)
    v_ref[0] = proj[:, s5w + 2 * dfw:].astype(v_ref.dtype)


def _even_in(x, mod, g, w_in, q_norm, k_norm, tables, s5w, dfw, tm):
    bsz, seq, d = x.shape
    ncol = w_in.shape[1]
    cos_t, sa_t, sb_t = tables
    rep = dfw // DT_HEAD
    kern = functools.partial(_even_in_kernel, s5w=s5w, dfw=dfw)
    tok = lambda w: pl.BlockSpec((1, tm, w), lambda b, i: (b, i, 0))
    return pl.pallas_call(
        kern,
        grid=(bsz, seq // tm),
        in_specs=[
            tok(d),
            pl.BlockSpec((1, SUBLANES, d), lambda b, i: (b, 0, 0)),
            _const_spec((1, d)),
            _const_spec((d, ncol)),
            _const_spec((1, dfw)),
            _const_spec((1, dfw)),
            tok(LANES), tok(LANES), tok(LANES),
            _const_spec((dfw, dfw)),
        ],
        out_specs=[
            pl.BlockSpec((tm, s5w), lambda b, i: (i, b)),
            tok(dfw), tok(dfw), tok(dfw),
        ],
        out_shape=[
            jax.ShapeDtypeStruct((seq, bsz * s5w), F32),
            jax.ShapeDtypeStruct((bsz, seq, dfw), BF16),
            jax.ShapeDtypeStruct((bsz, seq, dfw), BF16),
            jax.ShapeDtypeStruct((bsz, seq, dfw), BF16),
        ],
        compiler_params=_cparams("arbitrary", "arbitrary"),
        name="even_in_proj",
    )(x, mod, g.reshape(1, d), w_in.astype(BF16),
      jnp.tile(q_norm, rep).reshape(1, dfw), jnp.tile(k_norm, rep).reshape(1, dfw),
      cos_t, sa_t, sb_t, _group_mean_matrix(dfw, DT_HEAD))


def _s5_disc_kernel(lr_ref, li_ref, ldt_ref, br_ref, bi_ref, ar_ref, ai_ref, bbr_ref, bbi_ref,
                    *, bsz):
    lr = lr_ref[...]
    li = li_ref[...]
    dt = jnp.exp(ldt_ref[...])
    mag = jnp.exp(lr * dt)
    abar_r = mag * jnp.cos(li * dt)
    abar_i = mag * jnp.sin(li * dt)
    den = lr * lr + li * li
    num_r = abar_r - 1.0
    q_r = (num_r * lr + abar_i * li) / den
    q_i = (abar_i * lr - num_r * li) / den
    br = br_ref[...]
    bi = bi_ref[...]
    bbr_ref[...] = q_r * br - q_i * bi
    bbi_ref[...] = q_r * bi + q_i * br
    pr, pi = abar_r, abar_i
    for j in range(SUBLANES):
        if j and j % bsz == 0:
            pr, pi = pr * abar_r - pi * abar_i, pr * abar_i + pi * abar_r
        ar_ref[:, j:j + 1] = pr
        ai_ref[:, j:j + 1] = pi


def _s5_discretise(lam_re, lam_im, log_dt, b_re, b_im, bsz):
    g, p = lam_re.shape
    hh = b_re.shape[-1]
    n = g * p
    tab = jax.ShapeDtypeStruct((n, SUBLANES), F32)
    mat = jax.ShapeDtypeStruct((n, hh), F32)
    ldt = jnp.broadcast_to(log_dt[:, None], (g, p)).reshape(n, 1)
    return pl.pallas_call(
        functools.partial(_s5_disc_kernel, bsz=bsz),
        out_shape=[tab, tab, mat, mat],
        name="s5_discretise",
    )(lam_re.reshape(n, 1), lam_im.reshape(n, 1), ldt, b_re.reshape(n, hh), b_im.reshape(n, hh))


def _s5_scan_kernel(u_ref, bre_ref, bim_ref, cre_ref, cim_ref, dsk_ref, a1r_ref, a1i_ref,
                    pwr_ref, pwi_ref, wglu_ref, o_ref, xr_ref, xi_ref, cr_ref, ci_ref,
                    *, bsz, lane_chunk):
    rows, nstate = xr_ref.shape
    step = pl.program_id(0)

    @pl.when(step == 0)
    def _():
        cr_ref[...] = jnp.zeros_like(cr_ref)
        ci_ref[...] = jnp.zeros_like(ci_ref)

    nblk, wpart, spart = bre_ref.shape
    u = u_ref[...]
    u_bf = u.astype(BF16)
    for j in range(nblk):
        uj = u_bf[:, j * wpart:(j + 1) * wpart]
        xr_ref[:, j * spart:(j + 1) * spart] = _dot(uj, bre_ref[j])
        xi_ref[:, j * spart:(j + 1) * spart] = _dot(uj, bim_ref[j])

    row = lax.broadcasted_iota(jnp.int32, (SUBLANES, lane_chunk), 0)
    steps_per_tile = SUBLANES // bsz
    ntiles = rows // SUBLANES

    for c in range(nstate // lane_chunk):
        cols = pl.ds(c * lane_chunk, lane_chunk)
        a1r = a1r_ref[:, cols]
        a1i = a1i_ref[:, cols]
        pwr = pwr_ref[:, cols]
        pwi = pwi_ref[:, cols]

        def tile_body(i, carry):
            pr, pi = carry
            base = pl.multiple_of(i * SUBLANES, SUBLANES)
            zr = xr_ref[pl.ds(base, SUBLANES), cols]
            zi = xi_ref[pl.ds(base, SUBLANES), cols]
            sh = bsz
            apr, api = a1r, a1i
            for _ in range(steps_per_tile.bit_length() - 1):
                sr = jnp.where(row >= sh, pltpu.roll(zr, sh, axis=0), 0.0)
                si = jnp.where(row >= sh, pltpu.roll(zi, sh, axis=0), 0.0)
                zr, zi = zr + apr * sr - api * si, zi + apr * si + api * sr
                apr, api = apr * apr - api * api, 2.0 * apr * api
                sh *= 2
            last = SUBLANES - bsz
            br_, bi_ = pr, pi
            sh = bsz
            while sh < SUBLANES:
                br_ = jnp.where(row >= last, br_, pltpu.roll(br_, SUBLANES - sh, axis=0))
                bi_ = jnp.where(row >= last, bi_, pltpu.roll(bi_, SUBLANES - sh, axis=0))
                last -= sh
                sh *= 2
            xr = zr + pwr * br_ - pwi * bi_
            xi = zi + pwr * bi_ + pwi * br_
            xr_ref[pl.ds(base, SUBLANES), cols] = xr
            xi_ref[pl.ds(base, SUBLANES), cols] = xi
            return xr, xi

        fr, fi = lax.fori_loop(0, ntiles, tile_body, (cr_ref[:, cols], ci_ref[:, cols]))
        cr_ref[:, cols] = fr
        ci_ref[:, cols] = fi

    y = jnp.concatenate(
        [_dot(xr_ref[:, j * spart:(j + 1) * spart].astype(BF16), cre_ref[j])
         - _dot(xi_ref[:, j * spart:(j + 1) * spart].astype(BF16), cim_ref[j])
         for j in range(nblk)], axis=1) + dsk_ref[...] * u
    y = jax.nn.gelu(y)
    gate = jax.nn.sigmoid(_dot(y.astype(BF16), wglu_ref[...]))
    o_ref[...] = (y * gate).astype(o_ref.dtype)


def _s5_mixer(u2, bsz, lam_re, lam_im, log_dt, b_re, b_im, c_re, c_im, d_skip, w_glu, tchunk):
    total_rows, width = u2.shape
    g, p = lam_re.shape
    hh = b_re.shape[-1]
    nstate = g * p
    pw_r, pw_i, bbar_r, bbar_i = _s5_discretise(lam_re, lam_im, log_dt, b_re, b_im, bsz)
    gb = max(1, min(g, S5_MXU_DEPTH // hh))
    nblk = g // gb
    eye = jnp.eye(gb, dtype=F32)
    def in_map(bb):
        return jnp.einsum('jgph,gk->jghkp', bb.reshape(nblk, gb, p, hh), eye).reshape(
            nblk, gb * hh, gb * p)
    def out_map(cc):
        return jnp.einsum('jghp,gk->jgpkh', cc.reshape(nblk, gb, hh, p), eye).reshape(
            nblk, gb * p, gb * hh)
    pw_r = pw_r.T
    pw_i = pw_i.T
    a1r = jnp.broadcast_to(pw_r[0:1], (SUBLANES, nstate))
    a1i = jnp.broadcast_to(pw_i[0:1], (SUBLANES, nstate))
    rows = tchunk * bsz
    kern = functools.partial(_s5_scan_kernel, bsz=bsz, lane_chunk=512)
    return pl.pallas_call(
        kern,
        grid=(total_rows // rows,),
        in_specs=[
            pl.BlockSpec((rows, width), lambda i: (i, 0)),
            _const_spec((nblk, gb * hh, gb * p)), _const_spec((nblk, gb * hh, gb * p)),
            _const_spec((nblk, gb * p, gb * hh)), _const_spec((nblk, gb * p, gb * hh)),
            _const_spec((1, width)),
            _const_spec((SUBLANES, nstate)), _const_spec((SUBLANES, nstate)),
            _const_spec((SUBLANES, nstate)), _const_spec((SUBLANES, nstate)),
            _const_spec((width, width)),
        ],
        out_specs=pl.BlockSpec((rows, width), lambda i: (i, 0)),
        out_shape=jax.ShapeDtypeStruct((total_rows, width), BF16),
        scratch_shapes=[
            pltpu.VMEM((rows, nstate), F32), pltpu.VMEM((rows, nstate), F32),
            pltpu.VMEM((SUBLANES, nstate), F32), pltpu.VMEM((SUBLANES, nstate), F32),
        ],
        compiler_params=_cparams("arbitrary"),
        name="s5_scan_glu",
    )(u2, in_map(bbar_r).astype(BF16), in_map(bbar_i).astype(BF16),
      out_map(c_re).astype(BF16), out_map(c_im).astype(BF16),
      d_skip.reshape(1, width), a1r, a1i, pw_r, pw_i, w_glu.astype(BF16))


def _diff_attn_kernel(lam_ref, sub_ref, q_ref, k_ref, v_ref, o_ref, q2_ref, m_ref, acc_ref,
                      *, tq, lam_init):
    qi = pl.program_id(2)
    vdim = v_ref.shape[-1]
    q = q_ref[0]
    lane = lax.broadcasted_iota(jnp.int32, q.shape, 1)
    zero = jnp.zeros_like(q)
    q2_ref[:tq, :] = jnp.where(lane < DT_HEAD, q, zero)
    q2_ref[tq:, :] = jnp.where(lane >= DT_HEAD, q, zero)
    m_ref[...] = jnp.full(m_ref.shape, -jnp.inf, F32)
    acc_ref[...] = jnp.zeros_like(acc_ref)
    ones = jnp.ones((tq, vdim), BF16)

    def scores(j):
        start = pl.multiple_of(j * tq, tq)
        return _dot_nt(q2_ref[...], k_ref[0, pl.ds(start, tq), :])

    def update(j, s, masked):
        start = pl.multiple_of(j * tq, tq)
        v_ext = jnp.concatenate([v_ref[0, pl.ds(start, tq), :], ones], axis=1)
        if masked:
            rowi = lax.broadcasted_iota(jnp.int32, (tq, tq), 0)
            coli = lax.broadcasted_iota(jnp.int32, (tq, tq), 1)
            causal = coli <= rowi
            s = jnp.where(jnp.concatenate([causal, causal], axis=0), s, -jnp.inf)
        part = s[:, :LANES]
        for t in range(1, tq // LANES):
            part = jnp.maximum(part, s[:, t * LANES:(t + 1) * LANES])
        m_old = m_ref[...]
        m_new = jnp.maximum(m_old, jnp.max(part, axis=-1, keepdims=True))
        m_ref[...] = m_new
        alpha = jnp.exp2(m_old - m_new)
        p = jnp.concatenate([jnp.exp2(s[:, t * LANES:(t + 1) * LANES] - m_new)
                             for t in range(tq // LANES)], axis=1)
        acc_ref[...] = (jnp.concatenate([alpha] * (2 * vdim // LANES), axis=1) * acc_ref[...]
                        + _dot(p.astype(BF16), v_ext))

    def pair(j0, mask_second):
        s0 = scores(j0)
        s1 = scores(j0 + 1)
        update(j0, s0, False)
        update(j0 + 1, s1, mask_second)

    npairs = qi // 2
    lax.fori_loop(0, npairs, lambda jj, _: (pair(2 * jj, False), 0)[1], 0)

    @pl.when(qi % 2 == 1)
    def _():
        pair(qi - 1, True)

    @pl.when(qi % 2 == 0)
    def _():
        update(qi, scores(qi), True)

    lv = lam_ref[...]
    lam = (jnp.exp(jnp.sum(lv[0:1] * lv[1:2], axis=-1, keepdims=True))
           - jnp.exp(jnp.sum(lv[2:3] * lv[3:4], axis=-1, keepdims=True)) + lam_init)
    acc = acc_ref[...]
    o = (acc[:tq, :vdim] / acc[:tq, vdim:]) - lam * (acc[tq:, :vdim] / acc[tq:, vdim:])
    ms = jnp.mean(o * o, axis=-1, keepdims=True)
    o = o * lax.rsqrt(ms + RMS_EPS) * sub_ref[...] * (1.0 - lam_init)
    o_ref[0] = o.astype(o_ref.dtype)


def _diff_attention(q, k, v, lq1, lk1, lq2, lk2, subln, lam_init, tq):
    bsz, seq, dfw = q.shape
    vdim = 2 * DT_HEAD
    heads = dfw // vdim
    lamv = jnp.zeros((SUBLANES, LANES), F32)
    for i, t in enumerate((lq1, lk1, lq2, lk2)):
        lamv = lamv.at[i, :t.shape[0]].set(t)
    kern = functools.partial(_diff_attn_kernel, tq=tq, lam_init=lam_init)
    kv_spec = pl.BlockSpec((1, seq, vdim), lambda b, h, i: (b, 0, h))
    return pl.pallas_call(
        kern,
        grid=(bsz, heads, seq // tq),
        in_specs=[
            _const_spec((SUBLANES, LANES)),
            _const_spec((1, vdim)),
            pl.BlockSpec((1, tq, vdim), lambda b, h, i: (b, i, h)),
            kv_spec, kv_spec,
        ],
        out_specs=pl.BlockSpec((1, tq, vdim), lambda b, h, i: (b, i, h)),
        out_shape=jax.ShapeDtypeStruct((bsz, seq, dfw), BF16),
        scratch_shapes=[pltpu.VMEM((2 * tq, vdim), BF16), pltpu.VMEM((2 * tq, LANES), F32),
                        pltpu.VMEM((2 * tq, 2 * vdim), F32)],
        compiler_params=_cparams("arbitrary", "arbitrary", "arbitrary"),
        name="diff_attention",
    )(lamv, subln.reshape(1, vdim), q, k, v)


def _mix_ffn_kernel(*refs, n_in, hid_chunk):
    x_ref, mod_ref, g_ref = refs[:3]
    act_refs = refs[3:3 + n_in]
    w_refs = refs[3 + n_in:3 + 2 * n_in]
    wg_ref, wu_ref, wd_ref, o_ref = refs[3 + 2 * n_in:]
    mix = None
    for a_ref, w_ref in zip(act_refs, w_refs):
        a = a_ref[0] if len(a_ref.shape) == 3 else a_ref[...]
        t = _dot(a, w_ref[...])
        mix = t if mix is None else mix + t
    x1 = x_ref[0] + mod_ref[0, 2:3, :] * mix
    h = _modnorm(x1, g_ref[...], mod_ref[0, 4:5, :], mod_ref[0, 3:4, :]).astype(BF16)
    acc = None
    for j in range(wg_ref.shape[1] // hid_chunk):
        sl = slice(j * hid_chunk, (j + 1) * hid_chunk)
        gate = _dot(h, wg_ref[:, sl])
        up = _dot(h, wu_ref[:, sl])
        act = (gate * jax.nn.sigmoid(gate) * up).astype(BF16)
        t = _dot(act, wd_ref[sl, :])
        acc = t if acc is None else acc + t
    o_ref[0] = x1 + mod_ref[0, 5:6, :] * acc


def _mix_ffn(x, mod, g, acts, weights, w_gate, w_up, w_down, tm, hid_chunk):
    bsz, seq, d = x.shape
    assert w_gate.shape[1] % hid_chunk == 0
    wg = w_gate.astype(BF16)
    wu = w_up.astype(BF16)
    wd = w_down.astype(BF16)
    tok = pl.BlockSpec((1, tm, d), lambda b, i: (b, i, 0))
    return pl.pallas_call(
        functools.partial(_mix_ffn_kernel, n_in=len(acts), hid_chunk=hid_chunk),
        grid=(bsz, seq // tm),
        in_specs=[tok, pl.BlockSpec((1, SUBLANES, d), lambda b, i: (b, 0, 0)), _const_spec((1, d))]
                 + [spec for _, spec in acts]
                 + [_const_spec(w.shape) for w in weights]
                 + [_const_spec(wg.shape), _const_spec(wu.shape), _const_spec(wd.shape)],
        out_specs=tok,
        out_shape=jax.ShapeDtypeStruct((bsz, seq, d), F32),
        compiler_params=_cparams("arbitrary", "arbitrary"),
        name="mix_ffn",
    )(x, mod, g.reshape(1, d), *[a for a, _ in acts], *weights, wg, wu, wd)


def _rwkv_in_kernel(x_ref, xp_ref, mod_ref, g_ref, mu_ref, wr_ref, wk_ref, wv_ref, w1_ref, w2_ref,
                    a1_ref, a2_ref, g1_ref, g2_ref, w0_ref, a0_ref, kkw_ref, kaw_ref, bd_ref,
                    r_out, lw_out, k_out, v_out, kk_out, a_out, g_out):
    i = pl.program_id(1)
    g = g_ref[...]
    scale = mod_ref[0, 1:2, :]
    shift = mod_ref[0, 0:1, :]
    h = _modnorm(x_ref[0], g, scale, shift)
    hp = _modnorm(xp_ref[0][SUBLANES - 1:SUBLANES, :], g, scale, shift)
    hp = jnp.where(i == 0, 0.0, hp)
    row = lax.broadcasted_iota(jnp.int32, h.shape, 0)
    h_prev = jnp.where(row == 0, hp, pltpu.roll(h, 1, axis=0))
    dx = h_prev - h

    def lerp(j):
        return (h + dx * mu_ref[j:j + 1, :]).astype(BF16)

    r_out[0] = _dot(lerp(0), wr_ref[...])
    wl = jnp.tanh(_dot(lerp(1), w1_ref[...]))
    wdec = w0_ref[...] + _dot(wl.astype(BF16), w2_ref[...])
    w = -jax.nn.softplus(-wdec) - 0.5
    lw_out[0] = -jnp.exp(w)
    k = _dot(lerp(2), wk_ref[...])
    v_out[0] = _dot(lerp(3), wv_ref[...])
    al = _dot(lerp(4), a1_ref[...])
    a = jax.nn.sigmoid(a0_ref[...] + _dot(al.astype(BF16), a2_ref[...]))
    a_out[0] = a
    gl = jax.nn.sigmoid(_dot(lerp(5), g1_ref[...]))
    g_out[0] = _dot(gl.astype(BF16), g2_ref[...])
    kk = k * kkw_ref[...]
    bd = bd_ref[...]
    for j in range(kk.shape[1] // LANES):
        kj = kk[:, j * LANES:(j + 1) * LANES]
        ss = _dot_x2(kj * kj, bd) * float(RWKV_HEAD)
        kk_out[0, :, j * LANES:(j + 1) * LANES] = kj / jnp.maximum(jnp.sqrt(ss), 1e-12)
    k_out[0] = k * (1.0 + (a - 1.0) * kaw_ref[...])


def _rwkv_in(x, mod, g, mu, w_r, w_k, w_v, w0, w1, w2, a0, a1, a2, g1, g2, k_k, k_a, tm):
    bsz, seq, d = x.shape
    tok = pl.BlockSpec((1, tm, d), lambda b, i: (b, i, 0))
    prev = pl.BlockSpec((1, SUBLANES, d),
                        lambda b, i: (b, jnp.maximum(i * (tm // SUBLANES) - 1, 0), 0))
    bf = lambda w: w.astype(BF16)
    vec = lambda t: t.reshape(1, d)
    consts = [vec(g), mu, bf(w_r), bf(w_k), bf(w_v), bf(w1), bf(w2), bf(a1), bf(a2), bf(g1), bf(g2),
              vec(w0), vec(a0), vec(k_k), vec(k_a), _group_mean_matrix(LANES, RWKV_HEAD)]
    out = jax.ShapeDtypeStruct((bsz, seq, d), F32)
    return pl.pallas_call(
        _rwkv_in_kernel,
        grid=(bsz, seq // tm),
        in_specs=[tok, prev, pl.BlockSpec((1, SUBLANES, d), lambda b, i: (b, 0, 0))]
                 + [_const_spec(t.shape) for t in consts],
        out_specs=[tok] * 7,
        out_shape=[out] * 7,
        compiler_params=_cparams("arbitrary", "arbitrary"),
        name="rwkv_in_proj",
    )(x, x, mod, *consts)


def _rwkv_rec_kernel(r_ref, lw_ref, k_ref, v_ref, kk_ref, a_ref, g_ref, rk_ref, lng_ref, lnb_ref,
                     tri_ref, bd_ref, o_ref, m_ref, y_ref, *, chunk):
    tstep = pl.program_id(2)

    @pl.when(tstep == 0)
    def _():
        m_ref[...] = jnp.zeros_like(m_ref)

    tb, width = r_ref.shape[1:]
    hd = RWKV_HEAD
    n2 = 2 * chunk
    npair = width // LANES
    lane = lax.broadcasted_iota(jnp.int32, (chunk, LANES), 1)
    ri = lax.broadcasted_iota(jnp.int32, (n2, n2), 0)
    ci = lax.broadcasted_iota(jnp.int32, (n2, n2), 1)
    strict = ri > ci
    incl = ri >= ci
    diag = ri == ci
    eye = jnp.where(diag, 1.0, 0.0)
    tri = tri_ref[...]
    bd = bd_ref[...]
    bf = lambda t: t.astype(BF16)

    def stack(t):
        return jnp.concatenate([jnp.where(lane < hd, t, 0.0), jnp.where(lane >= hd, t, 0.0)], axis=0)

    pairs = range(npair)
    colsl = [slice(p * LANES, (p + 1) * LANES) for p in pairs]

    def chunk_body(c, _):
        rows = pl.ds(pl.multiple_of(c * chunk, chunk), chunk)
        ops = []
        for p in pairs:
            cols = colsl[p]
            lw = lw_ref[0, rows, cols]
            hi = bf(lw)
            r1 = lw - hi.astype(F32)
            mid = bf(r1)
            lo = bf(r1 - mid.astype(F32))
            ops.append((lw, _dot(tri, hi) + _dot(tri, mid) + _dot(tri, lo)))
        stk = []
        for p in pairs:
            cols = colsl[p]
            lw, cl = ops[p]
            k = k_ref[0, rows, cols]
            kk = kk_ref[0, rows, cols]
            b = kk * a_ref[0, rows, cols]
            cl_end = cl[chunk - 1:chunk, :]
            p_inv = jnp.exp(-cl)
            p_tail = jnp.exp(cl_end - cl)
            a_s = bf(stack(-kk * jnp.exp(cl - lw)))
            b_s = bf(stack(b * p_inv))
            k_s = bf(stack(k * p_inv))
            r_s = bf(stack(r_ref[0, rows, cols] * jnp.exp(cl)))
            v_s = bf(stack(v_ref[0, rows, cols]))
            tail_t = jnp.concatenate([bf(stack(b * p_tail).T), bf(stack(k * p_tail).T)], axis=1)
            pl_full = jnp.broadcast_to(jnp.exp(cl_end), (LANES, LANES))
            pl_col = jnp.sum(jnp.where(diag, pl_full, 0.0), axis=1, keepdims=True)
            stk.append((a_s, b_s, k_s, r_s, v_s, tail_t, pl_col))
        grams = [_dot_nt(jnp.concatenate([s[0], s[3]], axis=0), jnp.concatenate([s[1], s[2]], axis=0))
                 for s in stk]
        nms = [jnp.where(strict, g[:n2, :n2], 0.0) for g in grams]
        tinv = [eye + nm for nm in nms]
        pw = [bf(nm) for nm in nms]
        pw = [bf(_dot(q, q)) for q in pw]
        for _ in range(chunk.bit_length() - 3):
            both = [_dot(jnp.concatenate([bf(t), q], axis=0), q) for t, q in zip(tinv, pw)]
            tinv = [t + bo[:n2] for t, bo in zip(tinv, both)]
            pw = [bf(bo[n2:]) for bo in both]
        tinv = [bf(t + _dot(bf(t), q)) for t, q in zip(tinv, pw)]
        g_ak = [bf(jnp.where(strict, g[:n2, n2:], 0.0)) for g in grams]
        g_rb = [bf(jnp.where(incl, g[n2:, :n2], 0.0)) for g in grams]
        g_rk = [bf(jnp.where(incl, g[n2:, n2:], 0.0)) for g in grams]
        m0 = [m_ref[p] for p in pairs]
        m0_bf = [bf(m) for m in m0]
        rhs = [bf(_dot(jnp.concatenate([stk[p][0], g_ak[p]], axis=1),
                       jnp.concatenate([m0_bf[p], stk[p][4]], axis=0))) for p in pairs]
        u_s = [bf(_dot(tinv[p], rhs[p])) for p in pairs]
        for p in pairs:
            y_s = _dot(jnp.concatenate([stk[p][3], g_rb[p], g_rk[p]], axis=1),
                       jnp.concatenate([m0_bf[p], u_s[p], stk[p][4]], axis=0))
            y_ref[rows, colsl[p]] = y_s[:chunk] + y_s[chunk:]
        for p in pairs:
            m_ref[p] = stk[p][6] * m0[p] + _dot(stk[p][5], jnp.concatenate([u_s[p], stk[p][4]], axis=0))
        return 0

    lax.fori_loop(0, tb // chunk, chunk_body, 0)

    for p in range(npair):
        cols = slice(p * LANES, (p + 1) * LANES)
        y = y_ref[:, cols]
        mean = _dot_x2(y, bd)
        dlt = y - mean
        var = _dot(bf(dlt * dlt), bd)
        yn = dlt * lax.rsqrt(var + GN_EPS) * lng_ref[:, cols] + lnb_ref[:, cols]
        rk_sum = _dot(bf(r_ref[0, :, cols] * k_ref[0, :, cols] * rk_ref[:, cols]), bd) * float(hd)
        out = (yn + rk_sum * v_ref[0, :, cols]) * g_ref[0, :, cols]
        o_ref[0, :, cols] = out.astype(o_ref.dtype)


def _rwkv_recurrence(r, lw, k, v, kk, a, g, r_k, ln_g, ln_b, tb, chunk, wblk):
    bsz, seq, d = r.shape
    assert 2 * chunk == LANES and 2 * RWKV_HEAD == LANES
    tok = pl.BlockSpec((1, tb, wblk), lambda b, j, t: (b, t, j))
    vec = pl.BlockSpec((1, wblk), lambda b, j, t: (0, j))
    tri = jnp.tril(jnp.ones((chunk, chunk), F32)).astype(BF16)
    bd = _group_mean_matrix(LANES, RWKV_HEAD)
    return pl.pallas_call(
        functools.partial(_rwkv_rec_kernel, chunk=chunk),
        grid=(bsz, d // wblk, seq // tb),
        in_specs=[tok] * 7 + [vec] * 3 + [_const_spec(tri.shape), _const_spec(bd.shape)],
        out_specs=tok,
        out_shape=jax.ShapeDtypeStruct((bsz, seq, d), BF16),
        scratch_shapes=[pltpu.VMEM((wblk // LANES, LANES, LANES), F32),
                        pltpu.VMEM((tb, wblk), F32)],
        compiler_params=_cparams("arbitrary", "arbitrary", "arbitrary"),
        name="rwkv_recurrence",
    )(r, lw, k, v, kk, a, g, r_k.reshape(1, d), ln_g.reshape(1, d), ln_b.reshape(1, d), tri, bd)


def _odd_layer(x, mod, norm_mix, norm_ffn, w_gate, w_up, w_down, mu, w_r, w_k, w_v, w_o, w0, w1,
               w2, a0, a1, a2, g1, g2, k_k, k_a, r_k, ln_g, ln_b):
    bsz, seq, d = x.shape
    r, lw, k, v, kk, a, g = _rwkv_in(x, mod, norm_mix, mu, w_r, w_k, w_v, w0, w1, w2, a0, a1, a2,
                                     g1, g2, k_k, k_a, _pick_tile(seq, 256))
    yg = _rwkv_recurrence(r, lw, k, v, kk, a, g, r_k, ln_g, ln_b, _pick_tile(seq, 256), 64,
                          d)
    tm = _pick_tile(seq, 512)
    acts = [(yg, pl.BlockSpec((1, tm, d), lambda b, i: (b, i, 0)))]
    return _mix_ffn(x, mod, norm_ffn, acts, [w_o.astype(BF16)], w_gate, w_up, w_down, tm, 256)


def _pick_tile(n, pref):
    t = min(n, pref)
    assert n % t == 0, (n, t)
    return t


def _even_layer(x, mod, tables, lam_init, norm_mix, norm_ffn, w_gate, w_up, w_down, w_in,
                lam_re, lam_im, log_dt, b_re, b_im, c_re, c_im, d_skip, w_glu, q_norm, k_norm,
                lq1, lk1, lq2, lk2, subln, w_out):
    bsz, seq, d = x.shape
    s5w = lam_re.shape[0] * b_re.shape[-1]
    dfw = (w_in.shape[1] - s5w) // 3
    tm = _pick_tile(seq, 512)
    u2, q, k, v = _even_in(x, mod, norm_mix, w_in, q_norm, k_norm, tables, s5w, dfw, tm)
    ys = _s5_mixer(u2.reshape(seq * bsz, s5w), bsz, lam_re, lam_im, log_dt, b_re, b_im,
                   c_re, c_im, d_skip.reshape(-1), w_glu, _pick_tile(seq, 128))
    att = _diff_attention(q, k, v, lq1, lk1, lq2, lk2, subln, lam_init, _pick_tile(seq, 512))
    w_out_bf = w_out.astype(BF16)
    acts = [
        (ys.reshape(seq, bsz * s5w), pl.BlockSpec((tm, s5w), lambda b, i: (i, b))),
        (att, pl.BlockSpec((1, tm, dfw), lambda b, i: (b, i, 0))),
    ]
    return _mix_ffn(x, mod, norm_ffn, acts, [w_out_bf[:s5w], w_out_bf[s5w:]],
                    w_gate, w_up, w_down, tm, 256)


def kernel(x, c, positions, w_ada, b_ada, norm_mix, norm_ffn, ffn_w_gate, ffn_w_up, ffn_w_down,
           ev_w_in, ev_s5_lam_re, ev_s5_lam_im, ev_s5_log_dt, ev_s5_b_re, ev_s5_b_im, ev_s5_c_re,
           ev_s5_c_im, ev_s5_d, ev_s5_w_glu, ev_q_norm, ev_k_norm, ev_lambda_q1, ev_lambda_k1,
           ev_lambda_q2, ev_lambda_k2, ev_subln, ev_w_out, od_mu, od_w_r, od_w_k, od_w_v, od_w_o,
           od_w0, od_w1, od_w2, od_a0, od_a1, od_a2, od_g1, od_g2, od_k_k, od_k_a, od_r_k,
           od_ln_g, od_ln_b):
    depth = w_ada.shape[0]
    mod = _ada_mod(c, w_ada, b_ada)
    tables = _rope_tables(positions)
    for l in range(depth):
        if l % 2 == 0:
            e = l // 2
            lam_init = 0.8 - 0.6 * math.exp(-0.3 * l)
            x = _even_layer(x, mod[l], tables, lam_init, norm_mix[l], norm_ffn[l], ffn_w_gate[l],
                            ffn_w_up[l], ffn_w_down[l], ev_w_in[e], ev_s5_lam_re[e],
                            ev_s5_lam_im[e], ev_s5_log_dt[e], ev_s5_b_re[e], ev_s5_b_im[e],
                            ev_s5_c_re[e], ev_s5_c_im[e], ev_s5_d[e], ev_s5_w_glu[e], ev_q_norm[e],
                            ev_k_norm[e], ev_lambda_q1[e], ev_lambda_k1[e], ev_lambda_q2[e],
                            ev_lambda_k2[e], ev_subln[e], ev_w_out[e])
        else:
            o = l // 2
            x = _odd_layer(x, mod[l], norm_mix[l], norm_ffn[l], ffn_w_gate[l], ffn_w_up[l],
                           ffn_w_down[l], od_mu[o], od_w_r[o], od_w_k[o], od_w_v[o], od_w_o[o],
                           od_w0[o], od_w1[o], od_w2[o], od_a0[o], od_a1[o], od_a2[o], od_g1[o],
                           od_g2[o], od_k_k[o], od_k_a[o], od_r_k[o], od_ln_g[o], od_ln_b[o])
    return x
```

```python
import functools
import math

import jax
import jax.numpy as jnp
from jax import lax
from jax.experimental import pallas as pl
from jax.experimental.pallas import tpu as pltpu

F32 = jnp.float32
BF16 = jnp.bfloat16

RMS_EPS = 1e-6
GN_EPS = 64e-5
ROPE_THETA = 500000.0
DT_HEAD = 64
ROT_DIM = DT_HEAD // 4
S5_GROUP = 16
S5_STATE = 64
RWKV_HEAD = 64
LANES = 128
SUBLANES = 8
S5_MXU_DEPTH = 256
VMEM_LIMIT = 56 * 1024 * 1024


def _cparams(*sem):
    return pltpu.CompilerParams(dimension_semantics=sem, vmem_limit_bytes=VMEM_LIMIT)


def _const_spec(shape):
    nd = len(shape)
    return pl.BlockSpec(shape, lambda *_: (0,) * nd, pipeline_mode=pl.Buffered(1))


def _dot(a, b):
    return jnp.dot(a, b, preferred_element_type=F32)


def _dot_nt(a, b):
    return lax.dot_general(a, b, (((1,), (1,)), ((), ())), preferred_element_type=F32)


def _dot_tn(a, b):
    return lax.dot_general(a, b, (((0,), (0,)), ((), ())), preferred_element_type=F32)


def _split_bf16(x):
    hi = x.astype(BF16)
    lo = (x - hi.astype(F32)).astype(BF16)
    return hi, lo


def _dot_x2(x, w_bf16):
    hi, lo = _split_bf16(x)
    return _dot(hi, w_bf16) + _dot(lo, w_bf16)


def _modnorm(x, g, scale, shift):
    ms = jnp.mean(x * x, axis=-1, keepdims=True)
    return (x * lax.rsqrt(ms + RMS_EPS)) * g * (1.0 + scale) + shift


def _ada_kernel(c_ref, w_ref, b_ref, o_ref):
    c = c_ref[...]
    ca = c * jax.nn.sigmoid(c)
    hi, lo = _split_bf16(ca)
    w = w_ref[0]
    w_hi, w_lo = _split_bf16(w)
    o_ref[0] = _dot(hi, w_hi) + _dot(lo, w_hi) + _dot(hi, w_lo) + b_ref[0]


def _ada_mod(c, w_ada, b_ada):
    depth, d, n = w_ada.shape
    bsz = c.shape[0]
    rows = -(-bsz // SUBLANES) * SUBLANES
    c_pad = jnp.zeros((rows, d), F32).at[:bsz].set(c)
    tn = n // 4
    out = pl.pallas_call(
        _ada_kernel,
        grid=(depth, n // tn),
        in_specs=[
            pl.BlockSpec((rows, d), lambda l, j: (0, 0)),
            pl.BlockSpec((1, d, tn), lambda l, j: (l, 0, j)),
            pl.BlockSpec((1, 1, tn), lambda l, j: (l, 0, j)),
        ],
        out_specs=pl.BlockSpec((1, rows, tn), lambda l, j: (l, 0, j)),
        out_shape=jax.ShapeDtypeStruct((depth, rows, n), F32),
        compiler_params=_cparams("arbitrary", "arbitrary"),
        name="ada_mod",
    )(c_pad, w_ada, b_ada.reshape(depth, 1, n))
    mod = out[:, :bsz].reshape(depth, bsz, 6, d)
    return jnp.pad(mod, ((0, 0), (0, 0), (0, SUBLANES - 6), (0, 0)))


def _mul_trig_kernel(p_ref, f_ref, cos_ref, sin_ref):
    a = p_ref[...] * f_ref[...]
    cos_ref[...] = jnp.cos(a)
    sin_ref[...] = jnp.sin(a)


def _rope_tables(positions):
    bsz, seq = positions.shape
    half = ROT_DIM // 2
    inv_freq = ROPE_THETA ** (-jnp.arange(0, ROT_DIM, 2, dtype=F32) / ROT_DIM)
    n = bsz * seq * half
    pos_rep = jnp.broadcast_to(positions.astype(F32)[..., None], (bsz, seq, half))
    frq_rep = jnp.broadcast_to(inv_freq, (bsz, seq, half))
    rows = n // LANES
    tr = min(rows, 512)
    cos, sin = pl.pallas_call(
        _mul_trig_kernel,
        grid=(rows // tr,),
        in_specs=[pl.BlockSpec((tr, LANES), lambda i: (i, 0))] * 2,
        out_specs=[pl.BlockSpec((tr, LANES), lambda i: (i, 0))] * 2,
        out_shape=[jax.ShapeDtypeStruct((rows, LANES), F32)] * 2,
        compiler_params=_cparams("arbitrary"),
        name="rope_trig",
    )(pos_rep.reshape(rows, LANES), frq_rep.reshape(rows, LANES))
    cos = cos.reshape(bsz, seq, half)
    sin = sin.reshape(bsz, seq, half)
    one = jnp.ones((bsz, seq, DT_HEAD - ROT_DIM), F32)
    zero_h = jnp.zeros((bsz, seq, half), F32)
    zero_r = jnp.zeros((bsz, seq, DT_HEAD - ROT_DIM), F32)
    c64 = jnp.concatenate([cos, cos, one], axis=-1)
    sa64 = jnp.concatenate([-sin, zero_h, zero_r], axis=-1)
    sb64 = jnp.concatenate([zero_h, sin, zero_r], axis=-1)
    rep = LANES // DT_HEAD
    return (jnp.tile(c64, (1, 1, rep)), jnp.tile(sa64, (1, 1, rep)), jnp.tile(sb64, (1, 1, rep)))


def _group_mean_matrix(width, group):
    idx = jnp.arange(width) // group
    return jnp.where(idx[:, None] == idx[None, :], 1.0 / group, 0.0).astype(BF16)


def _even_in_kernel(x_ref, mod_ref, g_ref, w_ref, qn_ref, kn_ref, c_ref, sa_ref, sb_ref, bd_ref,
                    u_ref, q_ref, k_ref, v_ref, *, s5w, dfw):
    x = x_ref[0]
    h = _modnorm(x, g_ref[...], mod_ref[0, 1:2, :], mod_ref[0, 0:1, :])
    proj = _dot(h.astype(BF16), w_ref[...])
    u_ref[...] = proj[:, :s5w]
    cosv = c_ref[0]
    sav = sa_ref[0]
    sbv = sb_ref[0]
    half = ROT_DIM // 2

    def norm_rope(t, gn, out_ref, post_scale):
        ms = _dot((t * t).astype(BF16), bd_ref[...])
        t = t * lax.rsqrt(ms + RMS_EPS) * gn
        for j in range(dfw // LANES):
            tj = t[:, j * LANES:(j + 1) * LANES]
            up = pltpu.roll(tj, LANES - half, axis=1)
            dn = pltpu.roll(tj, half, axis=1)
            rj = tj * cosv + up * sav + dn * sbv
            out_ref[0, :, j * LANES:(j + 1) * LANES] = (rj * post_scale).astype(out_ref.dtype)

    norm_rope(proj[:, s5w:s5w + dfw], qn_ref[...], q_ref, DT_HEAD ** -0.5 * math.log2(math.e))
    norm_rope(proj[:, s5w + dfw:s5w + 2 * dfw], kn_ref[...], k_ref, 1.0)
    v_ref[0] = proj[:, s5w + 2 * dfw:].astype(v_ref.dtype)


def _even_in(x, mod, g, w_in, q_norm, k_norm, tables, s5w, dfw, tm):
    bsz, seq, d = x.shape
    ncol = w_in.shape[1]
    cos_t, sa_t, sb_t = tables
    rep = dfw // DT_HEAD
    kern = functools.partial(_even_in_kernel, s5w=s5w, dfw=dfw)
    tok = lambda w: pl.BlockSpec((1, tm, w), lambda i, b: (b, i, 0))
    return pl.pallas_call(
        kern,
        grid=(seq // tm, bsz),
        in_specs=[
            tok(d),
            pl.BlockSpec((1, SUBLANES, d), lambda i, b: (b, 0, 0)),
            _const_spec((1, d)),
            _const_spec((d, ncol)),
            _const_spec((1, dfw)),
            _const_spec((1, dfw)),
            tok(LANES), tok(LANES), tok(LANES),
            _const_spec((dfw, dfw)),
        ],
        out_specs=[
            pl.BlockSpec((tm, s5w), lambda i, b: (i, b)),
            tok(dfw), tok(dfw), tok(dfw),
        ],
        out_shape=[
            jax.ShapeDtypeStruct((seq, bsz * s5w), F32),
            jax.ShapeDtypeStruct((bsz, seq, dfw), BF16),
            jax.ShapeDtypeStruct((bsz, seq, dfw), BF16),
            jax.ShapeDtypeStruct((bsz, seq, dfw), BF16),
        ],
        compiler_params=_cparams("arbitrary", "arbitrary"),
        name="even_in_proj",
    )(x, mod, g.reshape(1, d), w_in.astype(BF16),
      jnp.tile(q_norm, rep).reshape(1, dfw), jnp.tile(k_norm, rep).reshape(1, dfw),
      cos_t, sa_t, sb_t, _group_mean_matrix(dfw, DT_HEAD))


def _s5_disc_kernel(lr_ref, li_ref, ldt_ref, br_ref, bi_ref, ar_ref, ai_ref, bbr_ref, bbi_ref,
                    *, bsz):
    lr = lr_ref[...]
    li = li_ref[...]
    dt = jnp.exp(ldt_ref[...])
    mag = jnp.exp(lr * dt)
    abar_r = mag * jnp.cos(li * dt)
    abar_i = mag * jnp.sin(li * dt)
    den = lr * lr + li * li
    num_r = abar_r - 1.0
    q_r = (num_r * lr + abar_i * li) / den
    q_i = (abar_i * lr - num_r * li) / den
    br = br_ref[...]
    bi = bi_ref[...]
    bbr_ref[...] = q_r * br - q_i * bi
    bbi_ref[...] = q_r * bi + q_i * br
    pr, pi = abar_r, abar_i
    for j in range(SUBLANES):
        if j and j % bsz == 0:
            pr, pi = pr * abar_r - pi * abar_i, pr * abar_i + pi * abar_r
        ar_ref[:, j:j + 1] = pr
        ai_ref[:, j:j + 1] = pi


def _s5_discretise(lam_re, lam_im, log_dt, b_re, b_im, bsz):
    g, p = lam_re.shape
    hh = b_re.shape[-1]
    n = g * p
    tab = jax.ShapeDtypeStruct((n, SUBLANES), F32)
    mat = jax.ShapeDtypeStruct((n, hh), F32)
    ldt = jnp.broadcast_to(log_dt[:, None], (g, p)).reshape(n, 1)
    return pl.pallas_call(
        functools.partial(_s5_disc_kernel, bsz=bsz),
        out_shape=[tab, tab, mat, mat],
        name="s5_discretise",
    )(lam_re.reshape(n, 1), lam_im.reshape(n, 1), ldt, b_re.reshape(n, hh), b_im.reshape(n, hh))


def _s5_scan_kernel(u_ref, bre_ref, bim_ref, cre_ref, cim_ref, dsk_ref, a1r_ref, a1i_ref,
                    pwr_ref, pwi_ref, wglu_ref, o_ref, xr_ref, xi_ref, cr_ref, ci_ref,
                    *, bsz, lane_chunk):
    rows, nstate = xr_ref.shape
    step = pl.program_id(0)

    @pl.when(step == 0)
    def _():
        cr_ref[...] = jnp.zeros_like(cr_ref)
        ci_ref[...] = jnp.zeros_like(ci_ref)

    nblk, wpart, spart = bre_ref.shape
    u = u_ref[...]
    u_bf = u.astype(BF16)
    for j in range(nblk):
        uj = u_bf[:, j * wpart:(j + 1) * wpart]
        xr_ref[:, j * spart:(j + 1) * spart] = _dot(uj, bre_ref[j])
        xi_ref[:, j * spart:(j + 1) * spart] = _dot(uj, bim_ref[j])

    row = lax.broadcasted_iota(jnp.int32, (SUBLANES, lane_chunk), 0)
    steps_per_tile = SUBLANES // bsz
    ntiles = rows // SUBLANES

    for c in range(nstate // lane_chunk):
        cols = pl.ds(c * lane_chunk, lane_chunk)
        a1r = a1r_ref[:, cols]
        a1i = a1i_ref[:, cols]
        pwr = pwr_ref[:, cols]
        pwi = pwi_ref[:, cols]

        def tile_body(i, carry):
            pr, pi = carry
            base = pl.multiple_of(i * SUBLANES, SUBLANES)
            zr = xr_ref[pl.ds(base, SUBLANES), cols]
            zi = xi_ref[pl.ds(base, SUBLANES), cols]
            sh = bsz
            apr, api = a1r, a1i
            for _ in range(steps_per_tile.bit_length() - 1):
                sr = jnp.where(row >= sh, pltpu.roll(zr, sh, axis=0), 0.0)
                si = jnp.where(row >= sh, pltpu.roll(zi, sh, axis=0), 0.0)
                zr, zi = zr + apr * sr - api * si, zi + apr * si + api * sr
                apr, api = apr * apr - api * api, 2.0 * apr * api
                sh *= 2
            last = SUBLANES - bsz
            br_, bi_ = pr, pi
            sh = bsz
            while sh < SUBLANES:
                br_ = jnp.where(row >= last, br_, pltpu.roll(br_, SUBLANES - sh, axis=0))
                bi_ = jnp.where(row >= last, bi_, pltpu.roll(bi_, SUBLANES - sh, axis=0))
                last -= sh
                sh *= 2
            xr = zr + pwr * br_ - pwi * bi_
            xi = zi + pwr * bi_ + pwi * br_
            xr_ref[pl.ds(base, SUBLANES), cols] = xr
            xi_ref[pl.ds(base, SUBLANES), cols] = xi
            return xr, xi

        fr, fi = lax.fori_loop(0, ntiles, tile_body, (cr_ref[:, cols], ci_ref[:, cols]))
        cr_ref[:, cols] = fr
        ci_ref[:, cols] = fi

    y = jnp.concatenate(
        [_dot(xr_ref[:, j * spart:(j + 1) * spart].astype(BF16), cre_ref[j])
         - _dot(xi_ref[:, j * spart:(j + 1) * spart].astype(BF16), cim_ref[j])
         for j in range(nblk)], axis=1) + dsk_ref[...] * u
    y = jax.nn.gelu(y)
    gate = jax.nn.sigmoid(_dot(y.astype(BF16), wglu_ref[...]))
    o_ref[...] = (y * gate).astype(o_ref.dtype)


def _s5_mixer(u2, bsz, lam_re, lam_im, log_dt, b_re, b_im, c_re, c_im, d_skip, w_glu, tchunk):
    total_rows, width = u2.shape
    g, p = lam_re.shape
    hh = b_re.shape[-1]
    nstate = g * p
    pw_r, pw_i, bbar_r, bbar_i = _s5_discretise(lam_re, lam_im, log_dt, b_re, b_im, bsz)
    gb = max(1, min(g, S5_MXU_DEPTH // hh))
    nblk = g // gb
    eye = jnp.eye(gb, dtype=F32)
    def in_map(bb):
        return jnp.einsum('jgph,gk->jghkp', bb.reshape(nblk, gb, p, hh), eye).reshape(
            nblk, gb * hh, gb * p)
    def out_map(cc):
        return jnp.einsum('jghp,gk->jgpkh', cc.reshape(nblk, gb, hh, p), eye).reshape(
            nblk, gb * p, gb * hh)
    pw_r = pw_r.T
    pw_i = pw_i.T
    a1r = jnp.broadcast_to(pw_r[0:1], (SUBLANES, nstate))
    a1i = jnp.broadcast_to(pw_i[0:1], (SUBLANES, nstate))
    rows = tchunk * bsz
    kern = functools.partial(_s5_scan_kernel, bsz=bsz, lane_chunk=512)
    return pl.pallas_call(
        kern,
        grid=(total_rows // rows,),
        in_specs=[
            pl.BlockSpec((rows, width), lambda i: (i, 0)),
            _const_spec((nblk, gb * hh, gb * p)), _const_spec((nblk, gb * hh, gb * p)),
            _const_spec((nblk, gb * p, gb * hh)), _const_spec((nblk, gb * p, gb * hh)),
            _const_spec((1, width)),
            _const_spec((SUBLANES, nstate)), _const_spec((SUBLANES, nstate)),
            _const_spec((SUBLANES, nstate)), _const_spec((SUBLANES, nstate)),
            _const_spec((width, width)),
        ],
        out_specs=pl.BlockSpec((rows, width), lambda i: (i, 0)),
        out_shape=jax.ShapeDtypeStruct((total_rows, width), BF16),
        scratch_shapes=[
            pltpu.VMEM((rows, nstate), F32), pltpu.VMEM((rows, nstate), F32),
            pltpu.VMEM((SUBLANES, nstate), F32), pltpu.VMEM((SUBLANES, nstate), F32),
        ],
        compiler_params=_cparams("arbitrary"),
        name="s5_scan_glu",
    )(u2, in_map(bbar_r).astype(BF16), in_map(bbar_i).astype(BF16),
      out_map(c_re).astype(BF16), out_map(c_im).astype(BF16),
      d_skip.reshape(1, width), a1r, a1i, pw_r, pw_i, w_glu.astype(BF16))


def _diff_attn_kernel(lam_ref, sub_ref, q_ref, k_ref, v_ref, o_ref, q2_ref, m_ref, acc_ref,
                      *, tq, lam_init):
    qi = pl.program_id(2)
    vdim = v_ref.shape[-1]
    q = q_ref[0]
    lane = lax.broadcasted_iota(jnp.int32, q.shape, 1)
    zero = jnp.zeros_like(q)
    q2_ref[:tq, :] = jnp.where(lane < DT_HEAD, q, zero)
    q2_ref[tq:, :] = jnp.where(lane >= DT_HEAD, q, zero)
    m_ref[...] = jnp.full(m_ref.shape, -jnp.inf, F32)
    acc_ref[...] = jnp.zeros_like(acc_ref)
    ones = jnp.ones((tq, vdim), BF16)

    def scores(j):
        start = pl.multiple_of(j * tq, tq)
        return _dot_nt(q2_ref[...], k_ref[0, pl.ds(start, tq), :])

    def update(j, s, masked):
        start = pl.multiple_of(j * tq, tq)
        v_ext = jnp.concatenate([v_ref[0, pl.ds(start, tq), :], ones], axis=1)
        if masked:
            rowi = lax.broadcasted_iota(jnp.int32, (tq, tq), 0)
            coli = lax.broadcasted_iota(jnp.int32, (tq, tq), 1)
            causal = coli <= rowi
            s = jnp.where(jnp.concatenate([causal, causal], axis=0), s, -jnp.inf)
        part = s[:, :LANES]
        for t in range(1, tq // LANES):
            part = jnp.maximum(part, s[:, t * LANES:(t + 1) * LANES])
        m_old = m_ref[...]
        m_new = jnp.maximum(m_old, jnp.max(part, axis=-1, keepdims=True))
        m_ref[...] = m_new
        alpha = jnp.exp2(m_old - m_new)
        p = jnp.concatenate([jnp.exp2(s[:, t * LANES:(t + 1) * LANES] - m_new)
                             for t in range(tq // LANES)], axis=1)
        acc_ref[...] = (jnp.concatenate([alpha] * (2 * vdim // LANES), axis=1) * acc_ref[...]
                        + _dot(p.astype(BF16), v_ext))

    def pair(j0, mask_second):
        s0 = scores(j0)
        s1 = scores(j0 + 1)
        update(j0, s0, False)
        update(j0 + 1, s1, mask_second)

    npairs = qi // 2
    lax.fori_loop(0, npairs, lambda jj, _: (pair(2 * jj, False), 0)[1], 0)

    @pl.when(qi % 2 == 1)
    def _():
        pair(qi - 1, True)

    @pl.when(qi % 2 == 0)
    def _():
        update(qi, scores(qi), True)

    lv = lam_ref[...]
    lam = (jnp.exp(jnp.sum(lv[0:1] * lv[1:2], axis=-1, keepdims=True))
           - jnp.exp(jnp.sum(lv[2:3] * lv[3:4], axis=-1, keepdims=True)) + lam_init)
    acc = acc_ref[...]
    o = (acc[:tq, :vdim] / acc[:tq, vdim:]) - lam * (acc[tq:, :vdim] / acc[tq:, vdim:])
    ms = jnp.mean(o * o, axis=-1, keepdims=True)
    o = o * lax.rsqrt(ms + RMS_EPS) * sub_ref[...] * (1.0 - lam_init)
    o_ref[0] = o.astype(o_ref.dtype)


def _diff_attention(q, k, v, lq1, lk1, lq2, lk2, subln, lam_init, tq):
    bsz, seq, dfw = q.shape
    vdim = 2 * DT_HEAD
    heads = dfw // vdim
    lamv = jnp.zeros((SUBLANES, LANES), F32)
    for i, t in enumerate((lq1, lk1, lq2, lk2)):
        lamv = lamv.at[i, :t.shape[0]].set(t)
    kern = functools.partial(_diff_attn_kernel, tq=tq, lam_init=lam_init)
    kv_spec = pl.BlockSpec((1, seq, vdim), lambda b, h, i: (b, 0, h))
    return pl.pallas_call(
        kern,
        grid=(bsz, heads, seq // tq),
        in_specs=[
            _const_spec((SUBLANES, LANES)),
            _const_spec((1, vdim)),
            pl.BlockSpec((1, tq, vdim), lambda b, h, i: (b, i, h)),
            kv_spec, kv_spec,
        ],
        out_specs=pl.BlockSpec((1, tq, vdim), lambda b, h, i: (b, i, h)),
        out_shape=jax.ShapeDtypeStruct((bsz, seq, dfw), BF16),
        scratch_shapes=[pltpu.VMEM((2 * tq, vdim), BF16), pltpu.VMEM((2 * tq, LANES), F32),
                        pltpu.VMEM((2 * tq, 2 * vdim), F32)],
        compiler_params=_cparams("arbitrary", "arbitrary", "arbitrary"),
        name="diff_attention",
    )(lamv, subln.reshape(1, vdim), q, k, v)


def _mix_ffn_kernel(*refs, n_in, hid_chunk):
    x_ref, mod_ref, g_ref = refs[:3]
    act_refs = refs[3:3 + n_in]
    w_refs = refs[3 + n_in:3 + 2 * n_in]
    wg_ref, wu_ref, wd_ref, o_ref = refs[3 + 2 * n_in:]
    mix = None
    for a_ref, w_ref in zip(act_refs, w_refs):
        a = a_ref[0] if len(a_ref.shape) == 3 else a_ref[...]
        t = _dot(a, w_ref[...])
        mix = t if mix is None else mix + t
    x1 = x_ref[0] + mod_ref[0, 2:3, :] * mix
    h = _modnorm(x1, g_ref[...], mod_ref[0, 4:5, :], mod_ref[0, 3:4, :]).astype(BF16)
    acc = None
    for j in range(wg_ref.shape[1] // hid_chunk):
        sl = slice(j * hid_chunk, (j + 1) * hid_chunk)
        gate = _dot(h, wg_ref[:, sl])
        up = _dot(h, wu_ref[:, sl])
        act = (gate * jax.nn.sigmoid(gate) * up).astype(BF16)
        t = _dot(act, wd_ref[sl, :])
        acc = t if acc is None else acc + t
    o_ref[0] = x1 + mod_ref[0, 5:6, :] * acc


def _mix_ffn(x, mod, g, acts, weights, w_gate, w_up, w_down, tm, hid_chunk):
    bsz, seq, d = x.shape
    assert w_gate.shape[1] % hid_chunk == 0
    wg = w_gate.astype(BF16)
    wu = w_up.astype(BF16)
    wd = w_down.astype(BF16)
    tok = pl.BlockSpec((1, tm, d), lambda i, b: (b, i, 0))
    return pl.pallas_call(
        functools.partial(_mix_ffn_kernel, n_in=len(acts), hid_chunk=hid_chunk),
        grid=(seq // tm, bsz),
        in_specs=[tok, pl.BlockSpec((1, SUBLANES, d), lambda i, b: (b, 0, 0)), _const_spec((1, d))]
                 + [spec for _, spec in acts]
                 + [_const_spec(w.shape) for w in weights]
                 + [_const_spec(wg.shape), _const_spec(wu.shape), _const_spec(wd.shape)],
        out_specs=tok,
        out_shape=jax.ShapeDtypeStruct((bsz, seq, d), F32),
        compiler_params=_cparams("arbitrary", "arbitrary"),
        name="mix_ffn",
    )(x, mod, g.reshape(1, d), *[a for a, _ in acts], *weights, wg, wu, wd)


def _rwkv_in_kernel(x_ref, xp_ref, mod_ref, g_ref, mu_ref, wr_ref, wk_ref, wv_ref, w1_ref, w2_ref,
                    a1_ref, a2_ref, g1_ref, g2_ref, w0_ref, a0_ref, kkw_ref, kaw_ref, bd_ref,
                    r_out, lw_out, k_out, v_out, kk_out, a_out, g_out):
    i = pl.program_id(1)
    g = g_ref[...]
    scale = mod_ref[0, 1:2, :]
    shift = mod_ref[0, 0:1, :]
    h = _modnorm(x_ref[0], g, scale, shift)
    hp = _modnorm(xp_ref[0][SUBLANES - 1:SUBLANES, :], g, scale, shift)
    hp = jnp.where(i == 0, 0.0, hp)
    row = lax.broadcasted_iota(jnp.int32, h.shape, 0)
    h_prev = jnp.where(row == 0, hp, pltpu.roll(h, 1, axis=0))
    dx = h_prev - h

    def lerp(j):
        return (h + dx * mu_ref[j:j + 1, :]).astype(BF16)

    r_out[0] = _dot(lerp(0), wr_ref[...])
    wl = jnp.tanh(_dot(lerp(1), w1_ref[...]))
    wdec = w0_ref[...] + _dot(wl.astype(BF16), w2_ref[...])
    w = -jax.nn.softplus(-wdec) - 0.5
    lw_out[0] = -jnp.exp(w)
    k = _dot(lerp(2), wk_ref[...])
    v_out[0] = _dot(lerp(3), wv_ref[...])
    al = _dot(lerp(4), a1_ref[...])
    a = jax.nn.sigmoid(a0_ref[...] + _dot(al.astype(BF16), a2_ref[...]))
    a_out[0] = a
    gl = jax.nn.sigmoid(_dot(lerp(5), g1_ref[...]))
    g_out[0] = _dot(gl.astype(BF16), g2_ref[...])
    kk = k * kkw_ref[...]
    bd = bd_ref[...]
    for j in range(kk.shape[1] // LANES):
        kj = kk[:, j * LANES:(j + 1) * LANES]
        ss = _dot_x2(kj * kj, bd) * float(RWKV_HEAD)
        kk_out[0, :, j * LANES:(j + 1) * LANES] = kj / jnp.maximum(jnp.sqrt(ss), 1e-12)
    k_out[0] = k * (1.0 + (a - 1.0) * kaw_ref[...])


def _rwkv_in(x, mod, g, mu, w_r, w_k, w_v, w0, w1, w2, a0, a1, a2, g1, g2, k_k, k_a, tm):
    bsz, seq, d = x.shape
    tok = pl.BlockSpec((1, tm, d), lambda b, i: (b, i, 0))
    prev = pl.BlockSpec((1, SUBLANES, d),
                        lambda b, i: (b, jnp.maximum(i * (tm // SUBLANES) - 1, 0), 0))
    bf = lambda w: w.astype(BF16)
    vec = lambda t: t.reshape(1, d)
    consts = [vec(g), mu, bf(w_r), bf(w_k), bf(w_v), bf(w1), bf(w2), bf(a1), bf(a2), bf(g1), bf(g2),
              vec(w0), vec(a0), vec(k_k), vec(k_a), _group_mean_matrix(LANES, RWKV_HEAD)]
    out = jax.ShapeDtypeStruct((bsz, seq, d), F32)
    return pl.pallas_call(
        _rwkv_in_kernel,
        grid=(bsz, seq // tm),
        in_specs=[tok, prev, pl.BlockSpec((1, SUBLANES, d), lambda b, i: (b, 0, 0))]
                 + [_const_spec(t.shape) for t in consts],
        out_specs=[tok] * 7,
        out_shape=[out] * 7,
        compiler_params=_cparams("arbitrary", "arbitrary"),
        name="rwkv_in_proj",
    )(x, x, mod, *consts)


def _rwkv_rec_kernel(r_ref, lw_ref, k_ref, v_ref, kk_ref, a_ref, g_ref, rk_ref, lng_ref, lnb_ref,
                     tri_ref, bd_ref, o_ref, m_ref, y_ref, *, chunk):
    tstep = pl.program_id(2)

    @pl.when(tstep == 0)
    def _():
        m_ref[...] = jnp.zeros_like(m_ref)

    tb, width = r_ref.shape[1:]
    hd = RWKV_HEAD
    gw = m_ref.shape[-1]
    nh = gw // hd
    groups = range(width // gw)
    colsl = [slice(q * gw, (q + 1) * gw) for q in groups]
    lane = lax.broadcasted_iota(jnp.int32, (chunk, gw), 1)
    trow = lax.broadcasted_iota(jnp.int32, (chunk, gw), 0)
    jpos = lane % hd
    strict = jpos < trow
    incl = jpos <= trow
    eye = jnp.where(jpos == trow, 1.0, 0.0)
    head_of_lane = lane // hd
    in_head = [head_of_lane == h for h in range(nh)]
    sq_r = lax.broadcasted_iota(jnp.int32, (gw, gw), 0)
    sq_c = lax.broadcasted_iota(jnp.int32, (gw, gw), 1)
    same_head = (sq_r // hd) == (sq_c // hd)
    diag = sq_r == sq_c
    tri = tri_ref[...]
    bd = bd_ref[...]
    bf = lambda t: t.astype(BF16)

    def blockdiag(y):
        return bf(jnp.concatenate([jnp.where(in_head[h], y, 0.0) for h in range(nh)], axis=0))

    def blockdiag_t(x):
        xt = jnp.concatenate([x] * nh, axis=0).T
        return bf(jnp.where(same_head, xt, 0.0))

    inst = [(c, q) for c in range(tb // chunk) for q in groups]
    cls = []
    for c, q in inst:
        lw = lw_ref[0, c * chunk:(c + 1) * chunk, colsl[q]]
        hi = bf(lw)
        r1 = lw - hi.astype(F32)
        mid = bf(r1)
        lo = bf(r1 - mid.astype(F32))
        cls.append((lw, _dot(tri, hi) + _dot(tri, mid) + _dot(tri, lo)))
    opnd = []
    for (c, q), (lw, cl) in zip(inst, cls):
        rows = slice(c * chunk, (c + 1) * chunk)
        cols = colsl[q]
        k = k_ref[0, rows, cols]
        kk = kk_ref[0, rows, cols]
        b = kk * a_ref[0, rows, cols]
        v = v_ref[0, rows, cols]
        cl_end = cl[chunk - 1:chunk, :]
        p_inv = jnp.exp(-cl)
        p_tail = jnp.exp(cl_end - cl)
        ar = jnp.concatenate([bf(-kk * jnp.exp(cl - lw)), bf(r_ref[0, rows, cols] * jnp.exp(cl))],
                             axis=0)
        tail_t = bf(jnp.concatenate([b * p_tail, k * p_tail], axis=0).T)
        pl_full = jnp.broadcast_to(jnp.exp(cl_end), (gw, gw))
        pl_col = jnp.sum(jnp.where(diag, pl_full, 0.0), axis=1, keepdims=True)
        opnd.append((ar, blockdiag_t(b * p_inv), blockdiag_t(k * p_inv), bf(v), blockdiag(v),
                     tail_t, pl_col))
    g_b = [_dot(o[0], o[1]) for o in opnd]
    g_k = [_dot(o[0], o[2]) for o in opnd]
    nms = [jnp.where(strict, g[:chunk], 0.0) for g in g_b]
    tinv = [eye + nm for nm in nms]
    pw = [bf(_dot(bf(nm), blockdiag(nm))) for nm in nms]
    for _ in range(chunk.bit_length() - 3):
        both = [_dot(jnp.concatenate([bf(t), p2], axis=0), blockdiag(p2)) for t, p2 in zip(tinv, pw)]
        tinv = [t + bo[:chunk] for t, bo in zip(tinv, both)]
        pw = [bf(bo[chunk:]) for bo in both]
    tinv = [bf(t + _dot(bf(t), blockdiag(p2))) for t, p2 in zip(tinv, pw)]
    g_ak = [bf(jnp.where(strict, g[:chunk], 0.0)) for g in g_k]
    g_rb = [bf(jnp.where(incl, g[chunk:], 0.0)) for g in g_b]
    g_rk = [bf(jnp.where(incl, g[chunk:], 0.0)) for g in g_k]

    state = [m_ref[q] for q in groups]
    for c in range(tb // chunk):
        ids = [c * len(groups) + q for q in groups]
        m0_bf = [bf(state[q]) for q in groups]
        rhs = [_dot(jnp.concatenate([opnd[i][0][:chunk], g_ak[i]], axis=1),
                    jnp.concatenate([m0_bf[q], opnd[i][4]], axis=0)) for q, i in zip(groups, ids)]
        u = [_dot(tinv[i], blockdiag(rhs[q])) for q, i in zip(groups, ids)]
        for q, i in zip(groups, ids):
            y_ref[c * chunk:(c + 1) * chunk, colsl[q]] = _dot(
                jnp.concatenate([opnd[i][0][chunk:], g_rb[i], g_rk[i]], axis=1),
                jnp.concatenate([m0_bf[q], blockdiag(u[q]), opnd[i][4]], axis=0))
        for q, i in zip(groups, ids):
            upd = _dot(opnd[i][5], jnp.concatenate([bf(u[q]), opnd[i][3]], axis=0))
            state[q] = opnd[i][6] * state[q] + jnp.where(same_head, upd, 0.0)
    for q in groups:
        m_ref[q] = state[q]

    npair = width // LANES
    for p in range(npair):
        cols = slice(p * LANES, (p + 1) * LANES)
        y = y_ref[:, cols]
        mean = _dot_x2(y, bd)
        dlt = y - mean
        var = _dot(bf(dlt * dlt), bd)
        yn = dlt * lax.rsqrt(var + GN_EPS) * lng_ref[:, cols] + lnb_ref[:, cols]
        rk_sum = _dot(bf(r_ref[0, :, cols] * k_ref[0, :, cols] * rk_ref[:, cols]), bd) * float(hd)
        out = (yn + rk_sum * v_ref[0, :, cols]) * g_ref[0, :, cols]
        o_ref[0, :, cols] = out.astype(o_ref.dtype)


def _rwkv_recurrence(r, lw, k, v, kk, a, g, r_k, ln_g, ln_b, tb, chunk, wblk):
    bsz, seq, d = r.shape
    gw = 4 * RWKV_HEAD
    assert chunk == RWKV_HEAD and wblk % gw == 0
    tok = pl.BlockSpec((1, tb, wblk), lambda b, j, t: (b, t, j))
    vec = pl.BlockSpec((1, wblk), lambda b, j, t: (0, j))
    tri = jnp.tril(jnp.ones((chunk, chunk), F32)).astype(BF16)
    bd = _group_mean_matrix(LANES, RWKV_HEAD)
    return pl.pallas_call(
        functools.partial(_rwkv_rec_kernel, chunk=chunk),
        grid=(bsz, d // wblk, seq // tb),
        in_specs=[tok] * 7 + [vec] * 3 + [_const_spec(tri.shape), _const_spec(bd.shape)],
        out_specs=tok,
        out_shape=jax.ShapeDtypeStruct((bsz, seq, d), BF16),
        scratch_shapes=[pltpu.VMEM((wblk // gw, gw, gw), F32),
                        pltpu.VMEM((tb, wblk), F32)],
        compiler_params=_cparams("arbitrary", "arbitrary", "arbitrary"),
        name="rwkv_recurrence",
    )(r, lw, k, v, kk, a, g, r_k.reshape(1, d), ln_g.reshape(1, d), ln_b.reshape(1, d), tri, bd)


def _odd_layer(x, mod, norm_mix, norm_ffn, w_gate, w_up, w_down, mu, w_r, w_k, w_v, w_o, w0, w1,
               w2, a0, a1, a2, g1, g2, k_k, k_a, r_k, ln_g, ln_b):
    bsz, seq, d = x.shape
    r, lw, k, v, kk, a, g = _rwkv_in(x, mod, norm_mix, mu, w_r, w_k, w_v, w0, w1, w2, a0, a1, a2,
                                     g1, g2, k_k, k_a, _pick_tile(seq, 256))
    yg = _rwkv_recurrence(r, lw, k, v, kk, a, g, r_k, ln_g, ln_b, _pick_tile(seq, 256), 64,
                          d)
    tm = _pick_tile(seq, 512)
    acts = [(yg, pl.BlockSpec((1, tm, d), lambda i, b: (b, i, 0)))]
    return _mix_ffn(x, mod, norm_ffn, acts, [w_o.astype(BF16)], w_gate, w_up, w_down, tm, 256)


def _pick_tile(n, pref):
    t = min(n, pref)
    assert n % t == 0, (n, t)
    return t


def _even_layer(x, mod, tables, lam_init, norm_mix, norm_ffn, w_gate, w_up, w_down, w_in,
                lam_re, lam_im, log_dt, b_re, b_im, c_re, c_im, d_skip, w_glu, q_norm, k_norm,
                lq1, lk1, lq2, lk2, subln, w_out):
    bsz, seq, d = x.shape
    s5w = lam_re.shape[0] * b_re.shape[-1]
    dfw = (w_in.shape[1] - s5w) // 3
    tm = _pick_tile(seq, 512)
    u2, q, k, v = _even_in(x, mod, norm_mix, w_in, q_norm, k_norm, tables, s5w, dfw, tm)
    ys = _s5_mixer(u2.reshape(seq * bsz, s5w), bsz, lam_re, lam_im, log_dt, b_re, b_im,
                   c_re, c_im, d_skip.reshape(-1), w_glu, _pick_tile(seq, 128))
    att = _diff_attention(q, k, v, lq1, lk1, lq2, lk2, subln, lam_init, _pick_tile(seq, 512))
    w_out_bf = w_out.astype(BF16)
    acts = [
        (ys.reshape(seq, bsz * s5w), pl.BlockSpec((tm, s5w), lambda i, b: (i, b))),
        (att, pl.BlockSpec((1, tm, dfw), lambda i, b: (b, i, 0))),
    ]
    return _mix_ffn(x, mod, norm_ffn, acts, [w_out_bf[:s5w], w_out_bf[s5w:]],
                    w_gate, w_up, w_down, tm, 256)


def kernel(x, c, positions, w_ada, b_ada, norm_mix, norm_ffn, ffn_w_gate, ffn_w_up, ffn_w_down,
           ev_w_in, ev_s5_lam_re, ev_s5_lam_im, ev_s5_log_dt, ev_s5_b_re, ev_s5_b_im, ev_s5_c_re,
           ev_s5_c_im, ev_s5_d, ev_s5_w_glu, ev_q_norm, ev_k_norm, ev_lambda_q1, ev_lambda_k1,
           ev_lambda_q2, ev_lambda_k2, ev_subln, ev_w_out, od_mu, od_w_r, od_w_k, od_w_v, od_w_o,
           od_w0, od_w1, od_w2, od_a0, od_a1, od_a2, od_g1, od_g2, od_k_k, od_k_a, od_r_k,
           od_ln_g, od_ln_b):
    depth = w_ada.shape[0]
    mod = _ada_mod(c, w_ada, b_ada)
    tables = _rope_tables(positions)
    for l in range(depth):
        if l % 2 == 0:
            e = l // 2
            lam_init = 0.8 - 0.6 * math.exp(-0.3 * l)
            x = _even_layer(x, mod[l], tables, lam_init, norm_mix[l], norm_ffn[l], ffn_w_gate[l],
                            ffn_w_up[l], ffn_w_down[l], ev_w_in[e], ev_s5_lam_re[e],
                            ev_s5_lam_im[e], ev_s5_log_dt[e], ev_s5_b_re[e], ev_s5_b_im[e],
                            ev_s5_c_re[e], ev_s5_c_im[e], ev_s5_d[e], ev_s5_w_glu[e], ev_q_norm[e],
                            ev_k_norm[e], ev_lambda_q1[e], ev_lambda_k1[e], ev_lambda_q2[e],
                            ev_lambda_k2[e], ev_subln[e], ev_w_out[e])
        else:
            o = l // 2
            x = _odd_layer(x, mod[l], norm_mix[l], norm_ffn[l], ffn_w_gate[l], ffn_w_up[l],
                           ffn_w_down[l], od_mu[o], od_w_r[o], od_w_k[o], od_w_v[o], od_w_o[o],
                           od_w0[o], od_w1[o], od_w2[o], od_a0[o], od_a1[o], od_a2[o], od_g1[o],
                           od_g2[o], od_k_k[o], od_k_a[o], od_r_k[o], od_ln_g[o], od_ln_b[o])
    return x
```

```python
import functools
import math

import jax
import jax.numpy as jnp
from jax import lax
from jax.experimental import pallas as pl
from jax.experimental.pallas import tpu as pltpu

F32 = jnp.float32
BF16 = jnp.bfloat16

RMS_EPS = 1e-6
GN_EPS = 64e-5
ROPE_THETA = 500000.0
DT_HEAD = 64
ROT_DIM = DT_HEAD // 4
S5_GROUP = 16
S5_STATE = 64
RWKV_HEAD = 64
LANES = 128
SUBLANES = 8
S5_MXU_DEPTH = 256
VMEM_LIMIT = 56 * 1024 * 1024


def _cparams(*sem):
    return pltpu.CompilerParams(dimension_semantics=sem, vmem_limit_bytes=VMEM_LIMIT)


def _const_spec(shape):
    nd = len(shape)
    return pl.BlockSpec(shape, lambda *_: (0,) * nd, pipeline_mode=pl.Buffered(1))


def _dot(a, b):
    return jnp.dot(a, b, preferred_element_type=F32)


def _dot_nt(a, b):
    return lax.dot_general(a, b, (((1,), (1,)), ((), ())), preferred_element_type=F32)


def _dot_tn(a, b):
    return lax.dot_general(a, b, (((0,), (0,)), ((), ())), preferred_element_type=F32)


def _split_bf16(x):
    hi = x.astype(BF16)
    lo = (x - hi.astype(F32)).astype(BF16)
    return hi, lo


def _dot_x2(x, w_bf16):
    hi, lo = _split_bf16(x)
    return _dot(hi, w_bf16) + _dot(lo, w_bf16)


def _modnorm(x, g, scale, shift):
    ms = jnp.mean(x * x, axis=-1, keepdims=True)
    return (x * lax.rsqrt(ms + RMS_EPS)) * g * (1.0 + scale) + shift


def _ada_kernel(c_ref, w_ref, b_ref, o_ref):
    c = c_ref[...]
    ca = c * jax.nn.sigmoid(c)
    hi, lo = _split_bf16(ca)
    w = w_ref[0]
    w_hi, w_lo = _split_bf16(w)
    o_ref[0] = _dot(hi, w_hi) + _dot(lo, w_hi) + _dot(hi, w_lo) + b_ref[0]


def _ada_mod(c, w_ada, b_ada):
    depth, d, n = w_ada.shape
    bsz = c.shape[0]
    rows = -(-bsz // SUBLANES) * SUBLANES
    c_pad = jnp.zeros((rows, d), F32).at[:bsz].set(c)
    tn = n // 4
    out = pl.pallas_call(
        _ada_kernel,
        grid=(depth, n // tn),
        in_specs=[
            pl.BlockSpec((rows, d), lambda l, j: (0, 0)),
            pl.BlockSpec((1, d, tn), lambda l, j: (l, 0, j)),
            pl.BlockSpec((1, 1, tn), lambda l, j: (l, 0, j)),
        ],
        out_specs=pl.BlockSpec((1, rows, tn), lambda l, j: (l, 0, j)),
        out_shape=jax.ShapeDtypeStruct((depth, rows, n), F32),
        compiler_params=_cparams("arbitrary", "arbitrary"),
        name="ada_mod",
    )(c_pad, w_ada, b_ada.reshape(depth, 1, n))
    mod = out[:, :bsz].reshape(depth, bsz, 6, d)
    return jnp.pad(mod, ((0, 0), (0, 0), (0, SUBLANES - 6), (0, 0)))


def _mul_trig_kernel(p_ref, f_ref, cos_ref, sin_ref):
    a = p_ref[...] * f_ref[...]
    cos_ref[...] = jnp.cos(a)
    sin_ref[...] = jnp.sin(a)


def _rope_tables(positions):
    bsz, seq = positions.shape
    half = ROT_DIM // 2
    inv_freq = ROPE_THETA ** (-jnp.arange(0, ROT_DIM, 2, dtype=F32) / ROT_DIM)
    n = bsz * seq * half
    pos_rep = jnp.broadcast_to(positions.astype(F32)[..., None], (bsz, seq, half))
    frq_rep = jnp.broadcast_to(inv_freq, (bsz, seq, half))
    rows = n // LANES
    tr = min(rows, 512)
    cos, sin = pl.pallas_call(
        _mul_trig_kernel,
        grid=(rows // tr,),
        in_specs=[pl.BlockSpec((tr, LANES), lambda i: (i, 0))] * 2,
        out_specs=[pl.BlockSpec((tr, LANES), lambda i: (i, 0))] * 2,
        out_shape=[jax.ShapeDtypeStruct((rows, LANES), F32)] * 2,
        compiler_params=_cparams("arbitrary"),
        name="rope_trig",
    )(pos_rep.reshape(rows, LANES), frq_rep.reshape(rows, LANES))
    cs = jnp.concatenate([cos.reshape(bsz, seq, half), sin.reshape(bsz, seq, half)], axis=-1)
    return jnp.pad(cs, ((0, 0), (0, 0), (0, LANES - ROT_DIM)))


def _expand_rope(cs):
    half = ROT_DIM // 2
    lane = lax.broadcasted_iota(jnp.int32, cs.shape, 1)
    c0 = jnp.where(lane < half, cs, jnp.where(lane < ROT_DIM, pltpu.roll(cs, half, axis=1), 1.0))
    sa0 = jnp.where(lane < half, -pltpu.roll(cs, LANES - half, axis=1), 0.0)
    sb0 = jnp.where((lane >= half) & (lane < ROT_DIM), cs, 0.0)
    second = lane >= DT_HEAD
    return (jnp.where(second, pltpu.roll(c0, DT_HEAD, axis=1), c0),
            jnp.where(second, pltpu.roll(sa0, DT_HEAD, axis=1), sa0),
            jnp.where(second, pltpu.roll(sb0, DT_HEAD, axis=1), sb0))


def _group_mean_matrix(width, group):
    idx = jnp.arange(width) // group
    return jnp.where(idx[:, None] == idx[None, :], 1.0 / group, 0.0).astype(BF16)


def _even_in_kernel(x_ref, mod_ref, g_ref, w_ref, qn_ref, kn_ref, cs_ref, bd_ref,
                    u_ref, q_ref, k_ref, v_ref, *, s5w, dfw):
    x = x_ref[0]
    h = _modnorm(x, g_ref[...], mod_ref[0, 1:2, :], mod_ref[0, 0:1, :])
    proj = _dot(h.astype(BF16), w_ref[...])
    u_ref[...] = proj[:, :s5w]
    cosv, sav, sbv = _expand_rope(cs_ref[0])
    half = ROT_DIM // 2

    def norm_rope(t, gn, out_ref, post_scale):
        ms = _dot((t * t).astype(BF16), bd_ref[...])
        t = t * lax.rsqrt(ms + RMS_EPS) * gn
        for j in range(dfw // LANES):
            tj = t[:, j * LANES:(j + 1) * LANES]
            up = pltpu.roll(tj, LANES - half, axis=1)
            dn = pltpu.roll(tj, half, axis=1)
            rj = tj * cosv + up * sav + dn * sbv
            out_ref[0, :, j * LANES:(j + 1) * LANES] = (rj * post_scale).astype(out_ref.dtype)

    norm_rope(proj[:, s5w:s5w + dfw], qn_ref[...], q_ref, DT_HEAD ** -0.5 * math.log2(math.e))
    norm_rope(proj[:, s5w + dfw:s5w + 2 * dfw], kn_ref[...], k_ref, 1.0)
    v_ref[0] = proj[:, s5w + 2 * dfw:].astype(v_ref.dtype)


def _even_in(x, mod, g, w_in, q_norm, k_norm, tables, s5w, dfw, tm):
    bsz, seq, d = x.shape
    ncol = w_in.shape[1]
    rep = dfw // DT_HEAD
    kern = functools.partial(_even_in_kernel, s5w=s5w, dfw=dfw)
    tok = lambda w: pl.BlockSpec((1, tm, w), lambda i, b: (b, i, 0))
    return pl.pallas_call(
        kern,
        grid=(seq // tm, bsz),
        in_specs=[
            tok(d),
            pl.BlockSpec((1, SUBLANES, d), lambda i, b: (b, 0, 0)),
            _const_spec((1, d)),
            _const_spec((d, ncol)),
            _const_spec((1, dfw)),
            _const_spec((1, dfw)),
            tok(LANES),
            _const_spec((dfw, dfw)),
        ],
        out_specs=[
            pl.BlockSpec((tm, s5w), lambda i, b: (i, b)),
            tok(dfw), tok(dfw), tok(dfw),
        ],
        out_shape=[
            jax.ShapeDtypeStruct((seq, bsz * s5w), F32),
            jax.ShapeDtypeStruct((bsz, seq, dfw), BF16),
            jax.ShapeDtypeStruct((bsz, seq, dfw), BF16),
            jax.ShapeDtypeStruct((bsz, seq, dfw), BF16),
        ],
        compiler_params=_cparams("arbitrary", "arbitrary"),
        name="even_in_proj",
    )(x, mod, g.reshape(1, d), w_in.astype(BF16),
      jnp.tile(q_norm, rep).reshape(1, dfw), jnp.tile(k_norm, rep).reshape(1, dfw),
      tables, _group_mean_matrix(dfw, DT_HEAD))


def _s5_disc_kernel(lr_ref, li_ref, ldt_ref, br_ref, bi_ref, ar_ref, ai_ref, bbr_ref, bbi_ref,
                    *, bsz):
    lr = lr_ref[...]
    li = li_ref[...]
    dt = jnp.exp(ldt_ref[...])
    mag = jnp.exp(lr * dt)
    abar_r = mag * jnp.cos(li * dt)
    abar_i = mag * jnp.sin(li * dt)
    den = lr * lr + li * li
    num_r = abar_r - 1.0
    q_r = (num_r * lr + abar_i * li) / den
    q_i = (abar_i * lr - num_r * li) / den
    br = br_ref[...]
    bi = bi_ref[...]
    bbr_ref[...] = q_r * br - q_i * bi
    bbi_ref[...] = q_r * bi + q_i * br
    pr, pi = abar_r, abar_i
    for j in range(SUBLANES):
        if j and j % bsz == 0:
            pr, pi = pr * abar_r - pi * abar_i, pr * abar_i + pi * abar_r
        ar_ref[:, j:j + 1] = pr
        ai_ref[:, j:j + 1] = pi


def _s5_discretise(lam_re, lam_im, log_dt, b_re, b_im, bsz):
    g, p = lam_re.shape
    hh = b_re.shape[-1]
    n = g * p
    tab = jax.ShapeDtypeStruct((n, SUBLANES), F32)
    mat = jax.ShapeDtypeStruct((n, hh), F32)
    ldt = jnp.broadcast_to(log_dt[:, None], (g, p)).reshape(n, 1)
    return pl.pallas_call(
        functools.partial(_s5_disc_kernel, bsz=bsz),
        out_shape=[tab, tab, mat, mat],
        name="s5_discretise",
    )(lam_re.reshape(n, 1), lam_im.reshape(n, 1), ldt, b_re.reshape(n, hh), b_im.reshape(n, hh))


def _s5_scan_kernel(u_ref, bre_ref, bim_ref, cre_ref, cim_ref, dsk_ref, a1r_ref, a1i_ref,
                    pwr_ref, pwi_ref, wglu_ref, o_ref, xr_ref, xi_ref, cr_ref, ci_ref,
                    *, bsz, lane_chunk):
    rows, nstate = xr_ref.shape
    step = pl.program_id(0)

    @pl.when(step == 0)
    def _():
        cr_ref[...] = jnp.zeros_like(cr_ref)
        ci_ref[...] = jnp.zeros_like(ci_ref)

    nblk, wpart, spart = bre_ref.shape
    u = u_ref[...]
    u_bf = u.astype(BF16)
    for j in range(nblk):
        uj = u_bf[:, j * wpart:(j + 1) * wpart]
        xr_ref[:, j * spart:(j + 1) * spart] = _dot(uj, bre_ref[j])
        xi_ref[:, j * spart:(j + 1) * spart] = _dot(uj, bim_ref[j])

    row = lax.broadcasted_iota(jnp.int32, (SUBLANES, lane_chunk), 0)
    steps_per_tile = SUBLANES // bsz
    ntiles = rows // SUBLANES

    for c in range(nstate // lane_chunk):
        cols = pl.ds(c * lane_chunk, lane_chunk)
        a1r = a1r_ref[:, cols]
        a1i = a1i_ref[:, cols]
        pwr = pwr_ref[:, cols]
        pwi = pwi_ref[:, cols]

        def tile_body(i, carry):
            pr, pi = carry
            base = pl.multiple_of(i * SUBLANES, SUBLANES)
            zr = xr_ref[pl.ds(base, SUBLANES), cols]
            zi = xi_ref[pl.ds(base, SUBLANES), cols]
            sh = bsz
            apr, api = a1r, a1i
            for _ in range(steps_per_tile.bit_length() - 1):
                sr = jnp.where(row >= sh, pltpu.roll(zr, sh, axis=0), 0.0)
                si = jnp.where(row >= sh, pltpu.roll(zi, sh, axis=0), 0.0)
                zr, zi = zr + apr * sr - api * si, zi + apr * si + api * sr
                apr, api = apr * apr - api * api, 2.0 * apr * api
                sh *= 2
            last = SUBLANES - bsz
            br_, bi_ = pr, pi
            sh = bsz
            while sh < SUBLANES:
                br_ = jnp.where(row >= last, br_, pltpu.roll(br_, SUBLANES - sh, axis=0))
                bi_ = jnp.where(row >= last, bi_, pltpu.roll(bi_, SUBLANES - sh, axis=0))
                last -= sh
                sh *= 2
            xr = zr + pwr * br_ - pwi * bi_
            xi = zi + pwr * bi_ + pwi * br_
            xr_ref[pl.ds(base, SUBLANES), cols] = xr
            xi_ref[pl.ds(base, SUBLANES), cols] = xi
            return xr, xi

        fr, fi = lax.fori_loop(0, ntiles, tile_body, (cr_ref[:, cols], ci_ref[:, cols]))
        cr_ref[:, cols] = fr
        ci_ref[:, cols] = fi

    y = jnp.concatenate(
        [_dot(xr_ref[:, j * spart:(j + 1) * spart].astype(BF16), cre_ref[j])
         - _dot(xi_ref[:, j * spart:(j + 1) * spart].astype(BF16), cim_ref[j])
         for j in range(nblk)], axis=1) + dsk_ref[...] * u
    y = jax.nn.gelu(y)
    gate = jax.nn.sigmoid(_dot(y.astype(BF16), wglu_ref[...]))
    o_ref[...] = (y * gate).astype(o_ref.dtype)


def _s5_mixer(u2, bsz, lam_re, lam_im, log_dt, b_re, b_im, c_re, c_im, d_skip, w_glu, tchunk):
    total_rows, width = u2.shape
    g, p = lam_re.shape
    hh = b_re.shape[-1]
    nstate = g * p
    pw_r, pw_i, bbar_r, bbar_i = _s5_discretise(lam_re, lam_im, log_dt, b_re, b_im, bsz)
    gb = max(1, min(g, S5_MXU_DEPTH // hh))
    nblk = g // gb
    eye = jnp.eye(gb, dtype=F32)
    def in_map(bb):
        return jnp.einsum('jgph,gk->jghkp', bb.reshape(nblk, gb, p, hh), eye).reshape(
            nblk, gb * hh, gb * p)
    def out_map(cc):
        return jnp.einsum('jghp,gk->jgpkh', cc.reshape(nblk, gb, hh, p), eye).reshape(
            nblk, gb * p, gb * hh)
    pw_r = pw_r.T
    pw_i = pw_i.T
    a1r = jnp.broadcast_to(pw_r[0:1], (SUBLANES, nstate))
    a1i = jnp.broadcast_to(pw_i[0:1], (SUBLANES, nstate))
    rows = tchunk * bsz
    kern = functools.partial(_s5_scan_kernel, bsz=bsz, lane_chunk=512)
    return pl.pallas_call(
        kern,
        grid=(total_rows // rows,),
        in_specs=[
            pl.BlockSpec((rows, width), lambda i: (i, 0)),
            _const_spec((nblk, gb * hh, gb * p)), _const_spec((nblk, gb * hh, gb * p)),
            _const_spec((nblk, gb * p, gb * hh)), _const_spec((nblk, gb * p, gb * hh)),
            _const_spec((1, width)),
            _const_spec((SUBLANES, nstate)), _const_spec((SUBLANES, nstate)),
            _const_spec((SUBLANES, nstate)), _const_spec((SUBLANES, nstate)),
            _const_spec((width, width)),
        ],
        out_specs=pl.BlockSpec((rows, width), lambda i: (i, 0)),
        out_shape=jax.ShapeDtypeStruct((total_rows, width), BF16),
        scratch_shapes=[
            pltpu.VMEM((rows, nstate), F32), pltpu.VMEM((rows, nstate), F32),
            pltpu.VMEM((SUBLANES, nstate), F32), pltpu.VMEM((SUBLANES, nstate), F32),
        ],
        compiler_params=_cparams("arbitrary"),
        name="s5_scan_glu",
    )(u2, in_map(bbar_r).astype(BF16), in_map(bbar_i).astype(BF16),
      out_map(c_re).astype(BF16), out_map(c_im).astype(BF16),
      d_skip.reshape(1, width), a1r, a1i, pw_r, pw_i, w_glu.astype(BF16))


def _diff_attn_kernel(lam_ref, sub_ref, q_ref, k_ref, v_ref, o_ref, q2_ref, m_ref, acc_ref,
                      *, tq, lam_init):
    qi = pl.program_id(2)
    vdim = v_ref.shape[-1]
    q = q_ref[0]
    lane = lax.broadcasted_iota(jnp.int32, q.shape, 1)
    zero = jnp.zeros_like(q)
    q2_ref[:tq, :] = jnp.where(lane < DT_HEAD, q, zero)
    q2_ref[tq:, :] = jnp.where(lane >= DT_HEAD, q, zero)
    m_ref[...] = jnp.full(m_ref.shape, -jnp.inf, F32)
    acc_ref[...] = jnp.zeros_like(acc_ref)
    all_rows = ((0, 2 * tq),)

    def gather(ref, row_slices):
        parts = [ref[a:b, :] for a, b in row_slices]
        return parts[0] if len(parts) == 1 else jnp.concatenate(parts, axis=0)

    def scores(kstart, ksize, row_slices=all_rows):
        start = pl.multiple_of(kstart, ksize)
        return _dot_nt(gather(q2_ref, row_slices), k_ref[0, pl.ds(start, ksize), :])

    def update(kstart, ksize, s, row_slices=all_rows, mask=None):
        start = pl.multiple_of(kstart, ksize)
        v_ext = jnp.concatenate([v_ref[0, pl.ds(start, ksize), :], jnp.ones((ksize, vdim), BF16)],
                                axis=1)
        if mask is not None:
            s = jnp.where(mask, s, -jnp.inf)
        part = s[:, :LANES]
        for t in range(1, ksize // LANES):
            part = jnp.maximum(part, s[:, t * LANES:(t + 1) * LANES])
        m_old = gather(m_ref, row_slices)
        m_new = jnp.maximum(m_old, jnp.max(part, axis=-1, keepdims=True))
        alpha = jnp.exp2(m_old - m_new)
        p = jnp.concatenate([jnp.exp2(s[:, t * LANES:(t + 1) * LANES] - m_new)
                             for t in range(ksize // LANES)], axis=1)
        acc = (jnp.concatenate([alpha] * (2 * vdim // LANES), axis=1) * gather(acc_ref, row_slices)
               + _dot(p.astype(BF16), v_ext))
        off = 0
        for a, b in row_slices:
            m_ref[a:b, :] = m_new[off:off + b - a]
            acc_ref[a:b, :] = acc[off:off + b - a]
            off += b - a

    def diagonal(kstart, with_previous):
        hq = tq // 2
        if with_previous:
            s_prev = scores(kstart - tq, tq)
        row_l = lax.broadcasted_iota(jnp.int32, (2 * tq, hq), 0)
        col_l = lax.broadcasted_iota(jnp.int32, (2 * tq, hq), 1)
        row_r = lax.broadcasted_iota(jnp.int32, (tq, hq), 0)
        col_r = lax.broadcasted_iota(jnp.int32, (tq, hq), 1)
        late_rows = ((hq, tq), (tq + hq, 2 * tq))
        s_left = scores(kstart, hq)
        s_right = scores(kstart + hq, hq, late_rows)
        if with_previous:
            update(kstart - tq, tq, s_prev)
        update(kstart, hq, s_left, mask=col_l <= row_l % tq)
        update(kstart + hq, hq, s_right, late_rows, col_r <= row_r % hq)

    def pair(j0):
        s0 = scores(j0 * tq, tq)
        s1 = scores((j0 + 1) * tq, tq)
        update(j0 * tq, tq, s0)
        update((j0 + 1) * tq, tq, s1)

    npairs = qi // 2
    lax.fori_loop(0, npairs, lambda jj, _: (pair(2 * jj), 0)[1], 0)

    @pl.when(qi % 2 == 1)
    def _():
        diagonal(qi * tq, True)

    @pl.when(qi % 2 == 0)
    def _():
        diagonal(qi * tq, False)

    lv = lam_ref[...]
    lam = (jnp.exp(jnp.sum(lv[0:1] * lv[1:2], axis=-1, keepdims=True))
           - jnp.exp(jnp.sum(lv[2:3] * lv[3:4], axis=-1, keepdims=True)) + lam_init)
    acc = acc_ref[...]
    o = (acc[:tq, :vdim] / acc[:tq, vdim:]) - lam * (acc[tq:, :vdim] / acc[tq:, vdim:])
    ms = jnp.mean(o * o, axis=-1, keepdims=True)
    o = o * lax.rsqrt(ms + RMS_EPS) * sub_ref[...] * (1.0 - lam_init)
    o_ref[0] = o.astype(o_ref.dtype)


def _diff_attention(q, k, v, lq1, lk1, lq2, lk2, subln, lam_init, tq):
    bsz, seq, dfw = q.shape
    vdim = 2 * DT_HEAD
    heads = dfw // vdim
    lamv = jnp.zeros((SUBLANES, LANES), F32)
    for i, t in enumerate((lq1, lk1, lq2, lk2)):
        lamv = lamv.at[i, :t.shape[0]].set(t)
    kern = functools.partial(_diff_attn_kernel, tq=tq, lam_init=lam_init)
    kv_spec = pl.BlockSpec((1, seq, vdim), lambda b, h, i: (b, 0, h))
    return pl.pallas_call(
        kern,
        grid=(bsz, heads, seq // tq),
        in_specs=[
            _const_spec((SUBLANES, LANES)),
            _const_spec((1, vdim)),
            pl.BlockSpec((1, tq, vdim), lambda b, h, i: (b, i, h)),
            kv_spec, kv_spec,
        ],
        out_specs=pl.BlockSpec((1, tq, vdim), lambda b, h, i: (b, i, h)),
        out_shape=jax.ShapeDtypeStruct((bsz, seq, dfw), BF16),
        scratch_shapes=[pltpu.VMEM((2 * tq, vdim), BF16), pltpu.VMEM((2 * tq, LANES), F32),
                        pltpu.VMEM((2 * tq, 2 * vdim), F32)],
        compiler_params=_cparams("arbitrary", "arbitrary", "arbitrary"),
        name="diff_attention",
    )(lamv, subln.reshape(1, vdim), q, k, v)


def _mix_ffn_kernel(*refs, n_in, hid_chunk):
    x_ref, mod_ref, g_ref = refs[:3]
    act_refs = refs[3:3 + n_in]
    w_refs = refs[3 + n_in:3 + 2 * n_in]
    wg_ref, wu_ref, wd_ref, o_ref = refs[3 + 2 * n_in:]
    mix = None
    for a_ref, w_ref in zip(act_refs, w_refs):
        a = a_ref[0] if len(a_ref.shape) == 3 else a_ref[...]
        t = _dot(a, w_ref[...])
        mix = t if mix is None else mix + t
    x1 = x_ref[0] + mod_ref[0, 2:3, :] * mix
    h = _modnorm(x1, g_ref[...], mod_ref[0, 4:5, :], mod_ref[0, 3:4, :]).astype(BF16)
    acc = None
    for j in range(wg_ref.shape[1] // hid_chunk):
        sl = slice(j * hid_chunk, (j + 1) * hid_chunk)
        gate = _dot(h, wg_ref[:, sl])
        up = _dot(h, wu_ref[:, sl])
        act = (gate * jax.nn.sigmoid(gate) * up).astype(BF16)
        t = _dot(act, wd_ref[sl, :])
        acc = t if acc is None else acc + t
    o_ref[0] = x1 + mod_ref[0, 5:6, :] * acc


def _mix_ffn(x, mod, g, acts, weights, w_gate, w_up, w_down, tm, hid_chunk):
    bsz, seq, d = x.shape
    assert w_gate.shape[1] % hid_chunk == 0
    wg = w_gate.astype(BF16)
    wu = w_up.astype(BF16)
    wd = w_down.astype(BF16)
    tok = pl.BlockSpec((1, tm, d), lambda i, b: (b, i, 0))
    return pl.pallas_call(
        functools.partial(_mix_ffn_kernel, n_in=len(acts), hid_chunk=hid_chunk),
        grid=(seq // tm, bsz),
        in_specs=[tok, pl.BlockSpec((1, SUBLANES, d), lambda i, b: (b, 0, 0)), _const_spec((1, d))]
                 + [spec for _, spec in acts]
                 + [_const_spec(w.shape) for w in weights]
                 + [_const_spec(wg.shape), _const_spec(wu.shape), _const_spec(wd.shape)],
        out_specs=tok,
        out_shape=jax.ShapeDtypeStruct((bsz, seq, d), F32),
        compiler_params=_cparams("arbitrary", "arbitrary"),
        name="mix_ffn",
    )(x, mod, g.reshape(1, d), *[a for a, _ in acts], *weights, wg, wu, wd)


def _rwkv_in_kernel(x_ref, xp_ref, mod_ref, g_ref, mu_ref, wr_ref, wk_ref, wv_ref, w1_ref, w2_ref,
                    a1_ref, a2_ref, g1_ref, g2_ref, w0_ref, a0_ref, kkw_ref, kaw_ref, bd_ref,
                    r_out, lw_out, k_out, v_out, kk_out, a_out, g_out):
    i = pl.program_id(1)
    g = g_ref[...]
    scale = mod_ref[0, 1:2, :]
    shift = mod_ref[0, 0:1, :]
    h = _modnorm(x_ref[0], g, scale, shift)
    hp = _modnorm(xp_ref[0][SUBLANES - 1:SUBLANES, :], g, scale, shift)
    hp = jnp.where(i == 0, 0.0, hp)
    row = lax.broadcasted_iota(jnp.int32, h.shape, 0)
    h_prev = jnp.where(row == 0, hp, pltpu.roll(h, 1, axis=0))
    dx = h_prev - h

    def lerp(j):
        return (h + dx * mu_ref[j:j + 1, :]).astype(BF16)

    r_out[0] = _dot(lerp(0), wr_ref[...])
    wl = jnp.tanh(_dot(lerp(1), w1_ref[...]))
    wdec = w0_ref[...] + _dot(wl.astype(BF16), w2_ref[...])
    w = -jax.nn.softplus(-wdec) - 0.5
    lw_out[0] = -jnp.exp(w)
    k = _dot(lerp(2), wk_ref[...])
    v_out[0] = _dot(lerp(3), wv_ref[...])
    al = _dot(lerp(4), a1_ref[...])
    a = jax.nn.sigmoid(a0_ref[...] + _dot(al.astype(BF16), a2_ref[...]))
    a_out[0] = a
    gl = jax.nn.sigmoid(_dot(lerp(5), g1_ref[...]))
    g_out[0] = _dot(gl.astype(BF16), g2_ref[...])
    kk = k * kkw_ref[...]
    bd = bd_ref[...]
    for j in range(kk.shape[1] // LANES):
        kj = kk[:, j * LANES:(j + 1) * LANES]
        ss = _dot_x2(kj * kj, bd) * float(RWKV_HEAD)
        kk_out[0, :, j * LANES:(j + 1) * LANES] = kj / jnp.maximum(jnp.sqrt(ss), 1e-12)
    k_out[0] = k * (1.0 + (a - 1.0) * kaw_ref[...])


def _rwkv_in(x, mod, g, mu, w_r, w_k, w_v, w0, w1, w2, a0, a1, a2, g1, g2, k_k, k_a, tm):
    bsz, seq, d = x.shape
    tok = pl.BlockSpec((1, tm, d), lambda b, i: (b, i, 0))
    prev = pl.BlockSpec((1, SUBLANES, d),
                        lambda b, i: (b, jnp.maximum(i * (tm // SUBLANES) - 1, 0), 0))
    bf = lambda w: w.astype(BF16)
    vec = lambda t: t.reshape(1, d)
    consts = [vec(g), mu, bf(w_r), bf(w_k), bf(w_v), bf(w1), bf(w2), bf(a1), bf(a2), bf(g1), bf(g2),
              vec(w0), vec(a0), vec(k_k), vec(k_a), _group_mean_matrix(LANES, RWKV_HEAD)]
    out = jax.ShapeDtypeStruct((bsz, seq, d), F32)
    return pl.pallas_call(
        _rwkv_in_kernel,
        grid=(bsz, seq // tm),
        in_specs=[tok, prev, pl.BlockSpec((1, SUBLANES, d), lambda b, i: (b, 0, 0))]
                 + [_const_spec(t.shape) for t in consts],
        out_specs=[tok] * 7,
        out_shape=[out] * 7,
        compiler_params=_cparams("arbitrary", "arbitrary"),
        name="rwkv_in_proj",
    )(x, x, mod, *consts)


def _rwkv_rec_kernel(r_ref, lw_ref, k_ref, v_ref, kk_ref, a_ref, g_ref, rk_ref, lng_ref, lnb_ref,
                     tri_ref, bd_ref, o_ref, m_ref, y_ref, *, chunk):
    tstep = pl.program_id(2)

    @pl.when(tstep == 0)
    def _():
        m_ref[...] = jnp.zeros_like(m_ref)

    tb, width = r_ref.shape[1:]
    hd = RWKV_HEAD
    gw = m_ref.shape[-1]
    nh = gw // hd
    groups = range(width // gw)
    colsl = [slice(q * gw, (q + 1) * gw) for q in groups]
    lane = lax.broadcasted_iota(jnp.int32, (chunk, gw), 1)
    trow = lax.broadcasted_iota(jnp.int32, (chunk, gw), 0)
    jpos = lane % hd
    strict = jpos < trow
    incl = jpos <= trow
    eye = jnp.where(jpos == trow, 1.0, 0.0)
    head_of_lane = lane // hd
    in_head = [head_of_lane == h for h in range(nh)]
    sq_r = lax.broadcasted_iota(jnp.int32, (gw, gw), 0)
    sq_c = lax.broadcasted_iota(jnp.int32, (gw, gw), 1)
    same_head = (sq_r // hd) == (sq_c // hd)
    diag = sq_r == sq_c
    tri = tri_ref[...]
    bd = bd_ref[...]
    bf = lambda t: t.astype(BF16)

    def blockdiag(y):
        return bf(jnp.concatenate([jnp.where(in_head[h], y, 0.0) for h in range(nh)], axis=0))

    def blockdiag_t(x):
        xt = jnp.concatenate([x] * nh, axis=0).T
        return bf(jnp.where(same_head, xt, 0.0))

    inst = [(c, q) for c in range(tb // chunk) for q in groups]
    cls = []
    for c, q in inst:
        lw = lw_ref[0, c * chunk:(c + 1) * chunk, colsl[q]]
        hi = bf(lw)
        r1 = lw - hi.astype(F32)
        mid = bf(r1)
        lo = bf(r1 - mid.astype(F32))
        cls.append((lw, _dot(tri, hi) + _dot(tri, mid) + _dot(tri, lo)))
    opnd = []
    for (c, q), (lw, cl) in zip(inst, cls):
        rows = slice(c * chunk, (c + 1) * chunk)
        cols = colsl[q]
        k = k_ref[0, rows, cols]
        kk = kk_ref[0, rows, cols]
        b = kk * a_ref[0, rows, cols]
        v = v_ref[0, rows, cols]
        cl_end = cl[chunk - 1:chunk, :]
        p_inv = jnp.exp(-cl)
        p_tail = jnp.exp(cl_end - cl)
        ar = jnp.concatenate([bf(-kk * jnp.exp(cl - lw)), bf(r_ref[0, rows, cols] * jnp.exp(cl))],
                             axis=0)
        tail_t = bf(jnp.concatenate([b * p_tail, k * p_tail], axis=0).T)
        pl_full = jnp.broadcast_to(jnp.exp(cl_end), (gw, gw))
        pl_col = jnp.sum(jnp.where(diag, pl_full, 0.0), axis=1, keepdims=True)
        opnd.append((ar, blockdiag_t(b * p_inv), blockdiag_t(k * p_inv), bf(v), blockdiag(v),
                     tail_t, pl_col))
    g_b = [_dot(o[0], o[1]) for o in opnd]
    g_k = [_dot(o[0], o[2]) for o in opnd]
    nms = [jnp.where(strict, g[:chunk], 0.0) for g in g_b]
    tinv = [eye + nm for nm in nms]
    pw = [bf(_dot(bf(nm), blockdiag(nm))) for nm in nms]
    for _ in range(chunk.bit_length() - 3):
        both = [_dot(jnp.concatenate([bf(t), p2], axis=0), blockdiag(p2)) for t, p2 in zip(tinv, pw)]
        tinv = [t + bo[:chunk] for t, bo in zip(tinv, both)]
        pw = [bf(bo[chunk:]) for bo in both]
    tinv = [bf(t + _dot(bf(t), blockdiag(p2))) for t, p2 in zip(tinv, pw)]
    g_ak = [bf(jnp.where(strict, g[:chunk], 0.0)) for g in g_k]
    g_rb = [bf(jnp.where(incl, g[chunk:], 0.0)) for g in g_b]
    g_rk = [bf(jnp.where(incl, g[chunk:], 0.0)) for g in g_k]

    state = [m_ref[q] for q in groups]
    for c in range(tb // chunk):
        ids = [c * len(groups) + q for q in groups]
        m0_bf = [bf(state[q]) for q in groups]
        rhs = [_dot(jnp.concatenate([opnd[i][0][:chunk], g_ak[i]], axis=1),
                    jnp.concatenate([m0_bf[q], opnd[i][4]], axis=0)) for q, i in zip(groups, ids)]
        u = [_dot(tinv[i], blockdiag(rhs[q])) for q, i in zip(groups, ids)]
        for q, i in zip(groups, ids):
            y_ref[c * chunk:(c + 1) * chunk, colsl[q]] = _dot(
                jnp.concatenate([opnd[i][0][chunk:], g_rb[i], g_rk[i]], axis=1),
                jnp.concatenate([m0_bf[q], blockdiag(u[q]), opnd[i][4]], axis=0))
        for q, i in zip(groups, ids):
            upd = _dot(opnd[i][5], jnp.concatenate([bf(u[q]), opnd[i][3]], axis=0))
            state[q] = opnd[i][6] * state[q] + jnp.where(same_head, upd, 0.0)
    for q in groups:
        m_ref[q] = state[q]

    npair = width // LANES
    for p in range(npair):
        cols = slice(p * LANES, (p + 1) * LANES)
        y = y_ref[:, cols]
        mean = _dot_x2(y, bd)
        dlt = y - mean
        var = _dot(bf(dlt * dlt), bd)
        yn = dlt * lax.rsqrt(var + GN_EPS) * lng_ref[:, cols] + lnb_ref[:, cols]
        rk_sum = _dot(bf(r_ref[0, :, cols] * k_ref[0, :, cols] * rk_ref[:, cols]), bd) * float(hd)
        out = (yn + rk_sum * v_ref[0, :, cols]) * g_ref[0, :, cols]
        o_ref[0, :, cols] = out.astype(o_ref.dtype)


def _rwkv_recurrence(r, lw, k, v, kk, a, g, r_k, ln_g, ln_b, tb, chunk, wblk):
    bsz, seq, d = r.shape
    gw = 4 * RWKV_HEAD
    assert chunk == RWKV_HEAD and wblk % gw == 0
    tok = pl.BlockSpec((1, tb, wblk), lambda b, j, t: (b, t, j))
    vec = pl.BlockSpec((1, wblk), lambda b, j, t: (0, j))
    tri = jnp.tril(jnp.ones((chunk, chunk), F32)).astype(BF16)
    bd = _group_mean_matrix(LANES, RWKV_HEAD)
    return pl.pallas_call(
        functools.partial(_rwkv_rec_kernel, chunk=chunk),
        grid=(bsz, d // wblk, seq // tb),
        in_specs=[tok] * 7 + [vec] * 3 + [_const_spec(tri.shape), _const_spec(bd.shape)],
        out_specs=tok,
        out_shape=jax.ShapeDtypeStruct((bsz, seq, d), BF16),
        scratch_shapes=[pltpu.VMEM((wblk // gw, gw, gw), F32),
                        pltpu.VMEM((tb, wblk), F32)],
        compiler_params=_cparams("arbitrary", "arbitrary", "arbitrary"),
        name="rwkv_recurrence",
    )(r, lw, k, v, kk, a, g, r_k.reshape(1, d), ln_g.reshape(1, d), ln_b.reshape(1, d), tri, bd)


def _odd_layer(x, mod, norm_mix, norm_ffn, w_gate, w_up, w_down, mu, w_r, w_k, w_v, w_o, w0, w1,
               w2, a0, a1, a2, g1, g2, k_k, k_a, r_k, ln_g, ln_b):
    bsz, seq, d = x.shape
    r, lw, k, v, kk, a, g = _rwkv_in(x, mod, norm_mix, mu, w_r, w_k, w_v, w0, w1, w2, a0, a1, a2,
                                     g1, g2, k_k, k_a, _pick_tile(seq, 512))
    yg = _rwkv_recurrence(r, lw, k, v, kk, a, g, r_k, ln_g, ln_b, _pick_tile(seq, 256), 64,
                          d)
    tm = _pick_tile(seq, 512)
    acts = [(yg, pl.BlockSpec((1, tm, d), lambda i, b: (b, i, 0)))]
    return _mix_ffn(x, mod, norm_ffn, acts, [w_o.astype(BF16)], w_gate, w_up, w_down, tm, 256)


def _pick_tile(n, pref):
    t = min(n, pref)
    assert n % t == 0, (n, t)
    return t


def _even_layer(x, mod, tables, lam_init, norm_mix, norm_ffn, w_gate, w_up, w_down, w_in,
                lam_re, lam_im, log_dt, b_re, b_im, c_re, c_im, d_skip, w_glu, q_norm, k_norm,
                lq1, lk1, lq2, lk2, subln, w_out):
    bsz, seq, d = x.shape
    s5w = lam_re.shape[0] * b_re.shape[-1]
    dfw = (w_in.shape[1] - s5w) // 3
    tm = _pick_tile(seq, 512)
    u2, q, k, v = _even_in(x, mod, norm_mix, w_in, q_norm, k_norm, tables, s5w, dfw, tm)
    ys = _s5_mixer(u2.reshape(seq * bsz, s5w), bsz, lam_re, lam_im, log_dt, b_re, b_im,
                   c_re, c_im, d_skip.reshape(-1), w_glu, _pick_tile(seq, 128))
    att = _diff_attention(q, k, v, lq1, lk1, lq2, lk2, subln, lam_init, _pick_tile(seq, 512))
    w_out_bf = w_out.astype(BF16)
    acts = [
        (ys.reshape(seq, bsz * s5w), pl.BlockSpec((tm, s5w), lambda i, b: (i, b))),
        (att, pl.BlockSpec((1, tm, dfw), lambda i, b: (b, i, 0))),
    ]
    return _mix_ffn(x, mod, norm_ffn, acts, [w_out_bf[:s5w], w_out_bf[s5w:]],
                    w_gate, w_up, w_down, tm, 256)


def kernel(x, c, positions, w_ada, b_ada, norm_mix, norm_ffn, ffn_w_gate, ffn_w_up, ffn_w_down,
           ev_w_in, ev_s5_lam_re, ev_s5_lam_im, ev_s5_log_dt, ev_s5_b_re, ev_s5_b_im, ev_s5_c_re,
           ev_s5_c_im, ev_s5_d, ev_s5_w_glu, ev_q_norm, ev_k_norm, ev_lambda_q1, ev_lambda_k1,
           ev_lambda_q2, ev_lambda_k2, ev_subln, ev_w_out, od_mu, od_w_r, od_w_k, od_w_v, od_w_o,
           od_w0, od_w1, od_w2, od_a0, od_a1, od_a2, od_g1, od_g2, od_k_k, od_k_a, od_r_k,
           od_ln_g, od_ln_b):
    depth = w_ada.shape[0]
    mod = _ada_mod(c, w_ada, b_ada)
    tables = _rope_tables(positions)
    for l in range(depth):
        if l % 2 == 0:
            e = l // 2
            lam_init = 0.8 - 0.6 * math.exp(-0.3 * l)
            x = _even_layer(x, mod[l], tables, lam_init, norm_mix[l], norm_ffn[l], ffn_w_gate[l],
                            ffn_w_up[l], ffn_w_down[l], ev_w_in[e], ev_s5_lam_re[e],
                            ev_s5_lam_im[e], ev_s5_log_dt[e], ev_s5_b_re[e], ev_s5_b_im[e],
                            ev_s5_c_re[e], ev_s5_c_im[e], ev_s5_d[e], ev_s5_w_glu[e], ev_q_norm[e],
                            ev_k_norm[e], ev_lambda_q1[e], ev_lambda_k1[e], ev_lambda_q2[e],
                            ev_lambda_k2[e], ev_subln[e], ev_w_out[e])
        else:
            o = l // 2
            x = _odd_layer(x, mod[l], norm_mix[l], norm_ffn[l], ffn_w_gate[l], ffn_w_up[l],
                           ffn_w_down[l], od_mu[o], od_w_r[o], od_w_k[o], od_w_v[o], od_w_o[o],
                           od_w0[o], od_w1[o], od_w2[o], od_a0[o], od_a1[o], od_a2[o], od_g1[o],
                           od_g2[o], od_k_k[o], od_k_a[o], od_r_k[o], od_ln_g[o], od_ln_b[o])
    return x
```

```python
import functools
import math

import jax
import jax.numpy as jnp
from jax import lax
from jax.experimental import pallas as pl
from jax.experimental.pallas import tpu as pltpu

F32 = jnp.float32
BF16 = jnp.bfloat16

RMS_EPS = 1e-6
GN_EPS = 64e-5
ROPE_THETA = 500000.0
DT_HEAD = 64
ROT_DIM = DT_HEAD // 4
S5_GROUP = 16
S5_STATE = 64
RWKV_HEAD = 64
LANES = 128
SUBLANES = 8
S5_MXU_DEPTH = 256
VMEM_LIMIT = 56 * 1024 * 1024


def _cparams(*sem):
    return pltpu.CompilerParams(dimension_semantics=sem, vmem_limit_bytes=VMEM_LIMIT)


def _const_spec(shape):
    nd = len(shape)
    return pl.BlockSpec(shape, lambda *_: (0,) * nd, pipeline_mode=pl.Buffered(1))


def _dot(a, b):
    return jnp.dot(a, b, preferred_element_type=F32)


def _dot_nt(a, b):
    return lax.dot_general(a, b, (((1,), (1,)), ((), ())), preferred_element_type=F32)


def _dot_tn(a, b):
    return lax.dot_general(a, b, (((0,), (0,)), ((), ())), preferred_element_type=F32)


def _split_bf16(x):
    hi = x.astype(BF16)
    lo = (x - hi.astype(F32)).astype(BF16)
    return hi, lo


def _dot_x2(x, w_bf16):
    hi, lo = _split_bf16(x)
    return _dot(hi, w_bf16) + _dot(lo, w_bf16)


def _dot_x2_rhs(w_bf16, x):
    hi, lo = _split_bf16(x)
    return _dot(w_bf16, hi) + _dot(w_bf16, lo)


def _modnorm(x, g, scale, shift):
    ms = jnp.mean(x * x, axis=-1, keepdims=True)
    return (x * lax.rsqrt(ms + RMS_EPS)) * g * (1.0 + scale) + shift


def _ada_kernel(c_ref, w_ref, b_ref, o_ref):
    c = c_ref[...]
    ca = c * jax.nn.sigmoid(c)
    hi, lo = _split_bf16(ca)
    w = w_ref[0]
    w_hi, w_lo = _split_bf16(w)
    o_ref[0] = _dot(hi, w_hi) + _dot(lo, w_hi) + _dot(hi, w_lo) + b_ref[0]


def _ada_mod(c, w_ada, b_ada):
    depth, d, n = w_ada.shape
    bsz = c.shape[0]
    rows = -(-bsz // SUBLANES) * SUBLANES
    c_pad = jnp.zeros((rows, d), F32).at[:bsz].set(c)
    tn = n // 4
    out = pl.pallas_call(
        _ada_kernel,
        grid=(depth, n // tn),
        in_specs=[
            pl.BlockSpec((rows, d), lambda l, j: (0, 0)),
            pl.BlockSpec((1, d, tn), lambda l, j: (l, 0, j)),
            pl.BlockSpec((1, 1, tn), lambda l, j: (l, 0, j)),
        ],
        out_specs=pl.BlockSpec((1, rows, tn), lambda l, j: (l, 0, j)),
        out_shape=jax.ShapeDtypeStruct((depth, rows, n), F32),
        compiler_params=_cparams("arbitrary", "arbitrary"),
        name="ada_mod",
    )(c_pad, w_ada, b_ada.reshape(depth, 1, n))
    mod = out[:, :bsz].reshape(depth, bsz, 6, d)
    return jnp.pad(mod, ((0, 0), (0, 0), (0, SUBLANES - 6), (0, 0)))


def _mul_trig_kernel(p_ref, f_ref, cos_ref, sin_ref):
    a = p_ref[...] * f_ref[...]
    cos_ref[...] = jnp.cos(a)
    sin_ref[...] = jnp.sin(a)


def _rope_tables(positions):
    bsz, seq = positions.shape
    half = ROT_DIM // 2
    inv_freq = ROPE_THETA ** (-jnp.arange(0, ROT_DIM, 2, dtype=F32) / ROT_DIM)
    n = bsz * seq * half
    pos_rep = jnp.broadcast_to(positions.astype(F32)[..., None], (bsz, seq, half))
    frq_rep = jnp.broadcast_to(inv_freq, (bsz, seq, half))
    rows = n // LANES
    tr = min(rows, 512)
    cos, sin = pl.pallas_call(
        _mul_trig_kernel,
        grid=(rows // tr,),
        in_specs=[pl.BlockSpec((tr, LANES), lambda i: (i, 0))] * 2,
        out_specs=[pl.BlockSpec((tr, LANES), lambda i: (i, 0))] * 2,
        out_shape=[jax.ShapeDtypeStruct((rows, LANES), F32)] * 2,
        compiler_params=_cparams("arbitrary"),
        name="rope_trig",
    )(pos_rep.reshape(rows, LANES), frq_rep.reshape(rows, LANES))
    cs = jnp.concatenate([cos.reshape(bsz, seq, half), sin.reshape(bsz, seq, half)], axis=-1)
    return jnp.pad(cs, ((0, 0), (0, 0), (0, LANES - ROT_DIM)))


def _expand_rope(cs):
    half = ROT_DIM // 2
    lane = lax.broadcasted_iota(jnp.int32, cs.shape, 1)
    c0 = jnp.where(lane < half, cs, jnp.where(lane < ROT_DIM, pltpu.roll(cs, half, axis=1), 1.0))
    sa0 = jnp.where(lane < half, -pltpu.roll(cs, LANES - half, axis=1), 0.0)
    sb0 = jnp.where((lane >= half) & (lane < ROT_DIM), cs, 0.0)
    second = lane >= DT_HEAD
    return (jnp.where(second, pltpu.roll(c0, DT_HEAD, axis=1), c0),
            jnp.where(second, pltpu.roll(sa0, DT_HEAD, axis=1), sa0),
            jnp.where(second, pltpu.roll(sb0, DT_HEAD, axis=1), sb0))


def _group_mean_matrix(width, group):
    idx = jnp.arange(width) // group
    return jnp.where(idx[:, None] == idx[None, :], 1.0 / group, 0.0).astype(BF16)


def _even_in_kernel(x_ref, mod_ref, g_ref, w_ref, qn_ref, kn_ref, cs_ref, bd_ref,
                    u_ref, q_ref, k_ref, v_ref, *, s5w, dfw):
    x = x_ref[0]
    h = _modnorm(x, g_ref[...], mod_ref[0, 1:2, :], mod_ref[0, 0:1, :])
    proj = _dot(h.astype(BF16), w_ref[...])
    u_ref[...] = proj[:, :s5w]
    cosv, sav, sbv = _expand_rope(cs_ref[0])
    half = ROT_DIM // 2

    def norm_rope(t, gn, out_ref, post_scale):
        ms = _dot((t * t).astype(BF16), bd_ref[...])
        t = t * lax.rsqrt(ms + RMS_EPS) * gn
        for j in range(dfw // LANES):
            tj = t[:, j * LANES:(j + 1) * LANES]
            up = pltpu.roll(tj, LANES - half, axis=1)
            dn = pltpu.roll(tj, half, axis=1)
            rj = tj * cosv + up * sav + dn * sbv
            out_ref[0, :, j * LANES:(j + 1) * LANES] = (rj * post_scale).astype(out_ref.dtype)

    norm_rope(proj[:, s5w:s5w + dfw], qn_ref[...], q_ref, DT_HEAD ** -0.5 * math.log2(math.e))
    norm_rope(proj[:, s5w + dfw:s5w + 2 * dfw], kn_ref[...], k_ref, 1.0)
    v_ref[0] = proj[:, s5w + 2 * dfw:].astype(v_ref.dtype)


def _even_in(x, mod, g, w_in, q_norm, k_norm, tables, s5w, dfw, tm):
    bsz, seq, d = x.shape
    ncol = w_in.shape[1]
    rep = dfw // DT_HEAD
    kern = functools.partial(_even_in_kernel, s5w=s5w, dfw=dfw)
    tok = lambda w: pl.BlockSpec((1, tm, w), lambda i, b: (b, i, 0))
    return pl.pallas_call(
        kern,
        grid=(seq // tm, bsz),
        in_specs=[
            tok(d),
            pl.BlockSpec((1, SUBLANES, d), lambda i, b: (b, 0, 0)),
            _const_spec((1, d)),
            _const_spec((d, ncol)),
            _const_spec((1, dfw)),
            _const_spec((1, dfw)),
            tok(LANES),
            _const_spec((dfw, dfw)),
        ],
        out_specs=[
            pl.BlockSpec((tm, s5w), lambda i, b: (i, b)),
            tok(dfw), tok(dfw), tok(dfw),
        ],
        out_shape=[
            jax.ShapeDtypeStruct((seq, bsz * s5w), F32),
            jax.ShapeDtypeStruct((bsz, seq, dfw), BF16),
            jax.ShapeDtypeStruct((bsz, seq, dfw), BF16),
            jax.ShapeDtypeStruct((bsz, seq, dfw), BF16),
        ],
        compiler_params=_cparams("arbitrary", "arbitrary"),
        name="even_in_proj",
    )(x, mod, g.reshape(1, d), w_in.astype(BF16),
      jnp.tile(q_norm, rep).reshape(1, dfw), jnp.tile(k_norm, rep).reshape(1, dfw),
      tables, _group_mean_matrix(dfw, DT_HEAD))


def _s5_disc_kernel(lr_ref, li_ref, ldt_ref, br_ref, bi_ref, ar_ref, ai_ref, bbr_ref, bbi_ref,
                    *, bsz):
    lr = lr_ref[...]
    li = li_ref[...]
    dt = jnp.exp(ldt_ref[...])
    mag = jnp.exp(lr * dt)
    abar_r = mag * jnp.cos(li * dt)
    abar_i = mag * jnp.sin(li * dt)
    den = lr * lr + li * li
    num_r = abar_r - 1.0
    q_r = (num_r * lr + abar_i * li) / den
    q_i = (abar_i * lr - num_r * li) / den
    br = br_ref[...]
    bi = bi_ref[...]
    bbr_ref[...] = q_r * br - q_i * bi
    bbi_ref[...] = q_r * bi + q_i * br
    pr, pi = abar_r, abar_i
    for j in range(SUBLANES):
        if j and j % bsz == 0:
            pr, pi = pr * abar_r - pi * abar_i, pr * abar_i + pi * abar_r
        ar_ref[:, j:j + 1] = pr
        ai_ref[:, j:j + 1] = pi


def _s5_discretise(lam_re, lam_im, log_dt, b_re, b_im, bsz):
    g, p = lam_re.shape
    hh = b_re.shape[-1]
    n = g * p
    tab = jax.ShapeDtypeStruct((n, SUBLANES), F32)
    mat = jax.ShapeDtypeStruct((n, hh), F32)
    ldt = jnp.broadcast_to(log_dt[:, None], (g, p)).reshape(n, 1)
    return pl.pallas_call(
        functools.partial(_s5_disc_kernel, bsz=bsz),
        out_shape=[tab, tab, mat, mat],
        name="s5_discretise",
    )(lam_re.reshape(n, 1), lam_im.reshape(n, 1), ldt, b_re.reshape(n, hh), b_im.reshape(n, hh))


def _s5_scan_kernel(u_ref, bre_ref, bim_ref, cre_ref, cim_ref, dsk_ref, a1r_ref, a1i_ref,
                    pwr_ref, pwi_ref, wglu_ref, perm_ref, permt_ref, o_ref, xr_ref, xi_ref,
                    cr_ref, ci_ref, *, bsz, lane_chunk):
    rows, nstate = xr_ref.shape
    step = pl.program_id(0)

    @pl.when(step == 0)
    def _():
        cr_ref[...] = jnp.zeros_like(cr_ref)
        ci_ref[...] = jnp.zeros_like(ci_ref)

    nblk, wpart, spart = bre_ref.shape
    width = wglu_ref.shape[0]
    blk = u_ref[...]
    u_bt = jnp.concatenate([blk[:, b * width:(b + 1) * width] for b in range(bsz)], axis=0)
    u = _dot_x2_rhs(perm_ref[...], u_bt)
    u_bf = u.astype(BF16)
    for j in range(nblk):
        uj = u_bf[:, j * wpart:(j + 1) * wpart]
        xr_ref[:, j * spart:(j + 1) * spart] = _dot(uj, bre_ref[j])
        xi_ref[:, j * spart:(j + 1) * spart] = _dot(uj, bim_ref[j])

    row = lax.broadcasted_iota(jnp.int32, (SUBLANES, lane_chunk), 0)
    steps_per_tile = SUBLANES // bsz
    ntiles = rows // SUBLANES

    for c in range(nstate // lane_chunk):
        cols = pl.ds(c * lane_chunk, lane_chunk)
        a1r = a1r_ref[:, cols]
        a1i = a1i_ref[:, cols]
        pwr = pwr_ref[:, cols]
        pwi = pwi_ref[:, cols]

        def tile_body(i, carry):
            pr, pi = carry
            base = pl.multiple_of(i * SUBLANES, SUBLANES)
            zr = xr_ref[pl.ds(base, SUBLANES), cols]
            zi = xi_ref[pl.ds(base, SUBLANES), cols]
            sh = bsz
            apr, api = a1r, a1i
            for _ in range(steps_per_tile.bit_length() - 1):
                sr = jnp.where(row >= sh, pltpu.roll(zr, sh, axis=0), 0.0)
                si = jnp.where(row >= sh, pltpu.roll(zi, sh, axis=0), 0.0)
                zr, zi = zr + apr * sr - api * si, zi + apr * si + api * sr
                apr, api = apr * apr - api * api, 2.0 * apr * api
                sh *= 2
            last = SUBLANES - bsz
            br_, bi_ = pr, pi
            sh = bsz
            while sh < SUBLANES:
                br_ = jnp.where(row >= last, br_, pltpu.roll(br_, SUBLANES - sh, axis=0))
                bi_ = jnp.where(row >= last, bi_, pltpu.roll(bi_, SUBLANES - sh, axis=0))
                last -= sh
                sh *= 2
            xr = zr + pwr * br_ - pwi * bi_
            xi = zi + pwr * bi_ + pwi * br_
            xr_ref[pl.ds(base, SUBLANES), cols] = xr
            xi_ref[pl.ds(base, SUBLANES), cols] = xi
            return xr, xi

        fr, fi = lax.fori_loop(0, ntiles, tile_body, (cr_ref[:, cols], ci_ref[:, cols]))
        cr_ref[:, cols] = fr
        ci_ref[:, cols] = fi

    y = jnp.concatenate(
        [_dot(xr_ref[:, j * spart:(j + 1) * spart].astype(BF16), cre_ref[j])
         - _dot(xi_ref[:, j * spart:(j + 1) * spart].astype(BF16), cim_ref[j])
         for j in range(nblk)], axis=1) + dsk_ref[...] * u
    y = jax.nn.gelu(y)
    gate = jax.nn.sigmoid(_dot(y.astype(BF16), wglu_ref[...]))
    out_bt = _dot(permt_ref[...], (y * gate).astype(BF16))
    tchunk = rows // bsz
    for b in range(bsz):
        o_ref[:, b * width:(b + 1) * width] = out_bt[b * tchunk:(b + 1) * tchunk].astype(o_ref.dtype)


def _s5_mixer(u2, bsz, lam_re, lam_im, log_dt, b_re, b_im, c_re, c_im, d_skip, w_glu, tchunk):
    seq = u2.shape[0]
    width = u2.shape[1] // bsz
    g, p = lam_re.shape
    hh = b_re.shape[-1]
    nstate = g * p
    pw_r, pw_i, bbar_r, bbar_i = _s5_discretise(lam_re, lam_im, log_dt, b_re, b_im, bsz)
    gb = max(1, min(g, S5_MXU_DEPTH // hh))
    nblk = g // gb
    eye = jnp.eye(gb, dtype=F32)
    def in_map(bb):
        return jnp.einsum('jgph,gk->jghkp', bb.reshape(nblk, gb, p, hh), eye).reshape(
            nblk, gb * hh, gb * p)
    def out_map(cc):
        return jnp.einsum('jghp,gk->jgpkh', cc.reshape(nblk, gb, hh, p), eye).reshape(
            nblk, gb * p, gb * hh)
    pw_r = pw_r.T
    pw_i = pw_i.T
    a1r = jnp.broadcast_to(pw_r[0:1], (SUBLANES, nstate))
    a1i = jnp.broadcast_to(pw_i[0:1], (SUBLANES, nstate))
    rows = tchunk * bsz
    src = jnp.arange(rows)
    perm = (src[None, :] == ((src % bsz) * tchunk + src // bsz)[:, None]).astype(BF16)
    kern = functools.partial(_s5_scan_kernel, bsz=bsz, lane_chunk=512)
    return pl.pallas_call(
        kern,
        grid=(seq // tchunk,),
        in_specs=[
            pl.BlockSpec((tchunk, bsz * width), lambda i: (i, 0)),
            _const_spec((nblk, gb * hh, gb * p)), _const_spec((nblk, gb * hh, gb * p)),
            _const_spec((nblk, gb * p, gb * hh)), _const_spec((nblk, gb * p, gb * hh)),
            _const_spec((1, width)),
            _const_spec((SUBLANES, nstate)), _const_spec((SUBLANES, nstate)),
            _const_spec((SUBLANES, nstate)), _const_spec((SUBLANES, nstate)),
            _const_spec((width, width)),
            _const_spec((rows, rows)), _const_spec((rows, rows)),
        ],
        out_specs=pl.BlockSpec((tchunk, bsz * width), lambda i: (i, 0)),
        out_shape=jax.ShapeDtypeStruct((seq, bsz * width), BF16),
        scratch_shapes=[
            pltpu.VMEM((rows, nstate), F32), pltpu.VMEM((rows, nstate), F32),
            pltpu.VMEM((SUBLANES, nstate), F32), pltpu.VMEM((SUBLANES, nstate), F32),
        ],
        compiler_params=_cparams("arbitrary"),
        name="s5_scan_glu",
    )(u2, in_map(bbar_r).astype(BF16), in_map(bbar_i).astype(BF16),
      out_map(c_re).astype(BF16), out_map(c_im).astype(BF16),
      d_skip.reshape(1, width), a1r, a1i, pw_r, pw_i, w_glu.astype(BF16), perm, perm.T)


def _diff_attn_kernel(lam_ref, sub_ref, q_ref, k_ref, v_ref, o_ref, q2_ref, m_ref, acc_ref,
                      *, tq, lam_init):
    qi = pl.program_id(2)
    vdim = v_ref.shape[-1]
    q = q_ref[0]
    lane = lax.broadcasted_iota(jnp.int32, q.shape, 1)
    zero = jnp.zeros_like(q)
    q2_ref[:tq, :] = jnp.where(lane < DT_HEAD, q, zero)
    q2_ref[tq:, :] = jnp.where(lane >= DT_HEAD, q, zero)
    m_ref[...] = jnp.full(m_ref.shape, -jnp.inf, F32)
    acc_ref[...] = jnp.zeros_like(acc_ref)
    all_rows = ((0, 2 * tq),)

    def gather(ref, row_slices):
        parts = [ref[a:b, :] for a, b in row_slices]
        return parts[0] if len(parts) == 1 else jnp.concatenate(parts, axis=0)

    def scores(kstart, ksize, row_slices=all_rows):
        start = pl.multiple_of(kstart, ksize)
        return _dot_nt(gather(q2_ref, row_slices), k_ref[0, pl.ds(start, ksize), :])

    def update(kstart, ksize, s, row_slices=all_rows, mask=None):
        start = pl.multiple_of(kstart, ksize)
        v_ext = jnp.concatenate([v_ref[0, pl.ds(start, ksize), :], jnp.ones((ksize, vdim), BF16)],
                                axis=1)
        if mask is not None:
            s = jnp.where(mask, s, -jnp.inf)
        part = s[:, :LANES]
        for t in range(1, ksize // LANES):
            part = jnp.maximum(part, s[:, t * LANES:(t + 1) * LANES])
        m_old = gather(m_ref, row_slices)
        m_new = jnp.maximum(m_old, jnp.max(part, axis=-1, keepdims=True))
        alpha = jnp.exp2(m_old - m_new)
        p = jnp.concatenate([jnp.exp2(s[:, t * LANES:(t + 1) * LANES] - m_new)
                             for t in range(ksize // LANES)], axis=1)
        acc = (jnp.concatenate([alpha] * (2 * vdim // LANES), axis=1) * gather(acc_ref, row_slices)
               + _dot(p.astype(BF16), v_ext))
        off = 0
        for a, b in row_slices:
            m_ref[a:b, :] = m_new[off:off + b - a]
            acc_ref[a:b, :] = acc[off:off + b - a]
            off += b - a

    def diagonal(kstart, with_previous):
        hq = tq // 2
        if with_previous:
            s_prev = scores(kstart - tq, tq)
        row_l = lax.broadcasted_iota(jnp.int32, (2 * tq, hq), 0)
        col_l = lax.broadcasted_iota(jnp.int32, (2 * tq, hq), 1)
        row_r = lax.broadcasted_iota(jnp.int32, (tq, hq), 0)
        col_r = lax.broadcasted_iota(jnp.int32, (tq, hq), 1)
        late_rows = ((hq, tq), (tq + hq, 2 * tq))
        s_left = scores(kstart, hq)
        s_right = scores(kstart + hq, hq, late_rows)
        if with_previous:
            update(kstart - tq, tq, s_prev)
        update(kstart, hq, s_left, mask=col_l <= row_l % tq)
        update(kstart + hq, hq, s_right, late_rows, col_r <= row_r % hq)

    def pair(j0):
        s0 = scores(j0 * tq, tq)
        s1 = scores((j0 + 1) * tq, tq)
        update(j0 * tq, tq, s0)
        update((j0 + 1) * tq, tq, s1)

    npairs = qi // 2
    lax.fori_loop(0, npairs, lambda jj, _: (pair(2 * jj), 0)[1], 0)

    @pl.when(qi % 2 == 1)
    def _():
        diagonal(qi * tq, True)

    @pl.when(qi % 2 == 0)
    def _():
        diagonal(qi * tq, False)

    lv = lam_ref[...]
    lam = (jnp.exp(jnp.sum(lv[0:1] * lv[1:2], axis=-1, keepdims=True))
           - jnp.exp(jnp.sum(lv[2:3] * lv[3:4], axis=-1, keepdims=True)) + lam_init)
    acc = acc_ref[...]
    o = (acc[:tq, :vdim] / acc[:tq, vdim:]) - lam * (acc[tq:, :vdim] / acc[tq:, vdim:])
    ms = jnp.mean(o * o, axis=-1, keepdims=True)
    o = o * lax.rsqrt(ms + RMS_EPS) * sub_ref[...] * (1.0 - lam_init)
    o_ref[0] = o.astype(o_ref.dtype)


def _diff_attention(q, k, v, lq1, lk1, lq2, lk2, subln, lam_init, tq):
    bsz, seq, dfw = q.shape
    vdim = 2 * DT_HEAD
    heads = dfw // vdim
    lamv = jnp.zeros((SUBLANES, LANES), F32)
    for i, t in enumerate((lq1, lk1, lq2, lk2)):
        lamv = lamv.at[i, :t.shape[0]].set(t)
    kern = functools.partial(_diff_attn_kernel, tq=tq, lam_init=lam_init)
    kv_spec = pl.BlockSpec((1, seq, vdim), lambda b, h, i: (b, 0, h))
    return pl.pallas_call(
        kern,
        grid=(bsz, heads, seq // tq),
        in_specs=[
            _const_spec((SUBLANES, LANES)),
            _const_spec((1, vdim)),
            pl.BlockSpec((1, tq, vdim), lambda b, h, i: (b, i, h)),
            kv_spec, kv_spec,
        ],
        out_specs=pl.BlockSpec((1, tq, vdim), lambda b, h, i: (b, i, h)),
        out_shape=jax.ShapeDtypeStruct((bsz, seq, dfw), BF16),
        scratch_shapes=[pltpu.VMEM((2 * tq, vdim), BF16), pltpu.VMEM((2 * tq, LANES), F32),
                        pltpu.VMEM((2 * tq, 2 * vdim), F32)],
        compiler_params=_cparams("arbitrary", "arbitrary", "arbitrary"),
        name="diff_attention",
    )(lamv, subln.reshape(1, vdim), q, k, v)


def _mix_ffn_kernel(*refs, n_in, hid_chunk):
    x_ref, mod_ref, g_ref = refs[:3]
    act_refs = refs[3:3 + n_in]
    w_refs = refs[3 + n_in:3 + 2 * n_in]
    wg_ref, wu_ref, wd_ref, o_ref = refs[3 + 2 * n_in:]
    mix = None
    for a_ref, w_ref in zip(act_refs, w_refs):
        a = a_ref[0] if len(a_ref.shape) == 3 else a_ref[...]
        t = _dot(a, w_ref[...])
        mix = t if mix is None else mix + t
    x1 = x_ref[0] + mod_ref[0, 2:3, :] * mix
    h = _modnorm(x1, g_ref[...], mod_ref[0, 4:5, :], mod_ref[0, 3:4, :]).astype(BF16)
    acc = None
    for j in range(wg_ref.shape[1] // hid_chunk):
        sl = slice(j * hid_chunk, (j + 1) * hid_chunk)
        gate = _dot(h, wg_ref[:, sl])
        up = _dot(h, wu_ref[:, sl])
        act = (gate * jax.nn.sigmoid(gate) * up).astype(BF16)
        t = _dot(act, wd_ref[sl, :])
        acc = t if acc is None else acc + t
    o_ref[0] = x1 + mod_ref[0, 5:6, :] * acc


def _mix_ffn(x, mod, g, acts, weights, w_gate, w_up, w_down, tm, hid_chunk, w_specs=None):
    bsz, seq, d = x.shape
    if w_specs is None:
        w_specs = [_const_spec(w.shape) for w in weights]
    assert w_gate.shape[1] % hid_chunk == 0
    wg = w_gate.astype(BF16)
    wu = w_up.astype(BF16)
    wd = w_down.astype(BF16)
    tok = pl.BlockSpec((1, tm, d), lambda i, b: (b, i, 0))
    return pl.pallas_call(
        functools.partial(_mix_ffn_kernel, n_in=len(acts), hid_chunk=hid_chunk),
        grid=(seq // tm, bsz),
        in_specs=[tok, pl.BlockSpec((1, SUBLANES, d), lambda i, b: (b, 0, 0)), _const_spec((1, d))]
                 + [spec for _, spec in acts]
                 + w_specs
                 + [_const_spec(wg.shape), _const_spec(wu.shape), _const_spec(wd.shape)],
        out_specs=tok,
        out_shape=jax.ShapeDtypeStruct((bsz, seq, d), F32),
        compiler_params=_cparams("arbitrary", "arbitrary"),
        name="mix_ffn",
    )(x, mod, g.reshape(1, d), *[a for a, _ in acts], *weights, wg, wu, wd)


def _rwkv_in_kernel(x_ref, xp_ref, mod_ref, g_ref, mu_ref, wr_ref, wk_ref, wv_ref, w1_ref, w2_ref,
                    a1_ref, a2_ref, g1_ref, g2_ref, w0_ref, a0_ref, kkw_ref, kaw_ref, bd_ref,
                    r_out, lw_out, k_out, v_out, kk_out, a_out, g_out):
    i = pl.program_id(1)
    g = g_ref[...]
    scale = mod_ref[0, 1:2, :]
    shift = mod_ref[0, 0:1, :]
    h = _modnorm(x_ref[0], g, scale, shift)
    hp = _modnorm(xp_ref[0][SUBLANES - 1:SUBLANES, :], g, scale, shift)
    hp = jnp.where(i == 0, 0.0, hp)
    row = lax.broadcasted_iota(jnp.int32, h.shape, 0)
    h_prev = jnp.where(row == 0, hp, pltpu.roll(h, 1, axis=0))
    dx = h_prev - h

    def lerp(j):
        return (h + dx * mu_ref[j:j + 1, :]).astype(BF16)

    r_out[0] = _dot(lerp(0), wr_ref[...])
    wl = jnp.tanh(_dot(lerp(1), w1_ref[...]))
    wdec = w0_ref[...] + _dot(wl.astype(BF16), w2_ref[...])
    w = -jax.nn.softplus(-wdec) - 0.5
    lw_out[0] = -jnp.exp(w)
    k = _dot(lerp(2), wk_ref[...])
    v_out[0] = _dot(lerp(3), wv_ref[...])
    al = _dot(lerp(4), a1_ref[...])
    a = jax.nn.sigmoid(a0_ref[...] + _dot(al.astype(BF16), a2_ref[...]))
    a_out[0] = a
    gl = jax.nn.sigmoid(_dot(lerp(5), g1_ref[...]))
    g_out[0] = _dot(gl.astype(BF16), g2_ref[...])
    kk = k * kkw_ref[...]
    bd = bd_ref[...]
    for j in range(kk.shape[1] // LANES):
        kj = kk[:, j * LANES:(j + 1) * LANES]
        ss = _dot((kj * kj).astype(BF16), bd) * float(RWKV_HEAD)
        kk_out[0, :, j * LANES:(j + 1) * LANES] = kj / jnp.maximum(jnp.sqrt(ss), 1e-12)
    k_out[0] = k * (1.0 + (a - 1.0) * kaw_ref[...])


def _rwkv_in(x, mod, g, mu, w_r, w_k, w_v, w0, w1, w2, a0, a1, a2, g1, g2, k_k, k_a, tm):
    bsz, seq, d = x.shape
    tok = pl.BlockSpec((1, tm, d), lambda b, i: (b, i, 0))
    prev = pl.BlockSpec((1, SUBLANES, d),
                        lambda b, i: (b, jnp.maximum(i * (tm // SUBLANES) - 1, 0), 0))
    bf = lambda w: w.astype(BF16)
    vec = lambda t: t.reshape(1, d)
    consts = [vec(g), mu, bf(w_r), bf(w_k), bf(w_v), bf(w1), bf(w2), bf(a1), bf(a2), bf(g1), bf(g2),
              vec(w0), vec(a0), vec(k_k), vec(k_a), _group_mean_matrix(LANES, RWKV_HEAD)]
    out = jax.ShapeDtypeStruct((bsz, seq, d), F32)
    return pl.pallas_call(
        _rwkv_in_kernel,
        grid=(bsz, seq // tm),
        in_specs=[tok, prev, pl.BlockSpec((1, SUBLANES, d), lambda b, i: (b, 0, 0))]
                 + [_const_spec(t.shape) for t in consts],
        out_specs=[tok] * 7,
        out_shape=[out] * 7,
        compiler_params=_cparams("arbitrary", "arbitrary"),
        name="rwkv_in_proj",
    )(x, x, mod, *consts)


def _rwkv_rec_kernel(r_ref, lw_ref, k_ref, v_ref, kk_ref, a_ref, g_ref, rk_ref, lng_ref, lnb_ref,
                     tri_ref, bd_ref, o_ref, m_ref, y_ref, *, chunk):
    tstep = pl.program_id(2)

    @pl.when(tstep == 0)
    def _():
        m_ref[...] = jnp.zeros_like(m_ref)

    tb, width = r_ref.shape[1:]
    hd = RWKV_HEAD
    gw = m_ref.shape[-1]
    nh = gw // hd
    groups = range(width // gw)
    colsl = [slice(q * gw, (q + 1) * gw) for q in groups]
    lane = lax.broadcasted_iota(jnp.int32, (chunk, gw), 1)
    trow = lax.broadcasted_iota(jnp.int32, (chunk, gw), 0)
    jpos = lane % hd
    strict = jpos < trow
    incl = jpos <= trow
    eye = jnp.where(jpos == trow, 1.0, 0.0)
    head_of_lane = lane // hd
    in_head = [head_of_lane == h for h in range(nh)]
    sq_r = lax.broadcasted_iota(jnp.int32, (gw, gw), 0)
    sq_c = lax.broadcasted_iota(jnp.int32, (gw, gw), 1)
    same_head = (sq_r // hd) == (sq_c // hd)
    diag = sq_r == sq_c
    tri = tri_ref[...]
    bd = bd_ref[...]
    bf = lambda t: t.astype(BF16)

    def blockdiag(y):
        return bf(jnp.concatenate([jnp.where(in_head[h], y, 0.0) for h in range(nh)], axis=0))

    def blockdiag_t(x):
        xt = jnp.concatenate([x] * nh, axis=0).T
        return bf(jnp.where(same_head, xt, 0.0))

    inst = [(c, q) for c in range(tb // chunk) for q in groups]
    cls = []
    for c, q in inst:
        lw = lw_ref[0, c * chunk:(c + 1) * chunk, colsl[q]]
        hi = bf(lw)
        r1 = lw - hi.astype(F32)
        mid = bf(r1)
        lo = bf(r1 - mid.astype(F32))
        cls.append((lw, _dot(tri, hi) + _dot(tri, mid) + _dot(tri, lo)))
    opnd = []
    for (c, q), (lw, cl) in zip(inst, cls):
        rows = slice(c * chunk, (c + 1) * chunk)
        cols = colsl[q]
        k = k_ref[0, rows, cols]
        kk = kk_ref[0, rows, cols]
        b = kk * a_ref[0, rows, cols]
        v = v_ref[0, rows, cols]
        cl_end = cl[chunk - 1:chunk, :]
        p_inv = jnp.exp(-cl)
        p_tail = jnp.exp(cl_end - cl)
        ar = jnp.concatenate([bf(-kk * jnp.exp(cl - lw)), bf(r_ref[0, rows, cols] * jnp.exp(cl))],
                             axis=0)
        tail_t = bf(jnp.concatenate([b * p_tail, k * p_tail], axis=0).T)
        pl_full = jnp.broadcast_to(jnp.exp(cl_end), (gw, gw))
        pl_col = jnp.sum(jnp.where(diag, pl_full, 0.0), axis=1, keepdims=True)
        opnd.append((ar, blockdiag_t(b * p_inv), blockdiag_t(k * p_inv), bf(v), blockdiag(v),
                     tail_t, pl_col))
    g_b = [_dot(o[0], o[1]) for o in opnd]
    g_k = [_dot(o[0], o[2]) for o in opnd]
    nms = [jnp.where(strict, g[:chunk], 0.0) for g in g_b]
    tinv = [eye + nm for nm in nms]
    pw = [bf(_dot(bf(nm), blockdiag(nm))) for nm in nms]
    for _ in range(chunk.bit_length() - 3):
        both = [_dot(jnp.concatenate([bf(t), p2], axis=0), blockdiag(p2)) for t, p2 in zip(tinv, pw)]
        tinv = [t + bo[:chunk] for t, bo in zip(tinv, both)]
        pw = [bf(bo[chunk:]) for bo in both]
    tinv = [bf(t + _dot(bf(t), blockdiag(p2))) for t, p2 in zip(tinv, pw)]
    g_ak = [bf(jnp.where(strict, g[:chunk], 0.0)) for g in g_k]
    g_rb = [bf(jnp.where(incl, g[chunk:], 0.0)) for g in g_b]
    g_rk = [bf(jnp.where(incl, g[chunk:], 0.0)) for g in g_k]

    state = [m_ref[q] for q in groups]
    for c in range(tb // chunk):
        ids = [c * len(groups) + q for q in groups]
        m0_bf = [bf(state[q]) for q in groups]
        rhs = [_dot(jnp.concatenate([opnd[i][0][:chunk], g_ak[i]], axis=1),
                    jnp.concatenate([m0_bf[q], opnd[i][4]], axis=0)) for q, i in zip(groups, ids)]
        u = [_dot(tinv[i], blockdiag(rhs[q])) for q, i in zip(groups, ids)]
        for q, i in zip(groups, ids):
            y_ref[c * chunk:(c + 1) * chunk, colsl[q]] = _dot(
                jnp.concatenate([opnd[i][0][chunk:], g_rb[i], g_rk[i]], axis=1),
                jnp.concatenate([m0_bf[q], blockdiag(u[q]), opnd[i][4]], axis=0))
        for q, i in zip(groups, ids):
            upd = _dot(opnd[i][5], jnp.concatenate([bf(u[q]), opnd[i][3]], axis=0))
            state[q] = opnd[i][6] * state[q] + jnp.where(same_head, upd, 0.0)
    for q in groups:
        m_ref[q] = state[q]

    npair = width // LANES
    for p in range(npair):
        cols = slice(p * LANES, (p + 1) * LANES)
        y = y_ref[:, cols]
        mean = _dot_x2(y, bd)
        dlt = y - mean
        var = _dot(bf(dlt * dlt), bd)
        yn = dlt * lax.rsqrt(var + GN_EPS) * lng_ref[:, cols] + lnb_ref[:, cols]
        rk_sum = _dot(bf(r_ref[0, :, cols] * k_ref[0, :, cols] * rk_ref[:, cols]), bd) * float(hd)
        out = (yn + rk_sum * v_ref[0, :, cols]) * g_ref[0, :, cols]
        o_ref[0, :, cols] = out.astype(o_ref.dtype)


def _rwkv_recurrence(r, lw, k, v, kk, a, g, r_k, ln_g, ln_b, tb, chunk, wblk):
    bsz, seq, d = r.shape
    gw = 4 * RWKV_HEAD
    assert chunk == RWKV_HEAD and wblk % gw == 0
    tok = pl.BlockSpec((1, tb, wblk), lambda b, j, t: (b, t, j))
    vec = pl.BlockSpec((1, wblk), lambda b, j, t: (0, j))
    tri = jnp.tril(jnp.ones((chunk, chunk), F32)).astype(BF16)
    bd = _group_mean_matrix(LANES, RWKV_HEAD)
    return pl.pallas_call(
        functools.partial(_rwkv_rec_kernel, chunk=chunk),
        grid=(bsz, d // wblk, seq // tb),
        in_specs=[tok] * 7 + [vec] * 3 + [_const_spec(tri.shape), _const_spec(bd.shape)],
        out_specs=tok,
        out_shape=jax.ShapeDtypeStruct((bsz, seq, d), BF16),
        scratch_shapes=[pltpu.VMEM((wblk // gw, gw, gw), F32),
                        pltpu.VMEM((tb, wblk), F32)],
        compiler_params=_cparams("arbitrary", "arbitrary", "arbitrary"),
        name="rwkv_recurrence",
    )(r, lw, k, v, kk, a, g, r_k.reshape(1, d), ln_g.reshape(1, d), ln_b.reshape(1, d), tri, bd)


def _odd_layer(x, mod, norm_mix, norm_ffn, w_gate, w_up, w_down, mu, w_r, w_k, w_v, w_o, w0, w1,
               w2, a0, a1, a2, g1, g2, k_k, k_a, r_k, ln_g, ln_b):
    bsz, seq, d = x.shape
    r, lw, k, v, kk, a, g = _rwkv_in(x, mod, norm_mix, mu, w_r, w_k, w_v, w0, w1, w2, a0, a1, a2,
                                     g1, g2, k_k, k_a, _pick_tile(seq, 512))
    yg = _rwkv_recurrence(r, lw, k, v, kk, a, g, r_k, ln_g, ln_b, _pick_tile(seq, 256), 64,
                          d)
    tm = _pick_tile(seq, 512)
    acts = [(yg, pl.BlockSpec((1, tm, d), lambda i, b: (b, i, 0)))]
    return _mix_ffn(x, mod, norm_ffn, acts, [w_o.astype(BF16)], w_gate, w_up, w_down, tm, 256)


def _pick_tile(n, pref):
    t = min(n, pref)
    assert n % t == 0, (n, t)
    return t


def _even_layer(x, mod, tables, lam_init, norm_mix, norm_ffn, w_gate, w_up, w_down, w_in,
                lam_re, lam_im, log_dt, b_re, b_im, c_re, c_im, d_skip, w_glu, q_norm, k_norm,
                lq1, lk1, lq2, lk2, subln, w_out):
    bsz, seq, d = x.shape
    s5w = lam_re.shape[0] * b_re.shape[-1]
    dfw = (w_in.shape[1] - s5w) // 3
    tm = _pick_tile(seq, 512)
    u2, q, k, v = _even_in(x, mod, norm_mix, w_in, q_norm, k_norm, tables, s5w, dfw, tm)
    ys = _s5_mixer(u2, bsz, lam_re, lam_im, log_dt, b_re, b_im,
                   c_re, c_im, d_skip.reshape(-1), w_glu, _pick_tile(seq, 128))
    att = _diff_attention(q, k, v, lq1, lk1, lq2, lk2, subln, lam_init, _pick_tile(seq, 512))
    w_out_bf = w_out.astype(BF16)
    assert s5w == dfw
    acts = [
        (ys, pl.BlockSpec((tm, s5w), lambda i, b: (i, b))),
        (att, pl.BlockSpec((1, tm, dfw), lambda i, b: (b, i, 0))),
    ]
    w_specs = [pl.BlockSpec((s5w, d), lambda i, b: (0, 0), pipeline_mode=pl.Buffered(1)),
               pl.BlockSpec((dfw, d), lambda i, b: (1, 0), pipeline_mode=pl.Buffered(1))]
    return _mix_ffn(x, mod, norm_ffn, acts, [w_out_bf, w_out_bf], w_gate, w_up, w_down, tm, 256,
                    w_specs)


def kernel(x, c, positions, w_ada, b_ada, norm_mix, norm_ffn, ffn_w_gate, ffn_w_up, ffn_w_down,
           ev_w_in, ev_s5_lam_re, ev_s5_lam_im, ev_s5_log_dt, ev_s5_b_re, ev_s5_b_im, ev_s5_c_re,
           ev_s5_c_im, ev_s5_d, ev_s5_w_glu, ev_q_norm, ev_k_norm, ev_lambda_q1, ev_lambda_k1,
           ev_lambda_q2, ev_lambda_k2, ev_subln, ev_w_out, od_mu, od_w_r, od_w_k, od_w_v, od_w_o,
           od_w0, od_w1, od_w2, od_a0, od_a1, od_a2, od_g1, od_g2, od_k_k, od_k_a, od_r_k,
           od_ln_g, od_ln_b):
    depth = w_ada.shape[0]
    mod = _ada_mod(c, w_ada, b_ada)
    tables = _rope_tables(positions)
    for l in range(depth):
        if l % 2 == 0:
            e = l // 2
            lam_init = 0.8 - 0.6 * math.exp(-0.3 * l)
            x = _even_layer(x, mod[l], tables, lam_init, norm_mix[l], norm_ffn[l], ffn_w_gate[l],
                            ffn_w_up[l], ffn_w_down[l], ev_w_in[e], ev_s5_lam_re[e],
                            ev_s5_lam_im[e], ev_s5_log_dt[e], ev_s5_b_re[e], ev_s5_b_im[e],
                            ev_s5_c_re[e], ev_s5_c_im[e], ev_s5_d[e], ev_s5_w_glu[e], ev_q_norm[e],
                            ev_k_norm[e], ev_lambda_q1[e], ev_lambda_k1[e], ev_lambda_q2[e],
                            ev_lambda_k2[e], ev_subln[e], ev_w_out[e])
        else:
            o = l // 2
            x = _odd_layer(x, mod[l], norm_mix[l], norm_ffn[l], ffn_w_gate[l], ffn_w_up[l],
                           ffn_w_down[l], od_mu[o], od_w_r[o], od_w_k[o], od_w_v[o], od_w_o[o],
                           od_w0[o], od_w1[o], od_w2[o], od_a0[o], od_a1[o], od_a2[o], od_g1[o],
                           od_g2[o], od_k_k[o], od_k_a[o], od_r_k[o], od_ln_g[o], od_ln_b[o])
    return x
```

```python
import functools
import math

import jax
import jax.numpy as jnp
from jax import lax
from jax.experimental import pallas as pl
from jax.experimental.pallas import tpu as pltpu

F32 = jnp.float32
BF16 = jnp.bfloat16

RMS_EPS = 1e-6
GN_EPS = 64e-5
ROPE_THETA = 500000.0
DT_HEAD = 64
ROT_DIM = DT_HEAD // 4
S5_GROUP = 16
S5_STATE = 64
RWKV_HEAD = 64
LANES = 128
SUBLANES = 8
S5_MXU_DEPTH = 256
VMEM_LIMIT = 56 * 1024 * 1024


def _cparams(*sem):
    return pltpu.CompilerParams(dimension_semantics=sem, vmem_limit_bytes=VMEM_LIMIT)


def _const_spec(shape):
    nd = len(shape)
    return pl.BlockSpec(shape, lambda *_: (0,) * nd, pipeline_mode=pl.Buffered(1))


def _dot(a, b):
    return jnp.dot(a, b, preferred_element_type=F32)


def _dot_nt(a, b):
    return lax.dot_general(a, b, (((1,), (1,)), ((), ())), preferred_element_type=F32)


def _dot_tn(a, b):
    return lax.dot_general(a, b, (((0,), (0,)), ((), ())), preferred_element_type=F32)


def _split_bf16(x):
    hi = x.astype(BF16)
    lo = (x - hi.astype(F32)).astype(BF16)
    return hi, lo


def _dot_x2(x, w_bf16):
    hi, lo = _split_bf16(x)
    return _dot(hi, w_bf16) + _dot(lo, w_bf16)


def _dot_x2_rhs(w_bf16, x):
    hi, lo = _split_bf16(x)
    return _dot(w_bf16, hi) + _dot(w_bf16, lo)


def _modnorm(x, g, scale, shift):
    ms = jnp.mean(x * x, axis=-1, keepdims=True)
    return (x * lax.rsqrt(ms + RMS_EPS)) * g * (1.0 + scale) + shift


def _ada_kernel(c_ref, w_ref, b_ref, o_ref):
    c = c_ref[...]
    ca = c * jax.nn.sigmoid(c)
    hi, lo = _split_bf16(ca)
    w = w_ref[0]
    w_hi, w_lo = _split_bf16(w)
    o_ref[0] = _dot(hi, w_hi) + _dot(lo, w_hi) + _dot(hi, w_lo) + b_ref[0]


def _ada_mod(c, w_ada, b_ada):
    depth, d, n = w_ada.shape
    bsz = c.shape[0]
    rows = -(-bsz // SUBLANES) * SUBLANES
    c_pad = jnp.zeros((rows, d), F32).at[:bsz].set(c)
    tn = n // 4
    out = pl.pallas_call(
        _ada_kernel,
        grid=(depth, n // tn),
        in_specs=[
            pl.BlockSpec((rows, d), lambda l, j: (0, 0)),
            pl.BlockSpec((1, d, tn), lambda l, j: (l, 0, j)),
            pl.BlockSpec((1, 1, tn), lambda l, j: (l, 0, j)),
        ],
        out_specs=pl.BlockSpec((1, rows, tn), lambda l, j: (l, 0, j)),
        out_shape=jax.ShapeDtypeStruct((depth, rows, n), F32),
        compiler_params=_cparams("arbitrary", "arbitrary"),
        name="ada_mod",
    )(c_pad, w_ada, b_ada.reshape(depth, 1, n))
    mod = out[:, :bsz].reshape(depth, bsz, 6, d)
    return jnp.pad(mod, ((0, 0), (0, 0), (0, SUBLANES - 6), (0, 0)))


def _mul_trig_kernel(p_ref, f_ref, cos_ref, sin_ref):
    a = p_ref[...] * f_ref[...]
    cos_ref[...] = jnp.cos(a)
    sin_ref[...] = jnp.sin(a)


def _rope_tables(positions):
    bsz, seq = positions.shape
    half = ROT_DIM // 2
    inv_freq = ROPE_THETA ** (-jnp.arange(0, ROT_DIM, 2, dtype=F32) / ROT_DIM)
    n = bsz * seq * half
    pos_rep = jnp.broadcast_to(positions.astype(F32)[..., None], (bsz, seq, half))
    frq_rep = jnp.broadcast_to(inv_freq, (bsz, seq, half))
    rows = n // LANES
    tr = min(rows, 512)
    cos, sin = pl.pallas_call(
        _mul_trig_kernel,
        grid=(rows // tr,),
        in_specs=[pl.BlockSpec((tr, LANES), lambda i: (i, 0))] * 2,
        out_specs=[pl.BlockSpec((tr, LANES), lambda i: (i, 0))] * 2,
        out_shape=[jax.ShapeDtypeStruct((rows, LANES), F32)] * 2,
        compiler_params=_cparams("arbitrary"),
        name="rope_trig",
    )(pos_rep.reshape(rows, LANES), frq_rep.reshape(rows, LANES))
    cs = jnp.concatenate([cos.reshape(bsz, seq, half), sin.reshape(bsz, seq, half)], axis=-1)
    return jnp.pad(cs, ((0, 0), (0, 0), (0, LANES - ROT_DIM)))


def _expand_rope(cs):
    half = ROT_DIM // 2
    lane = lax.broadcasted_iota(jnp.int32, cs.shape, 1)
    c0 = jnp.where(lane < half, cs, jnp.where(lane < ROT_DIM, pltpu.roll(cs, half, axis=1), 1.0))
    sa0 = jnp.where(lane < half, -pltpu.roll(cs, LANES - half, axis=1), 0.0)
    sb0 = jnp.where((lane >= half) & (lane < ROT_DIM), cs, 0.0)
    second = lane >= DT_HEAD
    return (jnp.where(second, pltpu.roll(c0, DT_HEAD, axis=1), c0),
            jnp.where(second, pltpu.roll(sa0, DT_HEAD, axis=1), sa0),
            jnp.where(second, pltpu.roll(sb0, DT_HEAD, axis=1), sb0))


def _group_mean_matrix(width, group):
    idx = jnp.arange(width) // group
    return jnp.where(idx[:, None] == idx[None, :], 1.0 / group, 0.0).astype(BF16)


def _even_in_kernel(x_ref, mod_ref, g_ref, w_ref, qn_ref, kn_ref, cs_ref, bd_ref,
                    u_ref, q_ref, k_ref, v_ref, *, s5w, dfw):
    x = x_ref[0]
    h = _modnorm(x, g_ref[...], mod_ref[0, 1:2, :], mod_ref[0, 0:1, :])
    proj = _dot(h.astype(BF16), w_ref[...])
    u_ref[...] = proj[:, :s5w]
    cosv, sav, sbv = _expand_rope(cs_ref[0])
    half = ROT_DIM // 2

    def norm_rope(t, gn, out_ref, post_scale):
        ms = _dot((t * t).astype(BF16), bd_ref[...])
        t = t * lax.rsqrt(ms + RMS_EPS) * gn
        for j in range(dfw // LANES):
            tj = t[:, j * LANES:(j + 1) * LANES]
            up = pltpu.roll(tj, LANES - half, axis=1)
            dn = pltpu.roll(tj, half, axis=1)
            rj = tj * cosv + up * sav + dn * sbv
            out_ref[0, :, j * LANES:(j + 1) * LANES] = (rj * post_scale).astype(out_ref.dtype)

    norm_rope(proj[:, s5w:s5w + dfw], qn_ref[...], q_ref, DT_HEAD ** -0.5 * math.log2(math.e))
    norm_rope(proj[:, s5w + dfw:s5w + 2 * dfw], kn_ref[...], k_ref, 1.0)
    v_ref[0] = proj[:, s5w + 2 * dfw:].astype(v_ref.dtype)


def _even_in(x, mod, g, w_in, q_norm, k_norm, tables, s5w, dfw, tm):
    bsz, seq, d = x.shape
    ncol = w_in.shape[1]
    rep = dfw // DT_HEAD
    kern = functools.partial(_even_in_kernel, s5w=s5w, dfw=dfw)
    tok = lambda w: pl.BlockSpec((1, tm, w), lambda i, b: (b, i, 0))
    return pl.pallas_call(
        kern,
        grid=(seq // tm, bsz),
        in_specs=[
            tok(d),
            pl.BlockSpec((1, SUBLANES, d), lambda i, b: (b, 0, 0)),
            _const_spec((1, d)),
            _const_spec((d, ncol)),
            _const_spec((1, dfw)),
            _const_spec((1, dfw)),
            tok(LANES),
            _const_spec((dfw, dfw)),
        ],
        out_specs=[
            pl.BlockSpec((tm, s5w), lambda i, b: (i, b)),
            tok(dfw), tok(dfw), tok(dfw),
        ],
        out_shape=[
            jax.ShapeDtypeStruct((seq, bsz * s5w), F32),
            jax.ShapeDtypeStruct((bsz, seq, dfw), BF16),
            jax.ShapeDtypeStruct((bsz, seq, dfw), BF16),
            jax.ShapeDtypeStruct((bsz, seq, dfw), BF16),
        ],
        compiler_params=_cparams("arbitrary", "arbitrary"),
        name="even_in_proj",
    )(x, mod, g.reshape(1, d), w_in.astype(BF16),
      jnp.tile(q_norm, rep).reshape(1, dfw), jnp.tile(k_norm, rep).reshape(1, dfw),
      tables, _group_mean_matrix(dfw, DT_HEAD))


def _s5_disc_kernel(lr_ref, li_ref, ldt_ref, br_ref, bi_ref, ar_ref, ai_ref, bbr_ref, bbi_ref,
                    *, bsz):
    lr = lr_ref[...]
    li = li_ref[...]
    dt = jnp.exp(ldt_ref[...])
    mag = jnp.exp(lr * dt)
    abar_r = mag * jnp.cos(li * dt)
    abar_i = mag * jnp.sin(li * dt)
    den = lr * lr + li * li
    num_r = abar_r - 1.0
    q_r = (num_r * lr + abar_i * li) / den
    q_i = (abar_i * lr - num_r * li) / den
    br = br_ref[...]
    bi = bi_ref[...]
    bbr_ref[...] = q_r * br - q_i * bi
    bbi_ref[...] = q_r * bi + q_i * br
    pr, pi = abar_r, abar_i
    for j in range(SUBLANES):
        if j and j % bsz == 0:
            pr, pi = pr * abar_r - pi * abar_i, pr * abar_i + pi * abar_r
        ar_ref[:, j:j + 1] = pr
        ai_ref[:, j:j + 1] = pi


def _s5_discretise(lam_re, lam_im, log_dt, b_re, b_im, bsz):
    g, p = lam_re.shape
    hh = b_re.shape[-1]
    n = g * p
    tab = jax.ShapeDtypeStruct((n, SUBLANES), F32)
    mat = jax.ShapeDtypeStruct((n, hh), F32)
    ldt = jnp.broadcast_to(log_dt[:, None], (g, p)).reshape(n, 1)
    return pl.pallas_call(
        functools.partial(_s5_disc_kernel, bsz=bsz),
        out_shape=[tab, tab, mat, mat],
        name="s5_discretise",
    )(lam_re.reshape(n, 1), lam_im.reshape(n, 1), ldt, b_re.reshape(n, hh), b_im.reshape(n, hh))


def _s5_scan_kernel(u_ref, bre_ref, bim_ref, cre_ref, cim_ref, dsk_ref, a1r_ref, a1i_ref,
                    pwr_ref, pwi_ref, wglu_ref, perm_ref, permt_ref, o_ref, xr_ref, xi_ref,
                    cr_ref, ci_ref, *, bsz, lane_chunk):
    rows, nstate = xr_ref.shape
    step = pl.program_id(0)

    @pl.when(step == 0)
    def _():
        cr_ref[...] = jnp.zeros_like(cr_ref)
        ci_ref[...] = jnp.zeros_like(ci_ref)

    nblk, wpart, spart = bre_ref.shape
    width = wglu_ref.shape[0]
    blk = u_ref[...]
    u_bt = jnp.concatenate([blk[:, b * width:(b + 1) * width] for b in range(bsz)], axis=0)
    u = _dot_x2_rhs(perm_ref[...], u_bt)
    u_bf = u.astype(BF16)
    for j in range(nblk):
        uj = u_bf[:, j * wpart:(j + 1) * wpart]
        xr_ref[:, j * spart:(j + 1) * spart] = _dot(uj, bre_ref[j])
        xi_ref[:, j * spart:(j + 1) * spart] = _dot(uj, bim_ref[j])

    row = lax.broadcasted_iota(jnp.int32, (SUBLANES, lane_chunk), 0)
    steps_per_tile = SUBLANES // bsz
    ntiles = rows // SUBLANES

    for c in range(nstate // lane_chunk):
        cols = pl.ds(c * lane_chunk, lane_chunk)
        a1r = a1r_ref[:, cols]
        a1i = a1i_ref[:, cols]
        pwr = pwr_ref[:, cols]
        pwi = pwi_ref[:, cols]

        def tile_body(i, carry):
            pr, pi = carry
            base = pl.multiple_of(i * SUBLANES, SUBLANES)
            zr = xr_ref[pl.ds(base, SUBLANES), cols]
            zi = xi_ref[pl.ds(base, SUBLANES), cols]
            sh = bsz
            apr, api = a1r, a1i
            for _ in range(steps_per_tile.bit_length() - 1):
                sr = jnp.where(row >= sh, pltpu.roll(zr, sh, axis=0), 0.0)
                si = jnp.where(row >= sh, pltpu.roll(zi, sh, axis=0), 0.0)
                zr, zi = zr + apr * sr - api * si, zi + apr * si + api * sr
                apr, api = apr * apr - api * api, 2.0 * apr * api
                sh *= 2
            last = SUBLANES - bsz
            br_, bi_ = pr, pi
            sh = bsz
            while sh < SUBLANES:
                br_ = jnp.where(row >= last, br_, pltpu.roll(br_, SUBLANES - sh, axis=0))
                bi_ = jnp.where(row >= last, bi_, pltpu.roll(bi_, SUBLANES - sh, axis=0))
                last -= sh
                sh *= 2
            xr = zr + pwr * br_ - pwi * bi_
            xi = zi + pwr * bi_ + pwi * br_
            xr_ref[pl.ds(base, SUBLANES), cols] = xr
            xi_ref[pl.ds(base, SUBLANES), cols] = xi
            return xr, xi

        fr, fi = lax.fori_loop(0, ntiles, tile_body, (cr_ref[:, cols], ci_ref[:, cols]))
        cr_ref[:, cols] = fr
        ci_ref[:, cols] = fi

    y = jnp.concatenate(
        [_dot(xr_ref[:, j * spart:(j + 1) * spart].astype(BF16), cre_ref[j])
         - _dot(xi_ref[:, j * spart:(j + 1) * spart].astype(BF16), cim_ref[j])
         for j in range(nblk)], axis=1) + dsk_ref[...] * u
    y = jax.nn.gelu(y)
    gate = jax.nn.sigmoid(_dot(y.astype(BF16), wglu_ref[...]))
    out_bt = _dot(permt_ref[...], (y * gate).astype(BF16))
    tchunk = rows // bsz
    for b in range(bsz):
        o_ref[:, b * width:(b + 1) * width] = out_bt[b * tchunk:(b + 1) * tchunk].astype(o_ref.dtype)


def _s5_mixer(u2, bsz, lam_re, lam_im, log_dt, b_re, b_im, c_re, c_im, d_skip, w_glu, tchunk):
    seq = u2.shape[0]
    width = u2.shape[1] // bsz
    g, p = lam_re.shape
    hh = b_re.shape[-1]
    nstate = g * p
    pw_r, pw_i, bbar_r, bbar_i = _s5_discretise(lam_re, lam_im, log_dt, b_re, b_im, bsz)
    gb = max(1, min(g, S5_MXU_DEPTH // hh))
    nblk = g // gb
    eye = jnp.eye(gb, dtype=F32)
    def in_map(bb):
        return jnp.einsum('jgph,gk->jghkp', bb.reshape(nblk, gb, p, hh), eye).reshape(
            nblk, gb * hh, gb * p)
    def out_map(cc):
        return jnp.einsum('jghp,gk->jgpkh', cc.reshape(nblk, gb, hh, p), eye).reshape(
            nblk, gb * p, gb * hh)
    pw_r = pw_r.T
    pw_i = pw_i.T
    a1r = jnp.broadcast_to(pw_r[0:1], (SUBLANES, nstate))
    a1i = jnp.broadcast_to(pw_i[0:1], (SUBLANES, nstate))
    rows = tchunk * bsz
    src = jnp.arange(rows)
    perm = (src[None, :] == ((src % bsz) * tchunk + src // bsz)[:, None]).astype(BF16)
    kern = functools.partial(_s5_scan_kernel, bsz=bsz, lane_chunk=512)
    return pl.pallas_call(
        kern,
        grid=(seq // tchunk,),
        in_specs=[
            pl.BlockSpec((tchunk, bsz * width), lambda i: (i, 0)),
            _const_spec((nblk, gb * hh, gb * p)), _const_spec((nblk, gb * hh, gb * p)),
            _const_spec((nblk, gb * p, gb * hh)), _const_spec((nblk, gb * p, gb * hh)),
            _const_spec((1, width)),
            _const_spec((SUBLANES, nstate)), _const_spec((SUBLANES, nstate)),
            _const_spec((SUBLANES, nstate)), _const_spec((SUBLANES, nstate)),
            _const_spec((width, width)),
            _const_spec((rows, rows)), _const_spec((rows, rows)),
        ],
        out_specs=pl.BlockSpec((tchunk, bsz * width), lambda i: (i, 0)),
        out_shape=jax.ShapeDtypeStruct((seq, bsz * width), BF16),
        scratch_shapes=[
            pltpu.VMEM((rows, nstate), F32), pltpu.VMEM((rows, nstate), F32),
            pltpu.VMEM((SUBLANES, nstate), F32), pltpu.VMEM((SUBLANES, nstate), F32),
        ],
        compiler_params=_cparams("arbitrary"),
        name="s5_scan_glu",
    )(u2, in_map(bbar_r).astype(BF16), in_map(bbar_i).astype(BF16),
      out_map(c_re).astype(BF16), out_map(c_im).astype(BF16),
      d_skip.reshape(1, width), a1r, a1i, pw_r, pw_i, w_glu.astype(BF16), perm, perm.T)


def _diff_attn_kernel(lam_ref, sub_ref, q_ref, k_ref, v_ref, o_ref, q2_ref, m_ref, acc_ref,
                      *, tq, lam_init):
    qi = pl.program_id(2)
    vdim = v_ref.shape[-1]
    q = q_ref[0]
    lane = lax.broadcasted_iota(jnp.int32, q.shape, 1)
    zero = jnp.zeros_like(q)
    q2_ref[:tq, :] = jnp.where(lane < DT_HEAD, q, zero)
    q2_ref[tq:, :] = jnp.where(lane >= DT_HEAD, q, zero)
    m_ref[...] = jnp.full(m_ref.shape, -jnp.inf, F32)
    acc_ref[...] = jnp.zeros_like(acc_ref)
    all_rows = ((0, 2 * tq),)

    def gather(ref, row_slices):
        parts = [ref[a:b, :] for a, b in row_slices]
        return parts[0] if len(parts) == 1 else jnp.concatenate(parts, axis=0)

    def scores(kstart, ksize, row_slices=all_rows):
        start = pl.multiple_of(kstart, ksize)
        return _dot_nt(gather(q2_ref, row_slices), k_ref[0, pl.ds(start, ksize), :])

    def update(kstart, ksize, s, row_slices=all_rows, mask=None):
        start = pl.multiple_of(kstart, ksize)
        v_ext = jnp.concatenate([v_ref[0, pl.ds(start, ksize), :], jnp.ones((ksize, vdim), BF16)],
                                axis=1)
        if mask is not None:
            s = jnp.where(mask, s, -jnp.inf)
        part = s[:, :LANES]
        for t in range(1, ksize // LANES):
            part = jnp.maximum(part, s[:, t * LANES:(t + 1) * LANES])
        m_old = gather(m_ref, row_slices)
        m_new = jnp.maximum(m_old, jnp.max(part, axis=-1, keepdims=True))
        alpha = jnp.exp2(m_old - m_new)
        p = jnp.concatenate([jnp.exp2(s[:, t * LANES:(t + 1) * LANES] - m_new)
                             for t in range(ksize // LANES)], axis=1)
        acc = (jnp.concatenate([alpha] * (2 * vdim // LANES), axis=1) * gather(acc_ref, row_slices)
               + _dot(p.astype(BF16), v_ext))
        off = 0
        for a, b in row_slices:
            m_ref[a:b, :] = m_new[off:off + b - a]
            acc_ref[a:b, :] = acc[off:off + b - a]
            off += b - a

    def diagonal(kstart, with_previous):
        hq = tq // 2
        if with_previous:
            s_prev = scores(kstart - tq, tq)
        row_l = lax.broadcasted_iota(jnp.int32, (2 * tq, hq), 0)
        col_l = lax.broadcasted_iota(jnp.int32, (2 * tq, hq), 1)
        row_r = lax.broadcasted_iota(jnp.int32, (tq, hq), 0)
        col_r = lax.broadcasted_iota(jnp.int32, (tq, hq), 1)
        late_rows = ((hq, tq), (tq + hq, 2 * tq))
        s_left = scores(kstart, hq)
        s_right = scores(kstart + hq, hq, late_rows)
        if with_previous:
            update(kstart - tq, tq, s_prev)
        update(kstart, hq, s_left, mask=col_l <= row_l % tq)
        update(kstart + hq, hq, s_right, late_rows, col_r <= row_r % hq)

    def pair(j0):
        s0 = scores(j0 * tq, tq)
        s1 = scores((j0 + 1) * tq, tq)
        update(j0 * tq, tq, s0)
        update((j0 + 1) * tq, tq, s1)

    npairs = qi // 2
    lax.fori_loop(0, npairs, lambda jj, _: (pair(2 * jj), 0)[1], 0)

    @pl.when(qi % 2 == 1)
    def _():
        diagonal(qi * tq, True)

    @pl.when(qi % 2 == 0)
    def _():
        diagonal(qi * tq, False)

    lv = lam_ref[...]
    lam = (jnp.exp(jnp.sum(lv[0:1] * lv[1:2], axis=-1, keepdims=True))
           - jnp.exp(jnp.sum(lv[2:3] * lv[3:4], axis=-1, keepdims=True)) + lam_init)
    acc = acc_ref[...]
    o = (acc[:tq, :vdim] / acc[:tq, vdim:]) - lam * (acc[tq:, :vdim] / acc[tq:, vdim:])
    ms = jnp.mean(o * o, axis=-1, keepdims=True)
    o = o * lax.rsqrt(ms + RMS_EPS) * sub_ref[...] * (1.0 - lam_init)
    o_ref[0] = o.astype(o_ref.dtype)


def _diff_attention(q, k, v, lq1, lk1, lq2, lk2, subln, lam_init, tq):
    bsz, seq, dfw = q.shape
    vdim = 2 * DT_HEAD
    heads = dfw // vdim
    lamv = jnp.zeros((SUBLANES, LANES), F32)
    for i, t in enumerate((lq1, lk1, lq2, lk2)):
        lamv = lamv.at[i, :t.shape[0]].set(t)
    kern = functools.partial(_diff_attn_kernel, tq=tq, lam_init=lam_init)
    kv_spec = pl.BlockSpec((1, seq, vdim), lambda b, h, i: (b, 0, h))
    return pl.pallas_call(
        kern,
        grid=(bsz, heads, seq // tq),
        in_specs=[
            _const_spec((SUBLANES, LANES)),
            _const_spec((1, vdim)),
            pl.BlockSpec((1, tq, vdim), lambda b, h, i: (b, i, h)),
            kv_spec, kv_spec,
        ],
        out_specs=pl.BlockSpec((1, tq, vdim), lambda b, h, i: (b, i, h)),
        out_shape=jax.ShapeDtypeStruct((bsz, seq, dfw), BF16),
        scratch_shapes=[pltpu.VMEM((2 * tq, vdim), BF16), pltpu.VMEM((2 * tq, LANES), F32),
                        pltpu.VMEM((2 * tq, 2 * vdim), F32)],
        compiler_params=_cparams("arbitrary", "arbitrary", "arbitrary"),
        name="diff_attention",
    )(lamv, subln.reshape(1, vdim), q, k, v)


def _mix_ffn_kernel(*refs, n_in, hid_chunk):
    x_ref, mod_ref, g_ref = refs[:3]
    act_refs = refs[3:3 + n_in]
    w_refs = refs[3 + n_in:3 + 2 * n_in]
    wg_ref, wu_ref, wd_ref, o_ref = refs[3 + 2 * n_in:]
    mix = None
    for a_ref, w_ref in zip(act_refs, w_refs):
        a = a_ref[0] if len(a_ref.shape) == 3 else a_ref[...]
        t = _dot(a, w_ref[...])
        mix = t if mix is None else mix + t
    x1 = x_ref[0] + mod_ref[0, 2:3, :] * mix
    h = _modnorm(x1, g_ref[...], mod_ref[0, 4:5, :], mod_ref[0, 3:4, :]).astype(BF16)
    acc = None
    for j in range(wg_ref.shape[2] // hid_chunk):
        sl = slice(j * hid_chunk, (j + 1) * hid_chunk)
        gate = _dot(h, wg_ref[0, :, sl])
        up = _dot(h, wu_ref[0, :, sl])
        act = (gate * jax.nn.sigmoid(gate) * up).astype(BF16)
        t = _dot(act, wd_ref[0, sl, :])
        acc = t if acc is None else acc + t
    o_ref[0] = x1 + mod_ref[0, 5:6, :] * acc


def _mix_ffn(x, mod, g, acts, weights, ffn, tm, hid_chunk, w_specs=None):
    bsz, seq, d = x.shape
    if w_specs is None:
        w_specs = [_const_spec(w.shape) for w in weights]
    wg, wu, wd, layer = ffn
    assert wg.shape[2] % hid_chunk == 0

    def layer_spec(w):
        return pl.BlockSpec((1,) + w.shape[1:], lambda i, b: (layer, 0, 0),
                            pipeline_mode=pl.Buffered(1))

    tok = pl.BlockSpec((1, tm, d), lambda i, b: (b, i, 0))
    return pl.pallas_call(
        functools.partial(_mix_ffn_kernel, n_in=len(acts), hid_chunk=hid_chunk),
        grid=(seq // tm, bsz),
        in_specs=[tok, pl.BlockSpec((1, SUBLANES, d), lambda i, b: (b, 0, 0)), _const_spec((1, d))]
                 + [spec for _, spec in acts]
                 + w_specs
                 + [layer_spec(wg), layer_spec(wu), layer_spec(wd)],
        out_specs=tok,
        out_shape=jax.ShapeDtypeStruct((bsz, seq, d), F32),
        compiler_params=_cparams("arbitrary", "arbitrary"),
        name="mix_ffn",
    )(x, mod, g.reshape(1, d), *[a for a, _ in acts], *weights, wg, wu, wd)


def _rwkv_in_kernel(x_ref, xp_ref, mod_ref, g_ref, mu_ref, wr_ref, wk_ref, wv_ref, w1_ref, w2_ref,
                    a1_ref, a2_ref, g1_ref, g2_ref, w0_ref, a0_ref, kkw_ref, kaw_ref, bd_ref,
                    r_out, lw_out, k_out, v_out, kk_out, a_out, g_out):
    i = pl.program_id(1)
    g = g_ref[...]
    scale = mod_ref[0, 1:2, :]
    shift = mod_ref[0, 0:1, :]
    h = _modnorm(x_ref[0], g, scale, shift)
    hp = _modnorm(xp_ref[0][SUBLANES - 1:SUBLANES, :], g, scale, shift)
    hp = jnp.where(i == 0, 0.0, hp)
    row = lax.broadcasted_iota(jnp.int32, h.shape, 0)
    h_prev = jnp.where(row == 0, hp, pltpu.roll(h, 1, axis=0))
    dx = h_prev - h

    def lerp(j):
        return (h + dx * mu_ref[j:j + 1, :]).astype(BF16)

    r_out[0] = _dot(lerp(0), wr_ref[...])
    wl = jnp.tanh(_dot(lerp(1), w1_ref[...]))
    wdec = w0_ref[...] + _dot(wl.astype(BF16), w2_ref[...])
    w = -jax.nn.softplus(-wdec) - 0.5
    lw_out[0] = -jnp.exp(w)
    k = _dot(lerp(2), wk_ref[...])
    v_out[0] = _dot(lerp(3), wv_ref[...])
    al = _dot(lerp(4), a1_ref[...])
    a = jax.nn.sigmoid(a0_ref[...] + _dot(al.astype(BF16), a2_ref[...]))
    a_out[0] = a
    gl = jax.nn.sigmoid(_dot(lerp(5), g1_ref[...]))
    g_out[0] = _dot(gl.astype(BF16), g2_ref[...])
    kk = k * kkw_ref[...]
    bd = bd_ref[...]
    for j in range(kk.shape[1] // LANES):
        kj = kk[:, j * LANES:(j + 1) * LANES]
        ss = _dot((kj * kj).astype(BF16), bd) * float(RWKV_HEAD)
        kk_out[0, :, j * LANES:(j + 1) * LANES] = kj / jnp.maximum(jnp.sqrt(ss), 1e-12)
    k_out[0] = k * (1.0 + (a - 1.0) * kaw_ref[...])


def _rwkv_in(x, mod, g, mu, w_r, w_k, w_v, w0, w1, w2, a0, a1, a2, g1, g2, k_k, k_a, tm):
    bsz, seq, d = x.shape
    tok = pl.BlockSpec((1, tm, d), lambda b, i: (b, i, 0))
    prev = pl.BlockSpec((1, SUBLANES, d),
                        lambda b, i: (b, jnp.maximum(i * (tm // SUBLANES) - 1, 0), 0))
    bf = lambda w: w.astype(BF16)
    vec = lambda t: t.reshape(1, d)
    consts = [vec(g), mu, bf(w_r), bf(w_k), bf(w_v), bf(w1), bf(w2), bf(a1), bf(a2), bf(g1), bf(g2),
              vec(w0), vec(a0), vec(k_k), vec(k_a), _group_mean_matrix(LANES, RWKV_HEAD)]
    out = jax.ShapeDtypeStruct((bsz, seq, d), F32)
    return pl.pallas_call(
        _rwkv_in_kernel,
        grid=(bsz, seq // tm),
        in_specs=[tok, prev, pl.BlockSpec((1, SUBLANES, d), lambda b, i: (b, 0, 0))]
                 + [_const_spec(t.shape) for t in consts],
        out_specs=[tok] * 7,
        out_shape=[out] * 7,
        compiler_params=_cparams("arbitrary", "arbitrary"),
        name="rwkv_in_proj",
    )(x, x, mod, *consts)


def _rwkv_rec_kernel(r_ref, lw_ref, k_ref, v_ref, kk_ref, a_ref, g_ref, rk_ref, lng_ref, lnb_ref,
                     tri_ref, bd_ref, o_ref, m_ref, y_ref, *, chunk):
    tstep = pl.program_id(2)

    @pl.when(tstep == 0)
    def _():
        m_ref[...] = jnp.zeros_like(m_ref)

    tb, width = r_ref.shape[1:]
    hd = RWKV_HEAD
    gw = m_ref.shape[-1]
    nh = gw // hd
    groups = range(width // gw)
    colsl = [slice(q * gw, (q + 1) * gw) for q in groups]
    lane = lax.broadcasted_iota(jnp.int32, (chunk, gw), 1)
    trow = lax.broadcasted_iota(jnp.int32, (chunk, gw), 0)
    jpos = lane % hd
    strict = jpos < trow
    incl = jpos <= trow
    eye = jnp.where(jpos == trow, 1.0, 0.0)
    head_of_lane = lane // hd
    in_head = [head_of_lane == h for h in range(nh)]
    sq_r = lax.broadcasted_iota(jnp.int32, (gw, gw), 0)
    sq_c = lax.broadcasted_iota(jnp.int32, (gw, gw), 1)
    same_head = (sq_r // hd) == (sq_c // hd)
    diag = sq_r == sq_c
    tri = tri_ref[...]
    bd = bd_ref[...]
    bf = lambda t: t.astype(BF16)

    def blockdiag(y):
        return bf(jnp.concatenate([jnp.where(in_head[h], y, 0.0) for h in range(nh)], axis=0))

    def blockdiag_t(x):
        xt = jnp.concatenate([x] * nh, axis=0).T
        return bf(jnp.where(same_head, xt, 0.0))

    inst = [(c, q) for c in range(tb // chunk) for q in groups]
    cls = []
    for c, q in inst:
        lw = lw_ref[0, c * chunk:(c + 1) * chunk, colsl[q]]
        hi = bf(lw)
        r1 = lw - hi.astype(F32)
        mid = bf(r1)
        lo = bf(r1 - mid.astype(F32))
        cls.append((lw, _dot(tri, hi) + _dot(tri, mid) + _dot(tri, lo)))
    opnd = []
    for (c, q), (lw, cl) in zip(inst, cls):
        rows = slice(c * chunk, (c + 1) * chunk)
        cols = colsl[q]
        k = k_ref[0, rows, cols]
        kk = kk_ref[0, rows, cols]
        b = kk * a_ref[0, rows, cols]
        v = v_ref[0, rows, cols]
        cl_end = cl[chunk - 1:chunk, :]
        p_inv = jnp.exp(-cl)
        p_tail = jnp.exp(cl_end - cl)
        ar = jnp.concatenate([bf(-kk * jnp.exp(cl - lw)), bf(r_ref[0, rows, cols] * jnp.exp(cl))],
                             axis=0)
        tail_t = bf(jnp.concatenate([b * p_tail, k * p_tail], axis=0).T)
        pl_full = jnp.broadcast_to(jnp.exp(cl_end), (gw, gw))
        pl_col = jnp.sum(jnp.where(diag, pl_full, 0.0), axis=1, keepdims=True)
        opnd.append((ar, blockdiag_t(b * p_inv), blockdiag_t(k * p_inv), bf(v), blockdiag(v),
                     tail_t, pl_col))
    g_b = [_dot(o[0], o[1]) for o in opnd]
    g_k = [_dot(o[0], o[2]) for o in opnd]
    nms = [jnp.where(strict, g[:chunk], 0.0) for g in g_b]
    tinv = [eye + nm for nm in nms]
    pw = [bf(_dot(bf(nm), blockdiag(nm))) for nm in nms]
    for _ in range(chunk.bit_length() - 3):
        both = [_dot(jnp.concatenate([bf(t), p2], axis=0), blockdiag(p2)) for t, p2 in zip(tinv, pw)]
        tinv = [t + bo[:chunk] for t, bo in zip(tinv, both)]
        pw = [bf(bo[chunk:]) for bo in both]
    tinv = [bf(t + _dot(bf(t), blockdiag(p2))) for t, p2 in zip(tinv, pw)]
    g_ak = [bf(jnp.where(strict, g[:chunk], 0.0)) for g in g_k]
    g_rb = [bf(jnp.where(incl, g[chunk:], 0.0)) for g in g_b]
    g_rk = [bf(jnp.where(incl, g[chunk:], 0.0)) for g in g_k]

    state = [m_ref[q] for q in groups]
    for c in range(tb // chunk):
        ids = [c * len(groups) + q for q in groups]
        m0_bf = [bf(state[q]) for q in groups]
        rhs = [_dot(jnp.concatenate([opnd[i][0][:chunk], g_ak[i]], axis=1),
                    jnp.concatenate([m0_bf[q], opnd[i][4]], axis=0)) for q, i in zip(groups, ids)]
        u = [_dot(tinv[i], blockdiag(rhs[q])) for q, i in zip(groups, ids)]
        for q, i in zip(groups, ids):
            y_ref[c * chunk:(c + 1) * chunk, colsl[q]] = _dot(
                jnp.concatenate([opnd[i][0][chunk:], g_rb[i], g_rk[i]], axis=1),
                jnp.concatenate([m0_bf[q], blockdiag(u[q]), opnd[i][4]], axis=0))
        for q, i in zip(groups, ids):
            upd = _dot(opnd[i][5], jnp.concatenate([bf(u[q]), opnd[i][3]], axis=0))
            state[q] = opnd[i][6] * state[q] + jnp.where(same_head, upd, 0.0)
    for q in groups:
        m_ref[q] = state[q]

    npair = width // LANES
    for p in range(npair):
        cols = slice(p * LANES, (p + 1) * LANES)
        y = y_ref[:, cols]
        mean = _dot_x2(y, bd)
        dlt = y - mean
        var = _dot(bf(dlt * dlt), bd)
        yn = dlt * lax.rsqrt(var + GN_EPS) * lng_ref[:, cols] + lnb_ref[:, cols]
        rk_sum = _dot(bf(r_ref[0, :, cols] * k_ref[0, :, cols] * rk_ref[:, cols]), bd) * float(hd)
        out = (yn + rk_sum * v_ref[0, :, cols]) * g_ref[0, :, cols]
        o_ref[0, :, cols] = out.astype(o_ref.dtype)


def _rwkv_recurrence(r, lw, k, v, kk, a, g, r_k, ln_g, ln_b, tb, chunk, wblk):
    bsz, seq, d = r.shape
    gw = 2 * RWKV_HEAD
    assert chunk == RWKV_HEAD and wblk % gw == 0
    tok = pl.BlockSpec((1, tb, wblk), lambda b, j, t: (b, t, j))
    vec = pl.BlockSpec((1, wblk), lambda b, j, t: (0, j))
    tri = jnp.tril(jnp.ones((chunk, chunk), F32)).astype(BF16)
    bd = _group_mean_matrix(LANES, RWKV_HEAD)
    return pl.pallas_call(
        functools.partial(_rwkv_rec_kernel, chunk=chunk),
        grid=(bsz, d // wblk, seq // tb),
        in_specs=[tok] * 7 + [vec] * 3 + [_const_spec(tri.shape), _const_spec(bd.shape)],
        out_specs=tok,
        out_shape=jax.ShapeDtypeStruct((bsz, seq, d), BF16),
        scratch_shapes=[pltpu.VMEM((wblk // gw, gw, gw), F32),
                        pltpu.VMEM((tb, wblk), F32)],
        compiler_params=_cparams("arbitrary", "arbitrary", "arbitrary"),
        name="rwkv_recurrence",
    )(r, lw, k, v, kk, a, g, r_k.reshape(1, d), ln_g.reshape(1, d), ln_b.reshape(1, d), tri, bd)


def _odd_layer(x, mod, norm_mix, norm_ffn, ffn, mu, w_r, w_k, w_v, w_o, w0, w1,
               w2, a0, a1, a2, g1, g2, k_k, k_a, r_k, ln_g, ln_b):
    bsz, seq, d = x.shape
    r, lw, k, v, kk, a, g = _rwkv_in(x, mod, norm_mix, mu, w_r, w_k, w_v, w0, w1, w2, a0, a1, a2,
                                     g1, g2, k_k, k_a, _pick_tile(seq, 512))
    yg = _rwkv_recurrence(r, lw, k, v, kk, a, g, r_k, ln_g, ln_b, _pick_tile(seq, 256), 64,
                          d)
    tm = _pick_tile(seq, 512)
    acts = [(yg, pl.BlockSpec((1, tm, d), lambda i, b: (b, i, 0)))]
    return _mix_ffn(x, mod, norm_ffn, acts, [w_o.astype(BF16)], ffn, tm, 256)


def _pick_tile(n, pref):
    t = min(n, pref)
    assert n % t == 0, (n, t)
    return t


def _even_layer(x, mod, tables, lam_init, norm_mix, norm_ffn, ffn, w_in,
                lam_re, lam_im, log_dt, b_re, b_im, c_re, c_im, d_skip, w_glu, q_norm, k_norm,
                lq1, lk1, lq2, lk2, subln, w_out):
    bsz, seq, d = x.shape
    s5w = lam_re.shape[0] * b_re.shape[-1]
    dfw = (w_in.shape[1] - s5w) // 3
    tm = _pick_tile(seq, 512)
    u2, q, k, v = _even_in(x, mod, norm_mix, w_in, q_norm, k_norm, tables, s5w, dfw, tm)
    ys = _s5_mixer(u2, bsz, lam_re, lam_im, log_dt, b_re, b_im,
                   c_re, c_im, d_skip.reshape(-1), w_glu, _pick_tile(seq, 128))
    att = _diff_attention(q, k, v, lq1, lk1, lq2, lk2, subln, lam_init, _pick_tile(seq, 512))
    w_out_bf = w_out.astype(BF16)
    assert s5w == dfw
    acts = [
        (ys, pl.BlockSpec((tm, s5w), lambda i, b: (i, b))),
        (att, pl.BlockSpec((1, tm, dfw), lambda i, b: (b, i, 0))),
    ]
    w_specs = [pl.BlockSpec((s5w, d), lambda i, b: (0, 0), pipeline_mode=pl.Buffered(1)),
               pl.BlockSpec((dfw, d), lambda i, b: (1, 0), pipeline_mode=pl.Buffered(1))]
    return _mix_ffn(x, mod, norm_ffn, acts, [w_out_bf, w_out_bf], ffn, tm, 256, w_specs)


def kernel(x, c, positions, w_ada, b_ada, norm_mix, norm_ffn, ffn_w_gate, ffn_w_up, ffn_w_down,
           ev_w_in, ev_s5_lam_re, ev_s5_lam_im, ev_s5_log_dt, ev_s5_b_re, ev_s5_b_im, ev_s5_c_re,
           ev_s5_c_im, ev_s5_d, ev_s5_w_glu, ev_q_norm, ev_k_norm, ev_lambda_q1, ev_lambda_k1,
           ev_lambda_q2, ev_lambda_k2, ev_subln, ev_w_out, od_mu, od_w_r, od_w_k, od_w_v, od_w_o,
           od_w0, od_w1, od_w2, od_a0, od_a1, od_a2, od_g1, od_g2, od_k_k, od_k_a, od_r_k,
           od_ln_g, od_ln_b):
    depth = w_ada.shape[0]
    mod = _ada_mod(c, w_ada, b_ada)
    tables = _rope_tables(positions)
    ffn_bf = (ffn_w_gate.astype(BF16), ffn_w_up.astype(BF16), ffn_w_down.astype(BF16))
    for l in range(depth):
        ffn = ffn_bf + (l,)
        if l % 2 == 0:
            e = l // 2
            lam_init = 0.8 - 0.6 * math.exp(-0.3 * l)
            x = _even_layer(x, mod[l], tables, lam_init, norm_mix[l], norm_ffn[l], ffn,
                            ev_w_in[e], ev_s5_lam_re[e],
                            ev_s5_lam_im[e], ev_s5_log_dt[e], ev_s5_b_re[e], ev_s5_b_im[e],
                            ev_s5_c_re[e], ev_s5_c_im[e], ev_s5_d[e], ev_s5_w_glu[e], ev_q_norm[e],
                            ev_k_norm[e], ev_lambda_q1[e], ev_lambda_k1[e], ev_lambda_q2[e],
                            ev_lambda_k2[e], ev_subln[e], ev_w_out[e])
        else:
            o = l // 2
            x = _odd_layer(x, mod[l], norm_mix[l], norm_ffn[l], ffn,
                           od_mu[o], od_w_r[o], od_w_k[o], od_w_v[o], od_w_o[o],
                           od_w0[o], od_w1[o], od_w2[o], od_a0[o], od_a1[o], od_a2[o], od_g1[o],
                           od_g2[o], od_k_k[o], od_k_a[o], od_r_k[o], od_ln_g[o], od_ln_b[o])
    return x
```

```python
import functools
import math

import jax
import jax.numpy as jnp
from jax import lax
from jax.experimental import pallas as pl
from jax.experimental.pallas import tpu as pltpu

F32 = jnp.float32
BF16 = jnp.bfloat16

RMS_EPS = 1e-6
GN_EPS = 64e-5
ROPE_THETA = 500000.0
DT_HEAD = 64
ROT_DIM = DT_HEAD // 4
RWKV_HEAD = 64
LANES = 128
SUBLANES = 8
VMEM_LIMIT = 56 * 1024 * 1024


def _cparams(*sem):
    return pltpu.CompilerParams(dimension_semantics=sem, vmem_limit_bytes=VMEM_LIMIT)


def _const_spec(shape):
    nd = len(shape)
    return pl.BlockSpec(shape, lambda *_: (0,) * nd, pipeline_mode=pl.Buffered(1))


def _dot(a, b):
    return jnp.dot(a, b, preferred_element_type=F32)


def _dot_nt(a, b):
    return lax.dot_general(a, b, (((1,), (1,)), ((), ())), preferred_element_type=F32)


def _split_bf16(x):
    hi = x.astype(BF16)
    lo = (x - hi.astype(F32)).astype(BF16)
    return hi, lo


def _dot_x2(x, w_bf16):
    hi, lo = _split_bf16(x)
    return _dot(hi, w_bf16) + _dot(lo, w_bf16)


def _dot_x2_rhs(w_bf16, x):
    hi, lo = _split_bf16(x)
    return _dot(w_bf16, hi) + _dot(w_bf16, lo)


def _modnorm(x, g, scale, shift):
    ms = jnp.mean(x * x, axis=-1, keepdims=True)
    return (x * lax.rsqrt(ms + RMS_EPS)) * g * (1.0 + scale) + shift


def _ada_kernel(c_ref, w_ref, b_ref, o_ref):
    c = c_ref[...]
    ca = c * jax.nn.sigmoid(c)
    hi, lo = _split_bf16(ca)
    w = w_ref[0]
    w_hi, w_lo = _split_bf16(w)
    o_ref[0] = _dot(hi, w_hi) + _dot(lo, w_hi) + _dot(hi, w_lo) + b_ref[0]


def _ada_mod(c, w_ada, b_ada):
    depth, d, n = w_ada.shape
    bsz = c.shape[0]
    rows = -(-bsz // SUBLANES) * SUBLANES
    c_pad = jnp.zeros((rows, d), F32).at[:bsz].set(c)
    tn = n // 4
    out = pl.pallas_call(
        _ada_kernel,
        grid=(depth, n // tn),
        in_specs=[
            pl.BlockSpec((rows, d), lambda l, j: (0, 0)),
            pl.BlockSpec((1, d, tn), lambda l, j: (l, 0, j)),
            pl.BlockSpec((1, 1, tn), lambda l, j: (l, 0, j)),
        ],
        out_specs=pl.BlockSpec((1, rows, tn), lambda l, j: (l, 0, j)),
        out_shape=jax.ShapeDtypeStruct((depth, rows, n), F32),
        compiler_params=_cparams("arbitrary", "arbitrary"),
        name="ada_mod",
    )(c_pad, w_ada, b_ada.reshape(depth, 1, n))
    mod = out[:, :bsz].reshape(depth, bsz, 6, d)
    return jnp.pad(mod, ((0, 0), (0, 0), (0, SUBLANES - 6), (0, 0)))


def _mul_trig_kernel(p_ref, f_ref, cos_ref, sin_ref):
    a = p_ref[...] * f_ref[...]
    cos_ref[...] = jnp.cos(a)
    sin_ref[...] = jnp.sin(a)


def _rope_tables(positions):
    bsz, seq = positions.shape
    half = ROT_DIM // 2
    inv_freq = ROPE_THETA ** (-jnp.arange(0, ROT_DIM, 2, dtype=F32) / ROT_DIM)
    n = bsz * seq * half
    pos_rep = jnp.broadcast_to(positions.astype(F32)[..., None], (bsz, seq, half))
    frq_rep = jnp.broadcast_to(inv_freq, (bsz, seq, half))
    rows = n // LANES
    tr = min(rows, 512)
    cos, sin = pl.pallas_call(
        _mul_trig_kernel,
        grid=(rows // tr,),
        in_specs=[pl.BlockSpec((tr, LANES), lambda i: (i, 0))] * 2,
        out_specs=[pl.BlockSpec((tr, LANES), lambda i: (i, 0))] * 2,
        out_shape=[jax.ShapeDtypeStruct((rows, LANES), F32)] * 2,
        compiler_params=_cparams("arbitrary"),
        name="rope_trig",
    )(pos_rep.reshape(rows, LANES), frq_rep.reshape(rows, LANES))
    cs = jnp.concatenate([cos.reshape(bsz, seq, half), sin.reshape(bsz, seq, half)], axis=-1)
    return jnp.pad(cs, ((0, 0), (0, 0), (0, LANES - ROT_DIM)))


def _expand_rope(cs):
    half = ROT_DIM // 2
    lane = lax.broadcasted_iota(jnp.int32, cs.shape, 1)
    c0 = jnp.where(lane < half, cs, jnp.where(lane < ROT_DIM, pltpu.roll(cs, half, axis=1), 1.0))
    sa0 = jnp.where(lane < half, -pltpu.roll(cs, LANES - half, axis=1), 0.0)
    sb0 = jnp.where((lane >= half) & (lane < ROT_DIM), cs, 0.0)
    second = lane >= DT_HEAD
    return (jnp.where(second, pltpu.roll(c0, DT_HEAD, axis=1), c0),
            jnp.where(second, pltpu.roll(sa0, DT_HEAD, axis=1), sa0),
            jnp.where(second, pltpu.roll(sb0, DT_HEAD, axis=1), sb0))


def _group_mean_matrix(width, group):
    idx = jnp.arange(width) // group
    return jnp.where(idx[:, None] == idx[None, :], 1.0 / group, 0.0).astype(BF16)


def _even_in_kernel(x_ref, mod_ref, g_ref, w_ref, qn_ref, kn_ref, cs_ref, bd_ref,
                    u_ref, q_ref, k_ref, v_ref, *, s5w, dfw):
    x = x_ref[0]
    h = _modnorm(x, g_ref[...], mod_ref[0, 1:2, :], mod_ref[0, 0:1, :])
    proj = _dot(h.astype(BF16), w_ref[...])
    u_ref[...] = proj[:, :s5w]
    cosv, sav, sbv = _expand_rope(cs_ref[0])
    half = ROT_DIM // 2

    def norm_rope(t, gn, out_ref, post_scale):
        ms = _dot((t * t).astype(BF16), bd_ref[...])
        t = t * lax.rsqrt(ms + RMS_EPS) * gn
        for j in range(dfw // LANES):
            tj = t[:, j * LANES:(j + 1) * LANES]
            up = pltpu.roll(tj, LANES - half, axis=1)
            dn = pltpu.roll(tj, half, axis=1)
            rj = tj * cosv + up * sav + dn * sbv
            out_ref[0, :, j * LANES:(j + 1) * LANES] = (rj * post_scale).astype(out_ref.dtype)

    norm_rope(proj[:, s5w:s5w + dfw], qn_ref[...], q_ref, DT_HEAD ** -0.5 * math.log2(math.e))
    norm_rope(proj[:, s5w + dfw:s5w + 2 * dfw], kn_ref[...], k_ref, 1.0)
    v_ref[0] = proj[:, s5w + 2 * dfw:].astype(v_ref.dtype)


def _even_in(x, mod, g, w_in, q_norm, k_norm, tables, s5w, dfw, tm):
    bsz, seq, d = x.shape
    ncol = w_in.shape[1]
    rep = dfw // DT_HEAD
    kern = functools.partial(_even_in_kernel, s5w=s5w, dfw=dfw)
    tok = lambda w: pl.BlockSpec((1, tm, w), lambda i, b: (b, i, 0))
    return pl.pallas_call(
        kern,
        grid=(seq // tm, bsz),
        in_specs=[
            tok(d),
            pl.BlockSpec((1, SUBLANES, d), lambda i, b: (b, 0, 0)),
            _const_spec((1, d)),
            _const_spec((d, ncol)),
            _const_spec((1, dfw)),
            _const_spec((1, dfw)),
            tok(LANES),
            _const_spec((dfw, dfw)),
        ],
        out_specs=[
            pl.BlockSpec((tm, s5w), lambda i, b: (i, b)),
            tok(dfw), tok(dfw), tok(dfw),
        ],
        out_shape=[
            jax.ShapeDtypeStruct((seq, bsz * s5w), F32),
            jax.ShapeDtypeStruct((bsz, seq, dfw), BF16),
            jax.ShapeDtypeStruct((bsz, seq, dfw), BF16),
            jax.ShapeDtypeStruct((bsz, seq, dfw), BF16),
        ],
        compiler_params=_cparams("arbitrary", "arbitrary"),
        name="even_in_proj",
    )(x, mod, g.reshape(1, d), w_in.astype(BF16),
      jnp.tile(q_norm, rep).reshape(1, dfw), jnp.tile(k_norm, rep).reshape(1, dfw),
      tables, _group_mean_matrix(dfw, DT_HEAD))


S5_CHUNK = 8


def _s5_maps_kernel(lr_ref, li_ref, ldt_ref, lrc_ref, lic_ref, ldtc_ref, cr_ref, ci_ref,
                    lrt_ref, lit_ref, ldtt_ref, br_ref, bi_ref,
                    k_out, wr_out, wi_out, car_out, cai_out, a8r_out, a8i_out):
    def abar(lr, li, ldt):
        dt = jnp.exp(ldt)
        mag = jnp.exp(lr * dt)
        return mag * jnp.cos(li * dt), mag * jnp.sin(li * dt)

    def cmul(ar, ai, br, bi):
        return ar * br - ai * bi, ar * bi + ai * br

    def mm3(a, b):
        a_hi, a_lo = _split_bf16(a)
        b_hi, b_lo = _split_bf16(b)
        return _dot(a_hi, b_hi) + _dot(a_hi, b_lo) + _dot(a_lo, b_hi)

    lr, li = lrt_ref[...], lit_ref[...]
    a_r, a_i = abar(lr, li, ldtt_ref[...])
    den = lr * lr + li * li
    num_r = a_r - 1.0
    q_r = (num_r * lr + a_i * li) / den
    q_i = (a_i * lr - num_r * li) / den
    bb_r, bb_i = cmul(q_r, q_i, br_ref[...], bi_ref[...])
    pr, pi = bb_r, bb_i
    for i in range(S5_CHUNK - 1, -1, -1):
        wr_out[i] = pr
        wi_out[i] = pi
        if i:
            pr, pi = cmul(a_r, a_i, pr, pi)
    c_ar, c_ai = abar(lrc_ref[...], lic_ref[...], ldtc_ref[...])
    car, cai = cr_ref[...], ci_ref[...]
    for k in range(S5_CHUNK + 1):
        if k:
            car, cai = cmul(car, cai, c_ar, c_ai)
            car_out[k - 1] = car
            cai_out[k - 1] = cai
        if k < S5_CHUNK:
            k_out[k] = mm3(car, bb_r) - mm3(cai, bb_i)
    e_r, e_i = abar(lr_ref[...], li_ref[...], ldt_ref[...])
    for _ in range(S5_CHUNK.bit_length() - 1):
        e_r, e_i = cmul(e_r, e_i, e_r, e_i)
    pr, pi = e_r, e_i
    for s in range(a8r_out.shape[0]):
        if s:
            pr, pi = cmul(pr, pi, e_r, e_i)
        a8r_out[s] = pr
        a8i_out[s] = pi


def _s5_chunk_kernel(u_ref, kmat_ref, wre_ref, wim_ref, mre_ref, mim_ref, dsk_ref, a1r_ref, a1i_ref,
                     pwr_ref, pwi_ref, wglu_ref, perm_ref, permt_ref, o_ref, xr_ref, xi_ref,
                     cr_ref, ci_ref, *, bsz, sub, lane_chunk):
    nrow, nstate = xr_ref.shape
    tchunk = u_ref.shape[0]
    width = wglu_ref.shape[0]
    nsub = tchunk // sub
    rps = sub // S5_CHUNK * bsz
    nblk, gl = kmat_ref.shape[0], kmat_ref.shape[1] // S5_CHUNK
    spart = wre_ref.shape[2]
    step = pl.program_id(0)

    @pl.when(step == 0)
    def _():
        cr_ref[...] = jnp.zeros_like(cr_ref)
        ci_ref[...] = jnp.zeros_like(ci_ref)

    blk = u_ref[...]
    parts = []
    for s in range(nsub):
        u_bt = jnp.concatenate([blk[s * sub:(s + 1) * sub, b * width:(b + 1) * width]
                                for b in range(bsz)], axis=0)
        parts.append(_dot_x2_rhs(perm_ref[...], u_bt))
    u_j = [jnp.concatenate([p[j * rps:(j + 1) * rps] for p in parts], axis=0)
           for j in range(S5_CHUNK)]
    u_jb = [t.astype(BF16) for t in u_j]
    u_g = [jnp.concatenate([t[:, g * gl:(g + 1) * gl] for t in u_jb], axis=1)
           for g in range(nblk)]

    for g in range(nblk):
        xr_ref[:, g * spart:(g + 1) * spart] = _dot(u_g[g], wre_ref[g])
        xi_ref[:, g * spart:(g + 1) * spart] = _dot(u_g[g], wim_ref[g])

    old_r = cr_ref[...]
    old_i = ci_ref[...]
    row = lax.broadcasted_iota(jnp.int32, (SUBLANES, lane_chunk), 0)
    steps_per_tile = SUBLANES // bsz
    ntiles = nrow // SUBLANES
    for c in range(nstate // lane_chunk):
        cols = pl.ds(c * lane_chunk, lane_chunk)
        a1r = a1r_ref[:, cols]
        a1i = a1i_ref[:, cols]
        pwr = pwr_ref[:, cols]
        pwi = pwi_ref[:, cols]

        def tile_body(i, carry):
            pr, pi = carry
            base = pl.multiple_of(i * SUBLANES, SUBLANES)
            zr = xr_ref[pl.ds(base, SUBLANES), cols]
            zi = xi_ref[pl.ds(base, SUBLANES), cols]
            sh = bsz
            apr, api = a1r, a1i
            for _ in range(steps_per_tile.bit_length() - 1):
                sr = jnp.where(row >= sh, pltpu.roll(zr, sh, axis=0), 0.0)
                si = jnp.where(row >= sh, pltpu.roll(zi, sh, axis=0), 0.0)
                zr, zi = zr + apr * sr - api * si, zi + apr * si + api * sr
                apr, api = apr * apr - api * api, 2.0 * apr * api
                sh *= 2
            last = SUBLANES - bsz
            br_, bi_ = pr, pi
            sh = bsz
            while sh < SUBLANES:
                br_ = jnp.where(row >= last, br_, pltpu.roll(br_, SUBLANES - sh, axis=0))
                bi_ = jnp.where(row >= last, bi_, pltpu.roll(bi_, SUBLANES - sh, axis=0))
                last -= sh
                sh *= 2
            xr = zr + pwr * br_ - pwi * bi_
            xi = zi + pwr * bi_ + pwi * br_
            xr_ref[pl.ds(base, SUBLANES), cols] = xr
            xi_ref[pl.ds(base, SUBLANES), cols] = xi
            return xr, xi

        fr, fi = lax.fori_loop(0, ntiles, tile_body, (cr_ref[:, cols], ci_ref[:, cols]))
        cr_ref[:, cols] = fr
        ci_ref[:, cols] = fi

    row8 = lax.broadcasted_iota(jnp.int32, (SUBLANES, nstate), 0)

    def state_in(x, old):
        xs = pltpu.roll(x, bsz, axis=0)
        first = jnp.where(row8 < bsz, pltpu.roll(old, bsz, axis=0), xs[:SUBLANES])
        return jnp.concatenate([first, xs[SUBLANES:]], axis=0).astype(BF16)

    xin_r = state_in(xr_ref[...], old_r)
    xin_i = state_in(xi_ref[...], old_i)

    y_g = [_dot(u_g[g], kmat_ref[g])
           + _dot(xin_r[:, g * spart:(g + 1) * spart], mre_ref[g])
           + _dot(xin_i[:, g * spart:(g + 1) * spart], mim_ref[g]) for g in range(nblk)]
    y = jnp.concatenate(
        [jnp.concatenate([y_g[g][:, t * gl:(t + 1) * gl] for g in range(nblk)], axis=1)
         + dsk_ref[...] * u_j[t] for t in range(S5_CHUNK)], axis=0)
    y = jax.nn.gelu(y)
    gate = jax.nn.sigmoid(_dot(y.astype(BF16), wglu_ref[...]))
    out = (y * gate).astype(BF16)
    for s in range(nsub):
        o_s = jnp.concatenate([out[t * nrow + s * rps:t * nrow + (s + 1) * rps]
                               for t in range(S5_CHUNK)], axis=0)
        o_bt = _dot(permt_ref[...], o_s)
        for b in range(bsz):
            o_ref[s * sub:(s + 1) * sub, b * width:(b + 1) * width] = (
                o_bt[b * sub:(b + 1) * sub].astype(o_ref.dtype))


def _s5_mixer_chunked(u2, bsz, lam_re, lam_im, log_dt, b_re, b_im, c_re, c_im, d_skip, w_glu,
                      tchunk):
    seq = u2.shape[0]
    width = u2.shape[1] // bsz
    g, p = lam_re.shape
    hh = b_re.shape[-1]
    gh = g * hh
    nstate = g * p
    rr = S5_CHUNK
    ldt = jnp.broadcast_to(log_dt[:, None], (g, p))
    rep_c = lambda t: jnp.repeat(t, hh, axis=0)
    rep_t = lambda t: jnp.repeat(t.T, hh, axis=1)
    to_t = lambda b: b.transpose(1, 0, 2).reshape(p, gh)
    f32 = lambda *s: jax.ShapeDtypeStruct(s, F32)
    spt = SUBLANES // bsz
    k_all, wt_r, wt_i, ca_r, ca_i, a8_r, a8_i = pl.pallas_call(
        _s5_maps_kernel,
        out_shape=[f32(rr, gh, gh), f32(rr, p, gh), f32(rr, p, gh), f32(rr, gh, p), f32(rr, gh, p),
                   f32(spt, g, p), f32(spt, g, p)],
        compiler_params=pltpu.CompilerParams(vmem_limit_bytes=VMEM_LIMIT),
        name="s5_chunk_maps",
    )(lam_re, lam_im, ldt, rep_c(lam_re), rep_c(lam_im), rep_c(ldt),
      c_re.reshape(gh, p), c_im.reshape(gh, p),
      rep_t(lam_re), rep_t(lam_im), rep_t(ldt), to_t(b_re), to_t(b_im))
    pw_r = jnp.repeat(a8_r.reshape(spt, nstate), bsz, axis=0)
    pw_i = jnp.repeat(a8_i.reshape(spt, nstate), bsz, axis=0)
    a1r = jnp.broadcast_to(pw_r[0:1], (SUBLANES, nstate))
    a1i = jnp.broadcast_to(pw_i[0:1], (SUBLANES, nstate))

    gb = LANES // hh
    nblk = g // gb
    eye = jnp.eye(gb, dtype=F32)
    kd = jnp.einsum('kgagb->kgab', k_all.reshape(rr, g, hh, g, hh))
    lag = jnp.arange(rr)[None, :] - jnp.arange(rr)[:, None]
    kt = jnp.where((lag >= 0)[:, :, None, None, None], kd[jnp.clip(lag, 0, rr - 1)], 0.0)
    kmat = jnp.einsum('jtbgca,gd->bjgatdc', kt.reshape(rr, rr, nblk, gb, hh, hh), eye).reshape(
        nblk, rr * gb * hh, rr * gb * hh)
    def w_map(wt):
        return jnp.einsum('ipbgh,gd->bighdp', wt.reshape(rr, p, nblk, gb, hh), eye).reshape(
            nblk, rr * gb * hh, gb * p)
    def m_map(ca):
        return jnp.einsum('tbgap,gd->bgptda', ca.reshape(rr, nblk, gb, hh, p), eye).reshape(
            nblk, gb * p, rr * gb * hh)
    consts = [kmat.astype(BF16), w_map(wt_r).astype(BF16), w_map(wt_i).astype(BF16),
              m_map(ca_r).astype(BF16), m_map(-ca_i).astype(BF16), d_skip.reshape(1, width),
              a1r, a1i, pw_r, pw_i, w_glu.astype(BF16)]

    sub = min(tchunk, 128)
    rows = sub * bsz
    dst = jnp.arange(rows)
    rps = sub // rr * bsz
    j_, rem = dst // rps, dst % rps
    src_of = (rem % bsz) * sub + (rem // bsz) * rr + j_
    perm = (jnp.arange(rows)[None, :] == src_of[:, None]).astype(BF16)
    nrow = tchunk // rr * bsz
    kern = functools.partial(_s5_chunk_kernel, bsz=bsz, sub=sub, lane_chunk=512)
    return pl.pallas_call(
        kern,
        grid=(seq // tchunk,),
        in_specs=[pl.BlockSpec((tchunk, bsz * width), lambda i: (i, 0))]
                 + [_const_spec(t.shape) for t in consts]
                 + [_const_spec((rows, rows)), _const_spec((rows, rows))],
        out_specs=pl.BlockSpec((tchunk, bsz * width), lambda i: (i, 0)),
        out_shape=jax.ShapeDtypeStruct((seq, bsz * width), BF16),
        scratch_shapes=[
            pltpu.VMEM((nrow, nstate), F32), pltpu.VMEM((nrow, nstate), F32),
            pltpu.VMEM((SUBLANES, nstate), F32), pltpu.VMEM((SUBLANES, nstate), F32),
        ],
        compiler_params=_cparams("arbitrary"),
        name="s5_chunked",
    )(u2, *consts, perm, perm.T)


def _diff_attn_kernel(lam_ref, sub_ref, q_ref, k_ref, v_ref, o_ref, q2_ref, m_ref, acc_ref,
                      *, tq, lam_init):
    qi = pl.program_id(2)
    vdim = v_ref.shape[-1]
    q = q_ref[0]
    lane = lax.broadcasted_iota(jnp.int32, q.shape, 1)
    zero = jnp.zeros_like(q)
    q2_ref[:tq, :] = jnp.where(lane < DT_HEAD, q, zero)
    q2_ref[tq:, :] = jnp.where(lane >= DT_HEAD, q, zero)
    m_ref[...] = jnp.full(m_ref.shape, -jnp.inf, F32)
    acc_ref[...] = jnp.zeros_like(acc_ref)
    all_rows = ((0, 2 * tq),)

    def gather(ref, row_slices):
        parts = [ref[a:b, :] for a, b in row_slices]
        return parts[0] if len(parts) == 1 else jnp.concatenate(parts, axis=0)

    def scores(kstart, ksize, row_slices=all_rows):
        start = pl.multiple_of(kstart, ksize)
        return _dot_nt(gather(q2_ref, row_slices), k_ref[0, pl.ds(start, ksize), :])

    def update(kstart, ksize, s, row_slices=all_rows, mask=None):
        start = pl.multiple_of(kstart, ksize)
        v_ext = jnp.concatenate([v_ref[0, pl.ds(start, ksize), :], jnp.ones((ksize, vdim), BF16)],
                                axis=1)
        if mask is not None:
            s = jnp.where(mask, s, -jnp.inf)
        part = s[:, :LANES]
        for t in range(1, ksize // LANES):
            part = jnp.maximum(part, s[:, t * LANES:(t + 1) * LANES])
        m_old = gather(m_ref, row_slices)
        m_new = jnp.maximum(m_old, jnp.max(part, axis=-1, keepdims=True))
        alpha = jnp.exp2(m_old - m_new)
        p = jnp.concatenate([jnp.exp2(s[:, t * LANES:(t + 1) * LANES] - m_new)
                             for t in range(ksize // LANES)], axis=1)
        acc = (jnp.concatenate([alpha] * (2 * vdim // LANES), axis=1) * gather(acc_ref, row_slices)
               + _dot(p.astype(BF16), v_ext))
        off = 0
        for a, b in row_slices:
            m_ref[a:b, :] = m_new[off:off + b - a]
            acc_ref[a:b, :] = acc[off:off + b - a]
            off += b - a

    def diagonal(kstart, with_previous):
        hq = tq // 2
        if with_previous:
            s_prev = scores(kstart - tq, tq)
        row_l = lax.broadcasted_iota(jnp.int32, (2 * tq, hq), 0)
        col_l = lax.broadcasted_iota(jnp.int32, (2 * tq, hq), 1)
        row_r = lax.broadcasted_iota(jnp.int32, (tq, hq), 0)
        col_r = lax.broadcasted_iota(jnp.int32, (tq, hq), 1)
        late_rows = ((hq, tq), (tq + hq, 2 * tq))
        s_left = scores(kstart, hq)
        s_right = scores(kstart + hq, hq, late_rows)
        if with_previous:
            update(kstart - tq, tq, s_prev)
        update(kstart, hq, s_left, mask=col_l <= row_l % tq)
        update(kstart + hq, hq, s_right, late_rows, col_r <= row_r % hq)

    def pair(j0):
        s0 = scores(j0 * tq, tq)
        s1 = scores((j0 + 1) * tq, tq)
        update(j0 * tq, tq, s0)
        update((j0 + 1) * tq, tq, s1)

    npairs = qi // 2
    lax.fori_loop(0, npairs, lambda jj, _: (pair(2 * jj), 0)[1], 0)

    @pl.when(qi % 2 == 1)
    def _():
        diagonal(qi * tq, True)

    @pl.when(qi % 2 == 0)
    def _():
        diagonal(qi * tq, False)

    lv = lam_ref[...]
    lam = (jnp.exp(jnp.sum(lv[0:1] * lv[1:2], axis=-1, keepdims=True))
           - jnp.exp(jnp.sum(lv[2:3] * lv[3:4], axis=-1, keepdims=True)) + lam_init)
    acc = acc_ref[...]
    o = (acc[:tq, :vdim] / acc[:tq, vdim:]) - lam * (acc[tq:, :vdim] / acc[tq:, vdim:])
    ms = jnp.mean(o * o, axis=-1, keepdims=True)
    o = o * lax.rsqrt(ms + RMS_EPS) * sub_ref[...] * (1.0 - lam_init)
    o_ref[0] = o.astype(o_ref.dtype)


def _diff_attention(q, k, v, lq1, lk1, lq2, lk2, subln, lam_init, tq):
    bsz, seq, dfw = q.shape
    vdim = 2 * DT_HEAD
    heads = dfw // vdim
    lamv = jnp.zeros((SUBLANES, LANES), F32)
    for i, t in enumerate((lq1, lk1, lq2, lk2)):
        lamv = lamv.at[i, :t.shape[0]].set(t)
    kern = functools.partial(_diff_attn_kernel, tq=tq, lam_init=lam_init)
    kv_spec = pl.BlockSpec((1, seq, vdim), lambda b, h, i: (b, 0, h))
    return pl.pallas_call(
        kern,
        grid=(bsz, heads, seq // tq),
        in_specs=[
            _const_spec((SUBLANES, LANES)),
            _const_spec((1, vdim)),
            pl.BlockSpec((1, tq, vdim), lambda b, h, i: (b, i, h)),
            kv_spec, kv_spec,
        ],
        out_specs=pl.BlockSpec((1, tq, vdim), lambda b, h, i: (b, i, h)),
        out_shape=jax.ShapeDtypeStruct((bsz, seq, dfw), BF16),
        scratch_shapes=[pltpu.VMEM((2 * tq, vdim), BF16), pltpu.VMEM((2 * tq, LANES), F32),
                        pltpu.VMEM((2 * tq, 2 * vdim), F32)],
        compiler_params=_cparams("arbitrary", "arbitrary", "arbitrary"),
        name="diff_attention",
    )(lamv, subln.reshape(1, vdim), q, k, v)


def _mix_ffn_kernel(*refs, n_in, hid_chunk):
    x_ref, mod_ref, g_ref = refs[:3]
    act_refs = refs[3:3 + n_in]
    w_refs = refs[3 + n_in:3 + 2 * n_in]
    wg_ref, wu_ref, wd_ref, o_ref = refs[3 + 2 * n_in:]
    mix = None
    for a_ref, w_ref in zip(act_refs, w_refs):
        a = a_ref[0] if len(a_ref.shape) == 3 else a_ref[...]
        t = _dot(a, w_ref[...])
        mix = t if mix is None else mix + t
    x1 = x_ref[0] + mod_ref[0, 2:3, :] * mix
    h = _modnorm(x1, g_ref[...], mod_ref[0, 4:5, :], mod_ref[0, 3:4, :]).astype(BF16)
    acc = None
    for j in range(wg_ref.shape[2] // hid_chunk):
        sl = slice(j * hid_chunk, (j + 1) * hid_chunk)
        gate = _dot(h, wg_ref[0, :, sl])
        up = _dot(h, wu_ref[0, :, sl])
        act = (gate * jax.nn.sigmoid(gate) * up).astype(BF16)
        t = _dot(act, wd_ref[0, sl, :])
        acc = t if acc is None else acc + t
    o_ref[0] = x1 + mod_ref[0, 5:6, :] * acc


def _mix_ffn(x, mod, g, acts, weights, ffn, tm, hid_chunk, w_specs=None):
    bsz, seq, d = x.shape
    if w_specs is None:
        w_specs = [_const_spec(w.shape) for w in weights]
    wg, wu, wd, layer = ffn
    assert wg.shape[2] % hid_chunk == 0

    def layer_spec(w):
        return pl.BlockSpec((1,) + w.shape[1:], lambda i, b: (layer, 0, 0),
                            pipeline_mode=pl.Buffered(1))

    tok = pl.BlockSpec((1, tm, d), lambda i, b: (b, i, 0))
    return pl.pallas_call(
        functools.partial(_mix_ffn_kernel, n_in=len(acts), hid_chunk=hid_chunk),
        grid=(seq // tm, bsz),
        in_specs=[tok, pl.BlockSpec((1, SUBLANES, d), lambda i, b: (b, 0, 0)), _const_spec((1, d))]
                 + [spec for _, spec in acts]
                 + w_specs
                 + [layer_spec(wg), layer_spec(wu), layer_spec(wd)],
        out_specs=tok,
        out_shape=jax.ShapeDtypeStruct((bsz, seq, d), F32),
        compiler_params=_cparams("arbitrary", "arbitrary"),
        name="mix_ffn",
    )(x, mod, g.reshape(1, d), *[a for a, _ in acts], *weights, wg, wu, wd)


def _rwkv_in_kernel(x_ref, xp_ref, mod_ref, g_ref, mu_ref, wr_ref, wk_ref, wv_ref, w1_ref, w2_ref,
                    a1_ref, a2_ref, g1_ref, g2_ref, w0_ref, a0_ref, kkw_ref, kaw_ref, bd_ref,
                    r_out, lw_out, k_out, v_out, kk_out, a_out, g_out):
    i = pl.program_id(1)
    g = g_ref[...]
    scale = mod_ref[0, 1:2, :]
    shift = mod_ref[0, 0:1, :]
    h = _modnorm(x_ref[0], g, scale, shift)
    hp = _modnorm(xp_ref[0][SUBLANES - 1:SUBLANES, :], g, scale, shift)
    hp = jnp.where(i == 0, 0.0, hp)
    row = lax.broadcasted_iota(jnp.int32, h.shape, 0)
    h_prev = jnp.where(row == 0, hp, pltpu.roll(h, 1, axis=0))
    dx = h_prev - h

    def lerp(j):
        return (h + dx * mu_ref[j:j + 1, :]).astype(BF16)

    r_out[0] = _dot(lerp(0), wr_ref[...])
    wl = jnp.tanh(_dot(lerp(1), w1_ref[...]))
    wdec = w0_ref[...] + _dot(wl.astype(BF16), w2_ref[...])
    w = -jax.nn.softplus(-wdec) - 0.5
    lw_out[0] = -jnp.exp(w)
    k = _dot(lerp(2), wk_ref[...])
    v_out[0] = _dot(lerp(3), wv_ref[...])
    al = _dot(lerp(4), a1_ref[...])
    a = jax.nn.sigmoid(a0_ref[...] + _dot(al.astype(BF16), a2_ref[...]))
    a_out[0] = a
    gl = jax.nn.sigmoid(_dot(lerp(5), g1_ref[...]))
    g_out[0] = _dot(gl.astype(BF16), g2_ref[...])
    kk = k * kkw_ref[...]
    bd = bd_ref[...]
    for j in range(kk.shape[1] // LANES):
        kj = kk[:, j * LANES:(j + 1) * LANES]
        ss = _dot((kj * kj).astype(BF16), bd) * float(RWKV_HEAD)
        kk_out[0, :, j * LANES:(j + 1) * LANES] = kj / jnp.maximum(jnp.sqrt(ss), 1e-12)
    k_out[0] = k * (1.0 + (a - 1.0) * kaw_ref[...])


def _rwkv_in(x, mod, g, mu, w_r, w_k, w_v, w0, w1, w2, a0, a1, a2, g1, g2, k_k, k_a, tm):
    bsz, seq, d = x.shape
    tok = pl.BlockSpec((1, tm, d), lambda b, i: (b, i, 0))
    prev = pl.BlockSpec((1, SUBLANES, d),
                        lambda b, i: (b, jnp.maximum(i * (tm // SUBLANES) - 1, 0), 0))
    bf = lambda w: w.astype(BF16)
    vec = lambda t: t.reshape(1, d)
    consts = [vec(g), mu, bf(w_r), bf(w_k), bf(w_v), bf(w1), bf(w2), bf(a1), bf(a2), bf(g1), bf(g2),
              vec(w0), vec(a0), vec(k_k), vec(k_a), _group_mean_matrix(LANES, RWKV_HEAD)]
    out = jax.ShapeDtypeStruct((bsz, seq, d), F32)
    return pl.pallas_call(
        _rwkv_in_kernel,
        grid=(bsz, seq // tm),
        in_specs=[tok, prev, pl.BlockSpec((1, SUBLANES, d), lambda b, i: (b, 0, 0))]
                 + [_const_spec(t.shape) for t in consts],
        out_specs=[tok] * 7,
        out_shape=[out] * 7,
        compiler_params=_cparams("arbitrary", "arbitrary"),
        name="rwkv_in_proj",
    )(x, x, mod, *consts)


def _rwkv_rec_kernel(r_ref, lw_ref, k_ref, v_ref, kk_ref, a_ref, g_ref, rk_ref, lng_ref, lnb_ref,
                     tri_ref, bd_ref, o_ref, m_ref, y_ref, *, chunk):
    tstep = pl.program_id(2)

    @pl.when(tstep == 0)
    def _():
        m_ref[...] = jnp.zeros_like(m_ref)

    tb, width = r_ref.shape[1:]
    hd = RWKV_HEAD
    gw = m_ref.shape[-1]
    nh = gw // hd
    groups = range(width // gw)
    colsl = [slice(q * gw, (q + 1) * gw) for q in groups]
    lane = lax.broadcasted_iota(jnp.int32, (chunk, gw), 1)
    trow = lax.broadcasted_iota(jnp.int32, (chunk, gw), 0)
    jpos = lane % hd
    strict = jpos < trow
    incl = jpos <= trow
    eye = jnp.where(jpos == trow, 1.0, 0.0)
    head_of_lane = lane // hd
    in_head = [head_of_lane == h for h in range(nh)]
    sq_r = lax.broadcasted_iota(jnp.int32, (gw, gw), 0)
    sq_c = lax.broadcasted_iota(jnp.int32, (gw, gw), 1)
    same_head = (sq_r // hd) == (sq_c // hd)
    diag = sq_r == sq_c
    tri = tri_ref[...]
    bd = bd_ref[...]
    bf = lambda t: t.astype(BF16)

    def blockdiag(y):
        return bf(jnp.concatenate([jnp.where(in_head[h], y, 0.0) for h in range(nh)], axis=0))

    def blockdiag_t(x):
        xt = jnp.concatenate([x] * nh, axis=0).T
        return bf(jnp.where(same_head, xt, 0.0))

    inst = [(c, q) for c in range(tb // chunk) for q in groups]
    cls = []
    for c, q in inst:
        lw = lw_ref[0, c * chunk:(c + 1) * chunk, colsl[q]]
        hi = bf(lw)
        r1 = lw - hi.astype(F32)
        mid = bf(r1)
        lo = bf(r1 - mid.astype(F32))
        cls.append((lw, _dot(tri, hi) + _dot(tri, mid) + _dot(tri, lo)))
    opnd = []
    for (c, q), (lw, cl) in zip(inst, cls):
        rows = slice(c * chunk, (c + 1) * chunk)
        cols = colsl[q]
        k = k_ref[0, rows, cols]
        kk = kk_ref[0, rows, cols]
        b = kk * a_ref[0, rows, cols]
        v = v_ref[0, rows, cols]
        cl_end = cl[chunk - 1:chunk, :]
        p_inv = jnp.exp(-cl)
        p_tail = jnp.exp(cl_end - cl)
        ar = jnp.concatenate([bf(-kk * jnp.exp(cl - lw)), bf(r_ref[0, rows, cols] * jnp.exp(cl))],
                             axis=0)
        tail_t = bf(jnp.concatenate([b * p_tail, k * p_tail], axis=0).T)
        pl_full = jnp.broadcast_to(jnp.exp(cl_end), (gw, gw))
        pl_col = jnp.sum(jnp.where(diag, pl_full, 0.0), axis=1, keepdims=True)
        opnd.append((ar, blockdiag_t(b * p_inv), blockdiag_t(k * p_inv), bf(v), blockdiag(v),
                     tail_t, pl_col))
    g_b = [_dot(o[0], o[1]) for o in opnd]
    g_k = [_dot(o[0], o[2]) for o in opnd]
    nms = [jnp.where(strict, g[:chunk], 0.0) for g in g_b]
    tinv = [eye + nm for nm in nms]
    pw = [bf(_dot(bf(nm), blockdiag(nm))) for nm in nms]
    for _ in range(chunk.bit_length() - 3):
        both = [_dot(jnp.concatenate([bf(t), p2], axis=0), blockdiag(p2)) for t, p2 in zip(tinv, pw)]
        tinv = [t + bo[:chunk] for t, bo in zip(tinv, both)]
        pw = [bf(bo[chunk:]) for bo in both]
    tinv = [bf(t + _dot(bf(t), blockdiag(p2))) for t, p2 in zip(tinv, pw)]
    g_ak = [bf(jnp.where(strict, g[:chunk], 0.0)) for g in g_k]
    g_rb = [bf(jnp.where(incl, g[chunk:], 0.0)) for g in g_b]
    g_rk = [bf(jnp.where(incl, g[chunk:], 0.0)) for g in g_k]

    state = [m_ref[q] for q in groups]
    for c in range(tb // chunk):
        ids = [c * len(groups) + q for q in groups]
        m0_bf = [bf(state[q]) for q in groups]
        rhs = [_dot(jnp.concatenate([opnd[i][0][:chunk], g_ak[i]], axis=1),
                    jnp.concatenate([m0_bf[q], opnd[i][4]], axis=0)) for q, i in zip(groups, ids)]
        u = [_dot(tinv[i], blockdiag(rhs[q])) for q, i in zip(groups, ids)]
        for q, i in zip(groups, ids):
            y_ref[c * chunk:(c + 1) * chunk, colsl[q]] = _dot(
                jnp.concatenate([opnd[i][0][chunk:], g_rb[i], g_rk[i]], axis=1),
                jnp.concatenate([m0_bf[q], blockdiag(u[q]), opnd[i][4]], axis=0))
        for q, i in zip(groups, ids):
            upd = _dot(opnd[i][5], jnp.concatenate([bf(u[q]), opnd[i][3]], axis=0))
            state[q] = opnd[i][6] * state[q] + jnp.where(same_head, upd, 0.0)
    for q in groups:
        m_ref[q] = state[q]

    npair = width // LANES
    for p in range(npair):
        cols = slice(p * LANES, (p + 1) * LANES)
        y = y_ref[:, cols]
        mean = _dot_x2(y, bd)
        dlt = y - mean
        var = _dot(bf(dlt * dlt), bd)
        yn = dlt * lax.rsqrt(var + GN_EPS) * lng_ref[:, cols] + lnb_ref[:, cols]
        rk_sum = _dot(bf(r_ref[0, :, cols] * k_ref[0, :, cols] * rk_ref[:, cols]), bd) * float(hd)
        out = (yn + rk_sum * v_ref[0, :, cols]) * g_ref[0, :, cols]
        o_ref[0, :, cols] = out.astype(o_ref.dtype)


def _rwkv_recurrence(r, lw, k, v, kk, a, g, r_k, ln_g, ln_b, tb, chunk, wblk):
    bsz, seq, d = r.shape
    gw = 2 * RWKV_HEAD
    assert chunk == RWKV_HEAD and wblk % gw == 0
    tok = pl.BlockSpec((1, tb, wblk), lambda b, j, t: (b, t, j))
    vec = pl.BlockSpec((1, wblk), lambda b, j, t: (0, j))
    tri = jnp.tril(jnp.ones((chunk, chunk), F32)).astype(BF16)
    bd = _group_mean_matrix(LANES, RWKV_HEAD)
    return pl.pallas_call(
        functools.partial(_rwkv_rec_kernel, chunk=chunk),
        grid=(bsz, d // wblk, seq // tb),
        in_specs=[tok] * 7 + [vec] * 3 + [_const_spec(tri.shape), _const_spec(bd.shape)],
        out_specs=tok,
        out_shape=jax.ShapeDtypeStruct((bsz, seq, d), BF16),
        scratch_shapes=[pltpu.VMEM((wblk // gw, gw, gw), F32),
                        pltpu.VMEM((tb, wblk), F32)],
        compiler_params=_cparams("arbitrary", "arbitrary", "arbitrary"),
        name="rwkv_recurrence",
    )(r, lw, k, v, kk, a, g, r_k.reshape(1, d), ln_g.reshape(1, d), ln_b.reshape(1, d), tri, bd)


def _odd_layer(x, mod, norm_mix, norm_ffn, ffn, mu, w_r, w_k, w_v, w_o, w0, w1,
               w2, a0, a1, a2, g1, g2, k_k, k_a, r_k, ln_g, ln_b):
    bsz, seq, d = x.shape
    r, lw, k, v, kk, a, g = _rwkv_in(x, mod, norm_mix, mu, w_r, w_k, w_v, w0, w1, w2, a0, a1, a2,
                                     g1, g2, k_k, k_a, _pick_tile(seq, 512))
    yg = _rwkv_recurrence(r, lw, k, v, kk, a, g, r_k, ln_g, ln_b, _pick_tile(seq, 256), 64,
                          d)
    tm = _pick_tile(seq, 512)
    acts = [(yg, pl.BlockSpec((1, tm, d), lambda i, b: (b, i, 0)))]
    return _mix_ffn(x, mod, norm_ffn, acts, [w_o.astype(BF16)], ffn, tm, 256)


def _pick_tile(n, pref):
    t = min(n, pref)
    assert n % t == 0, (n, t)
    return t


def _even_layer(x, mod, tables, lam_init, norm_mix, norm_ffn, ffn, w_in,
                lam_re, lam_im, log_dt, b_re, b_im, c_re, c_im, d_skip, w_glu, q_norm, k_norm,
                lq1, lk1, lq2, lk2, subln, w_out):
    bsz, seq, d = x.shape
    s5w = lam_re.shape[0] * b_re.shape[-1]
    dfw = (w_in.shape[1] - s5w) // 3
    tm = _pick_tile(seq, 512)
    u2, q, k, v = _even_in(x, mod, norm_mix, w_in, q_norm, k_norm, tables, s5w, dfw, tm)
    ys = _s5_mixer_chunked(u2, bsz, lam_re, lam_im, log_dt, b_re, b_im,
                           c_re, c_im, d_skip.reshape(-1), w_glu, _pick_tile(seq, 512))
    att = _diff_attention(q, k, v, lq1, lk1, lq2, lk2, subln, lam_init, _pick_tile(seq, 512))
    w_out_bf = w_out.astype(BF16)
    assert s5w == dfw
    acts = [
        (ys, pl.BlockSpec((tm, s5w), lambda i, b: (i, b))),
        (att, pl.BlockSpec((1, tm, dfw), lambda i, b: (b, i, 0))),
    ]
    w_specs = [pl.BlockSpec((s5w, d), lambda i, b: (0, 0), pipeline_mode=pl.Buffered(1)),
               pl.BlockSpec((dfw, d), lambda i, b: (1, 0), pipeline_mode=pl.Buffered(1))]
    return _mix_ffn(x, mod, norm_ffn, acts, [w_out_bf, w_out_bf], ffn, tm, 256, w_specs)


def kernel(x, c, positions, w_ada, b_ada, norm_mix, norm_ffn, ffn_w_gate, ffn_w_up, ffn_w_down,
           ev_w_in, ev_s5_lam_re, ev_s5_lam_im, ev_s5_log_dt, ev_s5_b_re, ev_s5_b_im, ev_s5_c_re,
           ev_s5_c_im, ev_s5_d, ev_s5_w_glu, ev_q_norm, ev_k_norm, ev_lambda_q1, ev_lambda_k1,
           ev_lambda_q2, ev_lambda_k2, ev_subln, ev_w_out, od_mu, od_w_r, od_w_k, od_w_v, od_w_o,
           od_w0, od_w1, od_w2, od_a0, od_a1, od_a2, od_g1, od_g2, od_k_k, od_k_a, od_r_k,
           od_ln_g, od_ln_b):
    depth = w_ada.shape[0]
    mod = _ada_mod(c, w_ada, b_ada)
    tables = _rope_tables(positions)
    ffn_bf = (ffn_w_gate.astype(BF16), ffn_w_up.astype(BF16), ffn_w_down.astype(BF16))
    for l in range(depth):
        ffn = ffn_bf + (l,)
        if l % 2 == 0:
            e = l // 2
            lam_init = 0.8 - 0.6 * math.exp(-0.3 * l)
            x = _even_layer(x, mod[l], tables, lam_init, norm_mix[l], norm_ffn[l], ffn,
                            ev_w_in[e], ev_s5_lam_re[e],
                            ev_s5_lam_im[e], ev_s5_log_dt[e], ev_s5_b_re[e], ev_s5_b_im[e],
                            ev_s5_c_re[e], ev_s5_c_im[e], ev_s5_d[e], ev_s5_w_glu[e], ev_q_norm[e],
                            ev_k_norm[e], ev_lambda_q1[e], ev_lambda_k1[e], ev_lambda_q2[e],
                            ev_lambda_k2[e], ev_subln[e], ev_w_out[e])
        else:
            o = l // 2
            x = _odd_layer(x, mod[l], norm_mix[l], norm_ffn[l], ffn,
                           od_mu[o], od_w_r[o], od_w_k[o], od_w_v[o], od_w_o[o],
                           od_w0[o], od_w1[o], od_w2[o], od_a0[o], od_a1[o], od_a2[o], od_g1[o],
                           od_g2[o], od_k_k[o], od_k_a[o], od_r_k[o], od_ln_g[o], od_ln_b[o])
    return x
```

```python
import functools
import math

import jax
import jax.numpy as jnp
from jax import lax
from jax.experimental import pallas as pl
from jax.experimental.pallas import tpu as pltpu

F32 = jnp.float32
BF16 = jnp.bfloat16

RMS_EPS = 1e-6
GN_EPS = 64e-5
ROPE_THETA = 500000.0
DT_HEAD = 64
ROT_DIM = DT_HEAD // 4
RWKV_HEAD = 64
LANES = 128
SUBLANES = 8
VMEM_LIMIT = 56 * 1024 * 1024


def _cparams(*sem):
    return pltpu.CompilerParams(dimension_semantics=sem, vmem_limit_bytes=VMEM_LIMIT)


def _const_spec(shape):
    nd = len(shape)
    return pl.BlockSpec(shape, lambda *_: (0,) * nd, pipeline_mode=pl.Buffered(1))


def _dot(a, b):
    return jnp.dot(a, b, preferred_element_type=F32)


def _dot_nt(a, b):
    return lax.dot_general(a, b, (((1,), (1,)), ((), ())), preferred_element_type=F32)


def _split_bf16(x):
    hi = x.astype(BF16)
    lo = (x - hi.astype(F32)).astype(BF16)
    return hi, lo


def _dot_x2(x, w_bf16):
    hi, lo = _split_bf16(x)
    return _dot(hi, w_bf16) + _dot(lo, w_bf16)


def _dot_x2_rhs(w_bf16, x):
    hi, lo = _split_bf16(x)
    return _dot(w_bf16, hi) + _dot(w_bf16, lo)


def _modnorm(x, g, scale, shift):
    ms = jnp.mean(x * x, axis=-1, keepdims=True)
    return (x * lax.rsqrt(ms + RMS_EPS)) * g * (1.0 + scale) + shift


def _ada_kernel(c_ref, w_ref, b_ref, o_ref):
    c = c_ref[...]
    ca = c * jax.nn.sigmoid(c)
    hi, lo = _split_bf16(ca)
    w = w_ref[0]
    w_hi, w_lo = _split_bf16(w)
    o_ref[0] = _dot(hi, w_hi) + _dot(lo, w_hi) + _dot(hi, w_lo) + b_ref[0]


def _ada_mod(c, w_ada, b_ada):
    depth, d, n = w_ada.shape
    bsz = c.shape[0]
    rows = -(-bsz // SUBLANES) * SUBLANES
    c_pad = jnp.zeros((rows, d), F32).at[:bsz].set(c)
    tn = n // 4
    out = pl.pallas_call(
        _ada_kernel,
        grid=(depth, n // tn),
        in_specs=[
            pl.BlockSpec((rows, d), lambda l, j: (0, 0)),
            pl.BlockSpec((1, d, tn), lambda l, j: (l, 0, j)),
            pl.BlockSpec((1, 1, tn), lambda l, j: (l, 0, j)),
        ],
        out_specs=pl.BlockSpec((1, rows, tn), lambda l, j: (l, 0, j)),
        out_shape=jax.ShapeDtypeStruct((depth, rows, n), F32),
        compiler_params=_cparams("arbitrary", "arbitrary"),
        name="ada_mod",
    )(c_pad, w_ada, b_ada.reshape(depth, 1, n))
    mod = out[:, :bsz].reshape(depth, bsz, 6, d)
    return jnp.pad(mod, ((0, 0), (0, 0), (0, SUBLANES - 6), (0, 0)))


def _mul_trig_kernel(p_ref, f_ref, cos_ref, sin_ref):
    a = p_ref[...] * f_ref[...]
    cos_ref[...] = jnp.cos(a)
    sin_ref[...] = jnp.sin(a)


def _rope_tables(positions):
    bsz, seq = positions.shape
    half = ROT_DIM // 2
    inv_freq = ROPE_THETA ** (-jnp.arange(0, ROT_DIM, 2, dtype=F32) / ROT_DIM)
    n = bsz * seq * half
    pos_rep = jnp.broadcast_to(positions.astype(F32)[..., None], (bsz, seq, half))
    frq_rep = jnp.broadcast_to(inv_freq, (bsz, seq, half))
    rows = n // LANES
    tr = min(rows, 512)
    cos, sin = pl.pallas_call(
        _mul_trig_kernel,
        grid=(rows // tr,),
        in_specs=[pl.BlockSpec((tr, LANES), lambda i: (i, 0))] * 2,
        out_specs=[pl.BlockSpec((tr, LANES), lambda i: (i, 0))] * 2,
        out_shape=[jax.ShapeDtypeStruct((rows, LANES), F32)] * 2,
        compiler_params=_cparams("arbitrary"),
        name="rope_trig",
    )(pos_rep.reshape(rows, LANES), frq_rep.reshape(rows, LANES))
    cs = jnp.concatenate([cos.reshape(bsz, seq, half), sin.reshape(bsz, seq, half)], axis=-1)
    return jnp.pad(cs, ((0, 0), (0, 0), (0, LANES - ROT_DIM)))


def _expand_rope(cs):
    half = ROT_DIM // 2
    lane = lax.broadcasted_iota(jnp.int32, cs.shape, 1)
    c0 = jnp.where(lane < half, cs, jnp.where(lane < ROT_DIM, pltpu.roll(cs, half, axis=1), 1.0))
    sa0 = jnp.where(lane < half, -pltpu.roll(cs, LANES - half, axis=1), 0.0)
    sb0 = jnp.where((lane >= half) & (lane < ROT_DIM), cs, 0.0)
    second = lane >= DT_HEAD
    return (jnp.where(second, pltpu.roll(c0, DT_HEAD, axis=1), c0),
            jnp.where(second, pltpu.roll(sa0, DT_HEAD, axis=1), sa0),
            jnp.where(second, pltpu.roll(sb0, DT_HEAD, axis=1), sb0))


def _group_mean_matrix(width, group):
    idx = jnp.arange(width) // group
    return jnp.where(idx[:, None] == idx[None, :], 1.0 / group, 0.0).astype(BF16)


def _even_in_kernel(x_ref, mod_ref, g_ref, w_ref, qn_ref, kn_ref, cs_ref, bd_ref,
                    u_ref, q_ref, k_ref, v_ref, *, s5w, dfw):
    x = x_ref[0]
    h = _modnorm(x, g_ref[...], mod_ref[0, 1:2, :], mod_ref[0, 0:1, :])
    proj = _dot(h.astype(BF16), w_ref[...])
    u_ref[...] = proj[:, :s5w]
    cosv, sav, sbv = _expand_rope(cs_ref[0])
    half = ROT_DIM // 2

    def norm_rope(t, gn, out_ref, post_scale):
        ms = _dot((t * t).astype(BF16), bd_ref[...])
        t = t * lax.rsqrt(ms + RMS_EPS) * gn
        for j in range(dfw // LANES):
            tj = t[:, j * LANES:(j + 1) * LANES]
            up = pltpu.roll(tj, LANES - half, axis=1)
            dn = pltpu.roll(tj, half, axis=1)
            rj = tj * cosv + up * sav + dn * sbv
            out_ref[0, :, j * LANES:(j + 1) * LANES] = (rj * post_scale).astype(out_ref.dtype)

    norm_rope(proj[:, s5w:s5w + dfw], qn_ref[...], q_ref, DT_HEAD ** -0.5 * math.log2(math.e))
    norm_rope(proj[:, s5w + dfw:s5w + 2 * dfw], kn_ref[...], k_ref, 1.0)
    v_ref[0] = proj[:, s5w + 2 * dfw:].astype(v_ref.dtype)


def _even_in(x, mod, g, w_in, q_norm, k_norm, tables, s5w, dfw, tm):
    bsz, seq, d = x.shape
    ncol = w_in.shape[1]
    rep = dfw // DT_HEAD
    kern = functools.partial(_even_in_kernel, s5w=s5w, dfw=dfw)
    tok = lambda w: pl.BlockSpec((1, tm, w), lambda i, b: (b, i, 0))
    return pl.pallas_call(
        kern,
        grid=(seq // tm, bsz),
        in_specs=[
            tok(d),
            pl.BlockSpec((1, SUBLANES, d), lambda i, b: (b, 0, 0)),
            _const_spec((1, d)),
            _const_spec((d, ncol)),
            _const_spec((1, dfw)),
            _const_spec((1, dfw)),
            tok(LANES),
            _const_spec((dfw, dfw)),
        ],
        out_specs=[
            pl.BlockSpec((tm, s5w), lambda i, b: (i, b)),
            tok(dfw), tok(dfw), tok(dfw),
        ],
        out_shape=[
            jax.ShapeDtypeStruct((seq, bsz * s5w), F32),
            jax.ShapeDtypeStruct((bsz, seq, dfw), BF16),
            jax.ShapeDtypeStruct((bsz, seq, dfw), BF16),
            jax.ShapeDtypeStruct((bsz, seq, dfw), BF16),
        ],
        compiler_params=_cparams("arbitrary", "arbitrary"),
        name="even_in_proj",
    )(x, mod, g.reshape(1, d), w_in.astype(BF16),
      jnp.tile(q_norm, rep).reshape(1, dfw), jnp.tile(k_norm, rep).reshape(1, dfw),
      tables, _group_mean_matrix(dfw, DT_HEAD))


S5_CHUNK = 8


def _s5_maps_kernel(lr_ref, li_ref, ldt_ref, lrc_ref, lic_ref, ldtc_ref, br_ref, bi_ref,
                    lrt_ref, lit_ref, ldtt_ref, cr_ref, ci_ref,
                    kmat_out, wre_out, wim_out, mre_out, mim_out, a8r_out, a8i_out, *, hh):
    rr = S5_CHUNK
    nblk = kmat_out.shape[0]
    pp = lrc_ref.shape[1]
    gb = LANES // hh

    def abar(lr, li, ldt):
        dt = jnp.exp(ldt)
        mag = jnp.exp(lr * dt)
        return mag * jnp.cos(li * dt), mag * jnp.sin(li * dt)

    def cmul(ar, ai, br, bi):
        return ar * br - ai * bi, ar * bi + ai * br

    def mm3(a, b):
        a_hi, a_lo = _split_bf16(a)
        b_hi, b_lo = _split_bf16(b)
        return _dot(a_hi, b_hi) + _dot(a_hi, b_lo) + _dot(a_lo, b_hi)

    def iota2(shape, axis):
        return lax.broadcasted_iota(jnp.int32, shape, axis)

    same_kk = (iota2((LANES, LANES), 0) // hh) == (iota2((LANES, LANES), 1) // hh)
    same_w = (iota2((LANES, gb * pp), 0) // hh) == (iota2((LANES, gb * pp), 1) // pp)
    same_m = (iota2((gb * pp, LANES), 0) // pp) == (iota2((gb * pp, LANES), 1) // hh)
    lane_tile = jnp.where(iota2((pp, gb * pp), 1) % pp == iota2((pp, gb * pp), 0), 1.0, 0.0
                          ).astype(BF16)

    lr, li = lrc_ref[...], lic_ref[...]
    a_r, a_i = abar(lr, li, ldtc_ref[...])
    den = lr * lr + li * li
    num_r = a_r - 1.0
    q_r = (num_r * lr + a_i * li) / den
    q_i = (a_i * lr - num_r * li) / den
    bb_r, bb_i = cmul(q_r, q_i, br_ref[...], bi_ref[...])
    pr, pi = bb_r, bb_i
    for i in range(rr - 1, -1, -1):
        for out, val in ((wre_out, pr), (wim_out, pi)):
            tiled = _dot(val.astype(BF16), lane_tile)
            for b in range(nblk):
                out[b, i * LANES:(i + 1) * LANES, :] = jnp.where(
                    same_w, tiled[b * LANES:(b + 1) * LANES], 0.0).astype(BF16)
        if i:
            pr, pi = cmul(a_r, a_i, pr, pi)

    kmat_out[...] = jnp.zeros_like(kmat_out)
    t_ar, t_ai = abar(lrt_ref[...], lit_ref[...], ldtt_ref[...])
    car, cai = cr_ref[...], ci_ref[...]
    for k in range(rr + 1):
        if k:
            car, cai = cmul(car, cai, t_ar, t_ai)
            for out, val in ((mre_out, car), (mim_out, -cai)):
                for b in range(nblk):
                    rows = jnp.concatenate([val[:, b * LANES:(b + 1) * LANES]] * gb, axis=0)
                    out[b, :, (k - 1) * LANES:k * LANES] = jnp.where(same_m, rows, 0.0
                                                                     ).astype(BF16)
        if k < rr:
            for b in range(nblk):
                sl = slice(b * LANES, (b + 1) * LANES)
                kk = jnp.where(same_kk, mm3(bb_r[sl], car[:, sl]) - mm3(bb_i[sl], cai[:, sl]), 0.0
                               ).astype(BF16)
                for j in range(rr - k):
                    kmat_out[b, j * LANES:(j + 1) * LANES, (j + k) * LANES:(j + k + 1) * LANES] = kk
    e_r, e_i = abar(lr_ref[...], li_ref[...], ldt_ref[...])
    for _ in range(S5_CHUNK.bit_length() - 1):
        e_r, e_i = cmul(e_r, e_i, e_r, e_i)
    pr, pi = e_r, e_i
    for s in range(a8r_out.shape[0]):
        if s:
            pr, pi = cmul(pr, pi, e_r, e_i)
        a8r_out[s] = pr
        a8i_out[s] = pi


def _s5_chunk_kernel(u_ref, kmat_ref, wre_ref, wim_ref, mre_ref, mim_ref, dsk_ref, a1r_ref, a1i_ref,
                     pwr_ref, pwi_ref, wglu_ref, perm_ref, permt_ref, o_ref, xr_ref, xi_ref,
                     cr_ref, ci_ref, *, bsz, sub, lane_chunk):
    nrow, nstate = xr_ref.shape
    tchunk = u_ref.shape[0]
    width = wglu_ref.shape[0]
    nsub = tchunk // sub
    rps = sub // S5_CHUNK * bsz
    nblk, gl = kmat_ref.shape[0], kmat_ref.shape[1] // S5_CHUNK
    spart = wre_ref.shape[2]
    step = pl.program_id(0)

    @pl.when(step == 0)
    def _():
        cr_ref[...] = jnp.zeros_like(cr_ref)
        ci_ref[...] = jnp.zeros_like(ci_ref)

    blk = u_ref[...]
    parts = []
    for s in range(nsub):
        u_bt = jnp.concatenate([blk[s * sub:(s + 1) * sub, b * width:(b + 1) * width]
                                for b in range(bsz)], axis=0)
        parts.append(_dot_x2_rhs(perm_ref[...], u_bt))
    u_j = [jnp.concatenate([p[j * rps:(j + 1) * rps] for p in parts], axis=0)
           for j in range(S5_CHUNK)]
    u_jb = [t.astype(BF16) for t in u_j]
    u_g = [jnp.concatenate([t[:, g * gl:(g + 1) * gl] for t in u_jb], axis=1)
           for g in range(nblk)]

    for g in range(nblk):
        xr_ref[:, g * spart:(g + 1) * spart] = _dot(u_g[g], wre_ref[g])
        xi_ref[:, g * spart:(g + 1) * spart] = _dot(u_g[g], wim_ref[g])

    old_r = cr_ref[...]
    old_i = ci_ref[...]
    row = lax.broadcasted_iota(jnp.int32, (SUBLANES, lane_chunk), 0)
    steps_per_tile = SUBLANES // bsz
    ntiles = nrow // SUBLANES
    for c in range(nstate // lane_chunk):
        cols = pl.ds(c * lane_chunk, lane_chunk)
        a1r = a1r_ref[:, cols]
        a1i = a1i_ref[:, cols]
        pwr = pwr_ref[:, cols]
        pwi = pwi_ref[:, cols]

        def tile_body(i, carry):
            pr, pi = carry
            base = pl.multiple_of(i * SUBLANES, SUBLANES)
            zr = xr_ref[pl.ds(base, SUBLANES), cols]
            zi = xi_ref[pl.ds(base, SUBLANES), cols]
            sh = bsz
            apr, api = a1r, a1i
            for _ in range(steps_per_tile.bit_length() - 1):
                sr = jnp.where(row >= sh, pltpu.roll(zr, sh, axis=0), 0.0)
                si = jnp.where(row >= sh, pltpu.roll(zi, sh, axis=0), 0.0)
                zr, zi = zr + apr * sr - api * si, zi + apr * si + api * sr
                apr, api = apr * apr - api * api, 2.0 * apr * api
                sh *= 2
            last = SUBLANES - bsz
            br_, bi_ = pr, pi
            sh = bsz
            while sh < SUBLANES:
                br_ = jnp.where(row >= last, br_, pltpu.roll(br_, SUBLANES - sh, axis=0))
                bi_ = jnp.where(row >= last, bi_, pltpu.roll(bi_, SUBLANES - sh, axis=0))
                last -= sh
                sh *= 2
            xr = zr + pwr * br_ - pwi * bi_
            xi = zi + pwr * bi_ + pwi * br_
            xr_ref[pl.ds(base, SUBLANES), cols] = xr
            xi_ref[pl.ds(base, SUBLANES), cols] = xi
            return xr, xi

        fr, fi = lax.fori_loop(0, ntiles, tile_body, (cr_ref[:, cols], ci_ref[:, cols]))
        cr_ref[:, cols] = fr
        ci_ref[:, cols] = fi

    row8 = lax.broadcasted_iota(jnp.int32, (SUBLANES, nstate), 0)

    def state_in(x, old):
        xs = pltpu.roll(x, bsz, axis=0)
        first = jnp.where(row8 < bsz, pltpu.roll(old, bsz, axis=0), xs[:SUBLANES])
        return jnp.concatenate([first, xs[SUBLANES:]], axis=0).astype(BF16)

    xin_r = state_in(xr_ref[...], old_r)
    xin_i = state_in(xi_ref[...], old_i)

    y_g = [_dot(u_g[g], kmat_ref[g])
           + _dot(xin_r[:, g * spart:(g + 1) * spart], mre_ref[g])
           + _dot(xin_i[:, g * spart:(g + 1) * spart], mim_ref[g]) for g in range(nblk)]
    y = jnp.concatenate(
        [jnp.concatenate([y_g[g][:, t * gl:(t + 1) * gl] for g in range(nblk)], axis=1)
         + dsk_ref[...] * u_j[t] for t in range(S5_CHUNK)], axis=0)
    y = jax.nn.gelu(y)
    gate = jax.nn.sigmoid(_dot(y.astype(BF16), wglu_ref[...]))
    out = (y * gate).astype(BF16)
    for s in range(nsub):
        o_s = jnp.concatenate([out[t * nrow + s * rps:t * nrow + (s + 1) * rps]
                               for t in range(S5_CHUNK)], axis=0)
        o_bt = _dot(permt_ref[...], o_s)
        for b in range(bsz):
            o_ref[s * sub:(s + 1) * sub, b * width:(b + 1) * width] = (
                o_bt[b * sub:(b + 1) * sub].astype(o_ref.dtype))


def _s5_mixer_chunked(u2, bsz, lam_re, lam_im, log_dt, b_re, b_im, c_re, c_im, d_skip, w_glu,
                      tchunk):
    seq = u2.shape[0]
    width = u2.shape[1] // bsz
    g, p = lam_re.shape
    hh = b_re.shape[-1]
    gh = g * hh
    nstate = g * p
    rr = S5_CHUNK
    ldt = jnp.broadcast_to(log_dt[:, None], (g, p))
    rep_c = lambda t: jnp.repeat(t, hh, axis=0)
    rep_t = lambda t: jnp.repeat(t.T, hh, axis=1)
    b_c = lambda b: b.transpose(0, 2, 1).reshape(gh, p)
    c_t = lambda c: c.transpose(2, 0, 1).reshape(p, gh)
    spt = SUBLANES // bsz
    gb = LANES // hh
    nblk = g // gb
    blk_w = rr * LANES
    bf = lambda *s: jax.ShapeDtypeStruct(s, BF16)
    f32 = lambda *s: jax.ShapeDtypeStruct(s, F32)
    kmat, w_re, w_im, m_re, m_im, a8_r, a8_i = pl.pallas_call(
        functools.partial(_s5_maps_kernel, hh=hh),
        out_shape=[bf(nblk, blk_w, blk_w), bf(nblk, blk_w, gb * p), bf(nblk, blk_w, gb * p),
                   bf(nblk, gb * p, blk_w), bf(nblk, gb * p, blk_w), f32(spt, g, p), f32(spt, g, p)],
        compiler_params=pltpu.CompilerParams(vmem_limit_bytes=VMEM_LIMIT),
        name="s5_chunk_maps",
    )(lam_re, lam_im, ldt, rep_c(lam_re), rep_c(lam_im), rep_c(ldt), b_c(b_re), b_c(b_im),
      rep_t(lam_re), rep_t(lam_im), rep_t(ldt), c_t(c_re), c_t(c_im))
    pw_r = jnp.repeat(a8_r.reshape(spt, nstate), bsz, axis=0)
    pw_i = jnp.repeat(a8_i.reshape(spt, nstate), bsz, axis=0)
    a1r = jnp.broadcast_to(pw_r[0:1], (SUBLANES, nstate))
    a1i = jnp.broadcast_to(pw_i[0:1], (SUBLANES, nstate))
    consts = [kmat, w_re, w_im, m_re, m_im, d_skip.reshape(1, width),
              a1r, a1i, pw_r, pw_i, w_glu.astype(BF16)]

    sub = min(tchunk, 128)
    rows = sub * bsz
    dst = jnp.arange(rows)
    rps = sub // rr * bsz
    j_, rem = dst // rps, dst % rps
    src_of = (rem % bsz) * sub + (rem // bsz) * rr + j_
    perm = (jnp.arange(rows)[None, :] == src_of[:, None]).astype(BF16)
    nrow = tchunk // rr * bsz
    kern = functools.partial(_s5_chunk_kernel, bsz=bsz, sub=sub, lane_chunk=512)
    return pl.pallas_call(
        kern,
        grid=(seq // tchunk,),
        in_specs=[pl.BlockSpec((tchunk, bsz * width), lambda i: (i, 0))]
                 + [_const_spec(t.shape) for t in consts]
                 + [_const_spec((rows, rows)), _const_spec((rows, rows))],
        out_specs=pl.BlockSpec((tchunk, bsz * width), lambda i: (i, 0)),
        out_shape=jax.ShapeDtypeStruct((seq, bsz * width), BF16),
        scratch_shapes=[
            pltpu.VMEM((nrow, nstate), F32), pltpu.VMEM((nrow, nstate), F32),
            pltpu.VMEM((SUBLANES, nstate), F32), pltpu.VMEM((SUBLANES, nstate), F32),
        ],
        compiler_params=_cparams("arbitrary"),
        name="s5_chunked",
    )(u2, *consts, perm, perm.T)


def _diff_attn_kernel(lam_ref, sub_ref, q_ref, k_ref, v_ref, o_ref, q2_ref, m_ref, acc_ref,
                      *, tq, lam_init):
    qi = pl.program_id(2)
    vdim = v_ref.shape[-1]
    q = q_ref[0]
    lane = lax.broadcasted_iota(jnp.int32, q.shape, 1)
    zero = jnp.zeros_like(q)
    q2_ref[:tq, :] = jnp.where(lane < DT_HEAD, q, zero)
    q2_ref[tq:, :] = jnp.where(lane >= DT_HEAD, q, zero)
    m_ref[...] = jnp.full(m_ref.shape, -jnp.inf, F32)
    acc_ref[...] = jnp.zeros_like(acc_ref)
    all_rows = ((0, 2 * tq),)

    def gather(ref, row_slices):
        parts = [ref[a:b, :] for a, b in row_slices]
        return parts[0] if len(parts) == 1 else jnp.concatenate(parts, axis=0)

    def scores(kstart, ksize, row_slices=all_rows):
        start = pl.multiple_of(kstart, ksize)
        return _dot_nt(gather(q2_ref, row_slices), k_ref[0, pl.ds(start, ksize), :])

    def update(kstart, ksize, s, row_slices=all_rows, mask=None):
        start = pl.multiple_of(kstart, ksize)
        v_ext = jnp.concatenate([v_ref[0, pl.ds(start, ksize), :], jnp.ones((ksize, vdim), BF16)],
                                axis=1)
        if mask is not None:
            s = jnp.where(mask, s, -jnp.inf)
        part = s[:, :LANES]
        for t in range(1, ksize // LANES):
            part = jnp.maximum(part, s[:, t * LANES:(t + 1) * LANES])
        m_old = gather(m_ref, row_slices)
        m_new = jnp.maximum(m_old, jnp.max(part, axis=-1, keepdims=True))
        alpha = jnp.exp2(m_old - m_new)
        p = jnp.concatenate([jnp.exp2(s[:, t * LANES:(t + 1) * LANES] - m_new)
                             for t in range(ksize // LANES)], axis=1)
        acc = (jnp.concatenate([alpha] * (2 * vdim // LANES), axis=1) * gather(acc_ref, row_slices)
               + _dot(p.astype(BF16), v_ext))
        off = 0
        for a, b in row_slices:
            m_ref[a:b, :] = m_new[off:off + b - a]
            acc_ref[a:b, :] = acc[off:off + b - a]
            off += b - a

    def diagonal(kstart, with_previous):
        hq = tq // 2
        if with_previous:
            s_prev = scores(kstart - tq, tq)
        row_l = lax.broadcasted_iota(jnp.int32, (2 * tq, hq), 0)
        col_l = lax.broadcasted_iota(jnp.int32, (2 * tq, hq), 1)
        row_r = lax.broadcasted_iota(jnp.int32, (tq, hq), 0)
        col_r = lax.broadcasted_iota(jnp.int32, (tq, hq), 1)
        late_rows = ((hq, tq), (tq + hq, 2 * tq))
        s_left = scores(kstart, hq)
        s_right = scores(kstart + hq, hq, late_rows)
        if with_previous:
            update(kstart - tq, tq, s_prev)
        update(kstart, hq, s_left, mask=col_l <= row_l % tq)
        update(kstart + hq, hq, s_right, late_rows, col_r <= row_r % hq)

    def pair(j0):
        s0 = scores(j0 * tq, tq)
        s1 = scores((j0 + 1) * tq, tq)
        update(j0 * tq, tq, s0)
        update((j0 + 1) * tq, tq, s1)

    npairs = qi // 2
    lax.fori_loop(0, npairs, lambda jj, _: (pair(2 * jj), 0)[1], 0)

    @pl.when(qi % 2 == 1)
    def _():
        diagonal(qi * tq, True)

    @pl.when(qi % 2 == 0)
    def _():
        diagonal(qi * tq, False)

    lv = lam_ref[...]
    lam = (jnp.exp(jnp.sum(lv[0:1] * lv[1:2], axis=-1, keepdims=True))
           - jnp.exp(jnp.sum(lv[2:3] * lv[3:4], axis=-1, keepdims=True)) + lam_init)
    acc = acc_ref[...]
    o = (acc[:tq, :vdim] / acc[:tq, vdim:]) - lam * (acc[tq:, :vdim] / acc[tq:, vdim:])
    ms = jnp.mean(o * o, axis=-1, keepdims=True)
    o = o * lax.rsqrt(ms + RMS_EPS) * sub_ref[...] * (1.0 - lam_init)
    o_ref[0] = o.astype(o_ref.dtype)


def _diff_attention(q, k, v, lq1, lk1, lq2, lk2, subln, lam_init, tq):
    bsz, seq, dfw = q.shape
    vdim = 2 * DT_HEAD
    heads = dfw // vdim
    lamv = jnp.zeros((SUBLANES, LANES), F32)
    for i, t in enumerate((lq1, lk1, lq2, lk2)):
        lamv = lamv.at[i, :t.shape[0]].set(t)
    kern = functools.partial(_diff_attn_kernel, tq=tq, lam_init=lam_init)
    kv_spec = pl.BlockSpec((1, seq, vdim), lambda b, h, i: (b, 0, h))
    return pl.pallas_call(
        kern,
        grid=(bsz, heads, seq // tq),
        in_specs=[
            _const_spec((SUBLANES, LANES)),
            _const_spec((1, vdim)),
            pl.BlockSpec((1, tq, vdim), lambda b, h, i: (b, i, h)),
            kv_spec, kv_spec,
        ],
        out_specs=pl.BlockSpec((1, tq, vdim), lambda b, h, i: (b, i, h)),
        out_shape=jax.ShapeDtypeStruct((bsz, seq, dfw), BF16),
        scratch_shapes=[pltpu.VMEM((2 * tq, vdim), BF16), pltpu.VMEM((2 * tq, LANES), F32),
                        pltpu.VMEM((2 * tq, 2 * vdim), F32)],
        compiler_params=_cparams("arbitrary", "arbitrary", "arbitrary"),
        name="diff_attention",
    )(lamv, subln.reshape(1, vdim), q, k, v)


def _mix_ffn_kernel(*refs, n_in, hid_chunk):
    x_ref, mod_ref, g_ref = refs[:3]
    act_refs = refs[3:3 + n_in]
    w_refs = refs[3 + n_in:3 + 2 * n_in]
    wg_ref, wu_ref, wd_ref, o_ref = refs[3 + 2 * n_in:]
    mix = None
    for a_ref, w_ref in zip(act_refs, w_refs):
        a = a_ref[0] if len(a_ref.shape) == 3 else a_ref[...]
        t = _dot(a, w_ref[...])
        mix = t if mix is None else mix + t
    x1 = x_ref[0] + mod_ref[0, 2:3, :] * mix
    h = _modnorm(x1, g_ref[...], mod_ref[0, 4:5, :], mod_ref[0, 3:4, :]).astype(BF16)
    acc = None
    for j in range(wg_ref.shape[2] // hid_chunk):
        sl = slice(j * hid_chunk, (j + 1) * hid_chunk)
        gate = _dot(h, wg_ref[0, :, sl])
        up = _dot(h, wu_ref[0, :, sl])
        act = (gate * jax.nn.sigmoid(gate) * up).astype(BF16)
        t = _dot(act, wd_ref[0, sl, :])
        acc = t if acc is None else acc + t
    o_ref[0] = x1 + mod_ref[0, 5:6, :] * acc


def _mix_ffn(x, mod, g, acts, weights, ffn, tm, hid_chunk, w_specs=None):
    bsz, seq, d = x.shape
    if w_specs is None:
        w_specs = [_const_spec(w.shape) for w in weights]
    wg, wu, wd, layer = ffn
    assert wg.shape[2] % hid_chunk == 0

    def layer_spec(w):
        return pl.BlockSpec((1,) + w.shape[1:], lambda i, b: (layer, 0, 0),
                            pipeline_mode=pl.Buffered(1))

    tok = pl.BlockSpec((1, tm, d), lambda i, b: (b, i, 0))
    return pl.pallas_call(
        functools.partial(_mix_ffn_kernel, n_in=len(acts), hid_chunk=hid_chunk),
        grid=(seq // tm, bsz),
        in_specs=[tok, pl.BlockSpec((1, SUBLANES, d), lambda i, b: (b, 0, 0)), _const_spec((1, d))]
                 + [spec for _, spec in acts]
                 + w_specs
                 + [layer_spec(wg), layer_spec(wu), layer_spec(wd)],
        out_specs=tok,
        out_shape=jax.ShapeDtypeStruct((bsz, seq, d), F32),
        compiler_params=_cparams("arbitrary", "arbitrary"),
        name="mix_ffn",
    )(x, mod, g.reshape(1, d), *[a for a, _ in acts], *weights, wg, wu, wd)


def _rwkv_in_kernel(x_ref, xp_ref, mod_ref, g_ref, mu_ref, wr_ref, wk_ref, wv_ref, w1_ref, w2_ref,
                    a1_ref, a2_ref, g1_ref, g2_ref, w0_ref, a0_ref, kkw_ref, kaw_ref, bd_ref,
                    r_out, lw_out, k_out, v_out, kk_out, a_out, g_out):
    i = pl.program_id(1)
    g = g_ref[...]
    scale = mod_ref[0, 1:2, :]
    shift = mod_ref[0, 0:1, :]
    h = _modnorm(x_ref[0], g, scale, shift)
    hp = _modnorm(xp_ref[0][SUBLANES - 1:SUBLANES, :], g, scale, shift)
    hp = jnp.where(i == 0, 0.0, hp)
    row = lax.broadcasted_iota(jnp.int32, h.shape, 0)
    h_prev = jnp.where(row == 0, hp, pltpu.roll(h, 1, axis=0))
    dx = h_prev - h

    def lerp(j):
        return (h + dx * mu_ref[j:j + 1, :]).astype(BF16)

    r_out[0] = _dot(lerp(0), wr_ref[...])
    wl = jnp.tanh(_dot(lerp(1), w1_ref[...]))
    wdec = w0_ref[...] + _dot(wl.astype(BF16), w2_ref[...])
    w = -jax.nn.softplus(-wdec) - 0.5
    lw_out[0] = -jnp.exp(w)
    k = _dot(lerp(2), wk_ref[...])
    v_out[0] = _dot(lerp(3), wv_ref[...])
    al = _dot(lerp(4), a1_ref[...])
    a = jax.nn.sigmoid(a0_ref[...] + _dot(al.astype(BF16), a2_ref[...]))
    a_out[0] = a
    gl = jax.nn.sigmoid(_dot(lerp(5), g1_ref[...]))
    g_out[0] = _dot(gl.astype(BF16), g2_ref[...])
    kk = k * kkw_ref[...]
    bd = bd_ref[...]
    for j in range(kk.shape[1] // LANES):
        kj = kk[:, j * LANES:(j + 1) * LANES]
        ss = _dot((kj * kj).astype(BF16), bd) * float(RWKV_HEAD)
        kk_out[0, :, j * LANES:(j + 1) * LANES] = kj / jnp.maximum(jnp.sqrt(ss), 1e-12)
    k_out[0] = k * (1.0 + (a - 1.0) * kaw_ref[...])


def _rwkv_in(x, mod, g, mu, w_r, w_k, w_v, w0, w1, w2, a0, a1, a2, g1, g2, k_k, k_a, tm):
    bsz, seq, d = x.shape
    tok = pl.BlockSpec((1, tm, d), lambda b, i: (b, i, 0))
    prev = pl.BlockSpec((1, SUBLANES, d),
                        lambda b, i: (b, jnp.maximum(i * (tm // SUBLANES) - 1, 0), 0))
    bf = lambda w: w.astype(BF16)
    vec = lambda t: t.reshape(1, d)
    consts = [vec(g), mu, bf(w_r), bf(w_k), bf(w_v), bf(w1), bf(w2), bf(a1), bf(a2), bf(g1), bf(g2),
              vec(w0), vec(a0), vec(k_k), vec(k_a), _group_mean_matrix(LANES, RWKV_HEAD)]
    out = jax.ShapeDtypeStruct((bsz, seq, d), F32)
    return pl.pallas_call(
        _rwkv_in_kernel,
        grid=(bsz, seq // tm),
        in_specs=[tok, prev, pl.BlockSpec((1, SUBLANES, d), lambda b, i: (b, 0, 0))]
                 + [_const_spec(t.shape) for t in consts],
        out_specs=[tok] * 7,
        out_shape=[out] * 7,
        compiler_params=_cparams("arbitrary", "arbitrary"),
        name="rwkv_in_proj",
    )(x, x, mod, *consts)


def _rwkv_rec_kernel(r_ref, lw_ref, k_ref, v_ref, kk_ref, a_ref, g_ref, rk_ref, lng_ref, lnb_ref,
                     tri_ref, bd_ref, o_ref, m_ref, y_ref, *, chunk):
    tstep = pl.program_id(2)

    @pl.when(tstep == 0)
    def _():
        m_ref[...] = jnp.zeros_like(m_ref)

    tb, width = r_ref.shape[1:]
    hd = RWKV_HEAD
    gw = m_ref.shape[-1]
    nh = gw // hd
    groups = range(width // gw)
    colsl = [slice(q * gw, (q + 1) * gw) for q in groups]
    lane = lax.broadcasted_iota(jnp.int32, (chunk, gw), 1)
    trow = lax.broadcasted_iota(jnp.int32, (chunk, gw), 0)
    jpos = lane % hd
    strict = jpos < trow
    incl = jpos <= trow
    eye = jnp.where(jpos == trow, 1.0, 0.0)
    head_of_lane = lane // hd
    in_head = [head_of_lane == h for h in range(nh)]
    sq_r = lax.broadcasted_iota(jnp.int32, (gw, gw), 0)
    sq_c = lax.broadcasted_iota(jnp.int32, (gw, gw), 1)
    same_head = (sq_r // hd) == (sq_c // hd)
    diag = sq_r == sq_c
    tri = tri_ref[...]
    bd = bd_ref[...]
    bf = lambda t: t.astype(BF16)

    def blockdiag(y):
        return bf(jnp.concatenate([jnp.where(in_head[h], y, 0.0) for h in range(nh)], axis=0))

    def blockdiag_t(x):
        xt = jnp.concatenate([x] * nh, axis=0).T
        return bf(jnp.where(same_head, xt, 0.0))

    inst = [(c, q) for c in range(tb // chunk) for q in groups]
    cls = []
    for c, q in inst:
        lw = lw_ref[0, c * chunk:(c + 1) * chunk, colsl[q]]
        hi = bf(lw)
        r1 = lw - hi.astype(F32)
        mid = bf(r1)
        lo = bf(r1 - mid.astype(F32))
        cls.append((lw, _dot(tri, hi) + _dot(tri, mid) + _dot(tri, lo)))
    opnd = []
    for (c, q), (lw, cl) in zip(inst, cls):
        rows = slice(c * chunk, (c + 1) * chunk)
        cols = colsl[q]
        k = k_ref[0, rows, cols]
        kk = kk_ref[0, rows, cols]
        b = kk * a_ref[0, rows, cols]
        v = v_ref[0, rows, cols]
        cl_end = cl[chunk - 1:chunk, :]
        p_inv = jnp.exp(-cl)
        p_tail = jnp.exp(cl_end - cl)
        ar = jnp.concatenate([bf(-kk * jnp.exp(cl - lw)), bf(r_ref[0, rows, cols] * jnp.exp(cl))],
                             axis=0)
        tail_t = bf(jnp.concatenate([b * p_tail, k * p_tail], axis=0).T)
        pl_full = jnp.broadcast_to(jnp.exp(cl_end), (gw, gw))
        pl_col = jnp.sum(jnp.where(diag, pl_full, 0.0), axis=1, keepdims=True)
        opnd.append((ar, blockdiag_t(b * p_inv), blockdiag_t(k * p_inv), bf(v), blockdiag(v),
                     tail_t, pl_col))
    g_b = [_dot(o[0], o[1]) for o in opnd]
    g_k = [_dot(o[0], o[2]) for o in opnd]
    nms = [jnp.where(strict, g[:chunk], 0.0) for g in g_b]
    tinv = [eye + nm for nm in nms]
    pw = [bf(_dot(bf(nm), blockdiag(nm))) for nm in nms]
    for _ in range(chunk.bit_length() - 3):
        both = [_dot(jnp.concatenate([bf(t), p2], axis=0), blockdiag(p2)) for t, p2 in zip(tinv, pw)]
        tinv = [t + bo[:chunk] for t, bo in zip(tinv, both)]
        pw = [bf(bo[chunk:]) for bo in both]
    tinv = [bf(t + _dot(bf(t), blockdiag(p2))) for t, p2 in zip(tinv, pw)]
    g_ak = [bf(jnp.where(strict, g[:chunk], 0.0)) for g in g_k]
    g_rb = [bf(jnp.where(incl, g[chunk:], 0.0)) for g in g_b]
    g_rk = [bf(jnp.where(incl, g[chunk:], 0.0)) for g in g_k]

    state = [m_ref[q] for q in groups]
    for c in range(tb // chunk):
        ids = [c * len(groups) + q for q in groups]
        m0_bf = [bf(state[q]) for q in groups]
        rhs = [_dot(jnp.concatenate([opnd[i][0][:chunk], g_ak[i]], axis=1),
                    jnp.concatenate([m0_bf[q], opnd[i][4]], axis=0)) for q, i in zip(groups, ids)]
        u = [_dot(tinv[i], blockdiag(rhs[q])) for q, i in zip(groups, ids)]
        for q, i in zip(groups, ids):
            y_ref[c * chunk:(c + 1) * chunk, colsl[q]] = _dot(
                jnp.concatenate([opnd[i][0][chunk:], g_rb[i], g_rk[i]], axis=1),
                jnp.concatenate([m0_bf[q], blockdiag(u[q]), opnd[i][4]], axis=0))
        for q, i in zip(groups, ids):
            upd = _dot(opnd[i][5], jnp.concatenate([bf(u[q]), opnd[i][3]], axis=0))
            state[q] = opnd[i][6] * state[q] + jnp.where(same_head, upd, 0.0)
    for q in groups:
        m_ref[q] = state[q]

    npair = width // LANES
    for p in range(npair):
        cols = slice(p * LANES, (p + 1) * LANES)
        y = y_ref[:, cols]
        mean = _dot_x2(y, bd)
        dlt = y - mean
        var = _dot(bf(dlt * dlt), bd)
        yn = dlt * lax.rsqrt(var + GN_EPS) * lng_ref[:, cols] + lnb_ref[:, cols]
        rk_sum = _dot(bf(r_ref[0, :, cols] * k_ref[0, :, cols] * rk_ref[:, cols]), bd) * float(hd)
        out = (yn + rk_sum * v_ref[0, :, cols]) * g_ref[0, :, cols]
        o_ref[0, :, cols] = out.astype(o_ref.dtype)


def _rwkv_recurrence(r, lw, k, v, kk, a, g, r_k, ln_g, ln_b, tb, chunk, wblk):
    bsz, seq, d = r.shape
    gw = 2 * RWKV_HEAD
    assert chunk == RWKV_HEAD and wblk % gw == 0
    tok = pl.BlockSpec((1, tb, wblk), lambda b, j, t: (b, t, j))
    vec = pl.BlockSpec((1, wblk), lambda b, j, t: (0, j))
    tri = jnp.tril(jnp.ones((chunk, chunk), F32)).astype(BF16)
    bd = _group_mean_matrix(LANES, RWKV_HEAD)
    return pl.pallas_call(
        functools.partial(_rwkv_rec_kernel, chunk=chunk),
        grid=(bsz, d // wblk, seq // tb),
        in_specs=[tok] * 7 + [vec] * 3 + [_const_spec(tri.shape), _const_spec(bd.shape)],
        out_specs=tok,
        out_shape=jax.ShapeDtypeStruct((bsz, seq, d), BF16),
        scratch_shapes=[pltpu.VMEM((wblk // gw, gw, gw), F32),
                        pltpu.VMEM((tb, wblk), F32)],
        compiler_params=_cparams("arbitrary", "arbitrary", "arbitrary"),
        name="rwkv_recurrence",
    )(r, lw, k, v, kk, a, g, r_k.reshape(1, d), ln_g.reshape(1, d), ln_b.reshape(1, d), tri, bd)


def _odd_layer(x, mod, norm_mix, norm_ffn, ffn, mu, w_r, w_k, w_v, w_o, w0, w1,
               w2, a0, a1, a2, g1, g2, k_k, k_a, r_k, ln_g, ln_b):
    bsz, seq, d = x.shape
    r, lw, k, v, kk, a, g = _rwkv_in(x, mod, norm_mix, mu, w_r, w_k, w_v, w0, w1, w2, a0, a1, a2,
                                     g1, g2, k_k, k_a, _pick_tile(seq, 512))
    yg = _rwkv_recurrence(r, lw, k, v, kk, a, g, r_k, ln_g, ln_b, _pick_tile(seq, 256), 64,
                          d)
    tm = _pick_tile(seq, 512)
    acts = [(yg, pl.BlockSpec((1, tm, d), lambda i, b: (b, i, 0)))]
    return _mix_ffn(x, mod, norm_ffn, acts, [w_o.astype(BF16)], ffn, tm, 256)


def _pick_tile(n, pref):
    t = min(n, pref)
    assert n % t == 0, (n, t)
    return t


def _even_layer(x, mod, tables, lam_init, norm_mix, norm_ffn, ffn, w_in,
                lam_re, lam_im, log_dt, b_re, b_im, c_re, c_im, d_skip, w_glu, q_norm, k_norm,
                lq1, lk1, lq2, lk2, subln, w_out):
    bsz, seq, d = x.shape
    s5w = lam_re.shape[0] * b_re.shape[-1]
    dfw = (w_in.shape[1] - s5w) // 3
    tm = _pick_tile(seq, 512)
    u2, q, k, v = _even_in(x, mod, norm_mix, w_in, q_norm, k_norm, tables, s5w, dfw, tm)
    ys = _s5_mixer_chunked(u2, bsz, lam_re, lam_im, log_dt, b_re, b_im,
                           c_re, c_im, d_skip.reshape(-1), w_glu, _pick_tile(seq, 512))
    att = _diff_attention(q, k, v, lq1, lk1, lq2, lk2, subln, lam_init, _pick_tile(seq, 512))
    w_out_bf = w_out.astype(BF16)
    assert s5w == dfw
    acts = [
        (ys, pl.BlockSpec((tm, s5w), lambda i, b: (i, b))),
        (att, pl.BlockSpec((1, tm, dfw), lambda i, b: (b, i, 0))),
    ]
    w_specs = [pl.BlockSpec((s5w, d), lambda i, b: (0, 0), pipeline_mode=pl.Buffered(1)),
               pl.BlockSpec((dfw, d), lambda i, b: (1, 0), pipeline_mode=pl.Buffered(1))]
    return _mix_ffn(x, mod, norm_ffn, acts, [w_out_bf, w_out_bf], ffn, tm, 256, w_specs)


def kernel(x, c, positions, w_ada, b_ada, norm_mix, norm_ffn, ffn_w_gate, ffn_w_up, ffn_w_down,
           ev_w_in, ev_s5_lam_re, ev_s5_lam_im, ev_s5_log_dt, ev_s5_b_re, ev_s5_b_im, ev_s5_c_re,
           ev_s5_c_im, ev_s5_d, ev_s5_w_glu, ev_q_norm, ev_k_norm, ev_lambda_q1, ev_lambda_k1,
           ev_lambda_q2, ev_lambda_k2, ev_subln, ev_w_out, od_mu, od_w_r, od_w_k, od_w_v, od_w_o,
           od_w0, od_w1, od_w2, od_a0, od_a1, od_a2, od_g1, od_g2, od_k_k, od_k_a, od_r_k,
           od_ln_g, od_ln_b):
    depth = w_ada.shape[0]
    mod = _ada_mod(c, w_ada, b_ada)
    tables = _rope_tables(positions)
    ffn_bf = (ffn_w_gate.astype(BF16), ffn_w_up.astype(BF16), ffn_w_down.astype(BF16))
    for l in range(depth):
        ffn = ffn_bf + (l,)
        if l % 2 == 0:
            e = l // 2
            lam_init = 0.8 - 0.6 * math.exp(-0.3 * l)
            x = _even_layer(x, mod[l], tables, lam_init, norm_mix[l], norm_ffn[l], ffn,
                            ev_w_in[e], ev_s5_lam_re[e],
                            ev_s5_lam_im[e], ev_s5_log_dt[e], ev_s5_b_re[e], ev_s5_b_im[e],
                            ev_s5_c_re[e], ev_s5_c_im[e], ev_s5_d[e], ev_s5_w_glu[e], ev_q_norm[e],
                            ev_k_norm[e], ev_lambda_q1[e], ev_lambda_k1[e], ev_lambda_q2[e],
                            ev_lambda_k2[e], ev_subln[e], ev_w_out[e])
        else:
            o = l // 2
            x = _odd_layer(x, mod[l], norm_mix[l], norm_ffn[l], ffn,
                           od_mu[o], od_w_r[o], od_w_k[o], od_w_v[o], od_w_o[o],
                           od_w0[o], od_w1[o], od_w2[o], od_a0[o], od_a1[o], od_a2[o], od_g1[o],
                           od_g2[o], od_k_k[o], od_k_a[o], od_r_k[o], od_ln_g[o], od_ln_b[o])
    return x
```

```python
import functools
import math

import jax
import jax.numpy as jnp
from jax import lax
from jax.experimental import pallas as pl
from jax.experimental.pallas import tpu as pltpu

F32 = jnp.float32
BF16 = jnp.bfloat16

RMS_EPS = 1e-6
GN_EPS = 64e-5
ROPE_THETA = 500000.0
DT_HEAD = 64
ROT_DIM = DT_HEAD // 4
RWKV_HEAD = 64
LANES = 128
SUBLANES = 8
VMEM_LIMIT = 56 * 1024 * 1024


def _cparams(*sem):
    return pltpu.CompilerParams(dimension_semantics=sem, vmem_limit_bytes=VMEM_LIMIT)


def _const_spec(shape):
    nd = len(shape)
    return pl.BlockSpec(shape, lambda *_: (0,) * nd, pipeline_mode=pl.Buffered(1))


def _dot(a, b):
    return jnp.dot(a, b, preferred_element_type=F32)


def _dot_nt(a, b):
    return lax.dot_general(a, b, (((1,), (1,)), ((), ())), preferred_element_type=F32)


def _split_bf16(x):
    hi = x.astype(BF16)
    lo = (x - hi.astype(F32)).astype(BF16)
    return hi, lo


def _dot_x2(x, w_bf16):
    hi, lo = _split_bf16(x)
    return _dot(hi, w_bf16) + _dot(lo, w_bf16)


def _dot_x2_rhs(w_bf16, x):
    hi, lo = _split_bf16(x)
    return _dot(w_bf16, hi) + _dot(w_bf16, lo)


def _modnorm(x, g, scale, shift):
    ms = jnp.mean(x * x, axis=-1, keepdims=True)
    return (x * lax.rsqrt(ms + RMS_EPS)) * (g * (1.0 + scale)) + shift


def _ada_kernel(c_ref, w_ref, b_ref, o_ref):
    c = c_ref[...]
    ca = c * jax.nn.sigmoid(c)
    hi, lo = _split_bf16(ca)
    w = w_ref[0]
    w_hi, w_lo = _split_bf16(w)
    o_ref[0] = _dot(hi, w_hi) + _dot(lo, w_hi) + _dot(hi, w_lo) + b_ref[0]


def _ada_mod(c, w_ada, b_ada):
    depth, d, n = w_ada.shape
    bsz = c.shape[0]
    rows = -(-bsz // SUBLANES) * SUBLANES
    c_pad = jnp.zeros((rows, d), F32).at[:bsz].set(c)
    tn = n // 4
    out = pl.pallas_call(
        _ada_kernel,
        grid=(depth, n // tn),
        in_specs=[
            pl.BlockSpec((rows, d), lambda l, j: (0, 0)),
            pl.BlockSpec((1, d, tn), lambda l, j: (l, 0, j)),
            pl.BlockSpec((1, 1, tn), lambda l, j: (l, 0, j)),
        ],
        out_specs=pl.BlockSpec((1, rows, tn), lambda l, j: (l, 0, j)),
        out_shape=jax.ShapeDtypeStruct((depth, rows, n), F32),
        compiler_params=_cparams("arbitrary", "arbitrary"),
        name="ada_mod",
    )(c_pad, w_ada, b_ada.reshape(depth, 1, n))
    mod = out[:, :bsz].reshape(depth, bsz, 6, d)
    return jnp.pad(mod, ((0, 0), (0, 0), (0, SUBLANES - 6), (0, 0)))


def _rope_trig_kernel(p_ref, f_ref, o_ref):
    half = f_ref.shape[0]
    pad = jnp.zeros((LANES - 2 * half, LANES), F32)
    for r in range(p_ref.shape[0]):
        a = p_ref[r:r + 1, :] * f_ref[...]
        o_ref[r * LANES:(r + 1) * LANES, :] = jnp.concatenate([jnp.cos(a), jnp.sin(a), pad], axis=0).T


def _rope_tables(positions):
    bsz, seq = positions.shape
    half = ROT_DIM // 2
    inv_freq = ROPE_THETA ** (-jnp.arange(0, ROT_DIM, 2, dtype=F32) / ROT_DIM)
    rows = bsz * seq // LANES
    tr = min(rows, SUBLANES)
    cs = pl.pallas_call(
        _rope_trig_kernel,
        grid=(rows // tr,),
        in_specs=[pl.BlockSpec((tr, LANES), lambda i: (i, 0)), _const_spec((half, LANES))],
        out_specs=pl.BlockSpec((tr * LANES, LANES), lambda i: (i, 0)),
        out_shape=jax.ShapeDtypeStruct((bsz * seq, LANES), F32),
        compiler_params=_cparams("arbitrary"),
        name="rope_trig",
    )(positions.astype(F32).reshape(rows, LANES), jnp.broadcast_to(inv_freq[:, None], (half, LANES)))
    return cs.reshape(bsz, seq, LANES)


def _expand_rope(cs):
    half = ROT_DIM // 2
    lane = lax.broadcasted_iota(jnp.int32, cs.shape, 1)
    c0 = jnp.where(lane < half, cs, jnp.where(lane < ROT_DIM, pltpu.roll(cs, half, axis=1), 1.0))
    sa0 = jnp.where(lane < half, -pltpu.roll(cs, LANES - half, axis=1), 0.0)
    sb0 = jnp.where((lane >= half) & (lane < ROT_DIM), cs, 0.0)
    second = lane >= DT_HEAD
    return (jnp.where(second, pltpu.roll(c0, DT_HEAD, axis=1), c0),
            jnp.where(second, pltpu.roll(sa0, DT_HEAD, axis=1), sa0),
            jnp.where(second, pltpu.roll(sb0, DT_HEAD, axis=1), sb0))


def _group_mean_matrix(width, group):
    idx = jnp.arange(width) // group
    return jnp.where(idx[:, None] == idx[None, :], 1.0 / group, 0.0).astype(BF16)


def _even_in_kernel(x_ref, mod_ref, g_ref, w_ref, qn_ref, kn_ref, cs_ref, bd_ref,
                    u_ref, q_ref, k_ref, v_ref, *, s5w, dfw):
    x = x_ref[0]
    h = _modnorm(x, g_ref[...], mod_ref[0, 1:2, :], mod_ref[0, 0:1, :])
    proj = _dot(h.astype(BF16), w_ref[...])
    u_ref[...] = proj[:, :s5w]
    cosv, sav, sbv = _expand_rope(cs_ref[0])
    half = ROT_DIM // 2

    def norm_rope(t, gn, out_ref, post_scale):
        ms = _dot((t * t).astype(BF16), bd_ref[...])
        t = t * lax.rsqrt(ms + RMS_EPS) * gn
        for j in range(dfw // LANES):
            tj = t[:, j * LANES:(j + 1) * LANES]
            up = pltpu.roll(tj, LANES - half, axis=1)
            dn = pltpu.roll(tj, half, axis=1)
            rj = tj * cosv + up * sav + dn * sbv
            out_ref[0, :, j * LANES:(j + 1) * LANES] = (rj * post_scale).astype(out_ref.dtype)

    norm_rope(proj[:, s5w:s5w + dfw], qn_ref[...], q_ref, DT_HEAD ** -0.5 * math.log2(math.e))
    norm_rope(proj[:, s5w + dfw:s5w + 2 * dfw], kn_ref[...], k_ref, 1.0)
    v_ref[0] = proj[:, s5w + 2 * dfw:].astype(v_ref.dtype)


def _even_in(x, mod, g, w_in, q_norm, k_norm, tables, s5w, dfw, tm):
    bsz, seq, d = x.shape
    ncol = w_in.shape[1]
    rep = dfw // DT_HEAD
    kern = functools.partial(_even_in_kernel, s5w=s5w, dfw=dfw)
    tok = lambda w: pl.BlockSpec((1, tm, w), lambda i, b: (b, i, 0))
    return pl.pallas_call(
        kern,
        grid=(seq // tm, bsz),
        in_specs=[
            tok(d),
            pl.BlockSpec((1, SUBLANES, d), lambda i, b: (b, 0, 0)),
            _const_spec((1, d)),
            _const_spec((d, ncol)),
            _const_spec((1, dfw)),
            _const_spec((1, dfw)),
            tok(LANES),
            _const_spec((dfw, dfw)),
        ],
        out_specs=[
            pl.BlockSpec((tm, s5w), lambda i, b: (i, b)),
            tok(dfw), tok(dfw), tok(dfw),
        ],
        out_shape=[
            jax.ShapeDtypeStruct((seq, bsz * s5w), F32),
            jax.ShapeDtypeStruct((bsz, seq, dfw), BF16),
            jax.ShapeDtypeStruct((bsz, seq, dfw), BF16),
            jax.ShapeDtypeStruct((bsz, seq, dfw), BF16),
        ],
        compiler_params=_cparams("arbitrary", "arbitrary"),
        name="even_in_proj",
    )(x, mod, g.reshape(1, d), w_in.astype(BF16),
      jnp.tile(q_norm, rep).reshape(1, dfw), jnp.tile(k_norm, rep).reshape(1, dfw),
      tables, _group_mean_matrix(dfw, DT_HEAD))


S5_CHUNK = 8


def _s5_maps_kernel(lr_ref, li_ref, ldt_ref, lrc_ref, lic_ref, ldtc_ref, br_ref, bi_ref,
                    lrt_ref, lit_ref, ldtt_ref, cr_ref, ci_ref,
                    kmat_out, wre_out, wim_out, mre_out, mim_out, a8r_out, a8i_out, *, hh):
    rr = S5_CHUNK
    nblk = kmat_out.shape[0]
    pp = lrc_ref.shape[1]
    gb = LANES // hh

    def abar(lr, li, ldt):
        dt = jnp.exp(ldt)
        mag = jnp.exp(lr * dt)
        return mag * jnp.cos(li * dt), mag * jnp.sin(li * dt)

    def cmul(ar, ai, br, bi):
        return ar * br - ai * bi, ar * bi + ai * br

    def mm3(a, b):
        a_hi, a_lo = _split_bf16(a)
        b_hi, b_lo = _split_bf16(b)
        return _dot(a_hi, b_hi) + _dot(a_hi, b_lo) + _dot(a_lo, b_hi)

    def iota2(shape, axis):
        return lax.broadcasted_iota(jnp.int32, shape, axis)

    same_kk = (iota2((LANES, LANES), 0) // hh) == (iota2((LANES, LANES), 1) // hh)
    same_w = (iota2((LANES, gb * pp), 0) // hh) == (iota2((LANES, gb * pp), 1) // pp)
    same_m = (iota2((gb * pp, LANES), 0) // pp) == (iota2((gb * pp, LANES), 1) // hh)
    lane_tile = jnp.where(iota2((pp, gb * pp), 1) % pp == iota2((pp, gb * pp), 0), 1.0, 0.0
                          ).astype(BF16)

    lr, li = lrc_ref[...], lic_ref[...]
    a_r, a_i = abar(lr, li, ldtc_ref[...])
    den = lr * lr + li * li
    num_r = a_r - 1.0
    q_r = (num_r * lr + a_i * li) / den
    q_i = (a_i * lr - num_r * li) / den
    bb_r, bb_i = cmul(q_r, q_i, br_ref[...], bi_ref[...])
    pr, pi = bb_r, bb_i
    for i in range(rr - 1, -1, -1):
        for out, val in ((wre_out, pr), (wim_out, pi)):
            tiled = _dot(val.astype(BF16), lane_tile)
            for b in range(nblk):
                out[b, i * LANES:(i + 1) * LANES, :] = jnp.where(
                    same_w, tiled[b * LANES:(b + 1) * LANES], 0.0).astype(BF16)
        if i:
            pr, pi = cmul(a_r, a_i, pr, pi)

    kmat_out[...] = jnp.zeros_like(kmat_out)
    t_ar, t_ai = abar(lrt_ref[...], lit_ref[...], ldtt_ref[...])
    car, cai = cr_ref[...], ci_ref[...]
    for k in range(rr + 1):
        if k:
            car, cai = cmul(car, cai, t_ar, t_ai)
            for out, val in ((mre_out, car), (mim_out, -cai)):
                for b in range(nblk):
                    rows = jnp.concatenate([val[:, b * LANES:(b + 1) * LANES]] * gb, axis=0)
                    out[b, :, (k - 1) * LANES:k * LANES] = jnp.where(same_m, rows, 0.0
                                                                     ).astype(BF16)
        if k < rr:
            for b in range(nblk):
                sl = slice(b * LANES, (b + 1) * LANES)
                kk = jnp.where(same_kk, mm3(bb_r[sl], car[:, sl]) - mm3(bb_i[sl], cai[:, sl]), 0.0
                               ).astype(BF16)
                for j in range(rr - k):
                    kmat_out[b, j * LANES:(j + 1) * LANES, (j + k) * LANES:(j + k + 1) * LANES] = kk
    e_r, e_i = abar(lr_ref[...], li_ref[...], ldt_ref[...])
    for _ in range(S5_CHUNK.bit_length() - 1):
        e_r, e_i = cmul(e_r, e_i, e_r, e_i)
    pr, pi = e_r, e_i
    for s in range(a8r_out.shape[0]):
        if s:
            pr, pi = cmul(pr, pi, e_r, e_i)
        a8r_out[s] = pr
        a8i_out[s] = pi


def _s5_chunk_kernel(u_ref, kmat_ref, wre_ref, wim_ref, mre_ref, mim_ref, dsk_ref, a1r_ref, a1i_ref,
                     pwr_ref, pwi_ref, wglu_ref, perm_ref, permt_ref, o_ref, xr_ref, xi_ref,
                     cr_ref, ci_ref, *, bsz, sub, lane_chunk):
    nrow, nstate = xr_ref.shape
    tchunk = u_ref.shape[0]
    width = wglu_ref.shape[0]
    nsub = tchunk // sub
    rps = sub // S5_CHUNK * bsz
    nblk, gl = kmat_ref.shape[0], kmat_ref.shape[1] // S5_CHUNK
    spart = wre_ref.shape[2]
    step = pl.program_id(0)

    @pl.when(step == 0)
    def _():
        cr_ref[...] = jnp.zeros_like(cr_ref)
        ci_ref[...] = jnp.zeros_like(ci_ref)

    blk = u_ref[...]
    parts = []
    for s in range(nsub):
        u_bt = jnp.concatenate([blk[s * sub:(s + 1) * sub, b * width:(b + 1) * width]
                                for b in range(bsz)], axis=0)
        parts.append(_dot_x2_rhs(perm_ref[...], u_bt))
    u_j = [jnp.concatenate([p[j * rps:(j + 1) * rps] for p in parts], axis=0)
           for j in range(S5_CHUNK)]
    u_jb = [t.astype(BF16) for t in u_j]
    u_g = [jnp.concatenate([t[:, g * gl:(g + 1) * gl] for t in u_jb], axis=1)
           for g in range(nblk)]

    for g in range(nblk):
        xr_ref[:, g * spart:(g + 1) * spart] = _dot(u_g[g], wre_ref[g])
        xi_ref[:, g * spart:(g + 1) * spart] = _dot(u_g[g], wim_ref[g])

    old_r = cr_ref[...]
    old_i = ci_ref[...]
    row = lax.broadcasted_iota(jnp.int32, (SUBLANES, lane_chunk), 0)
    steps_per_tile = SUBLANES // bsz
    ntiles = nrow // SUBLANES
    for c in range(nstate // lane_chunk):
        cols = pl.ds(c * lane_chunk, lane_chunk)
        a1r = a1r_ref[:, cols]
        a1i = a1i_ref[:, cols]
        pwr = pwr_ref[:, cols]
        pwi = pwi_ref[:, cols]

        def tile_body(i, carry):
            pr, pi = carry
            base = pl.multiple_of(i * SUBLANES, SUBLANES)
            zr = xr_ref[pl.ds(base, SUBLANES), cols]
            zi = xi_ref[pl.ds(base, SUBLANES), cols]
            sh = bsz
            apr, api = a1r, a1i
            for _ in range(steps_per_tile.bit_length() - 1):
                sr = jnp.where(row >= sh, pltpu.roll(zr, sh, axis=0), 0.0)
                si = jnp.where(row >= sh, pltpu.roll(zi, sh, axis=0), 0.0)
                zr, zi = zr + apr * sr - api * si, zi + apr * si + api * sr
                apr, api = apr * apr - api * api, 2.0 * apr * api
                sh *= 2
            last = SUBLANES - bsz
            br_, bi_ = pr, pi
            sh = bsz
            while sh < SUBLANES:
                br_ = jnp.where(row >= last, br_, pltpu.roll(br_, SUBLANES - sh, axis=0))
                bi_ = jnp.where(row >= last, bi_, pltpu.roll(bi_, SUBLANES - sh, axis=0))
                last -= sh
                sh *= 2
            xr = zr + pwr * br_ - pwi * bi_
            xi = zi + pwr * bi_ + pwi * br_
            xr_ref[pl.ds(base, SUBLANES), cols] = xr
            xi_ref[pl.ds(base, SUBLANES), cols] = xi
            return xr, xi

        fr, fi = lax.fori_loop(0, ntiles, tile_body, (cr_ref[:, cols], ci_ref[:, cols]))
        cr_ref[:, cols] = fr
        ci_ref[:, cols] = fi

    row8 = lax.broadcasted_iota(jnp.int32, (SUBLANES, nstate), 0)

    def state_in(x, old):
        xs = pltpu.roll(x, bsz, axis=0)
        first = jnp.where(row8 < bsz, pltpu.roll(old, bsz, axis=0), xs[:SUBLANES])
        return jnp.concatenate([first, xs[SUBLANES:]], axis=0).astype(BF16)

    xin_r = state_in(xr_ref[...], old_r)
    xin_i = state_in(xi_ref[...], old_i)

    tw = 2 * gl

    def k_part(g):
        strips = []
        for s in range(S5_CHUNK // 2):
            kdim = (2 * s + 2) * gl
            strips.append(_dot(u_g[g][:, :kdim], kmat_ref[g, :kdim, s * tw:(s + 1) * tw]))
        return jnp.concatenate(strips, axis=1)

    y_g = [k_part(g)
           + _dot(xin_r[:, g * spart:(g + 1) * spart], mre_ref[g])
           + _dot(xin_i[:, g * spart:(g + 1) * spart], mim_ref[g]) for g in range(nblk)]
    y = jnp.concatenate(
        [jnp.concatenate([y_g[g][:, t * gl:(t + 1) * gl] for g in range(nblk)], axis=1)
         + dsk_ref[...] * u_j[t] for t in range(S5_CHUNK)], axis=0)
    y = jax.nn.gelu(y)
    gate = jax.nn.sigmoid(_dot(y.astype(BF16), wglu_ref[...]))
    out = (y * gate).astype(BF16)
    for s in range(nsub):
        o_s = jnp.concatenate([out[t * nrow + s * rps:t * nrow + (s + 1) * rps]
                               for t in range(S5_CHUNK)], axis=0)
        o_bt = _dot(permt_ref[...], o_s)
        for b in range(bsz):
            o_ref[s * sub:(s + 1) * sub, b * width:(b + 1) * width] = (
                o_bt[b * sub:(b + 1) * sub].astype(o_ref.dtype))


def _s5_mixer_chunked(u2, bsz, lam_re, lam_im, log_dt, b_re, b_im, c_re, c_im, d_skip, w_glu,
                      tchunk):
    seq = u2.shape[0]
    width = u2.shape[1] // bsz
    g, p = lam_re.shape
    hh = b_re.shape[-1]
    gh = g * hh
    nstate = g * p
    rr = S5_CHUNK
    ldt = jnp.broadcast_to(log_dt[:, None], (g, p))
    rep_c = lambda t: jnp.repeat(t, hh, axis=0)
    rep_t = lambda t: jnp.repeat(t.T, hh, axis=1)
    b_c = lambda b: b.transpose(0, 2, 1).reshape(gh, p)
    c_t = lambda c: c.transpose(2, 0, 1).reshape(p, gh)
    spt = SUBLANES // bsz
    gb = LANES // hh
    nblk = g // gb
    blk_w = rr * LANES
    bf = lambda *s: jax.ShapeDtypeStruct(s, BF16)
    f32 = lambda *s: jax.ShapeDtypeStruct(s, F32)
    kmat, w_re, w_im, m_re, m_im, a8_r, a8_i = pl.pallas_call(
        functools.partial(_s5_maps_kernel, hh=hh),
        out_shape=[bf(nblk, blk_w, blk_w), bf(nblk, blk_w, gb * p), bf(nblk, blk_w, gb * p),
                   bf(nblk, gb * p, blk_w), bf(nblk, gb * p, blk_w), f32(spt, g, p), f32(spt, g, p)],
        compiler_params=pltpu.CompilerParams(vmem_limit_bytes=VMEM_LIMIT),
        name="s5_chunk_maps",
    )(lam_re, lam_im, ldt, rep_c(lam_re), rep_c(lam_im), rep_c(ldt), b_c(b_re), b_c(b_im),
      rep_t(lam_re), rep_t(lam_im), rep_t(ldt), c_t(c_re), c_t(c_im))
    pw_r = jnp.repeat(a8_r.reshape(spt, nstate), bsz, axis=0)
    pw_i = jnp.repeat(a8_i.reshape(spt, nstate), bsz, axis=0)
    a1r = jnp.broadcast_to(pw_r[0:1], (SUBLANES, nstate))
    a1i = jnp.broadcast_to(pw_i[0:1], (SUBLANES, nstate))
    consts = [kmat, w_re, w_im, m_re, m_im, d_skip.reshape(1, width),
              a1r, a1i, pw_r, pw_i, w_glu.astype(BF16)]

    sub = min(tchunk, 128)
    rows = sub * bsz
    dst = jnp.arange(rows)
    rps = sub // rr * bsz
    j_, rem = dst // rps, dst % rps
    src_of = (rem % bsz) * sub + (rem // bsz) * rr + j_
    perm = (jnp.arange(rows)[None, :] == src_of[:, None]).astype(BF16)
    nrow = tchunk // rr * bsz
    kern = functools.partial(_s5_chunk_kernel, bsz=bsz, sub=sub, lane_chunk=512)
    return pl.pallas_call(
        kern,
        grid=(seq // tchunk,),
        in_specs=[pl.BlockSpec((tchunk, bsz * width), lambda i: (i, 0))]
                 + [_const_spec(t.shape) for t in consts]
                 + [_const_spec((rows, rows)), _const_spec((rows, rows))],
        out_specs=pl.BlockSpec((tchunk, bsz * width), lambda i: (i, 0)),
        out_shape=jax.ShapeDtypeStruct((seq, bsz * width), BF16),
        scratch_shapes=[
            pltpu.VMEM((nrow, nstate), F32), pltpu.VMEM((nrow, nstate), F32),
            pltpu.VMEM((SUBLANES, nstate), F32), pltpu.VMEM((SUBLANES, nstate), F32),
        ],
        compiler_params=_cparams("arbitrary"),
        name="s5_chunked",
    )(u2, *consts, perm, perm.T)


def _diff_attn_kernel(lam_ref, sub_ref, q_ref, k_ref, v_ref, o_ref, q2_ref, m_ref, acc_ref,
                      *, tq, lam_init):
    qi = pl.program_id(2)
    vdim = v_ref.shape[-1]
    q = q_ref[0]
    lane = lax.broadcasted_iota(jnp.int32, q.shape, 1)
    zero = jnp.zeros_like(q)
    q2_ref[:tq, :] = jnp.where(lane < DT_HEAD, q, zero)
    q2_ref[tq:, :] = jnp.where(lane >= DT_HEAD, q, zero)
    m_ref[...] = jnp.full(m_ref.shape, -jnp.inf, F32)
    acc_ref[...] = jnp.zeros_like(acc_ref)
    all_rows = ((0, 2 * tq),)

    def gather(ref, row_slices):
        parts = [ref[a:b, :] for a, b in row_slices]
        return parts[0] if len(parts) == 1 else jnp.concatenate(parts, axis=0)

    def scores(kstart, ksize, row_slices=all_rows):
        start = pl.multiple_of(kstart, ksize)
        return _dot_nt(gather(q2_ref, row_slices), k_ref[0, pl.ds(start, ksize), :])

    def update(kstart, ksize, s, row_slices=all_rows, mask=None):
        start = pl.multiple_of(kstart, ksize)
        v_ext = jnp.concatenate([v_ref[0, pl.ds(start, ksize), :], jnp.ones((ksize, vdim), BF16)],
                                axis=1)
        if mask is not None:
            s = jnp.where(mask, s, -jnp.inf)
        part = s[:, :LANES]
        for t in range(1, ksize // LANES):
            part = jnp.maximum(part, s[:, t * LANES:(t + 1) * LANES])
        m_old = gather(m_ref, row_slices)
        m_new = jnp.maximum(m_old, jnp.max(part, axis=-1, keepdims=True))
        alpha = jnp.exp2(m_old - m_new)
        p = jnp.concatenate([jnp.exp2(s[:, t * LANES:(t + 1) * LANES] - m_new)
                             for t in range(ksize // LANES)], axis=1)
        acc = (jnp.concatenate([alpha] * (2 * vdim // LANES), axis=1) * gather(acc_ref, row_slices)
               + _dot(p.astype(BF16), v_ext))
        off = 0
        for a, b in row_slices:
            m_ref[a:b, :] = m_new[off:off + b - a]
            acc_ref[a:b, :] = acc[off:off + b - a]
            off += b - a

    def diagonal(kstart, with_previous):
        hq = tq // 2
        if with_previous:
            s_prev = scores(kstart - tq, tq)
        row_l = lax.broadcasted_iota(jnp.int32, (2 * tq, hq), 0)
        col_l = lax.broadcasted_iota(jnp.int32, (2 * tq, hq), 1)
        row_r = lax.broadcasted_iota(jnp.int32, (tq, hq), 0)
        col_r = lax.broadcasted_iota(jnp.int32, (tq, hq), 1)
        late_rows = ((hq, tq), (tq + hq, 2 * tq))
        s_left = scores(kstart, hq)
        s_right = scores(kstart + hq, hq, late_rows)
        if with_previous:
            update(kstart - tq, tq, s_prev)
        update(kstart, hq, s_left, mask=col_l <= row_l % tq)
        update(kstart + hq, hq, s_right, late_rows, col_r <= row_r % hq)

    def pair(j0):
        s0 = scores(j0 * tq, tq)
        s1 = scores((j0 + 1) * tq, tq)
        update(j0 * tq, tq, s0)
        update((j0 + 1) * tq, tq, s1)

    npairs = qi // 2
    lax.fori_loop(0, npairs, lambda jj, _: (pair(2 * jj), 0)[1], 0)

    @pl.when(qi % 2 == 1)
    def _():
        diagonal(qi * tq, True)

    @pl.when(qi % 2 == 0)
    def _():
        diagonal(qi * tq, False)

    lv = lam_ref[...]
    lam = (jnp.exp(jnp.sum(lv[0:1] * lv[1:2], axis=-1, keepdims=True))
           - jnp.exp(jnp.sum(lv[2:3] * lv[3:4], axis=-1, keepdims=True)) + lam_init)
    acc = acc_ref[...]
    o = (acc[:tq, :vdim] / acc[:tq, vdim:]) - lam * (acc[tq:, :vdim] / acc[tq:, vdim:])
    ms = jnp.mean(o * o, axis=-1, keepdims=True)
    o = o * lax.rsqrt(ms + RMS_EPS) * sub_ref[...] * (1.0 - lam_init)
    o_ref[0] = o.astype(o_ref.dtype)


def _diff_attention(q, k, v, lq1, lk1, lq2, lk2, subln, lam_init, tq):
    bsz, seq, dfw = q.shape
    vdim = 2 * DT_HEAD
    heads = dfw // vdim
    lamv = jnp.zeros((SUBLANES, LANES), F32)
    for i, t in enumerate((lq1, lk1, lq2, lk2)):
        lamv = lamv.at[i, :t.shape[0]].set(t)
    kern = functools.partial(_diff_attn_kernel, tq=tq, lam_init=lam_init)
    kv_spec = pl.BlockSpec((1, seq, vdim), lambda b, h, i: (b, 0, h))
    return pl.pallas_call(
        kern,
        grid=(bsz, heads, seq // tq),
        in_specs=[
            _const_spec((SUBLANES, LANES)),
            _const_spec((1, vdim)),
            pl.BlockSpec((1, tq, vdim), lambda b, h, i: (b, i, h)),
            kv_spec, kv_spec,
        ],
        out_specs=pl.BlockSpec((1, tq, vdim), lambda b, h, i: (b, i, h)),
        out_shape=jax.ShapeDtypeStruct((bsz, seq, dfw), BF16),
        scratch_shapes=[pltpu.VMEM((2 * tq, vdim), BF16), pltpu.VMEM((2 * tq, LANES), F32),
                        pltpu.VMEM((2 * tq, 2 * vdim), F32)],
        compiler_params=_cparams("arbitrary", "arbitrary", "arbitrary"),
        name="diff_attention",
    )(lamv, subln.reshape(1, vdim), q, k, v)


def _mix_ffn_kernel(*refs, n_in, hid_chunk):
    x_ref, mod_ref, g_ref = refs[:3]
    act_refs = refs[3:3 + n_in]
    w_refs = refs[3 + n_in:3 + 2 * n_in]
    wg_ref, wu_ref, wd_ref, o_ref = refs[3 + 2 * n_in:]
    mix = None
    for a_ref, w_ref in zip(act_refs, w_refs):
        a = a_ref[0] if len(a_ref.shape) == 3 else a_ref[...]
        t = _dot(a, w_ref[...])
        mix = t if mix is None else mix + t
    x1 = x_ref[0] + mod_ref[0, 2:3, :] * mix
    h = _modnorm(x1, g_ref[...], mod_ref[0, 4:5, :], mod_ref[0, 3:4, :]).astype(BF16)
    acc = None
    for j in range(wg_ref.shape[2] // hid_chunk):
        sl = slice(j * hid_chunk, (j + 1) * hid_chunk)
        gate = _dot(h, wg_ref[0, :, sl])
        up = _dot(h, wu_ref[0, :, sl])
        act = (gate * jax.nn.sigmoid(gate) * up).astype(BF16)
        t = _dot(act, wd_ref[0, sl, :])
        acc = t if acc is None else acc + t
    o_ref[0] = x1 + mod_ref[0, 5:6, :] * acc


def _mix_ffn(x, mod, g, acts, weights, ffn, tm, hid_chunk, w_specs=None):
    bsz, seq, d = x.shape
    if w_specs is None:
        w_specs = [_const_spec(w.shape) for w in weights]
    wg, wu, wd, layer = ffn
    assert wg.shape[2] % hid_chunk == 0

    def layer_spec(w):
        return pl.BlockSpec((1,) + w.shape[1:], lambda i, b: (layer, 0, 0),
                            pipeline_mode=pl.Buffered(1))

    tok = pl.BlockSpec((1, tm, d), lambda i, b: (b, i, 0))
    return pl.pallas_call(
        functools.partial(_mix_ffn_kernel, n_in=len(acts), hid_chunk=hid_chunk),
        grid=(seq // tm, bsz),
        in_specs=[tok, pl.BlockSpec((1, SUBLANES, d), lambda i, b: (b, 0, 0)), _const_spec((1, d))]
                 + [spec for _, spec in acts]
                 + w_specs
                 + [layer_spec(wg), layer_spec(wu), layer_spec(wd)],
        out_specs=tok,
        out_shape=jax.ShapeDtypeStruct((bsz, seq, d), F32),
        compiler_params=_cparams("arbitrary", "arbitrary"),
        name="mix_ffn",
    )(x, mod, g.reshape(1, d), *[a for a, _ in acts], *weights, wg, wu, wd)


def _rwkv_in_kernel(x_ref, xp_ref, mod_ref, g_ref, mu_ref, wr_ref, wk_ref, wv_ref, w1_ref, w2_ref,
                    a1_ref, a2_ref, g1_ref, g2_ref, w0_ref, a0_ref, kkw_ref, kaw_ref, bd_ref,
                    r_out, lw_out, k_out, v_out, kk_out, a_out, g_out):
    i = pl.program_id(1)
    g = g_ref[...]
    scale = mod_ref[0, 1:2, :]
    shift = mod_ref[0, 0:1, :]
    h = _modnorm(x_ref[0], g, scale, shift)
    hp = _modnorm(xp_ref[0][SUBLANES - 1:SUBLANES, :], g, scale, shift)
    hp = jnp.where(i == 0, 0.0, hp)
    row = lax.broadcasted_iota(jnp.int32, h.shape, 0)
    h_prev = jnp.where(row == 0, hp, pltpu.roll(h, 1, axis=0))
    dx = h_prev - h

    def lerp(j):
        return (h + dx * mu_ref[j:j + 1, :]).astype(BF16)

    r_out[0] = _dot(lerp(0), wr_ref[...])
    wl = jnp.tanh(_dot(lerp(1), w1_ref[...]))
    wdec = w0_ref[...] + _dot(wl.astype(BF16), w2_ref[...])
    w = jnp.minimum(wdec, 0.0) - jnp.log1p(jnp.exp(-jnp.abs(wdec))) - 0.5
    lw_out[0] = -jnp.exp(w)
    k = _dot(lerp(2), wk_ref[...])
    v_out[0] = _dot(lerp(3), wv_ref[...])
    al = _dot(lerp(4), a1_ref[...])
    a = jax.nn.sigmoid(a0_ref[...] + _dot(al.astype(BF16), a2_ref[...]))
    a_out[0] = a
    gl = jax.nn.sigmoid(_dot(lerp(5), g1_ref[...]))
    g_out[0] = _dot(gl.astype(BF16), g2_ref[...])
    kk = k * kkw_ref[...]
    bd = bd_ref[...]
    for j in range(kk.shape[1] // LANES):
        kj = kk[:, j * LANES:(j + 1) * LANES]
        ss = _dot((kj * kj).astype(BF16), bd) * float(RWKV_HEAD)
        kk_out[0, :, j * LANES:(j + 1) * LANES] = kj / jnp.maximum(jnp.sqrt(ss), 1e-12)
    k_out[0] = k * (1.0 + (a - 1.0) * kaw_ref[...])


def _rwkv_in(x, mod, g, mu, w_r, w_k, w_v, w0, w1, w2, a0, a1, a2, g1, g2, k_k, k_a, tm):
    bsz, seq, d = x.shape
    tok = pl.BlockSpec((1, tm, d), lambda b, i: (b, i, 0))
    prev = pl.BlockSpec((1, SUBLANES, d),
                        lambda b, i: (b, jnp.maximum(i * (tm // SUBLANES) - 1, 0), 0))
    bf = lambda w: w.astype(BF16)
    vec = lambda t: t.reshape(1, d)
    consts = [vec(g), mu, bf(w_r), bf(w_k), bf(w_v), bf(w1), bf(w2), bf(a1), bf(a2), bf(g1), bf(g2),
              vec(w0), vec(a0), vec(k_k), vec(k_a), _group_mean_matrix(LANES, RWKV_HEAD)]
    out = jax.ShapeDtypeStruct((bsz, seq, d), F32)
    return pl.pallas_call(
        _rwkv_in_kernel,
        grid=(bsz, seq // tm),
        in_specs=[tok, prev, pl.BlockSpec((1, SUBLANES, d), lambda b, i: (b, 0, 0))]
                 + [_const_spec(t.shape) for t in consts],
        out_specs=[tok] * 7,
        out_shape=[out] * 7,
        compiler_params=_cparams("arbitrary", "arbitrary"),
        name="rwkv_in_proj",
    )(x, x, mod, *consts)


def _rwkv_rec_kernel(r_ref, lw_ref, k_ref, v_ref, kk_ref, a_ref, g_ref, rk_ref, lng_ref, lnb_ref,
                     tri_ref, bd_ref, o_ref, m_ref, y_ref, *, chunk):
    tstep = pl.program_id(2)

    @pl.when(tstep == 0)
    def _():
        m_ref[...] = jnp.zeros_like(m_ref)

    tb, width = r_ref.shape[1:]
    hd = RWKV_HEAD
    gw = m_ref.shape[-1]
    nh = gw // hd
    groups = range(width // gw)
    colsl = [slice(q * gw, (q + 1) * gw) for q in groups]
    lane = lax.broadcasted_iota(jnp.int32, (chunk, gw), 1)
    trow = lax.broadcasted_iota(jnp.int32, (chunk, gw), 0)
    jpos = lane % hd
    strict = jpos < trow
    incl = jpos <= trow
    eye = jnp.where(jpos == trow, 1.0, 0.0)
    head_of_lane = lane // hd
    in_head = [head_of_lane == h for h in range(nh)]
    sq_r = lax.broadcasted_iota(jnp.int32, (gw, gw), 0)
    sq_c = lax.broadcasted_iota(jnp.int32, (gw, gw), 1)
    same_head = (sq_r // hd) == (sq_c // hd)
    diag = sq_r == sq_c
    tri = tri_ref[...]
    bd = bd_ref[...]
    bf = lambda t: t.astype(BF16)

    def blockdiag(y):
        return bf(jnp.concatenate([jnp.where(in_head[h], y, 0.0) for h in range(nh)], axis=0))

    def blockdiag_t(x):
        xt = jnp.concatenate([x] * nh, axis=0).T
        return bf(jnp.where(same_head, xt, 0.0))

    inst = [(c, q) for c in range(tb // chunk) for q in groups]
    cls = []
    for c, q in inst:
        lw = lw_ref[0, c * chunk:(c + 1) * chunk, colsl[q]]
        hi = bf(lw)
        r1 = lw - hi.astype(F32)
        mid = bf(r1)
        lo = bf(r1 - mid.astype(F32))
        cls.append((lw, _dot(tri, hi) + _dot(tri, mid) + _dot(tri, lo)))
    opnd = []
    for (c, q), (lw, cl) in zip(inst, cls):
        rows = slice(c * chunk, (c + 1) * chunk)
        cols = colsl[q]
        k = k_ref[0, rows, cols]
        kk = kk_ref[0, rows, cols]
        b = kk * a_ref[0, rows, cols]
        v = v_ref[0, rows, cols]
        cl_end = cl[chunk - 1:chunk, :]
        p_inv = jnp.exp(-cl)
        p_tail = jnp.exp(cl_end - cl)
        ar = jnp.concatenate([bf(-kk * jnp.exp(cl - lw)), bf(r_ref[0, rows, cols] * jnp.exp(cl))],
                             axis=0)
        tail_t = bf(jnp.concatenate([b * p_tail, k * p_tail], axis=0).T)
        pl_full = jnp.broadcast_to(jnp.exp(cl_end), (gw, gw))
        pl_col = jnp.sum(jnp.where(diag, pl_full, 0.0), axis=1, keepdims=True)
        opnd.append((ar, blockdiag_t(b * p_inv), blockdiag_t(k * p_inv), bf(v), blockdiag(v),
                     tail_t, pl_col))
    g_b = [_dot(o[0], o[1]) for o in opnd]
    g_k = [_dot(o[0], o[2]) for o in opnd]
    nms = [jnp.where(strict, g[:chunk], 0.0) for g in g_b]
    tinv = [eye + nm for nm in nms]
    pw = [bf(_dot(bf(nm), blockdiag(nm))) for nm in nms]
    for _ in range(chunk.bit_length() - 3):
        both = [_dot(jnp.concatenate([bf(t), p2], axis=0), blockdiag(p2)) for t, p2 in zip(tinv, pw)]
        tinv = [t + bo[:chunk] for t, bo in zip(tinv, both)]
        pw = [bf(bo[chunk:]) for bo in both]
    tinv = [bf(t + _dot(bf(t), blockdiag(p2))) for t, p2 in zip(tinv, pw)]
    g_ak = [bf(jnp.where(strict, g[:chunk], 0.0)) for g in g_k]
    g_rb = [bf(jnp.where(incl, g[chunk:], 0.0)) for g in g_b]
    g_rk = [bf(jnp.where(incl, g[chunk:], 0.0)) for g in g_k]

    state = [m_ref[q] for q in groups]
    for c in range(tb // chunk):
        ids = [c * len(groups) + q for q in groups]
        m0_bf = [bf(state[q]) for q in groups]
        rhs = [_dot(jnp.concatenate([opnd[i][0][:chunk], g_ak[i]], axis=1),
                    jnp.concatenate([m0_bf[q], opnd[i][4]], axis=0)) for q, i in zip(groups, ids)]
        u = [_dot(tinv[i], blockdiag(rhs[q])) for q, i in zip(groups, ids)]
        for q, i in zip(groups, ids):
            y_ref[c * chunk:(c + 1) * chunk, colsl[q]] = _dot(
                jnp.concatenate([opnd[i][0][chunk:], g_rb[i], g_rk[i]], axis=1),
                jnp.concatenate([m0_bf[q], blockdiag(u[q]), opnd[i][4]], axis=0))
        for q, i in zip(groups, ids):
            upd = _dot(opnd[i][5], jnp.concatenate([bf(u[q]), opnd[i][3]], axis=0))
            state[q] = opnd[i][6] * state[q] + jnp.where(same_head, upd, 0.0)
    for q in groups:
        m_ref[q] = state[q]

    npair = width // LANES
    for p in range(npair):
        cols = slice(p * LANES, (p + 1) * LANES)
        y = y_ref[:, cols]
        mean = _dot_x2(y, bd)
        dlt = y - mean
        var = _dot(bf(dlt * dlt), bd)
        yn = dlt * lax.rsqrt(var + GN_EPS) * lng_ref[:, cols] + lnb_ref[:, cols]
        rk_sum = _dot(bf(r_ref[0, :, cols] * k_ref[0, :, cols] * rk_ref[:, cols]), bd) * float(hd)
        out = (yn + rk_sum * v_ref[0, :, cols]) * g_ref[0, :, cols]
        o_ref[0, :, cols] = out.astype(o_ref.dtype)


def _rwkv_recurrence(r, lw, k, v, kk, a, g, r_k, ln_g, ln_b, tb, chunk, wblk):
    bsz, seq, d = r.shape
    gw = 2 * RWKV_HEAD
    assert chunk == RWKV_HEAD and wblk % gw == 0
    tok = pl.BlockSpec((1, tb, wblk), lambda b, j, t: (b, t, j))
    vec = pl.BlockSpec((1, wblk), lambda b, j, t: (0, j))
    tri = jnp.tril(jnp.ones((chunk, chunk), F32)).astype(BF16)
    bd = _group_mean_matrix(LANES, RWKV_HEAD)
    return pl.pallas_call(
        functools.partial(_rwkv_rec_kernel, chunk=chunk),
        grid=(bsz, d // wblk, seq // tb),
        in_specs=[tok] * 7 + [vec] * 3 + [_const_spec(tri.shape), _const_spec(bd.shape)],
        out_specs=tok,
        out_shape=jax.ShapeDtypeStruct((bsz, seq, d), BF16),
        scratch_shapes=[pltpu.VMEM((wblk // gw, gw, gw), F32),
                        pltpu.VMEM((tb, wblk), F32)],
        compiler_params=_cparams("arbitrary", "arbitrary", "arbitrary"),
        name="rwkv_recurrence",
    )(r, lw, k, v, kk, a, g, r_k.reshape(1, d), ln_g.reshape(1, d), ln_b.reshape(1, d), tri, bd)


def _odd_layer(x, mod, norm_mix, norm_ffn, ffn, mu, w_r, w_k, w_v, w_o, w0, w1,
               w2, a0, a1, a2, g1, g2, k_k, k_a, r_k, ln_g, ln_b):
    bsz, seq, d = x.shape
    r, lw, k, v, kk, a, g = _rwkv_in(x, mod, norm_mix, mu, w_r, w_k, w_v, w0, w1, w2, a0, a1, a2,
                                     g1, g2, k_k, k_a, _pick_tile(seq, 512))
    yg = _rwkv_recurrence(r, lw, k, v, kk, a, g, r_k, ln_g, ln_b, _pick_tile(seq, 256), 64,
                          d)
    tm = _pick_tile(seq, 512)
    acts = [(yg, pl.BlockSpec((1, tm, d), lambda i, b: (b, i, 0)))]
    return _mix_ffn(x, mod, norm_ffn, acts, [w_o.astype(BF16)], ffn, tm, 256)


def _pick_tile(n, pref):
    t = min(n, pref)
    assert n % t == 0, (n, t)
    return t


def _even_layer(x, mod, tables, lam_init, norm_mix, norm_ffn, ffn, w_in,
                lam_re, lam_im, log_dt, b_re, b_im, c_re, c_im, d_skip, w_glu, q_norm, k_norm,
                lq1, lk1, lq2, lk2, subln, w_out):
    bsz, seq, d = x.shape
    s5w = lam_re.shape[0] * b_re.shape[-1]
    dfw = (w_in.shape[1] - s5w) // 3
    tm = _pick_tile(seq, 512)
    u2, q, k, v = _even_in(x, mod, norm_mix, w_in, q_norm, k_norm, tables, s5w, dfw, tm)
    ys = _s5_mixer_chunked(u2, bsz, lam_re, lam_im, log_dt, b_re, b_im,
                           c_re, c_im, d_skip.reshape(-1), w_glu, _pick_tile(seq, 512))
    att = _diff_attention(q, k, v, lq1, lk1, lq2, lk2, subln, lam_init, _pick_tile(seq, 512))
    w_out_bf = w_out.astype(BF16)
    assert s5w == dfw
    acts = [
        (ys, pl.BlockSpec((tm, s5w), lambda i, b: (i, b))),
        (att, pl.BlockSpec((1, tm, dfw), lambda i, b: (b, i, 0))),
    ]
    w_specs = [pl.BlockSpec((s5w, d), lambda i, b: (0, 0), pipeline_mode=pl.Buffered(1)),
               pl.BlockSpec((dfw, d), lambda i, b: (1, 0), pipeline_mode=pl.Buffered(1))]
    return _mix_ffn(x, mod, norm_ffn, acts, [w_out_bf, w_out_bf], ffn, tm, 256, w_specs)


def kernel(x, c, positions, w_ada, b_ada, norm_mix, norm_ffn, ffn_w_gate, ffn_w_up, ffn_w_down,
           ev_w_in, ev_s5_lam_re, ev_s5_lam_im, ev_s5_log_dt, ev_s5_b_re, ev_s5_b_im, ev_s5_c_re,
           ev_s5_c_im, ev_s5_d, ev_s5_w_glu, ev_q_norm, ev_k_norm, ev_lambda_q1, ev_lambda_k1,
           ev_lambda_q2, ev_lambda_k2, ev_subln, ev_w_out, od_mu, od_w_r, od_w_k, od_w_v, od_w_o,
           od_w0, od_w1, od_w2, od_a0, od_a1, od_a2, od_g1, od_g2, od_k_k, od_k_a, od_r_k,
           od_ln_g, od_ln_b):
    depth = w_ada.shape[0]
    mod = _ada_mod(c, w_ada, b_ada)
    tables = _rope_tables(positions)
    ffn_bf = (ffn_w_gate.astype(BF16), ffn_w_up.astype(BF16), ffn_w_down.astype(BF16))
    for l in range(depth):
        ffn = ffn_bf + (l,)
        if l % 2 == 0:
            e = l // 2
            lam_init = 0.8 - 0.6 * math.exp(-0.3 * l)
            x = _even_layer(x, mod[l], tables, lam_init, norm_mix[l], norm_ffn[l], ffn,
                            ev_w_in[e], ev_s5_lam_re[e],
                            ev_s5_lam_im[e], ev_s5_log_dt[e], ev_s5_b_re[e], ev_s5_b_im[e],
                            ev_s5_c_re[e], ev_s5_c_im[e], ev_s5_d[e], ev_s5_w_glu[e], ev_q_norm[e],
                            ev_k_norm[e], ev_lambda_q1[e], ev_lambda_k1[e], ev_lambda_q2[e],
                            ev_lambda_k2[e], ev_subln[e], ev_w_out[e])
        else:
            o = l // 2
            x = _odd_layer(x, mod[l], norm_mix[l], norm_ffn[l], ffn,
                           od_mu[o], od_w_r[o], od_w_k[o], od_w_v[o], od_w_o[o],
                           od_w0[o], od_w1[o], od_w2[o], od_a0[o], od_a1[o], od_a2[o], od_g1[o],
                           od_g2[o], od_k_k[o], od_k_a[o], od_r_k[o], od_ln_g[o], od_ln_b[o])
    return x
```

```python
import functools
import math

import jax
import jax.numpy as jnp
from jax import lax
from jax.experimental import pallas as pl
from jax.experimental.pallas import tpu as pltpu

F32 = jnp.float32
BF16 = jnp.bfloat16

RMS_EPS = 1e-6
GN_EPS = 64e-5
ROPE_THETA = 500000.0
DT_HEAD = 64
ROT_DIM = DT_HEAD // 4
RWKV_HEAD = 64
LANES = 128
SUBLANES = 8
VMEM_LIMIT = 56 * 1024 * 1024


def _cparams(*sem):
    return pltpu.CompilerParams(dimension_semantics=sem, vmem_limit_bytes=VMEM_LIMIT)


def _const_spec(shape):
    nd = len(shape)
    return pl.BlockSpec(shape, lambda *_: (0,) * nd, pipeline_mode=pl.Buffered(1))


def _dot(a, b):
    return jnp.dot(a, b, preferred_element_type=F32)


def _dot_nt(a, b):
    return lax.dot_general(a, b, (((1,), (1,)), ((), ())), preferred_element_type=F32)


def _split_bf16(x):
    hi = x.astype(BF16)
    lo = (x - hi.astype(F32)).astype(BF16)
    return hi, lo


def _dot_x2(x, w_bf16):
    hi, lo = _split_bf16(x)
    return _dot(hi, w_bf16) + _dot(lo, w_bf16)


def _dot_x2_rhs(w_bf16, x):
    hi, lo = _split_bf16(x)
    return _dot(w_bf16, hi) + _dot(w_bf16, lo)


def _modnorm(x, g, scale, shift):
    ms = jnp.mean(x * x, axis=-1, keepdims=True)
    return (x * lax.rsqrt(ms + RMS_EPS)) * (g * (1.0 + scale)) + shift


def _ada_kernel(c_ref, w_ref, b_ref, o_ref):
    c = c_ref[...]
    ca = c * jax.nn.sigmoid(c)
    hi, lo = _split_bf16(ca)
    w = w_ref[0]
    w_hi, w_lo = _split_bf16(w)
    o_ref[0] = _dot(hi, w_hi) + _dot(lo, w_hi) + _dot(hi, w_lo) + b_ref[0]


def _ada_mod(c, w_ada, b_ada):
    depth, d, n = w_ada.shape
    bsz = c.shape[0]
    rows = -(-bsz // SUBLANES) * SUBLANES
    c_pad = jnp.zeros((rows, d), F32).at[:bsz].set(c)
    tn = n // 4
    out = pl.pallas_call(
        _ada_kernel,
        grid=(depth, n // tn),
        in_specs=[
            pl.BlockSpec((rows, d), lambda l, j: (0, 0)),
            pl.BlockSpec((1, d, tn), lambda l, j: (l, 0, j)),
            pl.BlockSpec((1, 1, tn), lambda l, j: (l, 0, j)),
        ],
        out_specs=pl.BlockSpec((1, rows, tn), lambda l, j: (l, 0, j)),
        out_shape=jax.ShapeDtypeStruct((depth, rows, n), F32),
        compiler_params=_cparams("arbitrary", "arbitrary"),
        name="ada_mod",
    )(c_pad, w_ada, b_ada.reshape(depth, 1, n))
    mod = out[:, :bsz].reshape(depth, bsz, 6, d)
    return jnp.pad(mod, ((0, 0), (0, 0), (0, SUBLANES - 6), (0, 0)))


def _rope_trig_kernel(p_ref, f_ref, o_ref):
    half = f_ref.shape[0]
    pad = jnp.zeros((LANES - 2 * half, LANES), F32)
    for r in range(p_ref.shape[0]):
        a = p_ref[r:r + 1, :] * f_ref[...]
        o_ref[r * LANES:(r + 1) * LANES, :] = jnp.concatenate([jnp.cos(a), jnp.sin(a), pad], axis=0).T


def _rope_tables(positions):
    bsz, seq = positions.shape
    half = ROT_DIM // 2
    inv_freq = ROPE_THETA ** (-jnp.arange(0, ROT_DIM, 2, dtype=F32) / ROT_DIM)
    rows = bsz * seq // LANES
    tr = min(rows, SUBLANES)
    cs = pl.pallas_call(
        _rope_trig_kernel,
        grid=(rows // tr,),
        in_specs=[pl.BlockSpec((tr, LANES), lambda i: (i, 0)), _const_spec((half, LANES))],
        out_specs=pl.BlockSpec((tr * LANES, LANES), lambda i: (i, 0)),
        out_shape=jax.ShapeDtypeStruct((bsz * seq, LANES), F32),
        compiler_params=_cparams("arbitrary"),
        name="rope_trig",
    )(positions.astype(F32).reshape(rows, LANES), jnp.broadcast_to(inv_freq[:, None], (half, LANES)))
    return cs.reshape(bsz, seq, LANES)


def _expand_rope(cs):
    half = ROT_DIM // 2
    lane = lax.broadcasted_iota(jnp.int32, cs.shape, 1)
    c0 = jnp.where(lane < half, cs, jnp.where(lane < ROT_DIM, pltpu.roll(cs, half, axis=1), 1.0))
    sa0 = jnp.where(lane < half, -pltpu.roll(cs, LANES - half, axis=1), 0.0)
    sb0 = jnp.where((lane >= half) & (lane < ROT_DIM), cs, 0.0)
    second = lane >= DT_HEAD
    return (jnp.where(second, pltpu.roll(c0, DT_HEAD, axis=1), c0),
            jnp.where(second, pltpu.roll(sa0, DT_HEAD, axis=1), sa0),
            jnp.where(second, pltpu.roll(sb0, DT_HEAD, axis=1), sb0))


def _group_mean_matrix(width, group):
    idx = jnp.arange(width) // group
    return jnp.where(idx[:, None] == idx[None, :], 1.0 / group, 0.0).astype(BF16)


def _even_in_kernel(x_ref, mod_ref, g_ref, w_ref, qn_ref, kn_ref, cs_ref, bd_ref,
                    u_ref, q_ref, k_ref, v_ref, *, s5w, dfw):
    x = x_ref[0]
    h = _modnorm(x, g_ref[...], mod_ref[0, 1:2, :], mod_ref[0, 0:1, :])
    proj = _dot(h.astype(BF16), w_ref[...])
    u_ref[...] = proj[:, :s5w]
    cosv, sav, sbv = _expand_rope(cs_ref[0])
    half = ROT_DIM // 2

    def norm_rope(t, gn, out_ref, post_scale):
        ms = _dot((t * t).astype(BF16), bd_ref[...])
        t = t * lax.rsqrt(ms + RMS_EPS) * gn
        for j in range(dfw // LANES):
            tj = t[:, j * LANES:(j + 1) * LANES]
            up = pltpu.roll(tj, LANES - half, axis=1)
            dn = pltpu.roll(tj, half, axis=1)
            rj = tj * cosv + up * sav + dn * sbv
            out_ref[0, :, j * LANES:(j + 1) * LANES] = (rj * post_scale).astype(out_ref.dtype)

    norm_rope(proj[:, s5w:s5w + dfw], qn_ref[...], q_ref, DT_HEAD ** -0.5 * math.log2(math.e))
    norm_rope(proj[:, s5w + dfw:s5w + 2 * dfw], kn_ref[...], k_ref, 1.0)
    v_ref[0] = proj[:, s5w + 2 * dfw:].astype(v_ref.dtype)


def _even_in(x, mod, g, w_in, q_norm, k_norm, tables, s5w, dfw, tm):
    bsz, seq, d = x.shape
    ncol = w_in.shape[1]
    rep = dfw // DT_HEAD
    kern = functools.partial(_even_in_kernel, s5w=s5w, dfw=dfw)
    tok = lambda w: pl.BlockSpec((1, tm, w), lambda i, b: (b, i, 0))
    return pl.pallas_call(
        kern,
        grid=(seq // tm, bsz),
        in_specs=[
            tok(d),
            pl.BlockSpec((1, SUBLANES, d), lambda i, b: (b, 0, 0)),
            _const_spec((1, d)),
            _const_spec((d, ncol)),
            _const_spec((1, dfw)),
            _const_spec((1, dfw)),
            tok(LANES),
            _const_spec((dfw, dfw)),
        ],
        out_specs=[
            pl.BlockSpec((tm, s5w), lambda i, b: (i, b)),
            tok(dfw), tok(dfw), tok(dfw),
        ],
        out_shape=[
            jax.ShapeDtypeStruct((seq, bsz * s5w), F32),
            jax.ShapeDtypeStruct((bsz, seq, dfw), BF16),
            jax.ShapeDtypeStruct((bsz, seq, dfw), BF16),
            jax.ShapeDtypeStruct((bsz, seq, dfw), BF16),
        ],
        compiler_params=_cparams("arbitrary", "arbitrary"),
        name="even_in_proj",
    )(x, mod, g.reshape(1, d), w_in.astype(BF16),
      jnp.tile(q_norm, rep).reshape(1, dfw), jnp.tile(k_norm, rep).reshape(1, dfw),
      tables, _group_mean_matrix(dfw, DT_HEAD))


S5_CHUNK = 8


def _s5_maps_kernel(lr_ref, li_ref, ldt_ref, lrc_ref, lic_ref, ldtc_ref, br_ref, bi_ref,
                    lrt_ref, lit_ref, ldtt_ref, cr_ref, ci_ref,
                    kmat_out, wre_out, wim_out, mre_out, mim_out, a8r_out, a8i_out, *, hh):
    rr = S5_CHUNK
    nblk = kmat_out.shape[0]
    pp = lrc_ref.shape[1]
    gb = LANES // hh

    def abar(lr, li, ldt):
        dt = jnp.exp(ldt)
        mag = jnp.exp(lr * dt)
        return mag * jnp.cos(li * dt), mag * jnp.sin(li * dt)

    def cmul(ar, ai, br, bi):
        return ar * br - ai * bi, ar * bi + ai * br

    def mm3(a, b):
        a_hi, a_lo = _split_bf16(a)
        b_hi, b_lo = _split_bf16(b)
        return _dot(a_hi, b_hi) + _dot(a_hi, b_lo) + _dot(a_lo, b_hi)

    def iota2(shape, axis):
        return lax.broadcasted_iota(jnp.int32, shape, axis)

    same_kk = (iota2((LANES, LANES), 0) // hh) == (iota2((LANES, LANES), 1) // hh)
    same_w = (iota2((LANES, gb * pp), 0) // hh) == (iota2((LANES, gb * pp), 1) // pp)
    same_m = (iota2((gb * pp, LANES), 0) // pp) == (iota2((gb * pp, LANES), 1) // hh)
    lane_tile = jnp.where(iota2((pp, gb * pp), 1) % pp == iota2((pp, gb * pp), 0), 1.0, 0.0
                          ).astype(BF16)

    lr, li = lrc_ref[...], lic_ref[...]
    a_r, a_i = abar(lr, li, ldtc_ref[...])
    den = lr * lr + li * li
    num_r = a_r - 1.0
    q_r = (num_r * lr + a_i * li) / den
    q_i = (a_i * lr - num_r * li) / den
    bb_r, bb_i = cmul(q_r, q_i, br_ref[...], bi_ref[...])
    pr, pi = bb_r, bb_i
    for i in range(rr - 1, -1, -1):
        for out, val in ((wre_out, pr), (wim_out, pi)):
            tiled = _dot(val.astype(BF16), lane_tile)
            for b in range(nblk):
                out[b, i * LANES:(i + 1) * LANES, :] = jnp.where(
                    same_w, tiled[b * LANES:(b + 1) * LANES], 0.0).astype(BF16)
        if i:
            pr, pi = cmul(a_r, a_i, pr, pi)

    kmat_out[...] = jnp.zeros_like(kmat_out)
    t_ar, t_ai = abar(lrt_ref[...], lit_ref[...], ldtt_ref[...])
    car, cai = cr_ref[...], ci_ref[...]
    for k in range(rr + 1):
        if k:
            car, cai = cmul(car, cai, t_ar, t_ai)
            for out, val in ((mre_out, car), (mim_out, -cai)):
                for b in range(nblk):
                    rows = jnp.concatenate([val[:, b * LANES:(b + 1) * LANES]] * gb, axis=0)
                    out[b, :, (k - 1) * LANES:k * LANES] = jnp.where(same_m, rows, 0.0
                                                                     ).astype(BF16)
        if k < rr:
            for b in range(nblk):
                sl = slice(b * LANES, (b + 1) * LANES)
                kk = jnp.where(same_kk, mm3(bb_r[sl], car[:, sl]) - mm3(bb_i[sl], cai[:, sl]), 0.0
                               ).astype(BF16)
                for j in range(rr - k):
                    kmat_out[b, j * LANES:(j + 1) * LANES, (j + k) * LANES:(j + k + 1) * LANES] = kk
    e_r, e_i = abar(lr_ref[...], li_ref[...], ldt_ref[...])
    for _ in range(S5_CHUNK.bit_length() - 1):
        e_r, e_i = cmul(e_r, e_i, e_r, e_i)
    pr, pi = e_r, e_i
    for s in range(a8r_out.shape[0]):
        if s:
            pr, pi = cmul(pr, pi, e_r, e_i)
        a8r_out[s] = pr
        a8i_out[s] = pi


def _s5_chunk_kernel(u_ref, kmat_ref, wre_ref, wim_ref, mre_ref, mim_ref, dsk_ref, a1r_ref, a1i_ref,
                     pwr_ref, pwi_ref, wglu_ref, perm_ref, permt_ref, o_ref, xr_ref, xi_ref,
                     cr_ref, ci_ref, *, bsz, sub, lane_chunk):
    nrow, nstate = xr_ref.shape
    tchunk = u_ref.shape[0]
    width = wglu_ref.shape[0]
    nsub = tchunk // sub
    rps = sub // S5_CHUNK * bsz
    nblk, gl = kmat_ref.shape[0], kmat_ref.shape[1] // S5_CHUNK
    spart = wre_ref.shape[2]
    step = pl.program_id(0)

    @pl.when(step == 0)
    def _():
        cr_ref[...] = jnp.zeros_like(cr_ref)
        ci_ref[...] = jnp.zeros_like(ci_ref)

    blk = u_ref[...]
    parts = []
    for s in range(nsub):
        u_bt = jnp.concatenate([blk[s * sub:(s + 1) * sub, b * width:(b + 1) * width]
                                for b in range(bsz)], axis=0)
        parts.append(_dot_x2_rhs(perm_ref[...], u_bt))
    u_j = [jnp.concatenate([p[j * rps:(j + 1) * rps] for p in parts], axis=0)
           for j in range(S5_CHUNK)]
    u_jb = [t.astype(BF16) for t in u_j]
    u_g = [jnp.concatenate([t[:, g * gl:(g + 1) * gl] for t in u_jb], axis=1)
           for g in range(nblk)]

    for g in range(nblk):
        xr_ref[:, g * spart:(g + 1) * spart] = _dot(u_g[g], wre_ref[g])
        xi_ref[:, g * spart:(g + 1) * spart] = _dot(u_g[g], wim_ref[g])

    old_r = cr_ref[...]
    old_i = ci_ref[...]
    row = lax.broadcasted_iota(jnp.int32, (SUBLANES, lane_chunk), 0)
    steps_per_tile = SUBLANES // bsz
    ntiles = nrow // SUBLANES
    for c in range(nstate // lane_chunk):
        cols = pl.ds(c * lane_chunk, lane_chunk)
        a1r = a1r_ref[:, cols]
        a1i = a1i_ref[:, cols]
        pwr = pwr_ref[:, cols]
        pwi = pwi_ref[:, cols]

        def tile_body(i, carry):
            pr, pi = carry
            base = pl.multiple_of(i * SUBLANES, SUBLANES)
            zr = xr_ref[pl.ds(base, SUBLANES), cols]
            zi = xi_ref[pl.ds(base, SUBLANES), cols]
            sh = bsz
            apr, api = a1r, a1i
            for _ in range(steps_per_tile.bit_length() - 1):
                sr = jnp.where(row >= sh, pltpu.roll(zr, sh, axis=0), 0.0)
                si = jnp.where(row >= sh, pltpu.roll(zi, sh, axis=0), 0.0)
                zr, zi = zr + apr * sr - api * si, zi + apr * si + api * sr
                apr, api = apr * apr - api * api, 2.0 * apr * api
                sh *= 2
            last = SUBLANES - bsz
            br_, bi_ = pr, pi
            sh = bsz
            while sh < SUBLANES:
                br_ = jnp.where(row >= last, br_, pltpu.roll(br_, SUBLANES - sh, axis=0))
                bi_ = jnp.where(row >= last, bi_, pltpu.roll(bi_, SUBLANES - sh, axis=0))
                last -= sh
                sh *= 2
            xr = zr + pwr * br_ - pwi * bi_
            xi = zi + pwr * bi_ + pwi * br_
            xr_ref[pl.ds(base, SUBLANES), cols] = xr
            xi_ref[pl.ds(base, SUBLANES), cols] = xi
            return xr, xi

        fr, fi = lax.fori_loop(0, ntiles, tile_body, (cr_ref[:, cols], ci_ref[:, cols]))
        cr_ref[:, cols] = fr
        ci_ref[:, cols] = fi

    row8 = lax.broadcasted_iota(jnp.int32, (SUBLANES, nstate), 0)

    def state_in(x, old):
        xs = pltpu.roll(x, bsz, axis=0)
        first = jnp.where(row8 < bsz, pltpu.roll(old, bsz, axis=0), xs[:SUBLANES])
        return jnp.concatenate([first, xs[SUBLANES:]], axis=0).astype(BF16)

    xin_r = state_in(xr_ref[...], old_r)
    xin_i = state_in(xi_ref[...], old_i)

    tw = 2 * gl

    def k_part(g):
        strips = []
        for s in range(S5_CHUNK // 2):
            kdim = (2 * s + 2) * gl
            strips.append(_dot(u_g[g][:, :kdim], kmat_ref[g, :kdim, s * tw:(s + 1) * tw]))
        return jnp.concatenate(strips, axis=1)

    y_g = [k_part(g)
           + _dot(xin_r[:, g * spart:(g + 1) * spart], mre_ref[g])
           + _dot(xin_i[:, g * spart:(g + 1) * spart], mim_ref[g]) for g in range(nblk)]
    y = jnp.concatenate(
        [jnp.concatenate([y_g[g][:, t * gl:(t + 1) * gl] for g in range(nblk)], axis=1)
         + dsk_ref[...] * u_j[t] for t in range(S5_CHUNK)], axis=0)
    y = jax.nn.gelu(y)
    gate = jax.nn.sigmoid(_dot(y.astype(BF16), wglu_ref[...]))
    out = (y * gate).astype(BF16)
    for s in range(nsub):
        o_s = jnp.concatenate([out[t * nrow + s * rps:t * nrow + (s + 1) * rps]
                               for t in range(S5_CHUNK)], axis=0)
        o_bt = _dot(permt_ref[...], o_s)
        for b in range(bsz):
            o_ref[s * sub:(s + 1) * sub, b * width:(b + 1) * width] = (
                o_bt[b * sub:(b + 1) * sub].astype(o_ref.dtype))


def _s5_mixer_chunked(u2, bsz, lam_re, lam_im, log_dt, b_re, b_im, c_re, c_im, d_skip, w_glu,
                      tchunk):
    seq = u2.shape[0]
    width = u2.shape[1] // bsz
    g, p = lam_re.shape
    hh = b_re.shape[-1]
    gh = g * hh
    nstate = g * p
    rr = S5_CHUNK
    ldt = jnp.broadcast_to(log_dt[:, None], (g, p))
    rep_c = lambda t: jnp.repeat(t, hh, axis=0)
    rep_t = lambda t: jnp.repeat(t.T, hh, axis=1)
    b_c = lambda b: b.transpose(0, 2, 1).reshape(gh, p)
    c_t = lambda c: c.transpose(2, 0, 1).reshape(p, gh)
    spt = SUBLANES // bsz
    gb = LANES // hh
    nblk = g // gb
    blk_w = rr * LANES
    bf = lambda *s: jax.ShapeDtypeStruct(s, BF16)
    f32 = lambda *s: jax.ShapeDtypeStruct(s, F32)
    kmat, w_re, w_im, m_re, m_im, a8_r, a8_i = pl.pallas_call(
        functools.partial(_s5_maps_kernel, hh=hh),
        out_shape=[bf(nblk, blk_w, blk_w), bf(nblk, blk_w, gb * p), bf(nblk, blk_w, gb * p),
                   bf(nblk, gb * p, blk_w), bf(nblk, gb * p, blk_w), f32(spt, g, p), f32(spt, g, p)],
        compiler_params=pltpu.CompilerParams(vmem_limit_bytes=VMEM_LIMIT),
        name="s5_chunk_maps",
    )(lam_re, lam_im, ldt, rep_c(lam_re), rep_c(lam_im), rep_c(ldt), b_c(b_re), b_c(b_im),
      rep_t(lam_re), rep_t(lam_im), rep_t(ldt), c_t(c_re), c_t(c_im))
    pw_r = jnp.repeat(a8_r.reshape(spt, nstate), bsz, axis=0)
    pw_i = jnp.repeat(a8_i.reshape(spt, nstate), bsz, axis=0)
    a1r = jnp.broadcast_to(pw_r[0:1], (SUBLANES, nstate))
    a1i = jnp.broadcast_to(pw_i[0:1], (SUBLANES, nstate))
    consts = [kmat, w_re, w_im, m_re, m_im, d_skip.reshape(1, width),
              a1r, a1i, pw_r, pw_i, w_glu.astype(BF16)]

    sub = min(tchunk, 128)
    rows = sub * bsz
    dst = jnp.arange(rows)
    rps = sub // rr * bsz
    j_, rem = dst // rps, dst % rps
    src_of = (rem % bsz) * sub + (rem // bsz) * rr + j_
    perm = (jnp.arange(rows)[None, :] == src_of[:, None]).astype(BF16)
    nrow = tchunk // rr * bsz
    kern = functools.partial(_s5_chunk_kernel, bsz=bsz, sub=sub, lane_chunk=512)
    return pl.pallas_call(
        kern,
        grid=(seq // tchunk,),
        in_specs=[pl.BlockSpec((tchunk, bsz * width), lambda i: (i, 0))]
                 + [_const_spec(t.shape) for t in consts]
                 + [_const_spec((rows, rows)), _const_spec((rows, rows))],
        out_specs=pl.BlockSpec((tchunk, bsz * width), lambda i: (i, 0)),
        out_shape=jax.ShapeDtypeStruct((seq, bsz * width), BF16),
        scratch_shapes=[
            pltpu.VMEM((nrow, nstate), F32), pltpu.VMEM((nrow, nstate), F32),
            pltpu.VMEM((SUBLANES, nstate), F32), pltpu.VMEM((SUBLANES, nstate), F32),
        ],
        compiler_params=_cparams("arbitrary"),
        name="s5_chunked",
    )(u2, *consts, perm, perm.T)


def _diff_attn_kernel(lam_ref, sub_ref, q_ref, k_ref, v_ref, o_ref, q2_ref, m_ref, acc_ref,
                      *, tq, lam_init):
    qi = pl.program_id(2)
    vdim = v_ref.shape[-1]
    q = q_ref[0]
    lane = lax.broadcasted_iota(jnp.int32, q.shape, 1)
    zero = jnp.zeros_like(q)
    q2_ref[:tq, :] = jnp.where(lane < DT_HEAD, q, zero)
    q2_ref[tq:, :] = jnp.where(lane >= DT_HEAD, q, zero)
    m_ref[...] = jnp.full(m_ref.shape, -jnp.inf, F32)
    acc_ref[...] = jnp.zeros_like(acc_ref)
    all_rows = ((0, 2 * tq),)

    def gather(ref, row_slices):
        parts = [ref[a:b, :] for a, b in row_slices]
        return parts[0] if len(parts) == 1 else jnp.concatenate(parts, axis=0)

    def scores(kstart, ksize, row_slices=all_rows):
        start = pl.multiple_of(kstart, ksize)
        return _dot_nt(gather(q2_ref, row_slices), k_ref[0, pl.ds(start, ksize), :])

    def update(kstart, ksize, s, row_slices=all_rows, mask=None):
        start = pl.multiple_of(kstart, ksize)
        v_ext = jnp.concatenate([v_ref[0, pl.ds(start, ksize), :], jnp.ones((ksize, vdim), BF16)],
                                axis=1)
        if mask is not None:
            s = jnp.where(mask, s, -jnp.inf)
        part = s[:, :LANES]
        for t in range(1, ksize // LANES):
            part = jnp.maximum(part, s[:, t * LANES:(t + 1) * LANES])
        m_old = gather(m_ref, row_slices)
        m_new = jnp.maximum(m_old, jnp.max(part, axis=-1, keepdims=True))
        alpha = jnp.exp2(m_old - m_new)
        p = jnp.concatenate([jnp.exp2((s[:, t * LANES:(t + 1) * LANES] - m_new).astype(BF16))
                             for t in range(ksize // LANES)], axis=1)
        acc = (jnp.concatenate([alpha] * (2 * vdim // LANES), axis=1) * gather(acc_ref, row_slices)
               + _dot(p, v_ext))
        off = 0
        for a, b in row_slices:
            m_ref[a:b, :] = m_new[off:off + b - a]
            acc_ref[a:b, :] = acc[off:off + b - a]
            off += b - a

    def diagonal(kstart, with_previous):
        hq = tq // 2
        if with_previous:
            s_prev = scores(kstart - tq, tq)
        row_l = lax.broadcasted_iota(jnp.int32, (2 * tq, hq), 0)
        col_l = lax.broadcasted_iota(jnp.int32, (2 * tq, hq), 1)
        row_r = lax.broadcasted_iota(jnp.int32, (tq, hq), 0)
        col_r = lax.broadcasted_iota(jnp.int32, (tq, hq), 1)
        late_rows = ((hq, tq), (tq + hq, 2 * tq))
        s_left = scores(kstart, hq)
        s_right = scores(kstart + hq, hq, late_rows)
        if with_previous:
            update(kstart - tq, tq, s_prev)
        update(kstart, hq, s_left, mask=col_l <= row_l % tq)
        update(kstart + hq, hq, s_right, late_rows, col_r <= row_r % hq)

    def pair(j0):
        s0 = scores(j0 * tq, tq)
        s1 = scores((j0 + 1) * tq, tq)
        update(j0 * tq, tq, s0)
        update((j0 + 1) * tq, tq, s1)

    npairs = qi // 2
    lax.fori_loop(0, npairs, lambda jj, _: (pair(2 * jj), 0)[1], 0)

    @pl.when(qi % 2 == 1)
    def _():
        diagonal(qi * tq, True)

    @pl.when(qi % 2 == 0)
    def _():
        diagonal(qi * tq, False)

    lv = lam_ref[...]
    lam = (jnp.exp(jnp.sum(lv[0:1] * lv[1:2], axis=-1, keepdims=True))
           - jnp.exp(jnp.sum(lv[2:3] * lv[3:4], axis=-1, keepdims=True)) + lam_init)
    acc = acc_ref[...]
    o = (acc[:tq, :vdim] / acc[:tq, vdim:]) - lam * (acc[tq:, :vdim] / acc[tq:, vdim:])
    ms = jnp.mean(o * o, axis=-1, keepdims=True)
    o = o * lax.rsqrt(ms + RMS_EPS) * sub_ref[...] * (1.0 - lam_init)
    o_ref[0] = o.astype(o_ref.dtype)


def _diff_attention(q, k, v, lq1, lk1, lq2, lk2, subln, lam_init, tq):
    bsz, seq, dfw = q.shape
    vdim = 2 * DT_HEAD
    heads = dfw // vdim
    lamv = jnp.zeros((SUBLANES, LANES), F32)
    for i, t in enumerate((lq1, lk1, lq2, lk2)):
        lamv = lamv.at[i, :t.shape[0]].set(t)
    kern = functools.partial(_diff_attn_kernel, tq=tq, lam_init=lam_init)
    kv_spec = pl.BlockSpec((1, seq, vdim), lambda b, h, i: (b, 0, h))
    return pl.pallas_call(
        kern,
        grid=(bsz, heads, seq // tq),
        in_specs=[
            _const_spec((SUBLANES, LANES)),
            _const_spec((1, vdim)),
            pl.BlockSpec((1, tq, vdim), lambda b, h, i: (b, i, h)),
            kv_spec, kv_spec,
        ],
        out_specs=pl.BlockSpec((1, tq, vdim), lambda b, h, i: (b, i, h)),
        out_shape=jax.ShapeDtypeStruct((bsz, seq, dfw), BF16),
        scratch_shapes=[pltpu.VMEM((2 * tq, vdim), BF16), pltpu.VMEM((2 * tq, LANES), F32),
                        pltpu.VMEM((2 * tq, 2 * vdim), F32)],
        compiler_params=_cparams("arbitrary", "arbitrary", "arbitrary"),
        name="diff_attention",
    )(lamv, subln.reshape(1, vdim), q, k, v)


def _mix_ffn_kernel(*refs, n_in, hid_chunk):
    x_ref, mod_ref, g_ref = refs[:3]
    act_refs = refs[3:3 + n_in]
    w_refs = refs[3 + n_in:3 + 2 * n_in]
    wg_ref, wu_ref, wd_ref, o_ref = refs[3 + 2 * n_in:]
    mix = None
    for a_ref, w_ref in zip(act_refs, w_refs):
        a = a_ref[0] if len(a_ref.shape) == 3 else a_ref[...]
        t = _dot(a, w_ref[...])
        mix = t if mix is None else mix + t
    x1 = x_ref[0] + mod_ref[0, 2:3, :] * mix
    h = _modnorm(x1, g_ref[...], mod_ref[0, 4:5, :], mod_ref[0, 3:4, :]).astype(BF16)
    acc = None
    for j in range(wg_ref.shape[2] // hid_chunk):
        sl = slice(j * hid_chunk, (j + 1) * hid_chunk)
        gate = _dot(h, wg_ref[0, :, sl])
        up = _dot(h, wu_ref[0, :, sl])
        act = (gate * jax.nn.sigmoid(gate) * up).astype(BF16)
        t = _dot(act, wd_ref[0, sl, :])
        acc = t if acc is None else acc + t
    o_ref[0] = x1 + mod_ref[0, 5:6, :] * acc


def _mix_ffn(x, mod, g, acts, weights, ffn, tm, hid_chunk, w_specs=None):
    bsz, seq, d = x.shape
    if w_specs is None:
        w_specs = [_const_spec(w.shape) for w in weights]
    wg, wu, wd, layer = ffn
    assert wg.shape[2] % hid_chunk == 0

    def layer_spec(w):
        return pl.BlockSpec((1,) + w.shape[1:], lambda i, b: (layer, 0, 0),
                            pipeline_mode=pl.Buffered(1))

    tok = pl.BlockSpec((1, tm, d), lambda i, b: (b, i, 0))
    return pl.pallas_call(
        functools.partial(_mix_ffn_kernel, n_in=len(acts), hid_chunk=hid_chunk),
        grid=(seq // tm, bsz),
        in_specs=[tok, pl.BlockSpec((1, SUBLANES, d), lambda i, b: (b, 0, 0)), _const_spec((1, d))]
                 + [spec for _, spec in acts]
                 + w_specs
                 + [layer_spec(wg), layer_spec(wu), layer_spec(wd)],
        out_specs=tok,
        out_shape=jax.ShapeDtypeStruct((bsz, seq, d), F32),
        compiler_params=_cparams("arbitrary", "arbitrary"),
        name="mix_ffn",
    )(x, mod, g.reshape(1, d), *[a for a, _ in acts], *weights, wg, wu, wd)


def _rwkv_in_kernel(x_ref, xp_ref, mod_ref, g_ref, mu_ref, wr_ref, wk_ref, wv_ref, w1_ref, w2_ref,
                    a1_ref, a2_ref, g1_ref, g2_ref, w0_ref, a0_ref, kkw_ref, kaw_ref, bd_ref,
                    r_out, lw_out, k_out, v_out, kk_out, a_out, g_out):
    i = pl.program_id(1)
    g = g_ref[...]
    scale = mod_ref[0, 1:2, :]
    shift = mod_ref[0, 0:1, :]
    h = _modnorm(x_ref[0], g, scale, shift)
    hp = _modnorm(xp_ref[0][SUBLANES - 1:SUBLANES, :], g, scale, shift)
    hp = jnp.where(i == 0, 0.0, hp)
    row = lax.broadcasted_iota(jnp.int32, h.shape, 0)
    h_prev = jnp.where(row == 0, hp, pltpu.roll(h, 1, axis=0))
    dx = h_prev - h

    def lerp(j):
        return (h + dx * mu_ref[j:j + 1, :]).astype(BF16)

    r_out[0] = _dot(lerp(0), wr_ref[...])
    wl = jnp.tanh(_dot(lerp(1), w1_ref[...]))
    wdec = w0_ref[...] + _dot(wl.astype(BF16), w2_ref[...])
    w = jnp.minimum(wdec, 0.0) - jnp.log1p(jnp.exp(-jnp.abs(wdec))) - 0.5
    lw_out[0] = -jnp.exp(w)
    k = _dot(lerp(2), wk_ref[...])
    v_out[0] = _dot(lerp(3), wv_ref[...])
    al = _dot(lerp(4), a1_ref[...])
    a = jax.nn.sigmoid(a0_ref[...] + _dot(al.astype(BF16), a2_ref[...]))
    a_out[0] = a
    gl = jax.nn.sigmoid(_dot(lerp(5), g1_ref[...]))
    g_out[0] = _dot(gl.astype(BF16), g2_ref[...])
    kk = k * kkw_ref[...]
    bd = bd_ref[...]
    for j in range(kk.shape[1] // LANES):
        kj = kk[:, j * LANES:(j + 1) * LANES]
        ss = _dot((kj * kj).astype(BF16), bd) * float(RWKV_HEAD)
        kk_out[0, :, j * LANES:(j + 1) * LANES] = kj / jnp.maximum(jnp.sqrt(ss), 1e-12)
    k_out[0] = k * (1.0 + (a - 1.0) * kaw_ref[...])


def _rwkv_in(x, mod, g, mu, w_r, w_k, w_v, w0, w1, w2, a0, a1, a2, g1, g2, k_k, k_a, tm):
    bsz, seq, d = x.shape
    tok = pl.BlockSpec((1, tm, d), lambda b, i: (b, i, 0))
    prev = pl.BlockSpec((1, SUBLANES, d),
                        lambda b, i: (b, jnp.maximum(i * (tm // SUBLANES) - 1, 0), 0))
    bf = lambda w: w.astype(BF16)
    vec = lambda t: t.reshape(1, d)
    consts = [vec(g), mu, bf(w_r), bf(w_k), bf(w_v), bf(w1), bf(w2), bf(a1), bf(a2), bf(g1), bf(g2),
              vec(w0), vec(a0), vec(k_k), vec(k_a), _group_mean_matrix(LANES, RWKV_HEAD)]
    out = jax.ShapeDtypeStruct((bsz, seq, d), F32)
    return pl.pallas_call(
        _rwkv_in_kernel,
        grid=(bsz, seq // tm),
        in_specs=[tok, prev, pl.BlockSpec((1, SUBLANES, d), lambda b, i: (b, 0, 0))]
                 + [_const_spec(t.shape) for t in consts],
        out_specs=[tok] * 7,
        out_shape=[out] * 7,
        compiler_params=_cparams("arbitrary", "arbitrary"),
        name="rwkv_in_proj",
    )(x, x, mod, *consts)


def _rwkv_rec_kernel(r_ref, lw_ref, k_ref, v_ref, kk_ref, a_ref, g_ref, rk_ref, lng_ref, lnb_ref,
                     tri_ref, bd_ref, o_ref, m_ref, y_ref, *, chunk):
    tstep = pl.program_id(2)

    @pl.when(tstep == 0)
    def _():
        m_ref[...] = jnp.zeros_like(m_ref)

    tb, width = r_ref.shape[1:]
    hd = RWKV_HEAD
    gw = m_ref.shape[-1]
    nh = gw // hd
    groups = range(width // gw)
    colsl = [slice(q * gw, (q + 1) * gw) for q in groups]
    lane = lax.broadcasted_iota(jnp.int32, (chunk, gw), 1)
    trow = lax.broadcasted_iota(jnp.int32, (chunk, gw), 0)
    jpos = lane % hd
    strict = jpos < trow
    incl = jpos <= trow
    eye = jnp.where(jpos == trow, 1.0, 0.0)
    head_of_lane = lane // hd
    in_head = [head_of_lane == h for h in range(nh)]
    sq_r = lax.broadcasted_iota(jnp.int32, (gw, gw), 0)
    sq_c = lax.broadcasted_iota(jnp.int32, (gw, gw), 1)
    same_head = (sq_r // hd) == (sq_c // hd)
    diag = sq_r == sq_c
    tri = tri_ref[...]
    bd = bd_ref[...]
    bf = lambda t: t.astype(BF16)

    def blockdiag(y):
        return bf(jnp.concatenate([jnp.where(in_head[h], y, 0.0) for h in range(nh)], axis=0))

    def blockdiag_t(x):
        xt = jnp.concatenate([x] * nh, axis=0).T
        return bf(jnp.where(same_head, xt, 0.0))

    inst = [(c, q) for c in range(tb // chunk) for q in groups]
    cls = []
    for c, q in inst:
        lw = lw_ref[0, c * chunk:(c + 1) * chunk, colsl[q]]
        hi = bf(lw)
        r1 = lw - hi.astype(F32)
        mid = bf(r1)
        lo = bf(r1 - mid.astype(F32))
        cls.append((lw, _dot(tri, hi) + _dot(tri, mid) + _dot(tri, lo)))
    opnd = []
    for (c, q), (lw, cl) in zip(inst, cls):
        rows = slice(c * chunk, (c + 1) * chunk)
        cols = colsl[q]
        k = k_ref[0, rows, cols]
        kk = kk_ref[0, rows, cols]
        b = kk * a_ref[0, rows, cols]
        v = v_ref[0, rows, cols]
        cl_end = cl[chunk - 1:chunk, :]
        p_inv = jnp.exp(-cl)
        p_tail = jnp.exp(cl_end - cl)
        ar = jnp.concatenate([bf(-kk * jnp.exp(cl - lw)), bf(r_ref[0, rows, cols] * jnp.exp(cl))],
                             axis=0)
        tail_t = bf(jnp.concatenate([b * p_tail, k * p_tail], axis=0).T)
        pl_full = jnp.broadcast_to(jnp.exp(cl_end), (gw, gw))
        pl_col = jnp.sum(jnp.where(diag, pl_full, 0.0), axis=1, keepdims=True)
        opnd.append((ar, blockdiag_t(b * p_inv), blockdiag_t(k * p_inv), bf(v), blockdiag(v),
                     tail_t, pl_col))
    gram = [_dot(o[0], jnp.concatenate([o[1], o[2]], axis=1)) for o in opnd]
    g_b = [g[:, :gw] for g in gram]
    g_k = [g[:, gw:] for g in gram]
    nms = [jnp.where(strict, g[:chunk], 0.0) for g in g_b]
    tinv = [eye + nm for nm in nms]
    pw = [bf(_dot(bf(nm), blockdiag(nm))) for nm in nms]
    for _ in range(chunk.bit_length() - 3):
        both = [_dot(jnp.concatenate([bf(t), p2], axis=0), blockdiag(p2)) for t, p2 in zip(tinv, pw)]
        tinv = [t + bo[:chunk] for t, bo in zip(tinv, both)]
        pw = [bf(bo[chunk:]) for bo in both]
    tinv = [bf(t + _dot(bf(t), blockdiag(p2))) for t, p2 in zip(tinv, pw)]
    g_rb = [bf(jnp.where(incl, g[chunk:], 0.0)) for g in g_b]
    gk_v = [_dot(jnp.concatenate([bf(jnp.where(strict, g[:chunk], 0.0)),
                                  bf(jnp.where(incl, g[chunk:], 0.0))], axis=0), o[4])
            for g, o in zip(g_k, opnd)]

    state = [m_ref[q] for q in groups]
    for c in range(tb // chunk):
        ids = [c * len(groups) + q for q in groups]
        ar_m = [_dot(opnd[i][0], bf(state[q])) + gk_v[i] for q, i in zip(groups, ids)]
        u = [_dot(tinv[i], blockdiag(ar_m[q][:chunk])) for q, i in zip(groups, ids)]
        for q, i in zip(groups, ids):
            y_ref[c * chunk:(c + 1) * chunk, colsl[q]] = (
                ar_m[q][chunk:] + _dot(g_rb[i], blockdiag(u[q])))
        for q, i in zip(groups, ids):
            upd = _dot(opnd[i][5], jnp.concatenate([bf(u[q]), opnd[i][3]], axis=0))
            state[q] = opnd[i][6] * state[q] + jnp.where(same_head, upd, 0.0)
    for q in groups:
        m_ref[q] = state[q]

    npair = width // LANES
    for p in range(npair):
        cols = slice(p * LANES, (p + 1) * LANES)
        y = y_ref[:, cols]
        mean = _dot_x2(y, bd)
        dlt = y - mean
        var = _dot(bf(dlt * dlt), bd)
        yn = dlt * lax.rsqrt(var + GN_EPS) * lng_ref[:, cols] + lnb_ref[:, cols]
        rk_sum = _dot(bf(r_ref[0, :, cols] * k_ref[0, :, cols] * rk_ref[:, cols]), bd) * float(hd)
        out = (yn + rk_sum * v_ref[0, :, cols]) * g_ref[0, :, cols]
        o_ref[0, :, cols] = out.astype(o_ref.dtype)


def _rwkv_recurrence(r, lw, k, v, kk, a, g, r_k, ln_g, ln_b, tb, chunk, wblk):
    bsz, seq, d = r.shape
    gw = 2 * RWKV_HEAD
    assert chunk == RWKV_HEAD and wblk % gw == 0
    tok = pl.BlockSpec((1, tb, wblk), lambda b, j, t: (b, t, j))
    vec = pl.BlockSpec((1, wblk), lambda b, j, t: (0, j))
    tri = jnp.tril(jnp.ones((chunk, chunk), F32)).astype(BF16)
    bd = _group_mean_matrix(LANES, RWKV_HEAD)
    return pl.pallas_call(
        functools.partial(_rwkv_rec_kernel, chunk=chunk),
        grid=(bsz, d // wblk, seq // tb),
        in_specs=[tok] * 7 + [vec] * 3 + [_const_spec(tri.shape), _const_spec(bd.shape)],
        out_specs=tok,
        out_shape=jax.ShapeDtypeStruct((bsz, seq, d), BF16),
        scratch_shapes=[pltpu.VMEM((wblk // gw, gw, gw), F32),
                        pltpu.VMEM((tb, wblk), F32)],
        compiler_params=_cparams("arbitrary", "arbitrary", "arbitrary"),
        name="rwkv_recurrence",
    )(r, lw, k, v, kk, a, g, r_k.reshape(1, d), ln_g.reshape(1, d), ln_b.reshape(1, d), tri, bd)


def _odd_layer(x, mod, norm_mix, norm_ffn, ffn, mu, w_r, w_k, w_v, w_o, w0, w1,
               w2, a0, a1, a2, g1, g2, k_k, k_a, r_k, ln_g, ln_b):
    bsz, seq, d = x.shape
    r, lw, k, v, kk, a, g = _rwkv_in(x, mod, norm_mix, mu, w_r, w_k, w_v, w0, w1, w2, a0, a1, a2,
                                     g1, g2, k_k, k_a, _pick_tile(seq, 512))
    yg = _rwkv_recurrence(r, lw, k, v, kk, a, g, r_k, ln_g, ln_b, _pick_tile(seq, 256), 64,
                          d)
    tm = _pick_tile(seq, 512)
    acts = [(yg, pl.BlockSpec((1, tm, d), lambda i, b: (b, i, 0)))]
    return _mix_ffn(x, mod, norm_ffn, acts, [w_o.astype(BF16)], ffn, tm, 256)


def _pick_tile(n, pref):
    t = min(n, pref)
    assert n % t == 0, (n, t)
    return t


def _even_layer(x, mod, tables, lam_init, norm_mix, norm_ffn, ffn, w_in,
                lam_re, lam_im, log_dt, b_re, b_im, c_re, c_im, d_skip, w_glu, q_norm, k_norm,
                lq1, lk1, lq2, lk2, subln, w_out):
    bsz, seq, d = x.shape
    s5w = lam_re.shape[0] * b_re.shape[-1]
    dfw = (w_in.shape[1] - s5w) // 3
    tm = _pick_tile(seq, 512)
    u2, q, k, v = _even_in(x, mod, norm_mix, w_in, q_norm, k_norm, tables, s5w, dfw, tm)
    ys = _s5_mixer_chunked(u2, bsz, lam_re, lam_im, log_dt, b_re, b_im,
                           c_re, c_im, d_skip.reshape(-1), w_glu, _pick_tile(seq, 512))
    att = _diff_attention(q, k, v, lq1, lk1, lq2, lk2, subln, lam_init, _pick_tile(seq, 512))
    w_out_bf = w_out.astype(BF16)
    assert s5w == dfw
    acts = [
        (ys, pl.BlockSpec((tm, s5w), lambda i, b: (i, b))),
        (att, pl.BlockSpec((1, tm, dfw), lambda i, b: (b, i, 0))),
    ]
    w_specs = [pl.BlockSpec((s5w, d), lambda i, b: (0, 0), pipeline_mode=pl.Buffered(1)),
               pl.BlockSpec((dfw, d), lambda i, b: (1, 0), pipeline_mode=pl.Buffered(1))]
    return _mix_ffn(x, mod, norm_ffn, acts, [w_out_bf, w_out_bf], ffn, tm, 256, w_specs)


def kernel(x, c, positions, w_ada, b_ada, norm_mix, norm_ffn, ffn_w_gate, ffn_w_up, ffn_w_down,
           ev_w_in, ev_s5_lam_re, ev_s5_lam_im, ev_s5_log_dt, ev_s5_b_re, ev_s5_b_im, ev_s5_c_re,
           ev_s5_c_im, ev_s5_d, ev_s5_w_glu, ev_q_norm, ev_k_norm, ev_lambda_q1, ev_lambda_k1,
           ev_lambda_q2, ev_lambda_k2, ev_subln, ev_w_out, od_mu, od_w_r, od_w_k, od_w_v, od_w_o,
           od_w0, od_w1, od_w2, od_a0, od_a1, od_a2, od_g1, od_g2, od_k_k, od_k_a, od_r_k,
           od_ln_g, od_ln_b):
    depth = w_ada.shape[0]
    mod = _ada_mod(c, w_ada, b_ada)
    tables = _rope_tables(positions)
    ffn_bf = (ffn_w_gate.astype(BF16), ffn_w_up.astype(BF16), ffn_w_down.astype(BF16))
    for l in range(depth):
        ffn = ffn_bf + (l,)
        if l % 2 == 0:
            e = l // 2
            lam_init = 0.8 - 0.6 * math.exp(-0.3 * l)
            x = _even_layer(x, mod[l], tables, lam_init, norm_mix[l], norm_ffn[l], ffn,
                            ev_w_in[e], ev_s5_lam_re[e],
                            ev_s5_lam_im[e], ev_s5_log_dt[e], ev_s5_b_re[e], ev_s5_b_im[e],
                            ev_s5_c_re[e], ev_s5_c_im[e], ev_s5_d[e], ev_s5_w_glu[e], ev_q_norm[e],
                            ev_k_norm[e], ev_lambda_q1[e], ev_lambda_k1[e], ev_lambda_q2[e],
                            ev_lambda_k2[e], ev_subln[e], ev_w_out[e])
        else:
            o = l // 2
            x = _odd_layer(x, mod[l], norm_mix[l], norm_ffn[l], ffn,
                           od_mu[o], od_w_r[o], od_w_k[o], od_w_v[o], od_w_o[o],
                           od_w0[o], od_w1[o], od_w2[o], od_a0[o], od_a1[o], od_a2[o], od_g1[o],
                           od_g2[o], od_k_k[o], od_k_a[o], od_r_k[o], od_ln_g[o], od_ln_b[o])
    return x
```

```python
import functools
import math

import jax
import jax.numpy as jnp
from jax import lax
from jax.experimental import pallas as pl
from jax.experimental.pallas import tpu as pltpu

F32 = jnp.float32
BF16 = jnp.bfloat16

RMS_EPS = 1e-6
GN_EPS = 64e-5
ROPE_THETA = 500000.0
DT_HEAD = 64
ROT_DIM = DT_HEAD // 4
RWKV_HEAD = 64
LANES = 128
SUBLANES = 8
VMEM_LIMIT = 56 * 1024 * 1024


def _cparams(*sem):
    return pltpu.CompilerParams(dimension_semantics=sem, vmem_limit_bytes=VMEM_LIMIT)


def _const_spec(shape):
    nd = len(shape)
    return pl.BlockSpec(shape, lambda *_: (0,) * nd, pipeline_mode=pl.Buffered(1))


def _dot(a, b):
    return jnp.dot(a, b, preferred_element_type=F32)


def _dot_nt(a, b):
    return lax.dot_general(a, b, (((1,), (1,)), ((), ())), preferred_element_type=F32)


def _split_bf16(x):
    hi = x.astype(BF16)
    lo = (x - hi.astype(F32)).astype(BF16)
    return hi, lo


def _dot_x2(x, w_bf16):
    hi, lo = _split_bf16(x)
    return _dot(hi, w_bf16) + _dot(lo, w_bf16)


def _dot_x2_rhs(w_bf16, x):
    hi, lo = _split_bf16(x)
    return _dot(w_bf16, hi) + _dot(w_bf16, lo)


def _modnorm(x, g, scale, shift):
    ms = jnp.mean(x * x, axis=-1, keepdims=True)
    return (x * lax.rsqrt(ms + RMS_EPS)) * (g * (1.0 + scale)) + shift


def _ada_kernel(c_ref, w_ref, b_ref, o_ref):
    c = c_ref[...]
    ca = c * jax.nn.sigmoid(c)
    hi, lo = _split_bf16(ca)
    w = w_ref[0]
    w_hi, w_lo = _split_bf16(w)
    o_ref[0] = _dot(hi, w_hi) + _dot(lo, w_hi) + _dot(hi, w_lo) + b_ref[0]


def _ada_mod(c, w_ada, b_ada):
    depth, d, n = w_ada.shape
    bsz = c.shape[0]
    rows = -(-bsz // SUBLANES) * SUBLANES
    c_pad = jnp.zeros((rows, d), F32).at[:bsz].set(c)
    tn = n // 4
    out = pl.pallas_call(
        _ada_kernel,
        grid=(depth, n // tn),
        in_specs=[
            pl.BlockSpec((rows, d), lambda l, j: (0, 0)),
            pl.BlockSpec((1, d, tn), lambda l, j: (l, 0, j)),
            pl.BlockSpec((1, 1, tn), lambda l, j: (l, 0, j)),
        ],
        out_specs=pl.BlockSpec((1, rows, tn), lambda l, j: (l, 0, j)),
        out_shape=jax.ShapeDtypeStruct((depth, rows, n), F32),
        compiler_params=_cparams("arbitrary", "arbitrary"),
        name="ada_mod",
    )(c_pad, w_ada, b_ada.reshape(depth, 1, n))
    mod = out[:, :bsz].reshape(depth, bsz, 6, d)
    return jnp.pad(mod, ((0, 0), (0, 0), (0, SUBLANES - 6), (0, 0)))


def _rope_trig_kernel(p_ref, f_ref, o_ref):
    half = f_ref.shape[0]
    pad = jnp.zeros((LANES - 2 * half, LANES), F32)
    for r in range(p_ref.shape[0]):
        a = p_ref[r:r + 1, :] * f_ref[...]
        o_ref[r * LANES:(r + 1) * LANES, :] = jnp.concatenate([jnp.cos(a), jnp.sin(a), pad], axis=0).T


def _rope_tables(positions):
    bsz, seq = positions.shape
    half = ROT_DIM // 2
    inv_freq = ROPE_THETA ** (-jnp.arange(0, ROT_DIM, 2, dtype=F32) / ROT_DIM)
    rows = bsz * seq // LANES
    tr = min(rows, SUBLANES)
    cs = pl.pallas_call(
        _rope_trig_kernel,
        grid=(rows // tr,),
        in_specs=[pl.BlockSpec((tr, LANES), lambda i: (i, 0)), _const_spec((half, LANES))],
        out_specs=pl.BlockSpec((tr * LANES, LANES), lambda i: (i, 0)),
        out_shape=jax.ShapeDtypeStruct((bsz * seq, LANES), F32),
        compiler_params=_cparams("arbitrary"),
        name="rope_trig",
    )(positions.astype(F32).reshape(rows, LANES), jnp.broadcast_to(inv_freq[:, None], (half, LANES)))
    return cs.reshape(bsz, seq, LANES)


def _expand_rope(cs):
    half = ROT_DIM // 2
    lane = lax.broadcasted_iota(jnp.int32, cs.shape, 1)
    c0 = jnp.where(lane < half, cs, jnp.where(lane < ROT_DIM, pltpu.roll(cs, half, axis=1), 1.0))
    sa0 = jnp.where(lane < half, -pltpu.roll(cs, LANES - half, axis=1), 0.0)
    sb0 = jnp.where((lane >= half) & (lane < ROT_DIM), cs, 0.0)
    second = lane >= DT_HEAD
    return (jnp.where(second, pltpu.roll(c0, DT_HEAD, axis=1), c0),
            jnp.where(second, pltpu.roll(sa0, DT_HEAD, axis=1), sa0),
            jnp.where(second, pltpu.roll(sb0, DT_HEAD, axis=1), sb0))


def _group_mean_matrix(width, group):
    idx = jnp.arange(width) // group
    return jnp.where(idx[:, None] == idx[None, :], 1.0 / group, 0.0).astype(BF16)


def _even_in_kernel(x_ref, mod_ref, g_ref, w_ref, qn_ref, kn_ref, cs_ref, bd_ref,
                    u_ref, q_ref, k_ref, v_ref, *, s5w, dfw):
    x = x_ref[0]
    h = _modnorm(x, g_ref[...], mod_ref[0, 1:2, :], mod_ref[0, 0:1, :])
    proj = _dot(h.astype(BF16), w_ref[...])
    u_ref[...] = proj[:, :s5w]
    cosv, sav, sbv = _expand_rope(cs_ref[0])
    half = ROT_DIM // 2

    def norm_rope(t, gn, out_ref, post_scale):
        ms = _dot((t * t).astype(BF16), bd_ref[...])
        t = t * lax.rsqrt(ms + RMS_EPS) * gn
        for j in range(dfw // LANES):
            tj = t[:, j * LANES:(j + 1) * LANES]
            up = pltpu.roll(tj, LANES - half, axis=1)
            dn = pltpu.roll(tj, half, axis=1)
            rj = tj * cosv + up * sav + dn * sbv
            out_ref[0, :, j * LANES:(j + 1) * LANES] = (rj * post_scale).astype(out_ref.dtype)

    norm_rope(proj[:, s5w:s5w + dfw], qn_ref[...], q_ref, DT_HEAD ** -0.5 * math.log2(math.e))
    norm_rope(proj[:, s5w + dfw:s5w + 2 * dfw], kn_ref[...], k_ref, 1.0)
    v_ref[0] = proj[:, s5w + 2 * dfw:].astype(v_ref.dtype)


def _even_in(x, mod, g, w_in, q_norm, k_norm, tables, s5w, dfw, tm):
    bsz, seq, d = x.shape
    ncol = w_in.shape[1]
    rep = dfw // DT_HEAD
    kern = functools.partial(_even_in_kernel, s5w=s5w, dfw=dfw)
    tok = lambda w: pl.BlockSpec((1, tm, w), lambda i, b: (b, i, 0))
    return pl.pallas_call(
        kern,
        grid=(seq // tm, bsz),
        in_specs=[
            tok(d),
            pl.BlockSpec((1, SUBLANES, d), lambda i, b: (b, 0, 0)),
            _const_spec((1, d)),
            _const_spec((d, ncol)),
            _const_spec((1, dfw)),
            _const_spec((1, dfw)),
            tok(LANES),
            _const_spec((dfw, dfw)),
        ],
        out_specs=[
            pl.BlockSpec((tm, s5w), lambda i, b: (i, b)),
            tok(dfw), tok(dfw), tok(dfw),
        ],
        out_shape=[
            jax.ShapeDtypeStruct((seq, bsz * s5w), F32),
            jax.ShapeDtypeStruct((bsz, seq, dfw), BF16),
            jax.ShapeDtypeStruct((bsz, seq, dfw), BF16),
            jax.ShapeDtypeStruct((bsz, seq, dfw), BF16),
        ],
        compiler_params=_cparams("arbitrary", "arbitrary"),
        name="even_in_proj",
    )(x, mod, g.reshape(1, d), w_in.astype(BF16),
      jnp.tile(q_norm, rep).reshape(1, dfw), jnp.tile(k_norm, rep).reshape(1, dfw),
      tables, _group_mean_matrix(dfw, DT_HEAD))


S5_CHUNK = 8


def _s5_maps_kernel(lr_ref, li_ref, ldt_ref, lrc_ref, lic_ref, ldtc_ref, br_ref, bi_ref,
                    lrt_ref, lit_ref, ldtt_ref, cr_ref, ci_ref,
                    kmat_out, wre_out, wim_out, mre_out, mim_out, a8r_out, a8i_out, *, hh):
    rr = S5_CHUNK
    nblk = kmat_out.shape[0]
    pp = lrc_ref.shape[1]
    gb = LANES // hh

    def abar(lr, li, ldt):
        dt = jnp.exp(ldt)
        mag = jnp.exp(lr * dt)
        return mag * jnp.cos(li * dt), mag * jnp.sin(li * dt)

    def cmul(ar, ai, br, bi):
        return ar * br - ai * bi, ar * bi + ai * br

    def mm3(a, b):
        a_hi, a_lo = _split_bf16(a)
        b_hi, b_lo = _split_bf16(b)
        return _dot(a_hi, b_hi) + _dot(a_hi, b_lo) + _dot(a_lo, b_hi)

    def iota2(shape, axis):
        return lax.broadcasted_iota(jnp.int32, shape, axis)

    same_kk = (iota2((LANES, LANES), 0) // hh) == (iota2((LANES, LANES), 1) // hh)
    same_w = (iota2((LANES, gb * pp), 0) // hh) == (iota2((LANES, gb * pp), 1) // pp)
    same_m = (iota2((gb * pp, LANES), 0) // pp) == (iota2((gb * pp, LANES), 1) // hh)
    lane_tile = jnp.where(iota2((pp, gb * pp), 1) % pp == iota2((pp, gb * pp), 0), 1.0, 0.0
                          ).astype(BF16)

    lr, li = lrc_ref[...], lic_ref[...]
    a_r, a_i = abar(lr, li, ldtc_ref[...])
    den = lr * lr + li * li
    num_r = a_r - 1.0
    q_r = (num_r * lr + a_i * li) / den
    q_i = (a_i * lr - num_r * li) / den
    bb_r, bb_i = cmul(q_r, q_i, br_ref[...], bi_ref[...])
    pr, pi = bb_r, bb_i
    for i in range(rr - 1, -1, -1):
        for out, val in ((wre_out, pr), (wim_out, pi)):
            tiled = _dot(val.astype(BF16), lane_tile)
            for b in range(nblk):
                out[b, i * LANES:(i + 1) * LANES, :] = jnp.where(
                    same_w, tiled[b * LANES:(b + 1) * LANES], 0.0).astype(BF16)
        if i:
            pr, pi = cmul(a_r, a_i, pr, pi)

    kmat_out[...] = jnp.zeros_like(kmat_out)
    t_ar, t_ai = abar(lrt_ref[...], lit_ref[...], ldtt_ref[...])
    car, cai = cr_ref[...], ci_ref[...]
    for k in range(rr + 1):
        if k:
            car, cai = cmul(car, cai, t_ar, t_ai)
            for out, val in ((mre_out, car), (mim_out, -cai)):
                for b in range(nblk):
                    rows = jnp.concatenate([val[:, b * LANES:(b + 1) * LANES]] * gb, axis=0)
                    out[b, :, (k - 1) * LANES:k * LANES] = jnp.where(same_m, rows, 0.0
                                                                     ).astype(BF16)
        if k < rr:
            for b in range(nblk):
                sl = slice(b * LANES, (b + 1) * LANES)
                kk = jnp.where(same_kk, mm3(bb_r[sl], car[:, sl]) - mm3(bb_i[sl], cai[:, sl]), 0.0
                               ).astype(BF16)
                for j in range(rr - k):
                    kmat_out[b, j * LANES:(j + 1) * LANES, (j + k) * LANES:(j + k + 1) * LANES] = kk
    e_r, e_i = abar(lr_ref[...], li_ref[...], ldt_ref[...])
    for _ in range(S5_CHUNK.bit_length() - 1):
        e_r, e_i = cmul(e_r, e_i, e_r, e_i)
    pr, pi = e_r, e_i
    for s in range(a8r_out.shape[0]):
        if s:
            pr, pi = cmul(pr, pi, e_r, e_i)
        a8r_out[s] = pr
        a8i_out[s] = pi


def _s5_chunk_kernel(u_ref, kmat_ref, wre_ref, wim_ref, mre_ref, mim_ref, dsk_ref, a1r_ref, a1i_ref,
                     pwr_ref, pwi_ref, wglu_ref, perm_ref, permt_ref, o_ref, xr_ref, xi_ref,
                     cr_ref, ci_ref, *, bsz, sub, lane_chunk):
    nrow, nstate = xr_ref.shape
    tchunk = u_ref.shape[0]
    width = wglu_ref.shape[0]
    nsub = tchunk // sub
    rps = sub // S5_CHUNK * bsz
    nblk, gl = kmat_ref.shape[0], kmat_ref.shape[1] // S5_CHUNK
    spart = wre_ref.shape[2]
    step = pl.program_id(0)

    @pl.when(step == 0)
    def _():
        cr_ref[...] = jnp.zeros_like(cr_ref)
        ci_ref[...] = jnp.zeros_like(ci_ref)

    blk = u_ref[...]
    parts = []
    for s in range(nsub):
        u_bt = jnp.concatenate([blk[s * sub:(s + 1) * sub, b * width:(b + 1) * width]
                                for b in range(bsz)], axis=0)
        parts.append(_dot_x2_rhs(perm_ref[...], u_bt))
    u_j = [jnp.concatenate([p[j * rps:(j + 1) * rps] for p in parts], axis=0)
           for j in range(S5_CHUNK)]
    u_jb = [t.astype(BF16) for t in u_j]
    u_g = [jnp.concatenate([t[:, g * gl:(g + 1) * gl] for t in u_jb], axis=1)
           for g in range(nblk)]

    for g in range(nblk):
        xr_ref[:, g * spart:(g + 1) * spart] = _dot(u_g[g], wre_ref[g])
        xi_ref[:, g * spart:(g + 1) * spart] = _dot(u_g[g], wim_ref[g])

    old_r = cr_ref[...]
    old_i = ci_ref[...]
    row = lax.broadcasted_iota(jnp.int32, (SUBLANES, lane_chunk), 0)
    steps_per_tile = SUBLANES // bsz
    ntiles = nrow // SUBLANES
    for c in range(nstate // lane_chunk):
        cols = pl.ds(c * lane_chunk, lane_chunk)
        a1r = a1r_ref[:, cols]
        a1i = a1i_ref[:, cols]
        pwr = pwr_ref[:, cols]
        pwi = pwi_ref[:, cols]

        def tile_body(i, carry):
            pr, pi = carry
            base = pl.multiple_of(i * SUBLANES, SUBLANES)
            zr = xr_ref[pl.ds(base, SUBLANES), cols]
            zi = xi_ref[pl.ds(base, SUBLANES), cols]
            sh = bsz
            apr, api = a1r, a1i
            for _ in range(steps_per_tile.bit_length() - 1):
                sr = jnp.where(row >= sh, pltpu.roll(zr, sh, axis=0), 0.0)
                si = jnp.where(row >= sh, pltpu.roll(zi, sh, axis=0), 0.0)
                zr, zi = zr + apr * sr - api * si, zi + apr * si + api * sr
                apr, api = apr * apr - api * api, 2.0 * apr * api
                sh *= 2
            last = SUBLANES - bsz
            br_, bi_ = pr, pi
            sh = bsz
            while sh < SUBLANES:
                br_ = jnp.where(row >= last, br_, pltpu.roll(br_, SUBLANES - sh, axis=0))
                bi_ = jnp.where(row >= last, bi_, pltpu.roll(bi_, SUBLANES - sh, axis=0))
                last -= sh
                sh *= 2
            xr = zr + pwr * br_ - pwi * bi_
            xi = zi + pwr * bi_ + pwi * br_
            xr_ref[pl.ds(base, SUBLANES), cols] = xr
            xi_ref[pl.ds(base, SUBLANES), cols] = xi
            return xr, xi

        fr, fi = lax.fori_loop(0, ntiles, tile_body, (cr_ref[:, cols], ci_ref[:, cols]))
        cr_ref[:, cols] = fr
        ci_ref[:, cols] = fi

    row8 = lax.broadcasted_iota(jnp.int32, (SUBLANES, nstate), 0)

    def state_in(x, old):
        xs = pltpu.roll(x, bsz, axis=0)
        first = jnp.where(row8 < bsz, pltpu.roll(old, bsz, axis=0), xs[:SUBLANES])
        return jnp.concatenate([first, xs[SUBLANES:]], axis=0).astype(BF16)

    xin_r = state_in(xr_ref[...], old_r)
    xin_i = state_in(xi_ref[...], old_i)

    tw = 2 * gl

    def k_part(g):
        strips = []
        for s in range(S5_CHUNK // 2):
            kdim = (2 * s + 2) * gl
            strips.append(_dot(u_g[g][:, :kdim], kmat_ref[g, :kdim, s * tw:(s + 1) * tw]))
        return jnp.concatenate(strips, axis=1)

    y_g = [k_part(g)
           + _dot(xin_r[:, g * spart:(g + 1) * spart], mre_ref[g])
           + _dot(xin_i[:, g * spart:(g + 1) * spart], mim_ref[g]) for g in range(nblk)]
    y = jnp.concatenate(
        [jnp.concatenate([y_g[g][:, t * gl:(t + 1) * gl] for g in range(nblk)], axis=1)
         + dsk_ref[...] * u_j[t] for t in range(S5_CHUNK)], axis=0)
    y = jax.nn.gelu(y)
    gate = jax.nn.sigmoid(_dot(y.astype(BF16), wglu_ref[...]))
    out = (y * gate).astype(BF16)
    for s in range(nsub):
        o_s = jnp.concatenate([out[t * nrow + s * rps:t * nrow + (s + 1) * rps]
                               for t in range(S5_CHUNK)], axis=0)
        o_bt = _dot(permt_ref[...], o_s)
        for b in range(bsz):
            o_ref[s * sub:(s + 1) * sub, b * width:(b + 1) * width] = (
                o_bt[b * sub:(b + 1) * sub].astype(o_ref.dtype))


def _s5_mixer_chunked(u2, bsz, lam_re, lam_im, log_dt, b_re, b_im, c_re, c_im, d_skip, w_glu,
                      tchunk):
    seq = u2.shape[0]
    width = u2.shape[1] // bsz
    g, p = lam_re.shape
    hh = b_re.shape[-1]
    gh = g * hh
    nstate = g * p
    rr = S5_CHUNK
    ldt = jnp.broadcast_to(log_dt[:, None], (g, p))
    rep_c = lambda t: jnp.repeat(t, hh, axis=0)
    rep_t = lambda t: jnp.repeat(t.T, hh, axis=1)
    b_c = lambda b: b.transpose(0, 2, 1).reshape(gh, p)
    c_t = lambda c: c.transpose(2, 0, 1).reshape(p, gh)
    spt = SUBLANES // bsz
    gb = LANES // hh
    nblk = g // gb
    blk_w = rr * LANES
    bf = lambda *s: jax.ShapeDtypeStruct(s, BF16)
    f32 = lambda *s: jax.ShapeDtypeStruct(s, F32)
    kmat, w_re, w_im, m_re, m_im, a8_r, a8_i = pl.pallas_call(
        functools.partial(_s5_maps_kernel, hh=hh),
        out_shape=[bf(nblk, blk_w, blk_w), bf(nblk, blk_w, gb * p), bf(nblk, blk_w, gb * p),
                   bf(nblk, gb * p, blk_w), bf(nblk, gb * p, blk_w), f32(spt, g, p), f32(spt, g, p)],
        compiler_params=pltpu.CompilerParams(vmem_limit_bytes=VMEM_LIMIT),
        name="s5_chunk_maps",
    )(lam_re, lam_im, ldt, rep_c(lam_re), rep_c(lam_im), rep_c(ldt), b_c(b_re), b_c(b_im),
      rep_t(lam_re), rep_t(lam_im), rep_t(ldt), c_t(c_re), c_t(c_im))
    pw_r = jnp.repeat(a8_r.reshape(spt, nstate), bsz, axis=0)
    pw_i = jnp.repeat(a8_i.reshape(spt, nstate), bsz, axis=0)
    a1r = jnp.broadcast_to(pw_r[0:1], (SUBLANES, nstate))
    a1i = jnp.broadcast_to(pw_i[0:1], (SUBLANES, nstate))
    consts = [kmat, w_re, w_im, m_re, m_im, d_skip.reshape(1, width),
              a1r, a1i, pw_r, pw_i, w_glu.astype(BF16)]

    sub = min(tchunk, 128)
    rows = sub * bsz
    dst = jnp.arange(rows)
    rps = sub // rr * bsz
    j_, rem = dst // rps, dst % rps
    src_of = (rem % bsz) * sub + (rem // bsz) * rr + j_
    perm = (jnp.arange(rows)[None, :] == src_of[:, None]).astype(BF16)
    nrow = tchunk // rr * bsz
    kern = functools.partial(_s5_chunk_kernel, bsz=bsz, sub=sub, lane_chunk=512)
    return pl.pallas_call(
        kern,
        grid=(seq // tchunk,),
        in_specs=[pl.BlockSpec((tchunk, bsz * width), lambda i: (i, 0))]
                 + [_const_spec(t.shape) for t in consts]
                 + [_const_spec((rows, rows)), _const_spec((rows, rows))],
        out_specs=pl.BlockSpec((tchunk, bsz * width), lambda i: (i, 0)),
        out_shape=jax.ShapeDtypeStruct((seq, bsz * width), BF16),
        scratch_shapes=[
            pltpu.VMEM((nrow, nstate), F32), pltpu.VMEM((nrow, nstate), F32),
            pltpu.VMEM((SUBLANES, nstate), F32), pltpu.VMEM((SUBLANES, nstate), F32),
        ],
        compiler_params=_cparams("arbitrary"),
        name="s5_chunked",
    )(u2, *consts, perm, perm.T)


def _diff_attn_kernel(lam_ref, sub_ref, q_ref, k_ref, v_ref, o_ref, q2_ref, m_ref, acc_ref,
                      *, tq, lam_init):
    qi = pl.program_id(2)
    vdim = v_ref.shape[-1]
    q = q_ref[0]
    lane = lax.broadcasted_iota(jnp.int32, q.shape, 1)
    zero = jnp.zeros_like(q)
    q2_ref[:tq, :] = jnp.where(lane < DT_HEAD, q, zero)
    q2_ref[tq:, :] = jnp.where(lane >= DT_HEAD, q, zero)
    m_ref[...] = jnp.full(m_ref.shape, -jnp.inf, F32)
    acc_ref[...] = jnp.zeros_like(acc_ref)
    all_rows = ((0, 2 * tq),)

    def gather(ref, row_slices):
        parts = [ref[a:b, :] for a, b in row_slices]
        return parts[0] if len(parts) == 1 else jnp.concatenate(parts, axis=0)

    def scores(kstart, ksize, row_slices=all_rows):
        start = pl.multiple_of(kstart, ksize)
        return _dot_nt(gather(q2_ref, row_slices), k_ref[0, pl.ds(start, ksize), :])

    def update(kstart, ksize, s, row_slices=all_rows, mask=None):
        start = pl.multiple_of(kstart, ksize)
        v_ext = jnp.concatenate([v_ref[0, pl.ds(start, ksize), :], jnp.ones((ksize, vdim), BF16)],
                                axis=1)
        if mask is not None:
            s = jnp.where(mask, s, -jnp.inf)
        part = s[:, :LANES]
        for t in range(1, ksize // LANES):
            part = jnp.maximum(part, s[:, t * LANES:(t + 1) * LANES])
        m_old = gather(m_ref, row_slices)
        m_new = jnp.maximum(m_old, jnp.max(part, axis=-1, keepdims=True))
        alpha = jnp.exp2(m_old - m_new)
        p = jnp.concatenate([jnp.exp2((s[:, t * LANES:(t + 1) * LANES] - m_new).astype(BF16))
                             for t in range(ksize // LANES)], axis=1)
        acc = (jnp.concatenate([alpha] * (2 * vdim // LANES), axis=1) * gather(acc_ref, row_slices)
               + _dot(p, v_ext))
        off = 0
        for a, b in row_slices:
            m_ref[a:b, :] = m_new[off:off + b - a]
            acc_ref[a:b, :] = acc[off:off + b - a]
            off += b - a

    def diagonal(kstart, with_previous):
        hq = tq // 2
        if with_previous:
            s_prev = scores(kstart - tq, tq)
        row_l = lax.broadcasted_iota(jnp.int32, (2 * tq, hq), 0)
        col_l = lax.broadcasted_iota(jnp.int32, (2 * tq, hq), 1)
        row_r = lax.broadcasted_iota(jnp.int32, (tq, hq), 0)
        col_r = lax.broadcasted_iota(jnp.int32, (tq, hq), 1)
        late_rows = ((hq, tq), (tq + hq, 2 * tq))
        s_left = scores(kstart, hq)
        s_right = scores(kstart + hq, hq, late_rows)
        if with_previous:
            update(kstart - tq, tq, s_prev)
        update(kstart, hq, s_left, mask=col_l <= row_l % tq)
        update(kstart + hq, hq, s_right, late_rows, col_r <= row_r % hq)

    def pair(j0):
        s0 = scores(j0 * tq, tq)
        s1 = scores((j0 + 1) * tq, tq)
        update(j0 * tq, tq, s0)
        update((j0 + 1) * tq, tq, s1)

    npairs = qi // 2
    lax.fori_loop(0, npairs, lambda jj, _: (pair(2 * jj), 0)[1], 0)

    @pl.when(qi % 2 == 1)
    def _():
        diagonal(qi * tq, True)

    @pl.when(qi % 2 == 0)
    def _():
        diagonal(qi * tq, False)

    lv = lam_ref[...]
    lam = (jnp.exp(jnp.sum(lv[0:1] * lv[1:2], axis=-1, keepdims=True))
           - jnp.exp(jnp.sum(lv[2:3] * lv[3:4], axis=-1, keepdims=True)) + lam_init)
    acc = acc_ref[...]
    o = (acc[:tq, :vdim] / acc[:tq, vdim:]) - lam * (acc[tq:, :vdim] / acc[tq:, vdim:])
    ms = jnp.mean(o * o, axis=-1, keepdims=True)
    o = o * lax.rsqrt(ms + RMS_EPS) * sub_ref[...] * (1.0 - lam_init)
    o_ref[0] = o.astype(o_ref.dtype)


def _diff_attention(q, k, v, lq1, lk1, lq2, lk2, subln, lam_init, tq):
    bsz, seq, dfw = q.shape
    vdim = 2 * DT_HEAD
    heads = dfw // vdim
    lamv = jnp.zeros((SUBLANES, LANES), F32)
    for i, t in enumerate((lq1, lk1, lq2, lk2)):
        lamv = lamv.at[i, :t.shape[0]].set(t)
    kern = functools.partial(_diff_attn_kernel, tq=tq, lam_init=lam_init)
    kv_spec = pl.BlockSpec((1, seq, vdim), lambda b, h, i: (b, 0, h))
    return pl.pallas_call(
        kern,
        grid=(bsz, heads, seq // tq),
        in_specs=[
            _const_spec((SUBLANES, LANES)),
            _const_spec((1, vdim)),
            pl.BlockSpec((1, tq, vdim), lambda b, h, i: (b, i, h)),
            kv_spec, kv_spec,
        ],
        out_specs=pl.BlockSpec((1, tq, vdim), lambda b, h, i: (b, i, h)),
        out_shape=jax.ShapeDtypeStruct((bsz, seq, dfw), BF16),
        scratch_shapes=[pltpu.VMEM((2 * tq, vdim), BF16), pltpu.VMEM((2 * tq, LANES), F32),
                        pltpu.VMEM((2 * tq, 2 * vdim), F32)],
        compiler_params=_cparams("arbitrary", "arbitrary", "arbitrary"),
        name="diff_attention",
    )(lamv, subln.reshape(1, vdim), q, k, v)


def _mix_ffn_kernel(*refs, n_in, hid_chunk):
    x_ref, mod_ref, g_ref = refs[:3]
    act_refs = refs[3:3 + n_in]
    w_refs = refs[3 + n_in:3 + 2 * n_in]
    wg_ref, wu_ref, wd_ref, o_ref = refs[3 + 2 * n_in:]
    mix = None
    for a_ref, w_ref in zip(act_refs, w_refs):
        a = a_ref[0] if len(a_ref.shape) == 3 else a_ref[...]
        t = _dot(a, w_ref[...])
        mix = t if mix is None else mix + t
    x1 = x_ref[0] + mod_ref[0, 2:3, :] * mix
    h = _modnorm(x1, g_ref[...], mod_ref[0, 4:5, :], mod_ref[0, 3:4, :]).astype(BF16)
    acc = None
    for j in range(wg_ref.shape[2] // hid_chunk):
        sl = slice(j * hid_chunk, (j + 1) * hid_chunk)
        gate = _dot(h, wg_ref[0, :, sl])
        up = _dot(h, wu_ref[0, :, sl])
        act = (gate * jax.nn.sigmoid(gate) * up).astype(BF16)
        t = _dot(act, wd_ref[0, sl, :])
        acc = t if acc is None else acc + t
    o_ref[0] = x1 + mod_ref[0, 5:6, :] * acc


def _mix_ffn(x, mod, g, acts, weights, ffn, tm, hid_chunk, w_specs=None):
    bsz, seq, d = x.shape
    if w_specs is None:
        w_specs = [_const_spec(w.shape) for w in weights]
    wg, wu, wd, layer = ffn
    assert wg.shape[2] % hid_chunk == 0

    def layer_spec(w):
        return pl.BlockSpec((1,) + w.shape[1:], lambda i, b: (layer, 0, 0),
                            pipeline_mode=pl.Buffered(1))

    tok = pl.BlockSpec((1, tm, d), lambda i, b: (b, i, 0))
    return pl.pallas_call(
        functools.partial(_mix_ffn_kernel, n_in=len(acts), hid_chunk=hid_chunk),
        grid=(seq // tm, bsz),
        in_specs=[tok, pl.BlockSpec((1, SUBLANES, d), lambda i, b: (b, 0, 0)), _const_spec((1, d))]
                 + [spec for _, spec in acts]
                 + w_specs
                 + [layer_spec(wg), layer_spec(wu), layer_spec(wd)],
        out_specs=tok,
        out_shape=jax.ShapeDtypeStruct((bsz, seq, d), F32),
        compiler_params=_cparams("arbitrary", "arbitrary"),
        name="mix_ffn",
    )(x, mod, g.reshape(1, d), *[a for a, _ in acts], *weights, wg, wu, wd)


def _rwkv_in_kernel(x_ref, xp_ref, mod_ref, g_ref, mu_ref, wr_ref, wk_ref, wv_ref, w1_ref, w2_ref,
                    a1_ref, a2_ref, g1_ref, g2_ref, w0_ref, a0_ref, kkw_ref, kaw_ref, bd_ref,
                    r_out, lw_out, k_out, v_out, kk_out, a_out, g_out):
    i = pl.program_id(1)
    g = g_ref[...]
    scale = mod_ref[0, 1:2, :]
    shift = mod_ref[0, 0:1, :]
    h = _modnorm(x_ref[0], g, scale, shift)
    hp = _modnorm(xp_ref[0][SUBLANES - 1:SUBLANES, :], g, scale, shift)
    hp = jnp.where(i == 0, 0.0, hp)
    row = lax.broadcasted_iota(jnp.int32, h.shape, 0)
    h_prev = jnp.where(row == 0, hp, pltpu.roll(h, 1, axis=0))
    dx = h_prev - h

    def lerp(j):
        return (h + dx * mu_ref[j:j + 1, :]).astype(BF16)

    r_out[0] = _dot(lerp(0), wr_ref[...])
    wl = jnp.tanh(_dot(lerp(1), w1_ref[...]))
    wdec = w0_ref[...] + _dot(wl.astype(BF16), w2_ref[...])
    w = jnp.minimum(wdec, 0.0) - jnp.log1p(jnp.exp(-jnp.abs(wdec))) - 0.5
    lw_out[0] = -jnp.exp(w)
    k = _dot(lerp(2), wk_ref[...])
    v_out[0] = _dot(lerp(3), wv_ref[...])
    al = _dot(lerp(4), a1_ref[...])
    a = jax.nn.sigmoid(a0_ref[...] + _dot(al.astype(BF16), a2_ref[...]))
    a_out[0] = a
    gl = jax.nn.sigmoid(_dot(lerp(5), g1_ref[...]))
    g_out[0] = _dot(gl.astype(BF16), g2_ref[...])
    kk = k * kkw_ref[...]
    bd = bd_ref[...]
    for j in range(kk.shape[1] // LANES):
        kj = kk[:, j * LANES:(j + 1) * LANES]
        ss = _dot((kj * kj).astype(BF16), bd) * float(RWKV_HEAD)
        kk_out[0, :, j * LANES:(j + 1) * LANES] = kj / jnp.maximum(jnp.sqrt(ss), 1e-12)
    k_out[0] = k * (1.0 + (a - 1.0) * kaw_ref[...])


def _rwkv_in(x, mod, g, mu, w_r, w_k, w_v, w0, w1, w2, a0, a1, a2, g1, g2, k_k, k_a, tm):
    bsz, seq, d = x.shape
    tok = pl.BlockSpec((1, tm, d), lambda b, i: (b, i, 0))
    prev = pl.BlockSpec((1, SUBLANES, d),
                        lambda b, i: (b, jnp.maximum(i * (tm // SUBLANES) - 1, 0), 0))
    bf = lambda w: w.astype(BF16)
    vec = lambda t: t.reshape(1, d)
    consts = [vec(g), mu, bf(w_r), bf(w_k), bf(w_v), bf(w1), bf(w2), bf(a1), bf(a2), bf(g1), bf(g2),
              vec(w0), vec(a0), vec(k_k), vec(k_a), _group_mean_matrix(LANES, RWKV_HEAD)]
    out = jax.ShapeDtypeStruct((bsz, seq, d), F32)
    return pl.pallas_call(
        _rwkv_in_kernel,
        grid=(bsz, seq // tm),
        in_specs=[tok, prev, pl.BlockSpec((1, SUBLANES, d), lambda b, i: (b, 0, 0))]
                 + [_const_spec(t.shape) for t in consts],
        out_specs=[tok] * 7,
        out_shape=[out] * 7,
        compiler_params=_cparams("arbitrary", "arbitrary"),
        name="rwkv_in_proj",
    )(x, x, mod, *consts)


def _rwkv_rec_kernel(r_ref, lw_ref, k_ref, v_ref, kk_ref, a_ref, g_ref, rk_ref, lng_ref, lnb_ref,
                     tri_ref, bd_ref, o_ref, m_ref, y_ref, *, chunk):
    tstep = pl.program_id(2)

    @pl.when(tstep == 0)
    def _():
        m_ref[...] = jnp.zeros_like(m_ref)

    tb, width = r_ref.shape[1:]
    hd = RWKV_HEAD
    gw = m_ref.shape[-1]
    nh = gw // hd
    groups = range(width // gw)
    colsl = [slice(q * gw, (q + 1) * gw) for q in groups]
    lane = lax.broadcasted_iota(jnp.int32, (chunk, gw), 1)
    trow = lax.broadcasted_iota(jnp.int32, (chunk, gw), 0)
    jpos = lane % hd
    strict = jpos < trow
    incl = jpos <= trow
    eye = jnp.where(jpos == trow, 1.0, 0.0)
    head_of_lane = lane // hd
    in_head = [head_of_lane == h for h in range(nh)]
    sq_r = lax.broadcasted_iota(jnp.int32, (gw, gw), 0)
    sq_c = lax.broadcasted_iota(jnp.int32, (gw, gw), 1)
    same_head = (sq_r // hd) == (sq_c // hd)
    diag = sq_r == sq_c
    tri = tri_ref[...]
    bd = bd_ref[...]
    bf = lambda t: t.astype(BF16)

    def blockdiag(y):
        return bf(jnp.concatenate([jnp.where(in_head[h], y, 0.0) for h in range(nh)], axis=0))

    def blockdiag_t(x):
        xt = jnp.concatenate([x] * nh, axis=0).T
        return bf(jnp.where(same_head, xt, 0.0))

    inst = [(c, q) for c in range(tb // chunk) for q in groups]
    cls = []
    for c, q in inst:
        lw = lw_ref[0, c * chunk:(c + 1) * chunk, colsl[q]]
        hi = bf(lw)
        r1 = lw - hi.astype(F32)
        mid = bf(r1)
        lo = bf(r1 - mid.astype(F32))
        cls.append((lw, _dot(tri, hi) + _dot(tri, mid) + _dot(tri, lo)))
    opnd = []
    for (c, q), (lw, cl) in zip(inst, cls):
        rows = slice(c * chunk, (c + 1) * chunk)
        cols = colsl[q]
        k = k_ref[0, rows, cols]
        kk = kk_ref[0, rows, cols]
        b = kk * a_ref[0, rows, cols]
        v = v_ref[0, rows, cols]
        cl_end = cl[chunk - 1:chunk, :]
        p_inv = jnp.exp(-cl)
        p_tail = jnp.exp(cl_end - cl)
        ar = jnp.concatenate([bf(-kk * jnp.exp(cl - lw)), bf(r_ref[0, rows, cols] * jnp.exp(cl))],
                             axis=0)
        tail_t = bf(jnp.concatenate([b * p_tail, k * p_tail], axis=0).T)
        pl_full = jnp.broadcast_to(jnp.exp(cl_end), (gw, gw))
        pl_col = jnp.sum(jnp.where(diag, pl_full, 0.0), axis=1, keepdims=True)
        opnd.append((ar, blockdiag_t(b * p_inv), blockdiag_t(k * p_inv), bf(v), blockdiag(v),
                     tail_t, pl_col))
    g_b = [_dot(o[0], o[1]) for o in opnd]
    g_k = [_dot(o[0], o[2]) for o in opnd]
    nms = [jnp.where(strict, g[:chunk], 0.0) for g in g_b]
    tinv = [eye + nm for nm in nms]
    pw = [bf(_dot(bf(nm), blockdiag(nm))) for nm in nms]
    for _ in range(chunk.bit_length() - 3):
        both = [_dot(jnp.concatenate([bf(t), p2], axis=0), blockdiag(p2)) for t, p2 in zip(tinv, pw)]
        tinv = [t + bo[:chunk] for t, bo in zip(tinv, both)]
        pw = [bf(bo[chunk:]) for bo in both]
    tinv = [bf(t + _dot(bf(t), blockdiag(p2))) for t, p2 in zip(tinv, pw)]
    g_ak = [bf(jnp.where(strict, g[:chunk], 0.0)) for g in g_k]
    g_rb = [bf(jnp.where(incl, g[chunk:], 0.0)) for g in g_b]
    g_rk = [bf(jnp.where(incl, g[chunk:], 0.0)) for g in g_k]

    state = [m_ref[q] for q in groups]
    for c in range(tb // chunk):
        ids = [c * len(groups) + q for q in groups]
        m0_bf = [bf(state[q]) for q in groups]
        rhs = [_dot(jnp.concatenate([opnd[i][0][:chunk], g_ak[i]], axis=1),
                    jnp.concatenate([m0_bf[q], opnd[i][4]], axis=0)) for q, i in zip(groups, ids)]
        u = [_dot(tinv[i], blockdiag(rhs[q])) for q, i in zip(groups, ids)]
        for q, i in zip(groups, ids):
            y_ref[c * chunk:(c + 1) * chunk, colsl[q]] = _dot(
                jnp.concatenate([opnd[i][0][chunk:], g_rb[i], g_rk[i]], axis=1),
                jnp.concatenate([m0_bf[q], blockdiag(u[q]), opnd[i][4]], axis=0))
        for q, i in zip(groups, ids):
            upd = _dot(opnd[i][5], jnp.concatenate([bf(u[q]), opnd[i][3]], axis=0))
            state[q] = opnd[i][6] * state[q] + jnp.where(same_head, upd, 0.0)
    for q in groups:
        m_ref[q] = state[q]

    npair = width // LANES
    for p in range(npair):
        cols = slice(p * LANES, (p + 1) * LANES)
        y = y_ref[:, cols]
        mean = _dot_x2(y, bd)
        dlt = y - mean
        var = _dot(bf(dlt * dlt), bd)
        yn = dlt * lax.rsqrt(var + GN_EPS) * lng_ref[:, cols] + lnb_ref[:, cols]
        rk_sum = _dot(bf(r_ref[0, :, cols] * k_ref[0, :, cols] * rk_ref[:, cols]), bd) * float(hd)
        out = (yn + rk_sum * v_ref[0, :, cols]) * g_ref[0, :, cols]
        o_ref[0, :, cols] = out.astype(o_ref.dtype)


def _rwkv_recurrence(r, lw, k, v, kk, a, g, r_k, ln_g, ln_b, tb, chunk, wblk):
    bsz, seq, d = r.shape
    gw = 2 * RWKV_HEAD
    assert chunk == RWKV_HEAD and wblk % gw == 0
    tok = pl.BlockSpec((1, tb, wblk), lambda b, j, t: (b, t, j))
    vec = pl.BlockSpec((1, wblk), lambda b, j, t: (0, j))
    tri = jnp.tril(jnp.ones((chunk, chunk), F32)).astype(BF16)
    bd = _group_mean_matrix(LANES, RWKV_HEAD)
    return pl.pallas_call(
        functools.partial(_rwkv_rec_kernel, chunk=chunk),
        grid=(bsz, d // wblk, seq // tb),
        in_specs=[tok] * 7 + [vec] * 3 + [_const_spec(tri.shape), _const_spec(bd.shape)],
        out_specs=tok,
        out_shape=jax.ShapeDtypeStruct((bsz, seq, d), BF16),
        scratch_shapes=[pltpu.VMEM((wblk // gw, gw, gw), F32),
                        pltpu.VMEM((tb, wblk), F32)],
        compiler_params=_cparams("arbitrary", "arbitrary", "arbitrary"),
        name="rwkv_recurrence",
    )(r, lw, k, v, kk, a, g, r_k.reshape(1, d), ln_g.reshape(1, d), ln_b.reshape(1, d), tri, bd)


def _odd_layer(x, mod, norm_mix, norm_ffn, ffn, mu, w_r, w_k, w_v, w_o, w0, w1,
               w2, a0, a1, a2, g1, g2, k_k, k_a, r_k, ln_g, ln_b):
    bsz, seq, d = x.shape
    r, lw, k, v, kk, a, g = _rwkv_in(x, mod, norm_mix, mu, w_r, w_k, w_v, w0, w1, w2, a0, a1, a2,
                                     g1, g2, k_k, k_a, _pick_tile(seq, 512))
    yg = _rwkv_recurrence(r, lw, k, v, kk, a, g, r_k, ln_g, ln_b, _pick_tile(seq, 512), 64,
                          d)
    tm = _pick_tile(seq, 512)
    acts = [(yg, pl.BlockSpec((1, tm, d), lambda i, b: (b, i, 0)))]
    return _mix_ffn(x, mod, norm_ffn, acts, [w_o.astype(BF16)], ffn, tm, 256)


def _pick_tile(n, pref):
    t = min(n, pref)
    assert n % t == 0, (n, t)
    return t


def _even_layer(x, mod, tables, lam_init, norm_mix, norm_ffn, ffn, w_in,
                lam_re, lam_im, log_dt, b_re, b_im, c_re, c_im, d_skip, w_glu, q_norm, k_norm,
                lq1, lk1, lq2, lk2, subln, w_out):
    bsz, seq, d = x.shape
    s5w = lam_re.shape[0] * b_re.shape[-1]
    dfw = (w_in.shape[1] - s5w) // 3
    tm = _pick_tile(seq, 512)
    u2, q, k, v = _even_in(x, mod, norm_mix, w_in, q_norm, k_norm, tables, s5w, dfw, tm)
    ys = _s5_mixer_chunked(u2, bsz, lam_re, lam_im, log_dt, b_re, b_im,
                           c_re, c_im, d_skip.reshape(-1), w_glu, _pick_tile(seq, 512))
    att = _diff_attention(q, k, v, lq1, lk1, lq2, lk2, subln, lam_init, _pick_tile(seq, 512))
    w_out_bf = w_out.astype(BF16)
    assert s5w == dfw
    acts = [
        (ys, pl.BlockSpec((tm, s5w), lambda i, b: (i, b))),
        (att, pl.BlockSpec((1, tm, dfw), lambda i, b: (b, i, 0))),
    ]
    w_specs = [pl.BlockSpec((s5w, d), lambda i, b: (0, 0), pipeline_mode=pl.Buffered(1)),
               pl.BlockSpec((dfw, d), lambda i, b: (1, 0), pipeline_mode=pl.Buffered(1))]
    return _mix_ffn(x, mod, norm_ffn, acts, [w_out_bf, w_out_bf], ffn, tm, 256, w_specs)


def kernel(x, c, positions, w_ada, b_ada, norm_mix, norm_ffn, ffn_w_gate, ffn_w_up, ffn_w_down,
           ev_w_in, ev_s5_lam_re, ev_s5_lam_im, ev_s5_log_dt, ev_s5_b_re, ev_s5_b_im, ev_s5_c_re,
           ev_s5_c_im, ev_s5_d, ev_s5_w_glu, ev_q_norm, ev_k_norm, ev_lambda_q1, ev_lambda_k1,
           ev_lambda_q2, ev_lambda_k2, ev_subln, ev_w_out, od_mu, od_w_r, od_w_k, od_w_v, od_w_o,
           od_w0, od_w1, od_w2, od_a0, od_a1, od_a2, od_g1, od_g2, od_k_k, od_k_a, od_r_k,
           od_ln_g, od_ln_b):
    depth = w_ada.shape[0]
    mod = _ada_mod(c, w_ada, b_ada)
    tables = _rope_tables(positions)
    ffn_bf = (ffn_w_gate.astype(BF16), ffn_w_up.astype(BF16), ffn_w_down.astype(BF16))
    for l in range(depth):
        ffn = ffn_bf + (l,)
        if l % 2 == 0:
            e = l // 2
            lam_init = 0.8 - 0.6 * math.exp(-0.3 * l)
            x = _even_layer(x, mod[l], tables, lam_init, norm_mix[l], norm_ffn[l], ffn,
                            ev_w_in[e], ev_s5_lam_re[e],
                            ev_s5_lam_im[e], ev_s5_log_dt[e], ev_s5_b_re[e], ev_s5_b_im[e],
                            ev_s5_c_re[e], ev_s5_c_im[e], ev_s5_d[e], ev_s5_w_glu[e], ev_q_norm[e],
                            ev_k_norm[e], ev_lambda_q1[e], ev_lambda_k1[e], ev_lambda_q2[e],
                            ev_lambda_k2[e], ev_subln[e], ev_w_out[e])
        else:
            o = l // 2
            x = _odd_layer(x, mod[l], norm_mix[l], norm_ffn[l], ffn,
                           od_mu[o], od_w_r[o], od_w_k[o], od_w_v[o], od_w_o[o],
                           od_w0[o], od_w1[o], od_w2[o], od_a0[o], od_a1[o], od_a2[o], od_g1[o],
                           od_g2[o], od_k_k[o], od_k_a[o], od_r_k[o], od_ln_g[o], od_ln_b[o])
    return x
```

```python
import functools
import math

import jax
import jax.numpy as jnp
from jax import lax
from jax.experimental import pallas as pl
from jax.experimental.pallas import tpu as pltpu

F32 = jnp.float32
BF16 = jnp.bfloat16

RMS_EPS = 1e-6
GN_EPS = 64e-5
ROPE_THETA = 500000.0
DT_HEAD = 64
ROT_DIM = DT_HEAD // 4
RWKV_HEAD = 64
LANES = 128
SUBLANES = 8
VMEM_LIMIT = 56 * 1024 * 1024


def _cparams(*sem):
    return pltpu.CompilerParams(dimension_semantics=sem, vmem_limit_bytes=VMEM_LIMIT)


def _const_spec(shape):
    nd = len(shape)
    return pl.BlockSpec(shape, lambda *_: (0,) * nd, pipeline_mode=pl.Buffered(1))


def _dot(a, b):
    return jnp.dot(a, b, preferred_element_type=F32)


def _dot_nt(a, b):
    return lax.dot_general(a, b, (((1,), (1,)), ((), ())), preferred_element_type=F32)


def _split_bf16(x):
    hi = x.astype(BF16)
    lo = (x - hi.astype(F32)).astype(BF16)
    return hi, lo


def _dot_x2(x, w_bf16):
    hi, lo = _split_bf16(x)
    return _dot(hi, w_bf16) + _dot(lo, w_bf16)


def _dot_x2_rhs(w_bf16, x):
    hi, lo = _split_bf16(x)
    return _dot(w_bf16, hi) + _dot(w_bf16, lo)


def _modnorm(x, g, scale, shift):
    ms = jnp.mean(x * x, axis=-1, keepdims=True)
    return (x * lax.rsqrt(ms + RMS_EPS)) * (g * (1.0 + scale)) + shift


def _ada_kernel(c_ref, w_ref, b_ref, o_ref):
    c = c_ref[...]
    ca = c * jax.nn.sigmoid(c)
    hi, lo = _split_bf16(ca)
    w = w_ref[0]
    w_hi, w_lo = _split_bf16(w)
    o_ref[0] = _dot(hi, w_hi) + _dot(lo, w_hi) + _dot(hi, w_lo) + b_ref[0]


def _ada_mod(c, w_ada, b_ada):
    depth, d, n = w_ada.shape
    bsz = c.shape[0]
    rows = -(-bsz // SUBLANES) * SUBLANES
    c_pad = jnp.zeros((rows, d), F32).at[:bsz].set(c)
    tn = n // 4
    out = pl.pallas_call(
        _ada_kernel,
        grid=(depth, n // tn),
        in_specs=[
            pl.BlockSpec((rows, d), lambda l, j: (0, 0)),
            pl.BlockSpec((1, d, tn), lambda l, j: (l, 0, j)),
            pl.BlockSpec((1, 1, tn), lambda l, j: (l, 0, j)),
        ],
        out_specs=pl.BlockSpec((1, rows, tn), lambda l, j: (l, 0, j)),
        out_shape=jax.ShapeDtypeStruct((depth, rows, n), F32),
        compiler_params=_cparams("arbitrary", "arbitrary"),
        name="ada_mod",
    )(c_pad, w_ada, b_ada.reshape(depth, 1, n))
    mod = out[:, :bsz].reshape(depth, bsz, 6, d)
    return jnp.pad(mod, ((0, 0), (0, 0), (0, SUBLANES - 6), (0, 0)))


def _rope_trig_kernel(p_ref, f_ref, o_ref):
    half = f_ref.shape[0]
    pad = jnp.zeros((LANES - 2 * half, LANES), F32)
    for r in range(p_ref.shape[0]):
        a = p_ref[r:r + 1, :] * f_ref[...]
        o_ref[r * LANES:(r + 1) * LANES, :] = jnp.concatenate([jnp.cos(a), jnp.sin(a), pad], axis=0).T


def _rope_tables(positions):
    bsz, seq = positions.shape
    half = ROT_DIM // 2
    inv_freq = ROPE_THETA ** (-jnp.arange(0, ROT_DIM, 2, dtype=F32) / ROT_DIM)
    rows = bsz * seq // LANES
    tr = min(rows, SUBLANES)
    cs = pl.pallas_call(
        _rope_trig_kernel,
        grid=(rows // tr,),
        in_specs=[pl.BlockSpec((tr, LANES), lambda i: (i, 0)), _const_spec((half, LANES))],
        out_specs=pl.BlockSpec((tr * LANES, LANES), lambda i: (i, 0)),
        out_shape=jax.ShapeDtypeStruct((bsz * seq, LANES), F32),
        compiler_params=_cparams("arbitrary"),
        name="rope_trig",
    )(positions.astype(F32).reshape(rows, LANES), jnp.broadcast_to(inv_freq[:, None], (half, LANES)))
    return cs.reshape(bsz, seq, LANES)


def _expand_rope(cs):
    half = ROT_DIM // 2
    lane = lax.broadcasted_iota(jnp.int32, cs.shape, 1)
    c0 = jnp.where(lane < half, cs, jnp.where(lane < ROT_DIM, pltpu.roll(cs, half, axis=1), 1.0))
    sa0 = jnp.where(lane < half, -pltpu.roll(cs, LANES - half, axis=1), 0.0)
    sb0 = jnp.where((lane >= half) & (lane < ROT_DIM), cs, 0.0)
    second = lane >= DT_HEAD
    return (jnp.where(second, pltpu.roll(c0, DT_HEAD, axis=1), c0),
            jnp.where(second, pltpu.roll(sa0, DT_HEAD, axis=1), sa0),
            jnp.where(second, pltpu.roll(sb0, DT_HEAD, axis=1), sb0))


def _group_mean_matrix(width, group):
    idx = jnp.arange(width) // group
    return jnp.where(idx[:, None] == idx[None, :], 1.0 / group, 0.0).astype(BF16)


def _even_in_kernel(x_ref, mod_ref, g_ref, w_ref, qn_ref, kn_ref, cs_ref, bd_ref,
                    u_ref, q_ref, k_ref, v_ref, *, s5w, dfw):
    x = x_ref[0]
    h = _modnorm(x, g_ref[...], mod_ref[0, 1:2, :], mod_ref[0, 0:1, :])
    proj = _dot(h.astype(BF16), w_ref[...])
    u_ref[...] = proj[:, :s5w]
    cosv, sav, sbv = _expand_rope(cs_ref[0])
    half = ROT_DIM // 2

    def norm_rope(t, gn, out_ref, post_scale):
        ms = _dot((t * t).astype(BF16), bd_ref[...])
        t = t * lax.rsqrt(ms + RMS_EPS) * gn
        for j in range(dfw // LANES):
            tj = t[:, j * LANES:(j + 1) * LANES]
            up = pltpu.roll(tj, LANES - half, axis=1)
            dn = pltpu.roll(tj, half, axis=1)
            rj = tj * cosv + up * sav + dn * sbv
            out_ref[0, :, j * LANES:(j + 1) * LANES] = (rj * post_scale).astype(out_ref.dtype)

    norm_rope(proj[:, s5w:s5w + dfw], qn_ref[...], q_ref, DT_HEAD ** -0.5 * math.log2(math.e))
    norm_rope(proj[:, s5w + dfw:s5w + 2 * dfw], kn_ref[...], k_ref, 1.0)
    v_ref[0] = proj[:, s5w + 2 * dfw:].astype(v_ref.dtype)


def _even_in(x, mod, g, w_in, q_norm, k_norm, tables, s5w, dfw, tm):
    bsz, seq, d = x.shape
    ncol = w_in.shape[1]
    rep = dfw // DT_HEAD
    kern = functools.partial(_even_in_kernel, s5w=s5w, dfw=dfw)
    tok = lambda w: pl.BlockSpec((1, tm, w), lambda i, b: (b, i, 0))
    return pl.pallas_call(
        kern,
        grid=(seq // tm, bsz),
        in_specs=[
            tok(d),
            pl.BlockSpec((1, SUBLANES, d), lambda i, b: (b, 0, 0)),
            _const_spec((1, d)),
            _const_spec((d, ncol)),
            _const_spec((1, dfw)),
            _const_spec((1, dfw)),
            tok(LANES),
            _const_spec((dfw, dfw)),
        ],
        out_specs=[
            pl.BlockSpec((tm, s5w), lambda i, b: (i, b)),
            tok(dfw), tok(dfw), tok(dfw),
        ],
        out_shape=[
            jax.ShapeDtypeStruct((seq, bsz * s5w), F32),
            jax.ShapeDtypeStruct((bsz, seq, dfw), BF16),
            jax.ShapeDtypeStruct((bsz, seq, dfw), BF16),
            jax.ShapeDtypeStruct((bsz, seq, dfw), BF16),
        ],
        compiler_params=_cparams("arbitrary", "arbitrary"),
        name="even_in_proj",
    )(x, mod, g.reshape(1, d), w_in.astype(BF16),
      jnp.tile(q_norm, rep).reshape(1, dfw), jnp.tile(k_norm, rep).reshape(1, dfw),
      tables, _group_mean_matrix(dfw, DT_HEAD))


S5_CHUNK = 8


def _s5_maps_kernel(lr_ref, li_ref, ldt_ref, lrc_ref, lic_ref, ldtc_ref, br_ref, bi_ref,
                    lrt_ref, lit_ref, ldtt_ref, cr_ref, ci_ref,
                    kmat_out, wre_out, wim_out, mre_out, mim_out, a8r_out, a8i_out, *, hh):
    rr = S5_CHUNK
    nblk = kmat_out.shape[0]
    pp = lrc_ref.shape[1]
    gb = LANES // hh

    def abar(lr, li, ldt):
        dt = jnp.exp(ldt)
        mag = jnp.exp(lr * dt)
        return mag * jnp.cos(li * dt), mag * jnp.sin(li * dt)

    def cmul(ar, ai, br, bi):
        return ar * br - ai * bi, ar * bi + ai * br

    def mm3(a, b):
        a_hi, a_lo = _split_bf16(a)
        b_hi, b_lo = _split_bf16(b)
        return _dot(a_hi, b_hi) + _dot(a_hi, b_lo) + _dot(a_lo, b_hi)

    def iota2(shape, axis):
        return lax.broadcasted_iota(jnp.int32, shape, axis)

    same_kk = (iota2((LANES, LANES), 0) // hh) == (iota2((LANES, LANES), 1) // hh)
    same_w = (iota2((LANES, gb * pp), 0) // hh) == (iota2((LANES, gb * pp), 1) // pp)
    same_m = (iota2((gb * pp, LANES), 0) // pp) == (iota2((gb * pp, LANES), 1) // hh)
    lane_tile = jnp.where(iota2((pp, gb * pp), 1) % pp == iota2((pp, gb * pp), 0), 1.0, 0.0
                          ).astype(BF16)

    lr, li = lrc_ref[...], lic_ref[...]
    a_r, a_i = abar(lr, li, ldtc_ref[...])
    den = lr * lr + li * li
    num_r = a_r - 1.0
    q_r = (num_r * lr + a_i * li) / den
    q_i = (a_i * lr - num_r * li) / den
    bb_r, bb_i = cmul(q_r, q_i, br_ref[...], bi_ref[...])
    pr, pi = bb_r, bb_i
    for i in range(rr - 1, -1, -1):
        for out, val in ((wre_out, pr), (wim_out, pi)):
            tiled = _dot(val.astype(BF16), lane_tile)
            for b in range(nblk):
                out[b, i * LANES:(i + 1) * LANES, :] = jnp.where(
                    same_w, tiled[b * LANES:(b + 1) * LANES], 0.0).astype(BF16)
        if i:
            pr, pi = cmul(a_r, a_i, pr, pi)

    kmat_out[...] = jnp.zeros_like(kmat_out)
    t_ar, t_ai = abar(lrt_ref[...], lit_ref[...], ldtt_ref[...])
    car, cai = cr_ref[...], ci_ref[...]
    for k in range(rr + 1):
        if k:
            car, cai = cmul(car, cai, t_ar, t_ai)
            for out, val in ((mre_out, car), (mim_out, -cai)):
                for b in range(nblk):
                    rows = jnp.concatenate([val[:, b * LANES:(b + 1) * LANES]] * gb, axis=0)
                    out[b, :, (k - 1) * LANES:k * LANES] = jnp.where(same_m, rows, 0.0
                                                                     ).astype(BF16)
        if k < rr:
            for b in range(nblk):
                sl = slice(b * LANES, (b + 1) * LANES)
                kk = jnp.where(same_kk, mm3(bb_r[sl], car[:, sl]) - mm3(bb_i[sl], cai[:, sl]), 0.0
                               ).astype(BF16)
                for j in range(rr - k):
                    kmat_out[b, j * LANES:(j + 1) * LANES, (j + k) * LANES:(j + k + 1) * LANES] = kk
    e_r, e_i = abar(lr_ref[...], li_ref[...], ldt_ref[...])
    for _ in range(S5_CHUNK.bit_length() - 1):
        e_r, e_i = cmul(e_r, e_i, e_r, e_i)
    pr, pi = e_r, e_i
    for s in range(a8r_out.shape[0]):
        if s:
            pr, pi = cmul(pr, pi, e_r, e_i)
        a8r_out[s] = pr
        a8i_out[s] = pi


def _s5_chunk_kernel(u_ref, kmat_ref, wre_ref, wim_ref, mre_ref, mim_ref, dsk_ref, a1r_ref, a1i_ref,
                     pwr_ref, pwi_ref, wglu_ref, perm_ref, permt_ref, o_ref, xr_ref, xi_ref,
                     cr_ref, ci_ref, *, bsz, sub, lane_chunk):
    nrow, nstate = xr_ref.shape
    tchunk = u_ref.shape[0]
    width = wglu_ref.shape[0]
    nsub = tchunk // sub
    rps = sub // S5_CHUNK * bsz
    nblk, gl = kmat_ref.shape[0], kmat_ref.shape[1] // S5_CHUNK
    spart = wre_ref.shape[2]
    step = pl.program_id(0)

    @pl.when(step == 0)
    def _():
        cr_ref[...] = jnp.zeros_like(cr_ref)
        ci_ref[...] = jnp.zeros_like(ci_ref)

    blk = u_ref[...]
    parts = []
    for s in range(nsub):
        u_bt = jnp.concatenate([blk[s * sub:(s + 1) * sub, b * width:(b + 1) * width]
                                for b in range(bsz)], axis=0)
        parts.append(_dot_x2_rhs(perm_ref[...], u_bt))
    u_j = [jnp.concatenate([p[j * rps:(j + 1) * rps] for p in parts], axis=0)
           for j in range(S5_CHUNK)]
    u_jb = [t.astype(BF16) for t in u_j]
    u_g = [jnp.concatenate([t[:, g * gl:(g + 1) * gl] for t in u_jb], axis=1)
           for g in range(nblk)]

    for g in range(nblk):
        xr_ref[:, g * spart:(g + 1) * spart] = _dot(u_g[g], wre_ref[g])
        xi_ref[:, g * spart:(g + 1) * spart] = _dot(u_g[g], wim_ref[g])

    old_r = cr_ref[...]
    old_i = ci_ref[...]
    row = lax.broadcasted_iota(jnp.int32, (SUBLANES, lane_chunk), 0)
    steps_per_tile = SUBLANES // bsz
    ntiles = nrow // SUBLANES
    for c in range(nstate // lane_chunk):
        cols = pl.ds(c * lane_chunk, lane_chunk)
        a1r = a1r_ref[:, cols]
        a1i = a1i_ref[:, cols]
        pwr = pwr_ref[:, cols]
        pwi = pwi_ref[:, cols]

        def tile_body(i, carry):
            pr, pi = carry
            base = pl.multiple_of(i * SUBLANES, SUBLANES)
            zr = xr_ref[pl.ds(base, SUBLANES), cols]
            zi = xi_ref[pl.ds(base, SUBLANES), cols]
            sh = bsz
            apr, api = a1r, a1i
            for _ in range(steps_per_tile.bit_length() - 1):
                sr = jnp.where(row >= sh, pltpu.roll(zr, sh, axis=0), 0.0)
                si = jnp.where(row >= sh, pltpu.roll(zi, sh, axis=0), 0.0)
                zr, zi = zr + apr * sr - api * si, zi + apr * si + api * sr
                apr, api = apr * apr - api * api, 2.0 * apr * api
                sh *= 2
            last = SUBLANES - bsz
            br_, bi_ = pr, pi
            sh = bsz
            while sh < SUBLANES:
                br_ = jnp.where(row >= last, br_, pltpu.roll(br_, SUBLANES - sh, axis=0))
                bi_ = jnp.where(row >= last, bi_, pltpu.roll(bi_, SUBLANES - sh, axis=0))
                last -= sh
                sh *= 2
            xr = zr + pwr * br_ - pwi * bi_
            xi = zi + pwr * bi_ + pwi * br_
            xr_ref[pl.ds(base, SUBLANES), cols] = xr
            xi_ref[pl.ds(base, SUBLANES), cols] = xi
            return xr, xi

        fr, fi = lax.fori_loop(0, ntiles, tile_body, (cr_ref[:, cols], ci_ref[:, cols]))
        cr_ref[:, cols] = fr
        ci_ref[:, cols] = fi

    row8 = lax.broadcasted_iota(jnp.int32, (SUBLANES, nstate), 0)

    def state_in(x, old):
        xs = pltpu.roll(x, bsz, axis=0)
        first = jnp.where(row8 < bsz, pltpu.roll(old, bsz, axis=0), xs[:SUBLANES])
        return jnp.concatenate([first, xs[SUBLANES:]], axis=0).astype(BF16)

    xin_r = state_in(xr_ref[...], old_r)
    xin_i = state_in(xi_ref[...], old_i)

    tw = 2 * gl

    def k_part(g):
        strips = []
        for s in range(S5_CHUNK // 2):
            kdim = (2 * s + 2) * gl
            strips.append(_dot(u_g[g][:, :kdim], kmat_ref[g, :kdim, s * tw:(s + 1) * tw]))
        return jnp.concatenate(strips, axis=1)

    y_g = [k_part(g)
           + _dot(xin_r[:, g * spart:(g + 1) * spart], mre_ref[g])
           + _dot(xin_i[:, g * spart:(g + 1) * spart], mim_ref[g]) for g in range(nblk)]
    y = jnp.concatenate(
        [jnp.concatenate([y_g[g][:, t * gl:(t + 1) * gl] for g in range(nblk)], axis=1)
         + dsk_ref[...] * u_j[t] for t in range(S5_CHUNK)], axis=0)
    y = jax.nn.gelu(y)
    gate = jax.nn.sigmoid(_dot(y.astype(BF16), wglu_ref[...]))
    out = (y * gate).astype(BF16)
    for s in range(nsub):
        o_s = jnp.concatenate([out[t * nrow + s * rps:t * nrow + (s + 1) * rps]
                               for t in range(S5_CHUNK)], axis=0)
        o_bt = _dot(permt_ref[...], o_s)
        for b in range(bsz):
            o_ref[s * sub:(s + 1) * sub, b * width:(b + 1) * width] = (
                o_bt[b * sub:(b + 1) * sub].astype(o_ref.dtype))


def _s5_mixer_chunked(u2, bsz, lam_re, lam_im, log_dt, b_re, b_im, c_re, c_im, d_skip, w_glu,
                      tchunk):
    seq = u2.shape[0]
    width = u2.shape[1] // bsz
    g, p = lam_re.shape
    hh = b_re.shape[-1]
    gh = g * hh
    nstate = g * p
    rr = S5_CHUNK
    ldt = jnp.broadcast_to(log_dt[:, None], (g, p))
    rep_c = lambda t: jnp.repeat(t, hh, axis=0)
    rep_t = lambda t: jnp.repeat(t.T, hh, axis=1)
    b_c = lambda b: b.transpose(0, 2, 1).reshape(gh, p)
    c_t = lambda c: c.transpose(2, 0, 1).reshape(p, gh)
    spt = SUBLANES // bsz
    gb = LANES // hh
    nblk = g // gb
    blk_w = rr * LANES
    bf = lambda *s: jax.ShapeDtypeStruct(s, BF16)
    f32 = lambda *s: jax.ShapeDtypeStruct(s, F32)
    kmat, w_re, w_im, m_re, m_im, a8_r, a8_i = pl.pallas_call(
        functools.partial(_s5_maps_kernel, hh=hh),
        out_shape=[bf(nblk, blk_w, blk_w), bf(nblk, blk_w, gb * p), bf(nblk, blk_w, gb * p),
                   bf(nblk, gb * p, blk_w), bf(nblk, gb * p, blk_w), f32(spt, g, p), f32(spt, g, p)],
        compiler_params=pltpu.CompilerParams(vmem_limit_bytes=VMEM_LIMIT),
        name="s5_chunk_maps",
    )(lam_re, lam_im, ldt, rep_c(lam_re), rep_c(lam_im), rep_c(ldt), b_c(b_re), b_c(b_im),
      rep_t(lam_re), rep_t(lam_im), rep_t(ldt), c_t(c_re), c_t(c_im))
    pw_r = jnp.repeat(a8_r.reshape(spt, nstate), bsz, axis=0)
    pw_i = jnp.repeat(a8_i.reshape(spt, nstate), bsz, axis=0)
    a1r = jnp.broadcast_to(pw_r[0:1], (SUBLANES, nstate))
    a1i = jnp.broadcast_to(pw_i[0:1], (SUBLANES, nstate))
    consts = [kmat, w_re, w_im, m_re, m_im, d_skip.reshape(1, width),
              a1r, a1i, pw_r, pw_i, w_glu.astype(BF16)]

    sub = min(tchunk, 128)
    rows = sub * bsz
    dst = jnp.arange(rows)
    rps = sub // rr * bsz
    j_, rem = dst // rps, dst % rps
    src_of = (rem % bsz) * sub + (rem // bsz) * rr + j_
    perm = (jnp.arange(rows)[None, :] == src_of[:, None]).astype(BF16)
    nrow = tchunk // rr * bsz
    kern = functools.partial(_s5_chunk_kernel, bsz=bsz, sub=sub, lane_chunk=512)
    return pl.pallas_call(
        kern,
        grid=(seq // tchunk,),
        in_specs=[pl.BlockSpec((tchunk, bsz * width), lambda i: (i, 0))]
                 + [_const_spec(t.shape) for t in consts]
                 + [_const_spec((rows, rows)), _const_spec((rows, rows))],
        out_specs=pl.BlockSpec((tchunk, bsz * width), lambda i: (i, 0)),
        out_shape=jax.ShapeDtypeStruct((seq, bsz * width), BF16),
        scratch_shapes=[
            pltpu.VMEM((nrow, nstate), F32), pltpu.VMEM((nrow, nstate), F32),
            pltpu.VMEM((SUBLANES, nstate), F32), pltpu.VMEM((SUBLANES, nstate), F32),
        ],
        compiler_params=_cparams("arbitrary"),
        name="s5_chunked",
    )(u2, *consts, perm, perm.T)


def _diff_attn_kernel(lam_ref, sub_ref, q_ref, k_ref, v_ref, o_ref, q2_ref, m_ref, acc_ref,
                      *, tq, lam_init):
    qi = pl.program_id(2)
    vdim = v_ref.shape[-1]
    q = q_ref[0]
    lane = lax.broadcasted_iota(jnp.int32, q.shape, 1)
    zero = jnp.zeros_like(q)
    q2_ref[:tq, :] = jnp.where(lane < DT_HEAD, q, zero)
    q2_ref[tq:, :] = jnp.where(lane >= DT_HEAD, q, zero)
    all_rows = ((0, 2 * tq),)

    def gather(ref, row_slices):
        parts = [ref[a:b, :] for a, b in row_slices]
        return parts[0] if len(parts) == 1 else jnp.concatenate(parts, axis=0)

    def scores(kstart, ksize, row_slices=all_rows):
        start = pl.multiple_of(kstart, ksize)
        return _dot_nt(gather(q2_ref, row_slices), k_ref[0, pl.ds(start, ksize), :])

    def update(kstart, ksize, s, row_slices=all_rows, mask=None, first=False):
        start = pl.multiple_of(kstart, ksize)
        v_ext = jnp.concatenate([v_ref[0, pl.ds(start, ksize), :], jnp.ones((ksize, vdim), BF16)],
                                axis=1)
        if mask is not None:
            s = jnp.where(mask, s, -jnp.inf)
        part = s[:, :LANES]
        for t in range(1, ksize // LANES):
            part = jnp.maximum(part, s[:, t * LANES:(t + 1) * LANES])
        row_max = jnp.max(part, axis=-1, keepdims=True)
        if first:
            assert row_slices == all_rows
            m_new = jnp.broadcast_to(row_max, part.shape)
        else:
            m_old = gather(m_ref, row_slices)
            m_new = jnp.maximum(m_old, row_max)
        p = jnp.concatenate([jnp.exp2((s[:, t * LANES:(t + 1) * LANES] - m_new).astype(BF16))
                             for t in range(ksize // LANES)], axis=1)
        acc = _dot(p, v_ext)
        if not first:
            alpha = jnp.exp2(m_old - m_new)
            acc = acc + jnp.concatenate([alpha] * (2 * vdim // LANES), axis=1) * gather(
                acc_ref, row_slices)
        off = 0
        for a, b in row_slices:
            m_ref[a:b, :] = m_new[off:off + b - a]
            acc_ref[a:b, :] = acc[off:off + b - a]
            off += b - a

    def diagonal(kstart, with_previous, first):
        hq = tq // 2
        if with_previous:
            s_prev = scores(kstart - tq, tq)
        row_l = lax.broadcasted_iota(jnp.int32, (2 * tq, hq), 0)
        col_l = lax.broadcasted_iota(jnp.int32, (2 * tq, hq), 1)
        row_r = lax.broadcasted_iota(jnp.int32, (tq, hq), 0)
        col_r = lax.broadcasted_iota(jnp.int32, (tq, hq), 1)
        late_rows = ((hq, tq), (tq + hq, 2 * tq))
        s_left = scores(kstart, hq)
        s_right = scores(kstart + hq, hq, late_rows)
        if with_previous:
            update(kstart - tq, tq, s_prev, first=first)
        update(kstart, hq, s_left, mask=col_l <= row_l % tq, first=first and not with_previous)
        update(kstart + hq, hq, s_right, late_rows, col_r <= row_r % hq)

    def pair(j0, first=False):
        s0 = scores(j0 * tq, tq)
        s1 = scores((j0 + 1) * tq, tq)
        update(j0 * tq, tq, s0, first=first)
        update((j0 + 1) * tq, tq, s1)

    npairs = qi // 2

    @pl.when(npairs > 0)
    def _():
        pair(0, first=True)

    lax.fori_loop(1, npairs, lambda jj, _: (pair(2 * jj), 0)[1], 0)
    for odd in (False, True):
        for is_first in (False, True):
            @pl.when((qi % 2 == int(odd)) & ((npairs == 0) if is_first else (npairs > 0)))
            def _():
                diagonal(qi * tq, odd, is_first)

    lv = lam_ref[...]
    lam = (jnp.exp(jnp.sum(lv[0:1] * lv[1:2], axis=-1, keepdims=True))
           - jnp.exp(jnp.sum(lv[2:3] * lv[3:4], axis=-1, keepdims=True)) + lam_init)
    acc = acc_ref[...]
    o = (acc[:tq, :vdim] / acc[:tq, vdim:]) - lam * (acc[tq:, :vdim] / acc[tq:, vdim:])
    ms = jnp.mean(o * o, axis=-1, keepdims=True)
    o = o * lax.rsqrt(ms + RMS_EPS) * sub_ref[...] * (1.0 - lam_init)
    o_ref[0] = o.astype(o_ref.dtype)


def _diff_attention(q, k, v, lq1, lk1, lq2, lk2, subln, lam_init, tq):
    bsz, seq, dfw = q.shape
    vdim = 2 * DT_HEAD
    heads = dfw // vdim
    lamv = jnp.zeros((SUBLANES, LANES), F32)
    for i, t in enumerate((lq1, lk1, lq2, lk2)):
        lamv = lamv.at[i, :t.shape[0]].set(t)
    kern = functools.partial(_diff_attn_kernel, tq=tq, lam_init=lam_init)
    kv_spec = pl.BlockSpec((1, seq, vdim), lambda b, h, i: (b, 0, h))
    return pl.pallas_call(
        kern,
        grid=(bsz, heads, seq // tq),
        in_specs=[
            _const_spec((SUBLANES, LANES)),
            _const_spec((1, vdim)),
            pl.BlockSpec((1, tq, vdim), lambda b, h, i: (b, i, h)),
            kv_spec, kv_spec,
        ],
        out_specs=pl.BlockSpec((1, tq, vdim), lambda b, h, i: (b, i, h)),
        out_shape=jax.ShapeDtypeStruct((bsz, seq, dfw), BF16),
        scratch_shapes=[pltpu.VMEM((2 * tq, vdim), BF16), pltpu.VMEM((2 * tq, LANES), F32),
                        pltpu.VMEM((2 * tq, 2 * vdim), F32)],
        compiler_params=_cparams("arbitrary", "arbitrary", "arbitrary"),
        name="diff_attention",
    )(lamv, subln.reshape(1, vdim), q, k, v)


def _mix_ffn_kernel(*refs, n_in, hid_chunk):
    x_ref, mod_ref, g_ref = refs[:3]
    act_refs = refs[3:3 + n_in]
    w_refs = refs[3 + n_in:3 + 2 * n_in]
    wg_ref, wu_ref, wd_ref, o_ref = refs[3 + 2 * n_in:]
    mix = None
    for a_ref, w_ref in zip(act_refs, w_refs):
        a = a_ref[0] if len(a_ref.shape) == 3 else a_ref[...]
        t = _dot(a, w_ref[...])
        mix = t if mix is None else mix + t
    x1 = x_ref[0] + mod_ref[0, 2:3, :] * mix
    h = _modnorm(x1, g_ref[...], mod_ref[0, 4:5, :], mod_ref[0, 3:4, :]).astype(BF16)
    acc = None
    for j in range(wg_ref.shape[2] // hid_chunk):
        sl = slice(j * hid_chunk, (j + 1) * hid_chunk)
        gate = _dot(h, wg_ref[0, :, sl])
        up = _dot(h, wu_ref[0, :, sl])
        act = (gate * jax.nn.sigmoid(gate) * up).astype(BF16)
        t = _dot(act, wd_ref[0, sl, :])
        acc = t if acc is None else acc + t
    o_ref[0] = x1 + mod_ref[0, 5:6, :] * acc


def _mix_ffn(x, mod, g, acts, weights, ffn, tm, hid_chunk, w_specs=None):
    bsz, seq, d = x.shape
    if w_specs is None:
        w_specs = [_const_spec(w.shape) for w in weights]
    wg, wu, wd, layer = ffn
    assert wg.shape[2] % hid_chunk == 0

    def layer_spec(w):
        return pl.BlockSpec((1,) + w.shape[1:], lambda i, b: (layer, 0, 0),
                            pipeline_mode=pl.Buffered(1))

    tok = pl.BlockSpec((1, tm, d), lambda i, b: (b, i, 0))
    return pl.pallas_call(
        functools.partial(_mix_ffn_kernel, n_in=len(acts), hid_chunk=hid_chunk),
        grid=(seq // tm, bsz),
        in_specs=[tok, pl.BlockSpec((1, SUBLANES, d), lambda i, b: (b, 0, 0)), _const_spec((1, d))]
                 + [spec for _, spec in acts]
                 + w_specs
                 + [layer_spec(wg), layer_spec(wu), layer_spec(wd)],
        out_specs=tok,
        out_shape=jax.ShapeDtypeStruct((bsz, seq, d), F32),
        compiler_params=_cparams("arbitrary", "arbitrary"),
        name="mix_ffn",
    )(x, mod, g.reshape(1, d), *[a for a, _ in acts], *weights, wg, wu, wd)


def _rwkv_in_kernel(x_ref, xp_ref, mod_ref, g_ref, mu_ref, wr_ref, wk_ref, wv_ref, w1_ref, w2_ref,
                    a1_ref, a2_ref, g1_ref, g2_ref, w0_ref, a0_ref, kkw_ref, kaw_ref, bd_ref,
                    r_out, lw_out, k_out, v_out, kk_out, a_out, g_out):
    i = pl.program_id(1)
    g = g_ref[...]
    scale = mod_ref[0, 1:2, :]
    shift = mod_ref[0, 0:1, :]
    h = _modnorm(x_ref[0], g, scale, shift)
    hp = _modnorm(xp_ref[0][SUBLANES - 1:SUBLANES, :], g, scale, shift)
    hp = jnp.where(i == 0, 0.0, hp)
    row = lax.broadcasted_iota(jnp.int32, h.shape, 0)
    h_prev = jnp.where(row == 0, hp, pltpu.roll(h, 1, axis=0))
    dx = h_prev - h

    def lerp(j):
        return (h + dx * mu_ref[j:j + 1, :]).astype(BF16)

    r_out[0] = _dot(lerp(0), wr_ref[...])
    wl = jnp.tanh(_dot(lerp(1), w1_ref[...]))
    wdec = w0_ref[...] + _dot(wl.astype(BF16), w2_ref[...])
    w = jnp.minimum(wdec, 0.0) - jnp.log1p(jnp.exp(-jnp.abs(wdec))) - 0.5
    lw_out[0] = -jnp.exp(w)
    k = _dot(lerp(2), wk_ref[...])
    v_out[0] = _dot(lerp(3), wv_ref[...])
    al = _dot(lerp(4), a1_ref[...])
    a = jax.nn.sigmoid(a0_ref[...] + _dot(al.astype(BF16), a2_ref[...]))
    a_out[0] = a
    gl = jax.nn.sigmoid(_dot(lerp(5), g1_ref[...]))
    g_out[0] = _dot(gl.astype(BF16), g2_ref[...])
    kk = k * kkw_ref[...]
    bd = bd_ref[...]
    for j in range(kk.shape[1] // LANES):
        kj = kk[:, j * LANES:(j + 1) * LANES]
        ss = _dot((kj * kj).astype(BF16), bd) * float(RWKV_HEAD)
        kk_out[0, :, j * LANES:(j + 1) * LANES] = kj / jnp.maximum(jnp.sqrt(ss), 1e-12)
    k_out[0] = k * (1.0 + (a - 1.0) * kaw_ref[...])


def _rwkv_in(x, mod, g, mu, w_r, w_k, w_v, w0, w1, w2, a0, a1, a2, g1, g2, k_k, k_a, tm):
    bsz, seq, d = x.shape
    tok = pl.BlockSpec((1, tm, d), lambda b, i: (b, i, 0))
    prev = pl.BlockSpec((1, SUBLANES, d),
                        lambda b, i: (b, jnp.maximum(i * (tm // SUBLANES) - 1, 0), 0))
    bf = lambda w: w.astype(BF16)
    vec = lambda t: t.reshape(1, d)
    consts = [vec(g), mu, bf(w_r), bf(w_k), bf(w_v), bf(w1), bf(w2), bf(a1), bf(a2), bf(g1), bf(g2),
              vec(w0), vec(a0), vec(k_k), vec(k_a), _group_mean_matrix(LANES, RWKV_HEAD)]
    out = jax.ShapeDtypeStruct((bsz, seq, d), F32)
    return pl.pallas_call(
        _rwkv_in_kernel,
        grid=(bsz, seq // tm),
        in_specs=[tok, prev, pl.BlockSpec((1, SUBLANES, d), lambda b, i: (b, 0, 0))]
                 + [_const_spec(t.shape) for t in consts],
        out_specs=[tok] * 7,
        out_shape=[out] * 7,
        compiler_params=_cparams("arbitrary", "arbitrary"),
        name="rwkv_in_proj",
    )(x, x, mod, *consts)


def _rwkv_rec_kernel(r_ref, lw_ref, k_ref, v_ref, kk_ref, a_ref, g_ref, rk_ref, lng_ref, lnb_ref,
                     tri_ref, bd_ref, o_ref, m_ref, y_ref, *, chunk):
    tstep = pl.program_id(2)

    @pl.when(tstep == 0)
    def _():
        m_ref[...] = jnp.zeros_like(m_ref)

    tb, width = r_ref.shape[1:]
    hd = RWKV_HEAD
    gw = m_ref.shape[-1]
    nh = gw // hd
    groups = range(width // gw)
    colsl = [slice(q * gw, (q + 1) * gw) for q in groups]
    lane = lax.broadcasted_iota(jnp.int32, (chunk, gw), 1)
    trow = lax.broadcasted_iota(jnp.int32, (chunk, gw), 0)
    jpos = lane % hd
    strict = jpos < trow
    incl = jpos <= trow
    eye = jnp.where(jpos == trow, 1.0, 0.0)
    head_of_lane = lane // hd
    in_head = [head_of_lane == h for h in range(nh)]
    sq_r = lax.broadcasted_iota(jnp.int32, (gw, gw), 0)
    sq_c = lax.broadcasted_iota(jnp.int32, (gw, gw), 1)
    same_head = (sq_r // hd) == (sq_c // hd)
    diag = sq_r == sq_c
    tri = tri_ref[...]
    bd = bd_ref[...]
    bf = lambda t: t.astype(BF16)

    def blockdiag(y):
        return bf(jnp.concatenate([jnp.where(in_head[h], y, 0.0) for h in range(nh)], axis=0))

    def blockdiag_t(x):
        xt = jnp.concatenate([x] * nh, axis=0).T
        return bf(jnp.where(same_head, xt, 0.0))

    inst = [(c, q) for c in range(tb // chunk) for q in groups]
    cls = []
    for c, q in inst:
        lw = lw_ref[0, c * chunk:(c + 1) * chunk, colsl[q]]
        hi = bf(lw)
        r1 = lw - hi.astype(F32)
        mid = bf(r1)
        lo = bf(r1 - mid.astype(F32))
        cls.append((lw, _dot(tri, hi) + _dot(tri, mid) + _dot(tri, lo)))
    opnd = []
    for (c, q), (lw, cl) in zip(inst, cls):
        rows = slice(c * chunk, (c + 1) * chunk)
        cols = colsl[q]
        k = k_ref[0, rows, cols]
        kk = kk_ref[0, rows, cols]
        b = kk * a_ref[0, rows, cols]
        v = v_ref[0, rows, cols]
        cl_end = cl[chunk - 1:chunk, :]
        p_inv = jnp.exp(-cl)
        p_tail = jnp.exp(cl_end - cl)
        ar = jnp.concatenate([bf(-kk * jnp.exp(cl - lw)), bf(r_ref[0, rows, cols] * jnp.exp(cl))],
                             axis=0)
        tail_t = bf(jnp.concatenate([b * p_tail, k * p_tail], axis=0).T)
        pl_full = jnp.broadcast_to(jnp.exp(cl_end), (gw, gw))
        pl_col = jnp.sum(jnp.where(diag, pl_full, 0.0), axis=1, keepdims=True)
        opnd.append((ar, blockdiag_t(b * p_inv), blockdiag_t(k * p_inv), bf(v), blockdiag(v),
                     tail_t, pl_col))
    g_b = [_dot(o[0], o[1]) for o in opnd]
    g_k = [_dot(o[0], o[2]) for o in opnd]
    nms = [jnp.where(strict, g[:chunk], 0.0) for g in g_b]
    tinv = [eye + nm for nm in nms]
    pw = [bf(_dot(bf(nm), blockdiag(nm))) for nm in nms]
    for _ in range(chunk.bit_length() - 3):
        both = [_dot(jnp.concatenate([bf(t), p2], axis=0), blockdiag(p2)) for t, p2 in zip(tinv, pw)]
        tinv = [t + bo[:chunk] for t, bo in zip(tinv, both)]
        pw = [bf(bo[chunk:]) for bo in both]
    tinv = [bf(t + _dot(bf(t), blockdiag(p2))) for t, p2 in zip(tinv, pw)]
    g_ak = [bf(jnp.where(strict, g[:chunk], 0.0)) for g in g_k]
    g_rb = [bf(jnp.where(incl, g[chunk:], 0.0)) for g in g_b]
    g_rk = [bf(jnp.where(incl, g[chunk:], 0.0)) for g in g_k]

    state = [m_ref[q] for q in groups]
    for c in range(tb // chunk):
        ids = [c * len(groups) + q for q in groups]
        m0_bf = [bf(state[q]) for q in groups]
        rhs = [_dot(jnp.concatenate([opnd[i][0][:chunk], g_ak[i]], axis=1),
                    jnp.concatenate([m0_bf[q], opnd[i][4]], axis=0)) for q, i in zip(groups, ids)]
        u = [_dot(tinv[i], blockdiag(rhs[q])) for q, i in zip(groups, ids)]
        for q, i in zip(groups, ids):
            y_ref[c * chunk:(c + 1) * chunk, colsl[q]] = _dot(
                jnp.concatenate([opnd[i][0][chunk:], g_rb[i], g_rk[i]], axis=1),
                jnp.concatenate([m0_bf[q], blockdiag(u[q]), opnd[i][4]], axis=0))
        for q, i in zip(groups, ids):
            upd = _dot(opnd[i][5], jnp.concatenate([bf(u[q]), opnd[i][3]], axis=0))
            state[q] = opnd[i][6] * state[q] + jnp.where(same_head, upd, 0.0)
    for q in groups:
        m_ref[q] = state[q]

    npair = width // LANES
    for p in range(npair):
        cols = slice(p * LANES, (p + 1) * LANES)
        y = y_ref[:, cols]
        mean = _dot_x2(y, bd)
        dlt = y - mean
        var = _dot(bf(dlt * dlt), bd)
        yn = dlt * lax.rsqrt(var + GN_EPS) * lng_ref[:, cols] + lnb_ref[:, cols]
        rk_sum = _dot(bf(r_ref[0, :, cols] * k_ref[0, :, cols] * rk_ref[:, cols]), bd) * float(hd)
        out = (yn + rk_sum * v_ref[0, :, cols]) * g_ref[0, :, cols]
        o_ref[0, :, cols] = out.astype(o_ref.dtype)


def _rwkv_recurrence(r, lw, k, v, kk, a, g, r_k, ln_g, ln_b, tb, chunk, wblk):
    bsz, seq, d = r.shape
    gw = 2 * RWKV_HEAD
    assert chunk == RWKV_HEAD and wblk % gw == 0
    tok = pl.BlockSpec((1, tb, wblk), lambda b, j, t: (b, t, j))
    vec = pl.BlockSpec((1, wblk), lambda b, j, t: (0, j))
    tri = jnp.tril(jnp.ones((chunk, chunk), F32)).astype(BF16)
    bd = _group_mean_matrix(LANES, RWKV_HEAD)
    return pl.pallas_call(
        functools.partial(_rwkv_rec_kernel, chunk=chunk),
        grid=(bsz, d // wblk, seq // tb),
        in_specs=[tok] * 7 + [vec] * 3 + [_const_spec(tri.shape), _const_spec(bd.shape)],
        out_specs=tok,
        out_shape=jax.ShapeDtypeStruct((bsz, seq, d), BF16),
        scratch_shapes=[pltpu.VMEM((wblk // gw, gw, gw), F32),
                        pltpu.VMEM((tb, wblk), F32)],
        compiler_params=_cparams("arbitrary", "arbitrary", "arbitrary"),
        name="rwkv_recurrence",
    )(r, lw, k, v, kk, a, g, r_k.reshape(1, d), ln_g.reshape(1, d), ln_b.reshape(1, d), tri, bd)


def _odd_layer(x, mod, norm_mix, norm_ffn, ffn, mu, w_r, w_k, w_v, w_o, w0, w1,
               w2, a0, a1, a2, g1, g2, k_k, k_a, r_k, ln_g, ln_b):
    bsz, seq, d = x.shape
    r, lw, k, v, kk, a, g = _rwkv_in(x, mod, norm_mix, mu, w_r, w_k, w_v, w0, w1, w2, a0, a1, a2,
                                     g1, g2, k_k, k_a, _pick_tile(seq, 512))
    yg = _rwkv_recurrence(r, lw, k, v, kk, a, g, r_k, ln_g, ln_b, _pick_tile(seq, 512), 64,
                          d)
    tm = _pick_tile(seq, 512)
    acts = [(yg, pl.BlockSpec((1, tm, d), lambda i, b: (b, i, 0)))]
    return _mix_ffn(x, mod, norm_ffn, acts, [w_o.astype(BF16)], ffn, tm, 256)


def _pick_tile(n, pref):
    t = min(n, pref)
    assert n % t == 0, (n, t)
    return t


def _even_layer(x, mod, tables, lam_init, norm_mix, norm_ffn, ffn, w_in,
                lam_re, lam_im, log_dt, b_re, b_im, c_re, c_im, d_skip, w_glu, q_norm, k_norm,
                lq1, lk1, lq2, lk2, subln, w_out):
    bsz, seq, d = x.shape
    s5w = lam_re.shape[0] * b_re.shape[-1]
    dfw = (w_in.shape[1] - s5w) // 3
    tm = _pick_tile(seq, 512)
    u2, q, k, v = _even_in(x, mod, norm_mix, w_in, q_norm, k_norm, tables, s5w, dfw, tm)
    ys = _s5_mixer_chunked(u2, bsz, lam_re, lam_im, log_dt, b_re, b_im,
                           c_re, c_im, d_skip.reshape(-1), w_glu, _pick_tile(seq, 512))
    att = _diff_attention(q, k, v, lq1, lk1, lq2, lk2, subln, lam_init, _pick_tile(seq, 512))
    w_out_bf = w_out.astype(BF16)
    assert s5w == dfw
    acts = [
        (ys, pl.BlockSpec((tm, s5w), lambda i, b: (i, b))),
        (att, pl.BlockSpec((1, tm, dfw), lambda i, b: (b, i, 0))),
    ]
    w_specs = [pl.BlockSpec((s5w, d), lambda i, b: (0, 0), pipeline_mode=pl.Buffered(1)),
               pl.BlockSpec((dfw, d), lambda i, b: (1, 0), pipeline_mode=pl.Buffered(1))]
    return _mix_ffn(x, mod, norm_ffn, acts, [w_out_bf, w_out_bf], ffn, tm, 256, w_specs)


def kernel(x, c, positions, w_ada, b_ada, norm_mix, norm_ffn, ffn_w_gate, ffn_w_up, ffn_w_down,
           ev_w_in, ev_s5_lam_re, ev_s5_lam_im, ev_s5_log_dt, ev_s5_b_re, ev_s5_b_im, ev_s5_c_re,
           ev_s5_c_im, ev_s5_d, ev_s5_w_glu, ev_q_norm, ev_k_norm, ev_lambda_q1, ev_lambda_k1,
           ev_lambda_q2, ev_lambda_k2, ev_subln, ev_w_out, od_mu, od_w_r, od_w_k, od_w_v, od_w_o,
           od_w0, od_w1, od_w2, od_a0, od_a1, od_a2, od_g1, od_g2, od_k_k, od_k_a, od_r_k,
           od_ln_g, od_ln_b):
    depth = w_ada.shape[0]
    mod = _ada_mod(c, w_ada, b_ada)
    tables = _rope_tables(positions)
    ffn_bf = (ffn_w_gate.astype(BF16), ffn_w_up.astype(BF16), ffn_w_down.astype(BF16))
    for l in range(depth):
        ffn = ffn_bf + (l,)
        if l % 2 == 0:
            e = l // 2
            lam_init = 0.8 - 0.6 * math.exp(-0.3 * l)
            x = _even_layer(x, mod[l], tables, lam_init, norm_mix[l], norm_ffn[l], ffn,
                            ev_w_in[e], ev_s5_lam_re[e],
                            ev_s5_lam_im[e], ev_s5_log_dt[e], ev_s5_b_re[e], ev_s5_b_im[e],
                            ev_s5_c_re[e], ev_s5_c_im[e], ev_s5_d[e], ev_s5_w_glu[e], ev_q_norm[e],
                            ev_k_norm[e], ev_lambda_q1[e], ev_lambda_k1[e], ev_lambda_q2[e],
                            ev_lambda_k2[e], ev_subln[e], ev_w_out[e])
        else:
            o = l // 2
            x = _odd_layer(x, mod[l], norm_mix[l], norm_ffn[l], ffn,
                           od_mu[o], od_w_r[o], od_w_k[o], od_w_v[o], od_w_o[o],
                           od_w0[o], od_w1[o], od_w2[o], od_a0[o], od_a1[o], od_a2[o], od_g1[o],
                           od_g2[o], od_k_k[o], od_k_a[o], od_r_k[o], od_ln_g[o], od_ln_b[o])
    return x
```

```python
import functools
import math

import jax
import jax.numpy as jnp
from jax import lax
from jax.experimental import pallas as pl
from jax.experimental.pallas import tpu as pltpu

F32 = jnp.float32
BF16 = jnp.bfloat16

RMS_EPS = 1e-6
GN_EPS = 64e-5
ROPE_THETA = 500000.0
DT_HEAD = 64
ROT_DIM = DT_HEAD // 4
RWKV_HEAD = 64
LANES = 128
SUBLANES = 8
VMEM_LIMIT = 56 * 1024 * 1024


def _cparams(*sem):
    return pltpu.CompilerParams(dimension_semantics=sem, vmem_limit_bytes=VMEM_LIMIT)


def _const_spec(shape):
    nd = len(shape)
    return pl.BlockSpec(shape, lambda *_: (0,) * nd, pipeline_mode=pl.Buffered(1))


def _dot(a, b):
    return jnp.dot(a, b, preferred_element_type=F32)


def _dot_nt(a, b):
    return lax.dot_general(a, b, (((1,), (1,)), ((), ())), preferred_element_type=F32)


def _split_bf16(x):
    hi = x.astype(BF16)
    lo = (x - hi.astype(F32)).astype(BF16)
    return hi, lo


def _dot_x2(x, w_bf16):
    hi, lo = _split_bf16(x)
    return _dot(hi, w_bf16) + _dot(lo, w_bf16)


def _dot_x2_rhs(w_bf16, x):
    hi, lo = _split_bf16(x)
    return _dot(w_bf16, hi) + _dot(w_bf16, lo)


def _modnorm(x, g, scale, shift):
    ms = jnp.mean(x * x, axis=-1, keepdims=True)
    return (x * lax.rsqrt(ms + RMS_EPS)) * (g * (1.0 + scale)) + shift


def _ada_kernel(c_ref, w_ref, b_ref, o_ref):
    c = c_ref[...]
    ca = c * jax.nn.sigmoid(c)
    hi, lo = _split_bf16(ca)
    w = w_ref[0]
    w_hi, w_lo = _split_bf16(w)
    o_ref[0] = _dot(hi, w_hi) + _dot(lo, w_hi) + _dot(hi, w_lo) + b_ref[0]


def _ada_mod(c, w_ada, b_ada):
    depth, d, n = w_ada.shape
    bsz = c.shape[0]
    rows = -(-bsz // SUBLANES) * SUBLANES
    c_pad = jnp.zeros((rows, d), F32).at[:bsz].set(c)
    tn = n // 4
    out = pl.pallas_call(
        _ada_kernel,
        grid=(depth, n // tn),
        in_specs=[
            pl.BlockSpec((rows, d), lambda l, j: (0, 0)),
            pl.BlockSpec((1, d, tn), lambda l, j: (l, 0, j)),
            pl.BlockSpec((1, 1, tn), lambda l, j: (l, 0, j)),
        ],
        out_specs=pl.BlockSpec((1, rows, tn), lambda l, j: (l, 0, j)),
        out_shape=jax.ShapeDtypeStruct((depth, rows, n), F32),
        compiler_params=_cparams("arbitrary", "arbitrary"),
        name="ada_mod",
    )(c_pad, w_ada, b_ada.reshape(depth, 1, n))
    mod = out[:, :bsz].reshape(depth, bsz, 6, d)
    return jnp.pad(mod, ((0, 0), (0, 0), (0, SUBLANES - 6), (0, 0)))


def _rope_trig_kernel(p_ref, f_ref, o_ref):
    half = f_ref.shape[0]
    pad = jnp.zeros((LANES - 2 * half, LANES), F32)
    for r in range(p_ref.shape[0]):
        a = p_ref[r:r + 1, :] * f_ref[...]
        o_ref[r * LANES:(r + 1) * LANES, :] = jnp.concatenate([jnp.cos(a), jnp.sin(a), pad], axis=0).T


def _rope_tables(positions):
    bsz, seq = positions.shape
    half = ROT_DIM // 2
    inv_freq = ROPE_THETA ** (-jnp.arange(0, ROT_DIM, 2, dtype=F32) / ROT_DIM)
    rows = bsz * seq // LANES
    tr = min(rows, SUBLANES)
    cs = pl.pallas_call(
        _rope_trig_kernel,
        grid=(rows // tr,),
        in_specs=[pl.BlockSpec((tr, LANES), lambda i: (i, 0)), _const_spec((half, LANES))],
        out_specs=pl.BlockSpec((tr * LANES, LANES), lambda i: (i, 0)),
        out_shape=jax.ShapeDtypeStruct((bsz * seq, LANES), F32),
        compiler_params=_cparams("arbitrary"),
        name="rope_trig",
    )(positions.astype(F32).reshape(rows, LANES), jnp.broadcast_to(inv_freq[:, None], (half, LANES)))
    return cs.reshape(bsz, seq, LANES)


def _expand_rope(cs):
    half = ROT_DIM // 2
    lane = lax.broadcasted_iota(jnp.int32, cs.shape, 1)
    c0 = jnp.where(lane < half, cs, jnp.where(lane < ROT_DIM, pltpu.roll(cs, half, axis=1), 1.0))
    sa0 = jnp.where(lane < half, -pltpu.roll(cs, LANES - half, axis=1), 0.0)
    sb0 = jnp.where((lane >= half) & (lane < ROT_DIM), cs, 0.0)
    second = lane >= DT_HEAD
    return (jnp.where(second, pltpu.roll(c0, DT_HEAD, axis=1), c0),
            jnp.where(second, pltpu.roll(sa0, DT_HEAD, axis=1), sa0),
            jnp.where(second, pltpu.roll(sb0, DT_HEAD, axis=1), sb0))


def _group_mean_matrix(width, group):
    idx = jnp.arange(width) // group
    return jnp.where(idx[:, None] == idx[None, :], 1.0 / group, 0.0).astype(BF16)


def _even_in_kernel(x_ref, mod_ref, g_ref, w_ref, qn_ref, kn_ref, cs_ref, bd_ref,
                    u_ref, q_ref, k_ref, v_ref, *, s5w, dfw):
    x = x_ref[0]
    h = _modnorm(x, g_ref[...], mod_ref[0, 1:2, :], mod_ref[0, 0:1, :])
    proj = _dot(h.astype(BF16), w_ref[...])
    u_ref[...] = proj[:, :s5w]
    cosv, sav, sbv = _expand_rope(cs_ref[0])
    half = ROT_DIM // 2

    def norm_rope(t, gn, out_ref, post_scale):
        ms = _dot((t * t).astype(BF16), bd_ref[...])
        t = t * lax.rsqrt(ms + RMS_EPS) * gn
        for j in range(dfw // LANES):
            tj = t[:, j * LANES:(j + 1) * LANES]
            up = pltpu.roll(tj, LANES - half, axis=1)
            dn = pltpu.roll(tj, half, axis=1)
            rj = tj * cosv + up * sav + dn * sbv
            out_ref[0, :, j * LANES:(j + 1) * LANES] = (rj * post_scale).astype(out_ref.dtype)

    norm_rope(proj[:, s5w:s5w + dfw], qn_ref[...], q_ref, DT_HEAD ** -0.5 * math.log2(math.e))
    norm_rope(proj[:, s5w + dfw:s5w + 2 * dfw], kn_ref[...], k_ref, 1.0)
    v_ref[0] = proj[:, s5w + 2 * dfw:].astype(v_ref.dtype)


def _even_in(x, mod, g, w_in, q_norm, k_norm, tables, s5w, dfw, tm):
    bsz, seq, d = x.shape
    ncol = w_in.shape[1]
    rep = dfw // DT_HEAD
    kern = functools.partial(_even_in_kernel, s5w=s5w, dfw=dfw)
    tok = lambda w: pl.BlockSpec((1, tm, w), lambda i, b: (b, i, 0))
    return pl.pallas_call(
        kern,
        grid=(seq // tm, bsz),
        in_specs=[
            tok(d),
            pl.BlockSpec((1, SUBLANES, d), lambda i, b: (b, 0, 0)),
            _const_spec((1, d)),
            _const_spec((d, ncol)),
            _const_spec((1, dfw)),
            _const_spec((1, dfw)),
            tok(LANES),
            _const_spec((dfw, dfw)),
        ],
        out_specs=[
            pl.BlockSpec((tm, s5w), lambda i, b: (i, b)),
            tok(dfw), tok(dfw), tok(dfw),
        ],
        out_shape=[
            jax.ShapeDtypeStruct((seq, bsz * s5w), F32),
            jax.ShapeDtypeStruct((bsz, seq, dfw), BF16),
            jax.ShapeDtypeStruct((bsz, seq, dfw), BF16),
            jax.ShapeDtypeStruct((bsz, seq, dfw), BF16),
        ],
        compiler_params=_cparams("arbitrary", "arbitrary"),
        name="even_in_proj",
    )(x, mod, g.reshape(1, d), w_in.astype(BF16),
      jnp.tile(q_norm, rep).reshape(1, dfw), jnp.tile(k_norm, rep).reshape(1, dfw),
      tables, _group_mean_matrix(dfw, DT_HEAD))


S5_CHUNK = 8
FFN_TILE = 1024


def _s5_maps_kernel(lr_ref, li_ref, ldt_ref, lrc_ref, lic_ref, ldtc_ref, br_ref, bi_ref,
                    lrt_ref, lit_ref, ldtt_ref, cr_ref, ci_ref,
                    kmat_out, wre_out, wim_out, mre_out, mim_out, a8r_out, a8i_out, *, hh):
    rr = S5_CHUNK
    nblk = kmat_out.shape[0]
    pp = lrc_ref.shape[1]
    gb = LANES // hh

    def abar(lr, li, ldt):
        dt = jnp.exp(ldt)
        mag = jnp.exp(lr * dt)
        return mag * jnp.cos(li * dt), mag * jnp.sin(li * dt)

    def cmul(ar, ai, br, bi):
        return ar * br - ai * bi, ar * bi + ai * br

    def mm3(a, b):
        a_hi, a_lo = _split_bf16(a)
        b_hi, b_lo = _split_bf16(b)
        return _dot(a_hi, b_hi) + _dot(a_hi, b_lo) + _dot(a_lo, b_hi)

    def iota2(shape, axis):
        return lax.broadcasted_iota(jnp.int32, shape, axis)

    same_kk = (iota2((LANES, LANES), 0) // hh) == (iota2((LANES, LANES), 1) // hh)
    same_w = (iota2((LANES, gb * pp), 0) // hh) == (iota2((LANES, gb * pp), 1) // pp)
    same_m = (iota2((gb * pp, LANES), 0) // pp) == (iota2((gb * pp, LANES), 1) // hh)
    lane_tile = jnp.where(iota2((pp, gb * pp), 1) % pp == iota2((pp, gb * pp), 0), 1.0, 0.0
                          ).astype(BF16)

    lr, li = lrc_ref[...], lic_ref[...]
    a_r, a_i = abar(lr, li, ldtc_ref[...])
    den = lr * lr + li * li
    num_r = a_r - 1.0
    q_r = (num_r * lr + a_i * li) / den
    q_i = (a_i * lr - num_r * li) / den
    bb_r, bb_i = cmul(q_r, q_i, br_ref[...], bi_ref[...])
    pr, pi = bb_r, bb_i
    for i in range(rr - 1, -1, -1):
        for out, val in ((wre_out, pr), (wim_out, pi)):
            tiled = _dot(val.astype(BF16), lane_tile)
            for b in range(nblk):
                out[b, i * LANES:(i + 1) * LANES, :] = jnp.where(
                    same_w, tiled[b * LANES:(b + 1) * LANES], 0.0).astype(BF16)
        if i:
            pr, pi = cmul(a_r, a_i, pr, pi)

    kmat_out[...] = jnp.zeros_like(kmat_out)
    t_ar, t_ai = abar(lrt_ref[...], lit_ref[...], ldtt_ref[...])
    car, cai = cr_ref[...], ci_ref[...]
    for k in range(rr + 1):
        if k:
            car, cai = cmul(car, cai, t_ar, t_ai)
            for out, val in ((mre_out, car), (mim_out, -cai)):
                for b in range(nblk):
                    rows = jnp.concatenate([val[:, b * LANES:(b + 1) * LANES]] * gb, axis=0)
                    out[b, :, (k - 1) * LANES:k * LANES] = jnp.where(same_m, rows, 0.0
                                                                     ).astype(BF16)
        if k < rr:
            for b in range(nblk):
                sl = slice(b * LANES, (b + 1) * LANES)
                kk = jnp.where(same_kk, mm3(bb_r[sl], car[:, sl]) - mm3(bb_i[sl], cai[:, sl]), 0.0
                               ).astype(BF16)
                for j in range(rr - k):
                    kmat_out[b, j * LANES:(j + 1) * LANES, (j + k) * LANES:(j + k + 1) * LANES] = kk
    e_r, e_i = abar(lr_ref[...], li_ref[...], ldt_ref[...])
    for _ in range(S5_CHUNK.bit_length() - 1):
        e_r, e_i = cmul(e_r, e_i, e_r, e_i)
    pr, pi = e_r, e_i
    for s in range(a8r_out.shape[0]):
        if s:
            pr, pi = cmul(pr, pi, e_r, e_i)
        a8r_out[s] = pr
        a8i_out[s] = pi


def _s5_chunk_kernel(u_ref, kmat_ref, wre_ref, wim_ref, mre_ref, mim_ref, dsk_ref, a1r_ref, a1i_ref,
                     pwr_ref, pwi_ref, wglu_ref, perm_ref, permt_ref, o_ref, xr_ref, xi_ref,
                     cr_ref, ci_ref, *, bsz, sub, lane_chunk):
    nrow, nstate = xr_ref.shape
    tchunk = u_ref.shape[0]
    width = wglu_ref.shape[0]
    nsub = tchunk // sub
    rps = sub // S5_CHUNK * bsz
    nblk, gl = kmat_ref.shape[0], kmat_ref.shape[1] // S5_CHUNK
    spart = wre_ref.shape[2]
    step = pl.program_id(0)

    @pl.when(step == 0)
    def _():
        cr_ref[...] = jnp.zeros_like(cr_ref)
        ci_ref[...] = jnp.zeros_like(ci_ref)

    blk = u_ref[...]
    parts = []
    for s in range(nsub):
        u_bt = jnp.concatenate([blk[s * sub:(s + 1) * sub, b * width:(b + 1) * width]
                                for b in range(bsz)], axis=0)
        parts.append(_dot_x2_rhs(perm_ref[...], u_bt))
    u_j = [jnp.concatenate([p[j * rps:(j + 1) * rps] for p in parts], axis=0)
           for j in range(S5_CHUNK)]
    u_jb = [t.astype(BF16) for t in u_j]
    u_g = [jnp.concatenate([t[:, g * gl:(g + 1) * gl] for t in u_jb], axis=1)
           for g in range(nblk)]

    for g in range(nblk):
        xr_ref[:, g * spart:(g + 1) * spart] = _dot(u_g[g], wre_ref[g])
        xi_ref[:, g * spart:(g + 1) * spart] = _dot(u_g[g], wim_ref[g])

    old_r = cr_ref[...]
    old_i = ci_ref[...]
    row = lax.broadcasted_iota(jnp.int32, (SUBLANES, lane_chunk), 0)
    steps_per_tile = SUBLANES // bsz
    ntiles = nrow // SUBLANES
    for c in range(nstate // lane_chunk):
        cols = pl.ds(c * lane_chunk, lane_chunk)
        a1r = a1r_ref[:, cols]
        a1i = a1i_ref[:, cols]
        pwr = pwr_ref[:, cols]
        pwi = pwi_ref[:, cols]

        def tile_body(i, carry):
            pr, pi = carry
            base = pl.multiple_of(i * SUBLANES, SUBLANES)
            zr = xr_ref[pl.ds(base, SUBLANES), cols]
            zi = xi_ref[pl.ds(base, SUBLANES), cols]
            sh = bsz
            apr, api = a1r, a1i
            for _ in range(steps_per_tile.bit_length() - 1):
                sr = jnp.where(row >= sh, pltpu.roll(zr, sh, axis=0), 0.0)
                si = jnp.where(row >= sh, pltpu.roll(zi, sh, axis=0), 0.0)
                zr, zi = zr + apr * sr - api * si, zi + apr * si + api * sr
                apr, api = apr * apr - api * api, 2.0 * apr * api
                sh *= 2
            last = SUBLANES - bsz
            br_, bi_ = pr, pi
            sh = bsz
            while sh < SUBLANES:
                br_ = jnp.where(row >= last, br_, pltpu.roll(br_, SUBLANES - sh, axis=0))
                bi_ = jnp.where(row >= last, bi_, pltpu.roll(bi_, SUBLANES - sh, axis=0))
                last -= sh
                sh *= 2
            xr = zr + pwr * br_ - pwi * bi_
            xi = zi + pwr * bi_ + pwi * br_
            xr_ref[pl.ds(base, SUBLANES), cols] = xr
            xi_ref[pl.ds(base, SUBLANES), cols] = xi
            return xr, xi

        fr, fi = lax.fori_loop(0, ntiles, tile_body, (cr_ref[:, cols], ci_ref[:, cols]))
        cr_ref[:, cols] = fr
        ci_ref[:, cols] = fi

    row8 = lax.broadcasted_iota(jnp.int32, (SUBLANES, nstate), 0)

    def state_in(x, old):
        xs = pltpu.roll(x, bsz, axis=0)
        first = jnp.where(row8 < bsz, pltpu.roll(old, bsz, axis=0), xs[:SUBLANES])
        return jnp.concatenate([first, xs[SUBLANES:]], axis=0).astype(BF16)

    xin_r = state_in(xr_ref[...], old_r)
    xin_i = state_in(xi_ref[...], old_i)

    tw = 2 * gl

    def k_part(g):
        strips = []
        for s in range(S5_CHUNK // 2):
            kdim = (2 * s + 2) * gl
            strips.append(_dot(u_g[g][:, :kdim], kmat_ref[g, :kdim, s * tw:(s + 1) * tw]))
        return jnp.concatenate(strips, axis=1)

    y_g = [k_part(g)
           + _dot(xin_r[:, g * spart:(g + 1) * spart], mre_ref[g])
           + _dot(xin_i[:, g * spart:(g + 1) * spart], mim_ref[g]) for g in range(nblk)]
    y = jnp.concatenate(
        [jnp.concatenate([y_g[g][:, t * gl:(t + 1) * gl] for g in range(nblk)], axis=1)
         + dsk_ref[...] * u_j[t] for t in range(S5_CHUNK)], axis=0)
    y = jax.nn.gelu(y)
    gate = jax.nn.sigmoid(_dot(y.astype(BF16), wglu_ref[...]))
    out = (y * gate).astype(BF16)
    for s in range(nsub):
        o_s = jnp.concatenate([out[t * nrow + s * rps:t * nrow + (s + 1) * rps]
                               for t in range(S5_CHUNK)], axis=0)
        o_bt = _dot(permt_ref[...], o_s)
        for b in range(bsz):
            o_ref[s * sub:(s + 1) * sub, b * width:(b + 1) * width] = (
                o_bt[b * sub:(b + 1) * sub].astype(o_ref.dtype))


def _s5_mixer_chunked(u2, bsz, lam_re, lam_im, log_dt, b_re, b_im, c_re, c_im, d_skip, w_glu,
                      tchunk):
    seq = u2.shape[0]
    width = u2.shape[1] // bsz
    g, p = lam_re.shape
    hh = b_re.shape[-1]
    gh = g * hh
    nstate = g * p
    rr = S5_CHUNK
    ldt = jnp.broadcast_to(log_dt[:, None], (g, p))
    rep_c = lambda t: jnp.repeat(t, hh, axis=0)
    rep_t = lambda t: jnp.repeat(t.T, hh, axis=1)
    b_c = lambda b: b.transpose(0, 2, 1).reshape(gh, p)
    c_t = lambda c: c.transpose(2, 0, 1).reshape(p, gh)
    spt = SUBLANES // bsz
    gb = LANES // hh
    nblk = g // gb
    blk_w = rr * LANES
    bf = lambda *s: jax.ShapeDtypeStruct(s, BF16)
    f32 = lambda *s: jax.ShapeDtypeStruct(s, F32)
    kmat, w_re, w_im, m_re, m_im, a8_r, a8_i = pl.pallas_call(
        functools.partial(_s5_maps_kernel, hh=hh),
        out_shape=[bf(nblk, blk_w, blk_w), bf(nblk, blk_w, gb * p), bf(nblk, blk_w, gb * p),
                   bf(nblk, gb * p, blk_w), bf(nblk, gb * p, blk_w), f32(spt, g, p), f32(spt, g, p)],
        compiler_params=pltpu.CompilerParams(vmem_limit_bytes=VMEM_LIMIT),
        name="s5_chunk_maps",
    )(lam_re, lam_im, ldt, rep_c(lam_re), rep_c(lam_im), rep_c(ldt), b_c(b_re), b_c(b_im),
      rep_t(lam_re), rep_t(lam_im), rep_t(ldt), c_t(c_re), c_t(c_im))
    pw_r = jnp.repeat(a8_r.reshape(spt, nstate), bsz, axis=0)
    pw_i = jnp.repeat(a8_i.reshape(spt, nstate), bsz, axis=0)
    a1r = jnp.broadcast_to(pw_r[0:1], (SUBLANES, nstate))
    a1i = jnp.broadcast_to(pw_i[0:1], (SUBLANES, nstate))
    consts = [kmat, w_re, w_im, m_re, m_im, d_skip.reshape(1, width),
              a1r, a1i, pw_r, pw_i, w_glu.astype(BF16)]

    sub = min(tchunk, 128)
    rows = sub * bsz
    dst = jnp.arange(rows)
    rps = sub // rr * bsz
    j_, rem = dst // rps, dst % rps
    src_of = (rem % bsz) * sub + (rem // bsz) * rr + j_
    perm = (jnp.arange(rows)[None, :] == src_of[:, None]).astype(BF16)
    nrow = tchunk // rr * bsz
    kern = functools.partial(_s5_chunk_kernel, bsz=bsz, sub=sub, lane_chunk=512)
    return pl.pallas_call(
        kern,
        grid=(seq // tchunk,),
        in_specs=[pl.BlockSpec((tchunk, bsz * width), lambda i: (i, 0))]
                 + [_const_spec(t.shape) for t in consts]
                 + [_const_spec((rows, rows)), _const_spec((rows, rows))],
        out_specs=pl.BlockSpec((tchunk, bsz * width), lambda i: (i, 0)),
        out_shape=jax.ShapeDtypeStruct((seq, bsz * width), BF16),
        scratch_shapes=[
            pltpu.VMEM((nrow, nstate), F32), pltpu.VMEM((nrow, nstate), F32),
            pltpu.VMEM((SUBLANES, nstate), F32), pltpu.VMEM((SUBLANES, nstate), F32),
        ],
        compiler_params=_cparams("arbitrary"),
        name="s5_chunked",
    )(u2, *consts, perm, perm.T)


def _diff_attn_kernel(lam_ref, sub_ref, q_ref, k_ref, v_ref, o_ref, q2_ref, m_ref, acc_ref,
                      *, tq, lam_init):
    qi = pl.program_id(2)
    vdim = v_ref.shape[-1]
    q = q_ref[0]
    lane = lax.broadcasted_iota(jnp.int32, q.shape, 1)
    zero = jnp.zeros_like(q)
    q2_ref[:tq, :] = jnp.where(lane < DT_HEAD, q, zero)
    q2_ref[tq:, :] = jnp.where(lane >= DT_HEAD, q, zero)
    all_rows = ((0, 2 * tq),)

    def gather(ref, row_slices):
        parts = [ref[a:b, :] for a, b in row_slices]
        return parts[0] if len(parts) == 1 else jnp.concatenate(parts, axis=0)

    def scores(kstart, ksize, row_slices=all_rows):
        start = pl.multiple_of(kstart, ksize)
        return _dot_nt(gather(q2_ref, row_slices), k_ref[0, pl.ds(start, ksize), :])

    def update(kstart, ksize, s, row_slices=all_rows, mask=None, first=False):
        start = pl.multiple_of(kstart, ksize)
        v_ext = jnp.concatenate([v_ref[0, pl.ds(start, ksize), :], jnp.ones((ksize, vdim), BF16)],
                                axis=1)
        if mask is not None:
            s = jnp.where(mask, s, -jnp.inf)
        part = s[:, :LANES]
        for t in range(1, ksize // LANES):
            part = jnp.maximum(part, s[:, t * LANES:(t + 1) * LANES])
        row_max = jnp.max(part, axis=-1, keepdims=True)
        if first:
            assert row_slices == all_rows
            m_new = jnp.broadcast_to(row_max, part.shape)
        else:
            m_old = gather(m_ref, row_slices)
            m_new = jnp.maximum(m_old, row_max)
        p = jnp.concatenate([jnp.exp2((s[:, t * LANES:(t + 1) * LANES] - m_new).astype(BF16))
                             for t in range(ksize // LANES)], axis=1)
        acc = _dot(p, v_ext)
        if not first:
            alpha = jnp.exp2(m_old - m_new)
            acc = acc + jnp.concatenate([alpha] * (2 * vdim // LANES), axis=1) * gather(
                acc_ref, row_slices)
        off = 0
        for a, b in row_slices:
            m_ref[a:b, :] = m_new[off:off + b - a]
            acc_ref[a:b, :] = acc[off:off + b - a]
            off += b - a

    def diagonal(kstart, with_previous, first):
        hq = tq // 2
        if with_previous:
            s_prev = scores(kstart - tq, tq)
        row_l = lax.broadcasted_iota(jnp.int32, (2 * tq, hq), 0)
        col_l = lax.broadcasted_iota(jnp.int32, (2 * tq, hq), 1)
        row_r = lax.broadcasted_iota(jnp.int32, (tq, hq), 0)
        col_r = lax.broadcasted_iota(jnp.int32, (tq, hq), 1)
        late_rows = ((hq, tq), (tq + hq, 2 * tq))
        s_left = scores(kstart, hq)
        s_right = scores(kstart + hq, hq, late_rows)
        if with_previous:
            update(kstart - tq, tq, s_prev, first=first)
        update(kstart, hq, s_left, mask=col_l <= row_l % tq, first=first and not with_previous)
        update(kstart + hq, hq, s_right, late_rows, col_r <= row_r % hq)

    def pair(j0, first=False):
        s0 = scores(j0 * tq, tq)
        s1 = scores((j0 + 1) * tq, tq)
        update(j0 * tq, tq, s0, first=first)
        update((j0 + 1) * tq, tq, s1)

    npairs = qi // 2

    @pl.when(npairs > 0)
    def _():
        pair(0, first=True)

    lax.fori_loop(1, npairs, lambda jj, _: (pair(2 * jj), 0)[1], 0)
    for odd in (False, True):
        for is_first in (False, True):
            @pl.when((qi % 2 == int(odd)) & ((npairs == 0) if is_first else (npairs > 0)))
            def _():
                diagonal(qi * tq, odd, is_first)

    lv = lam_ref[...]
    lam = (jnp.exp(jnp.sum(lv[0:1] * lv[1:2], axis=-1, keepdims=True))
           - jnp.exp(jnp.sum(lv[2:3] * lv[3:4], axis=-1, keepdims=True)) + lam_init)
    acc = acc_ref[...]
    o = (acc[:tq, :vdim] / acc[:tq, vdim:]) - lam * (acc[tq:, :vdim] / acc[tq:, vdim:])
    ms = jnp.mean(o * o, axis=-1, keepdims=True)
    o = o * lax.rsqrt(ms + RMS_EPS) * sub_ref[...] * (1.0 - lam_init)
    o_ref[0] = o.astype(o_ref.dtype)


def _diff_attention(q, k, v, lq1, lk1, lq2, lk2, subln, lam_init, tq):
    bsz, seq, dfw = q.shape
    vdim = 2 * DT_HEAD
    heads = dfw // vdim
    lamv = jnp.zeros((SUBLANES, LANES), F32)
    for i, t in enumerate((lq1, lk1, lq2, lk2)):
        lamv = lamv.at[i, :t.shape[0]].set(t)
    kern = functools.partial(_diff_attn_kernel, tq=tq, lam_init=lam_init)
    kv_spec = pl.BlockSpec((1, seq, vdim), lambda b, h, i: (b, 0, h))
    return pl.pallas_call(
        kern,
        grid=(bsz, heads, seq // tq),
        in_specs=[
            _const_spec((SUBLANES, LANES)),
            _const_spec((1, vdim)),
            pl.BlockSpec((1, tq, vdim), lambda b, h, i: (b, i, h)),
            kv_spec, kv_spec,
        ],
        out_specs=pl.BlockSpec((1, tq, vdim), lambda b, h, i: (b, i, h)),
        out_shape=jax.ShapeDtypeStruct((bsz, seq, dfw), BF16),
        scratch_shapes=[pltpu.VMEM((2 * tq, vdim), BF16), pltpu.VMEM((2 * tq, LANES), F32),
                        pltpu.VMEM((2 * tq, 2 * vdim), F32)],
        compiler_params=_cparams("arbitrary", "arbitrary", "arbitrary"),
        name="diff_attention",
    )(lamv, subln.reshape(1, vdim), q, k, v)


def _mix_ffn_kernel(*refs, n_in, hid_chunk):
    x_ref, mod_ref, g_ref = refs[:3]
    act_refs = refs[3:3 + n_in]
    w_refs = refs[3 + n_in:3 + 2 * n_in]
    wg_ref, wu_ref, wd_ref, o_ref = refs[3 + 2 * n_in:]
    mix = None
    for a_ref, w_ref in zip(act_refs, w_refs):
        a = a_ref[0] if len(a_ref.shape) == 3 else a_ref[...]
        t = _dot(a, w_ref[...])
        mix = t if mix is None else mix + t
    x1 = x_ref[0] + mod_ref[0, 2:3, :] * mix
    h = _modnorm(x1, g_ref[...], mod_ref[0, 4:5, :], mod_ref[0, 3:4, :]).astype(BF16)
    acc = None
    for j in range(wg_ref.shape[2] // hid_chunk):
        sl = slice(j * hid_chunk, (j + 1) * hid_chunk)
        gate = _dot(h, wg_ref[0, :, sl])
        up = _dot(h, wu_ref[0, :, sl])
        act = (gate * jax.nn.sigmoid(gate) * up).astype(BF16)
        t = _dot(act, wd_ref[0, sl, :])
        acc = t if acc is None else acc + t
    o_ref[0] = x1 + mod_ref[0, 5:6, :] * acc


def _mix_ffn(x, mod, g, acts, weights, ffn, tm, hid_chunk, w_specs=None):
    bsz, seq, d = x.shape
    if w_specs is None:
        w_specs = [_const_spec(w.shape) for w in weights]
    wg, wu, wd, layer = ffn
    assert wg.shape[2] % hid_chunk == 0

    def layer_spec(w):
        return pl.BlockSpec((1,) + w.shape[1:], lambda i, b: (layer, 0, 0),
                            pipeline_mode=pl.Buffered(1))

    tok = pl.BlockSpec((1, tm, d), lambda i, b: (b, i, 0))
    return pl.pallas_call(
        functools.partial(_mix_ffn_kernel, n_in=len(acts), hid_chunk=hid_chunk),
        grid=(seq // tm, bsz),
        in_specs=[tok, pl.BlockSpec((1, SUBLANES, d), lambda i, b: (b, 0, 0)), _const_spec((1, d))]
                 + [spec for _, spec in acts]
                 + w_specs
                 + [layer_spec(wg), layer_spec(wu), layer_spec(wd)],
        out_specs=tok,
        out_shape=jax.ShapeDtypeStruct((bsz, seq, d), F32),
        compiler_params=_cparams("arbitrary", "arbitrary"),
        name="mix_ffn",
    )(x, mod, g.reshape(1, d), *[a for a, _ in acts], *weights, wg, wu, wd)


def _rwkv_in_kernel(x_ref, xp_ref, mod_ref, g_ref, mu_ref, wr_ref, wk_ref, wv_ref, w1_ref, w2_ref,
                    a1_ref, a2_ref, g1_ref, g2_ref, w0_ref, a0_ref, kkw_ref, kaw_ref, bd_ref,
                    r_out, lw_out, k_out, v_out, kk_out, a_out, g_out):
    i = pl.program_id(1)
    g = g_ref[...]
    scale = mod_ref[0, 1:2, :]
    shift = mod_ref[0, 0:1, :]
    h = _modnorm(x_ref[0], g, scale, shift)
    hp = _modnorm(xp_ref[0][SUBLANES - 1:SUBLANES, :], g, scale, shift)
    hp = jnp.where(i == 0, 0.0, hp)
    row = lax.broadcasted_iota(jnp.int32, h.shape, 0)
    h_prev = jnp.where(row == 0, hp, pltpu.roll(h, 1, axis=0))
    dx = h_prev - h

    def lerp(j):
        return (h + dx * mu_ref[j:j + 1, :]).astype(BF16)

    r_out[0] = _dot(lerp(0), wr_ref[...])
    wl = jnp.tanh(_dot(lerp(1), w1_ref[...]))
    wdec = w0_ref[...] + _dot(wl.astype(BF16), w2_ref[...])
    w = jnp.minimum(wdec, 0.0) - jnp.log1p(jnp.exp(-jnp.abs(wdec))) - 0.5
    lw_out[0] = -jnp.exp(w)
    k = _dot(lerp(2), wk_ref[...])
    v_out[0] = _dot(lerp(3), wv_ref[...])
    al = _dot(lerp(4), a1_ref[...])
    a = jax.nn.sigmoid(a0_ref[...] + _dot(al.astype(BF16), a2_ref[...]))
    a_out[0] = a
    gl = jax.nn.sigmoid(_dot(lerp(5), g1_ref[...]))
    g_out[0] = _dot(gl.astype(BF16), g2_ref[...])
    kk = k * kkw_ref[...]
    bd = bd_ref[...]
    for j in range(kk.shape[1] // LANES):
        kj = kk[:, j * LANES:(j + 1) * LANES]
        ss = _dot((kj * kj).astype(BF16), bd) * float(RWKV_HEAD)
        kk_out[0, :, j * LANES:(j + 1) * LANES] = kj * lax.rsqrt(jnp.maximum(ss, 1e-24))
    k_out[0] = k * (1.0 + (a - 1.0) * kaw_ref[...])


def _rwkv_in(x, mod, g, mu, w_r, w_k, w_v, w0, w1, w2, a0, a1, a2, g1, g2, k_k, k_a, tm):
    bsz, seq, d = x.shape
    tok = pl.BlockSpec((1, tm, d), lambda b, i: (b, i, 0))
    prev = pl.BlockSpec((1, SUBLANES, d),
                        lambda b, i: (b, jnp.maximum(i * (tm // SUBLANES) - 1, 0), 0))
    bf = lambda w: w.astype(BF16)
    vec = lambda t: t.reshape(1, d)
    consts = [vec(g), mu, bf(w_r), bf(w_k), bf(w_v), bf(w1), bf(w2), bf(a1), bf(a2), bf(g1), bf(g2),
              vec(w0), vec(a0), vec(k_k), vec(k_a), _group_mean_matrix(LANES, RWKV_HEAD)]
    out = jax.ShapeDtypeStruct((bsz, seq, d), F32)
    return pl.pallas_call(
        _rwkv_in_kernel,
        grid=(bsz, seq // tm),
        in_specs=[tok, prev, pl.BlockSpec((1, SUBLANES, d), lambda b, i: (b, 0, 0))]
                 + [_const_spec(t.shape) for t in consts],
        out_specs=[tok] * 7,
        out_shape=[out] * 7,
        compiler_params=_cparams("arbitrary", "arbitrary"),
        name="rwkv_in_proj",
    )(x, x, mod, *consts)


def _rwkv_rec_kernel(r_ref, lw_ref, k_ref, v_ref, kk_ref, a_ref, g_ref, rk_ref, lng_ref, lnb_ref,
                     tri_ref, bd_ref, o_ref, m_ref, y_ref, *, chunk):
    tstep = pl.program_id(2)

    @pl.when(tstep == 0)
    def _():
        m_ref[...] = jnp.zeros_like(m_ref)

    tb, width = r_ref.shape[1:]
    hd = RWKV_HEAD
    gw = m_ref.shape[-1]
    nh = gw // hd
    groups = range(width // gw)
    colsl = [slice(q * gw, (q + 1) * gw) for q in groups]
    lane = lax.broadcasted_iota(jnp.int32, (chunk, gw), 1)
    trow = lax.broadcasted_iota(jnp.int32, (chunk, gw), 0)
    jpos = lane % hd
    strict = jpos < trow
    incl = jpos <= trow
    eye = jnp.where(jpos == trow, 1.0, 0.0)
    head_of_lane = lane // hd
    in_head = [head_of_lane == h for h in range(nh)]
    sq_r = lax.broadcasted_iota(jnp.int32, (gw, gw), 0)
    sq_c = lax.broadcasted_iota(jnp.int32, (gw, gw), 1)
    same_head = (sq_r // hd) == (sq_c // hd)
    diag = sq_r == sq_c
    tri = tri_ref[...]
    bd = bd_ref[...]
    bf = lambda t: t.astype(BF16)

    def blockdiag(y):
        return bf(jnp.concatenate([jnp.where(in_head[h], y, 0.0) for h in range(nh)], axis=0))

    def blockdiag_t(x):
        xt = jnp.concatenate([x] * nh, axis=0).T
        return bf(jnp.where(same_head, xt, 0.0))

    inst = [(c, q) for c in range(tb // chunk) for q in groups]
    cls = []
    for c, q in inst:
        lw = lw_ref[0, c * chunk:(c + 1) * chunk, colsl[q]]
        hi = bf(lw)
        r1 = lw - hi.astype(F32)
        mid = bf(r1)
        lo = bf(r1 - mid.astype(F32))
        cls.append((lw, _dot(tri, hi) + _dot(tri, mid) + _dot(tri, lo)))
    opnd = []
    for (c, q), (lw, cl) in zip(inst, cls):
        rows = slice(c * chunk, (c + 1) * chunk)
        cols = colsl[q]
        k = k_ref[0, rows, cols]
        kk = kk_ref[0, rows, cols]
        b = kk * a_ref[0, rows, cols]
        v = v_ref[0, rows, cols]
        cl_end = cl[chunk - 1:chunk, :]
        p_inv = jnp.exp(-cl)
        p_tail = jnp.exp(cl_end - cl)
        ar = jnp.concatenate([bf(-kk * jnp.exp(cl - lw)), bf(r_ref[0, rows, cols] * jnp.exp(cl))],
                             axis=0)
        tail_t = bf(jnp.concatenate([b * p_tail, k * p_tail], axis=0).T)
        pl_full = jnp.broadcast_to(jnp.exp(cl_end), (gw, gw))
        pl_col = jnp.sum(jnp.where(diag, pl_full, 0.0), axis=1, keepdims=True)
        opnd.append((ar, blockdiag_t(b * p_inv), blockdiag_t(k * p_inv), bf(v), blockdiag(v),
                     tail_t, pl_col))
    g_b = [_dot(o[0], o[1]) for o in opnd]
    g_k = [_dot(o[0], o[2]) for o in opnd]
    nms = [jnp.where(strict, g[:chunk], 0.0) for g in g_b]
    tinv = [eye + nm for nm in nms]
    pw = [bf(_dot(bf(nm), blockdiag(nm))) for nm in nms]
    for _ in range(chunk.bit_length() - 3):
        both = [_dot(jnp.concatenate([bf(t), p2], axis=0), blockdiag(p2)) for t, p2 in zip(tinv, pw)]
        tinv = [t + bo[:chunk] for t, bo in zip(tinv, both)]
        pw = [bf(bo[chunk:]) for bo in both]
    tinv = [bf(t + _dot(bf(t), blockdiag(p2))) for t, p2 in zip(tinv, pw)]
    g_ak = [bf(jnp.where(strict, g[:chunk], 0.0)) for g in g_k]
    g_rb = [bf(jnp.where(incl, g[chunk:], 0.0)) for g in g_b]
    g_rk = [bf(jnp.where(incl, g[chunk:], 0.0)) for g in g_k]

    state = [m_ref[q] for q in groups]
    for c in range(tb // chunk):
        ids = [c * len(groups) + q for q in groups]
        m0_bf = [bf(state[q]) for q in groups]
        rhs = [_dot(jnp.concatenate([opnd[i][0][:chunk], g_ak[i]], axis=1),
                    jnp.concatenate([m0_bf[q], opnd[i][4]], axis=0)) for q, i in zip(groups, ids)]
        u = [_dot(tinv[i], blockdiag(rhs[q])) for q, i in zip(groups, ids)]
        for q, i in zip(groups, ids):
            y_ref[c * chunk:(c + 1) * chunk, colsl[q]] = _dot(
                jnp.concatenate([opnd[i][0][chunk:], g_rb[i], g_rk[i]], axis=1),
                jnp.concatenate([m0_bf[q], blockdiag(u[q]), opnd[i][4]], axis=0))
        for q, i in zip(groups, ids):
            upd = _dot(opnd[i][5], jnp.concatenate([bf(u[q]), opnd[i][3]], axis=0))
            state[q] = opnd[i][6] * state[q] + jnp.where(same_head, upd, 0.0)
    for q in groups:
        m_ref[q] = state[q]

    npair = width // LANES
    for p in range(npair):
        cols = slice(p * LANES, (p + 1) * LANES)
        y = y_ref[:, cols]
        mean = _dot_x2(y, bd)
        dlt = y - mean
        var = _dot(bf(dlt * dlt), bd)
        yn = dlt * lax.rsqrt(var + GN_EPS) * lng_ref[:, cols] + lnb_ref[:, cols]
        rk_sum = _dot(bf(r_ref[0, :, cols] * k_ref[0, :, cols] * rk_ref[:, cols]), bd) * float(hd)
        out = (yn + rk_sum * v_ref[0, :, cols]) * g_ref[0, :, cols]
        o_ref[0, :, cols] = out.astype(o_ref.dtype)


def _rwkv_recurrence(r, lw, k, v, kk, a, g, r_k, ln_g, ln_b, tb, chunk, wblk):
    bsz, seq, d = r.shape
    gw = 2 * RWKV_HEAD
    assert chunk == RWKV_HEAD and wblk % gw == 0
    tok = pl.BlockSpec((1, tb, wblk), lambda b, j, t: (b, t, j))
    vec = pl.BlockSpec((1, wblk), lambda b, j, t: (0, j))
    tri = jnp.tril(jnp.ones((chunk, chunk), F32)).astype(BF16)
    bd = _group_mean_matrix(LANES, RWKV_HEAD)
    return pl.pallas_call(
        functools.partial(_rwkv_rec_kernel, chunk=chunk),
        grid=(bsz, d // wblk, seq // tb),
        in_specs=[tok] * 7 + [vec] * 3 + [_const_spec(tri.shape), _const_spec(bd.shape)],
        out_specs=tok,
        out_shape=jax.ShapeDtypeStruct((bsz, seq, d), BF16),
        scratch_shapes=[pltpu.VMEM((wblk // gw, gw, gw), F32),
                        pltpu.VMEM((tb, wblk), F32)],
        compiler_params=_cparams("arbitrary", "arbitrary", "arbitrary"),
        name="rwkv_recurrence",
    )(r, lw, k, v, kk, a, g, r_k.reshape(1, d), ln_g.reshape(1, d), ln_b.reshape(1, d), tri, bd)


def _odd_layer(x, mod, norm_mix, norm_ffn, ffn, mu, w_r, w_k, w_v, w_o, w0, w1,
               w2, a0, a1, a2, g1, g2, k_k, k_a, r_k, ln_g, ln_b):
    bsz, seq, d = x.shape
    r, lw, k, v, kk, a, g = _rwkv_in(x, mod, norm_mix, mu, w_r, w_k, w_v, w0, w1, w2, a0, a1, a2,
                                     g1, g2, k_k, k_a, _pick_tile(seq, 512))
    yg = _rwkv_recurrence(r, lw, k, v, kk, a, g, r_k, ln_g, ln_b, _pick_tile(seq, 512), 64,
                          d)
    tm = _pick_tile(seq, FFN_TILE)
    acts = [(yg, pl.BlockSpec((1, tm, d), lambda i, b: (b, i, 0)))]
    return _mix_ffn(x, mod, norm_ffn, acts, [w_o.astype(BF16)], ffn, tm, 256)


def _pick_tile(n, pref):
    t = min(n, pref)
    assert n % t == 0, (n, t)
    return t


def _even_layer(x, mod, tables, lam_init, norm_mix, norm_ffn, ffn, w_in,
                lam_re, lam_im, log_dt, b_re, b_im, c_re, c_im, d_skip, w_glu, q_norm, k_norm,
                lq1, lk1, lq2, lk2, subln, w_out):
    bsz, seq, d = x.shape
    s5w = lam_re.shape[0] * b_re.shape[-1]
    dfw = (w_in.shape[1] - s5w) // 3
    u2, q, k, v = _even_in(x, mod, norm_mix, w_in, q_norm, k_norm, tables, s5w, dfw,
                           _pick_tile(seq, 512))
    ys = _s5_mixer_chunked(u2, bsz, lam_re, lam_im, log_dt, b_re, b_im,
                           c_re, c_im, d_skip.reshape(-1), w_glu, _pick_tile(seq, 512))
    att = _diff_attention(q, k, v, lq1, lk1, lq2, lk2, subln, lam_init, _pick_tile(seq, 512))
    w_out_bf = w_out.astype(BF16)
    assert s5w == dfw
    tm = _pick_tile(seq, FFN_TILE)
    acts = [
        (ys, pl.BlockSpec((tm, s5w), lambda i, b: (i, b))),
        (att, pl.BlockSpec((1, tm, dfw), lambda i, b: (b, i, 0))),
    ]
    w_specs = [pl.BlockSpec((s5w, d), lambda i, b: (0, 0), pipeline_mode=pl.Buffered(1)),
               pl.BlockSpec((dfw, d), lambda i, b: (1, 0), pipeline_mode=pl.Buffered(1))]
    return _mix_ffn(x, mod, norm_ffn, acts, [w_out_bf, w_out_bf], ffn, tm, 256, w_specs)


def kernel(x, c, positions, w_ada, b_ada, norm_mix, norm_ffn, ffn_w_gate, ffn_w_up, ffn_w_down,
           ev_w_in, ev_s5_lam_re, ev_s5_lam_im, ev_s5_log_dt, ev_s5_b_re, ev_s5_b_im, ev_s5_c_re,
           ev_s5_c_im, ev_s5_d, ev_s5_w_glu, ev_q_norm, ev_k_norm, ev_lambda_q1, ev_lambda_k1,
           ev_lambda_q2, ev_lambda_k2, ev_subln, ev_w_out, od_mu, od_w_r, od_w_k, od_w_v, od_w_o,
           od_w0, od_w1, od_w2, od_a0, od_a1, od_a2, od_g1, od_g2, od_k_k, od_k_a, od_r_k,
           od_ln_g, od_ln_b):
    depth = w_ada.shape[0]
    mod = _ada_mod(c, w_ada, b_ada)
    tables = _rope_tables(positions)
    ffn_bf = (ffn_w_gate.astype(BF16), ffn_w_up.astype(BF16), ffn_w_down.astype(BF16))
    for l in range(depth):
        ffn = ffn_bf + (l,)
        if l % 2 == 0:
            e = l // 2
            lam_init = 0.8 - 0.6 * math.exp(-0.3 * l)
            x = _even_layer(x, mod[l], tables, lam_init, norm_mix[l], norm_ffn[l], ffn,
                            ev_w_in[e], ev_s5_lam_re[e],
                            ev_s5_lam_im[e], ev_s5_log_dt[e], ev_s5_b_re[e], ev_s5_b_im[e],
                            ev_s5_c_re[e], ev_s5_c_im[e], ev_s5_d[e], ev_s5_w_glu[e], ev_q_norm[e],
                            ev_k_norm[e], ev_lambda_q1[e], ev_lambda_k1[e], ev_lambda_q2[e],
                            ev_lambda_k2[e], ev_subln[e], ev_w_out[e])
        else:
            o = l // 2
            x = _odd_layer(x, mod[l], norm_mix[l], norm_ffn[l], ffn,
                           od_mu[o], od_w_r[o], od_w_k[o], od_w_v[o], od_w_o[o],
                           od_w0[o], od_w1[o], od_w2[o], od_a0[o], od_a1[o], od_a2[o], od_g1[o],
                           od_g2[o], od_k_k[o], od_k_a[o], od_r_k[o], od_ln_g[o], od_ln_b[o])
    return x
```

```python
import functools
import math

import jax
import jax.numpy as jnp
from jax import lax
from jax.experimental import pallas as pl
from jax.experimental.pallas import tpu as pltpu

F32 = jnp.float32
BF16 = jnp.bfloat16

RMS_EPS = 1e-6
GN_EPS = 64e-5
ROPE_THETA = 500000.0
DT_HEAD = 64
ROT_DIM = DT_HEAD // 4
RWKV_HEAD = 64
LANES = 128
SUBLANES = 8
VMEM_LIMIT = 56 * 1024 * 1024


def _cparams(*sem):
    return pltpu.CompilerParams(dimension_semantics=sem, vmem_limit_bytes=VMEM_LIMIT)


def _const_spec(shape):
    nd = len(shape)
    return pl.BlockSpec(shape, lambda *_: (0,) * nd, pipeline_mode=pl.Buffered(1))


def _dot(a, b):
    return jnp.dot(a, b, preferred_element_type=F32)


def _dot_nt(a, b):
    return lax.dot_general(a, b, (((1,), (1,)), ((), ())), preferred_element_type=F32)


def _split_bf16(x):
    hi = x.astype(BF16)
    lo = (x - hi.astype(F32)).astype(BF16)
    return hi, lo


def _dot_x2(x, w_bf16):
    hi, lo = _split_bf16(x)
    return _dot(hi, w_bf16) + _dot(lo, w_bf16)


def _dot_x2_rhs(w_bf16, x):
    hi, lo = _split_bf16(x)
    return _dot(w_bf16, hi) + _dot(w_bf16, lo)


def _modnorm(x, g, scale, shift):
    ms = jnp.mean(x * x, axis=-1, keepdims=True)
    return (x * lax.rsqrt(ms + RMS_EPS)) * (g * (1.0 + scale)) + shift


def _ada_kernel(c_ref, w_ref, b_ref, o_ref):
    c = c_ref[...]
    ca = c * jax.nn.sigmoid(c)
    hi, lo = _split_bf16(ca)
    w = w_ref[0]
    w_hi, w_lo = _split_bf16(w)
    o_ref[0] = _dot(hi, w_hi) + _dot(lo, w_hi) + _dot(hi, w_lo) + b_ref[0]


def _ada_mod(c, w_ada, b_ada):
    depth, d, n = w_ada.shape
    bsz = c.shape[0]
    rows = -(-bsz // SUBLANES) * SUBLANES
    c_pad = jnp.zeros((rows, d), F32).at[:bsz].set(c)
    tn = n // 4
    out = pl.pallas_call(
        _ada_kernel,
        grid=(depth, n // tn),
        in_specs=[
            pl.BlockSpec((rows, d), lambda l, j: (0, 0)),
            pl.BlockSpec((1, d, tn), lambda l, j: (l, 0, j)),
            pl.BlockSpec((1, 1, tn), lambda l, j: (l, 0, j)),
        ],
        out_specs=pl.BlockSpec((1, rows, tn), lambda l, j: (l, 0, j)),
        out_shape=jax.ShapeDtypeStruct((depth, rows, n), F32),
        compiler_params=_cparams("arbitrary", "arbitrary"),
        name="ada_mod",
    )(c_pad, w_ada, b_ada.reshape(depth, 1, n))
    mod = out[:, :bsz].reshape(depth, bsz, 6, d)
    return jnp.pad(mod, ((0, 0), (0, 0), (0, SUBLANES - 6), (0, 0)))


def _rope_trig_kernel(p_ref, f_ref, o_ref):
    half = f_ref.shape[0]
    pad = jnp.zeros((LANES - 2 * half, LANES), F32)
    for r in range(p_ref.shape[0]):
        a = p_ref[r:r + 1, :] * f_ref[...]
        o_ref[r * LANES:(r + 1) * LANES, :] = jnp.concatenate([jnp.cos(a), jnp.sin(a), pad], axis=0).T


def _rope_tables(positions):
    bsz, seq = positions.shape
    half = ROT_DIM // 2
    inv_freq = ROPE_THETA ** (-jnp.arange(0, ROT_DIM, 2, dtype=F32) / ROT_DIM)
    rows = bsz * seq // LANES
    tr = min(rows, SUBLANES)
    cs = pl.pallas_call(
        _rope_trig_kernel,
        grid=(rows // tr,),
        in_specs=[pl.BlockSpec((tr, LANES), lambda i: (i, 0)), _const_spec((half, LANES))],
        out_specs=pl.BlockSpec((tr * LANES, LANES), lambda i: (i, 0)),
        out_shape=jax.ShapeDtypeStruct((bsz * seq, LANES), F32),
        compiler_params=_cparams("arbitrary"),
        name="rope_trig",
    )(positions.astype(F32).reshape(rows, LANES), jnp.broadcast_to(inv_freq[:, None], (half, LANES)))
    return cs.reshape(bsz, seq, LANES)


def _expand_rope(cs):
    half = ROT_DIM // 2
    lane = lax.broadcasted_iota(jnp.int32, cs.shape, 1)
    c0 = jnp.where(lane < half, cs, jnp.where(lane < ROT_DIM, pltpu.roll(cs, half, axis=1), 1.0))
    sa0 = jnp.where(lane < half, -pltpu.roll(cs, LANES - half, axis=1), 0.0)
    sb0 = jnp.where((lane >= half) & (lane < ROT_DIM), cs, 0.0)
    second = lane >= DT_HEAD
    return (jnp.where(second, pltpu.roll(c0, DT_HEAD, axis=1), c0),
            jnp.where(second, pltpu.roll(sa0, DT_HEAD, axis=1), sa0),
            jnp.where(second, pltpu.roll(sb0, DT_HEAD, axis=1), sb0))


def _group_mean_matrix(width, group):
    idx = jnp.arange(width) // group
    return jnp.where(idx[:, None] == idx[None, :], 1.0 / group, 0.0).astype(BF16)


def _even_in_kernel(x_ref, mod_ref, g_ref, w_ref, qn_ref, kn_ref, cs_ref, bd_ref,
                    u_ref, q_ref, k_ref, v_ref, *, s5w, dfw):
    x = x_ref[0]
    h = _modnorm(x, g_ref[...], mod_ref[0, 1:2, :], mod_ref[0, 0:1, :])
    proj = _dot(h.astype(BF16), w_ref[...])
    u_ref[...] = proj[:, :s5w]
    cosv, sav, sbv = _expand_rope(cs_ref[0])
    half = ROT_DIM // 2

    def norm_rope(t, gn, out_ref, post_scale):
        ms = _dot((t * t).astype(BF16), bd_ref[...])
        t = t * lax.rsqrt(ms + RMS_EPS) * gn
        for j in range(dfw // LANES):
            tj = t[:, j * LANES:(j + 1) * LANES]
            up = pltpu.roll(tj, LANES - half, axis=1)
            dn = pltpu.roll(tj, half, axis=1)
            rj = tj * cosv + up * sav + dn * sbv
            out_ref[0, :, j * LANES:(j + 1) * LANES] = (rj * post_scale).astype(out_ref.dtype)

    norm_rope(proj[:, s5w:s5w + dfw], qn_ref[...], q_ref, DT_HEAD ** -0.5 * math.log2(math.e))
    norm_rope(proj[:, s5w + dfw:s5w + 2 * dfw], kn_ref[...], k_ref, 1.0)
    v_ref[0] = proj[:, s5w + 2 * dfw:].astype(v_ref.dtype)


def _even_in(x, mod, g, w_in, q_norm, k_norm, tables, s5w, dfw, tm):
    bsz, seq, d = x.shape
    ncol = w_in.shape[1]
    rep = dfw // DT_HEAD
    kern = functools.partial(_even_in_kernel, s5w=s5w, dfw=dfw)
    tok = lambda w: pl.BlockSpec((1, tm, w), lambda i, b: (b, i, 0))
    return pl.pallas_call(
        kern,
        grid=(seq // tm, bsz),
        in_specs=[
            tok(d),
            pl.BlockSpec((1, SUBLANES, d), lambda i, b: (b, 0, 0)),
            _const_spec((1, d)),
            _const_spec((d, ncol)),
            _const_spec((1, dfw)),
            _const_spec((1, dfw)),
            tok(LANES),
            _const_spec((dfw, dfw)),
        ],
        out_specs=[
            pl.BlockSpec((tm, s5w), lambda i, b: (i, b)),
            tok(dfw), tok(dfw), tok(dfw),
        ],
        out_shape=[
            jax.ShapeDtypeStruct((seq, bsz * s5w), F32),
            jax.ShapeDtypeStruct((bsz, seq, dfw), BF16),
            jax.ShapeDtypeStruct((bsz, seq, dfw), BF16),
            jax.ShapeDtypeStruct((bsz, seq, dfw), BF16),
        ],
        compiler_params=_cparams("arbitrary", "arbitrary"),
        name="even_in_proj",
    )(x, mod, g.reshape(1, d), w_in.astype(BF16),
      jnp.tile(q_norm, rep).reshape(1, dfw), jnp.tile(k_norm, rep).reshape(1, dfw),
      tables, _group_mean_matrix(dfw, DT_HEAD))


S5_CHUNK = 8
FFN_TILE = 1024


def _s5_maps_kernel(lr_ref, li_ref, ldt_ref, lrc_ref, lic_ref, ldtc_ref, br_ref, bi_ref,
                    lrt_ref, lit_ref, ldtt_ref, cr_ref, ci_ref,
                    kmat_out, wre_out, wim_out, mre_out, mim_out, a8r_out, a8i_out, *, hh):
    rr = S5_CHUNK
    nblk = kmat_out.shape[0]
    pp = lrc_ref.shape[1]
    gb = LANES // hh

    def abar(lr, li, ldt):
        dt = jnp.exp(ldt)
        mag = jnp.exp(lr * dt)
        return mag * jnp.cos(li * dt), mag * jnp.sin(li * dt)

    def cmul(ar, ai, br, bi):
        return ar * br - ai * bi, ar * bi + ai * br

    def mm3(a, b):
        a_hi, a_lo = _split_bf16(a)
        b_hi, b_lo = _split_bf16(b)
        return _dot(a_hi, b_hi) + _dot(a_hi, b_lo) + _dot(a_lo, b_hi)

    def iota2(shape, axis):
        return lax.broadcasted_iota(jnp.int32, shape, axis)

    same_kk = (iota2((LANES, LANES), 0) // hh) == (iota2((LANES, LANES), 1) // hh)
    same_w = (iota2((LANES, gb * pp), 0) // hh) == (iota2((LANES, gb * pp), 1) // pp)
    same_m = (iota2((gb * pp, LANES), 0) // pp) == (iota2((gb * pp, LANES), 1) // hh)
    lane_tile = jnp.where(iota2((pp, gb * pp), 1) % pp == iota2((pp, gb * pp), 0), 1.0, 0.0
                          ).astype(BF16)

    lr, li = lrc_ref[...], lic_ref[...]
    a_r, a_i = abar(lr, li, ldtc_ref[...])
    den = lr * lr + li * li
    num_r = a_r - 1.0
    q_r = (num_r * lr + a_i * li) / den
    q_i = (a_i * lr - num_r * li) / den
    bb_r, bb_i = cmul(q_r, q_i, br_ref[...], bi_ref[...])
    pr, pi = bb_r, bb_i
    for i in range(rr - 1, -1, -1):
        for out, val in ((wre_out, pr), (wim_out, pi)):
            tiled = _dot(val.astype(BF16), lane_tile)
            for b in range(nblk):
                out[b, i * LANES:(i + 1) * LANES, :] = jnp.where(
                    same_w, tiled[b * LANES:(b + 1) * LANES], 0.0).astype(BF16)
        if i:
            pr, pi = cmul(a_r, a_i, pr, pi)

    kmat_out[...] = jnp.zeros_like(kmat_out)
    t_ar, t_ai = abar(lrt_ref[...], lit_ref[...], ldtt_ref[...])
    car, cai = cr_ref[...], ci_ref[...]
    for k in range(rr + 1):
        if k:
            car, cai = cmul(car, cai, t_ar, t_ai)
            for out, val in ((mre_out, car), (mim_out, -cai)):
                for b in range(nblk):
                    rows = jnp.concatenate([val[:, b * LANES:(b + 1) * LANES]] * gb, axis=0)
                    out[b, :, (k - 1) * LANES:k * LANES] = jnp.where(same_m, rows, 0.0
                                                                     ).astype(BF16)
        if k < rr:
            for b in range(nblk):
                sl = slice(b * LANES, (b + 1) * LANES)
                kk = jnp.where(same_kk, mm3(bb_r[sl], car[:, sl]) - mm3(bb_i[sl], cai[:, sl]), 0.0
                               ).astype(BF16)
                for j in range(rr - k):
                    kmat_out[b, j * LANES:(j + 1) * LANES, (j + k) * LANES:(j + k + 1) * LANES] = kk
    e_r, e_i = abar(lr_ref[...], li_ref[...], ldt_ref[...])
    for _ in range(S5_CHUNK.bit_length() - 1):
        e_r, e_i = cmul(e_r, e_i, e_r, e_i)
    pr, pi = e_r, e_i
    for s in range(a8r_out.shape[0]):
        if s:
            pr, pi = cmul(pr, pi, e_r, e_i)
        a8r_out[s] = pr
        a8i_out[s] = pi


def _s5_chunk_kernel(u_ref, kmat_ref, wre_ref, wim_ref, mre_ref, mim_ref, dsk_ref, a1r_ref, a1i_ref,
                     pwr_ref, pwi_ref, wglu_ref, perm_ref, permt_ref, o_ref, xr_ref, xi_ref,
                     cr_ref, ci_ref, *, bsz, sub, lane_chunk):
    nrow, nstate = xr_ref.shape
    tchunk = u_ref.shape[0]
    width = wglu_ref.shape[0]
    nsub = tchunk // sub
    rps = sub // S5_CHUNK * bsz
    nblk, gl = kmat_ref.shape[0], kmat_ref.shape[1] // S5_CHUNK
    spart = wre_ref.shape[2]
    step = pl.program_id(0)

    @pl.when(step == 0)
    def _():
        cr_ref[...] = jnp.zeros_like(cr_ref)
        ci_ref[...] = jnp.zeros_like(ci_ref)

    blk = u_ref[...]
    parts = []
    for s in range(nsub):
        u_bt = jnp.concatenate([blk[s * sub:(s + 1) * sub, b * width:(b + 1) * width]
                                for b in range(bsz)], axis=0)
        parts.append(_dot_x2_rhs(perm_ref[...], u_bt))
    u_j = [jnp.concatenate([p[j * rps:(j + 1) * rps] for p in parts], axis=0)
           for j in range(S5_CHUNK)]
    u_jb = [t.astype(BF16) for t in u_j]
    u_g = [jnp.concatenate([t[:, g * gl:(g + 1) * gl] for t in u_jb], axis=1)
           for g in range(nblk)]

    for g in range(nblk):
        xr_ref[:, g * spart:(g + 1) * spart] = _dot(u_g[g], wre_ref[g])
        xi_ref[:, g * spart:(g + 1) * spart] = _dot(u_g[g], wim_ref[g])

    old_r = cr_ref[...]
    old_i = ci_ref[...]
    row = lax.broadcasted_iota(jnp.int32, (SUBLANES, lane_chunk), 0)
    steps_per_tile = SUBLANES // bsz
    ntiles = nrow // SUBLANES
    for c in range(nstate // lane_chunk):
        cols = pl.ds(c * lane_chunk, lane_chunk)
        a1r = a1r_ref[:, cols]
        a1i = a1i_ref[:, cols]
        pwr = pwr_ref[:, cols]
        pwi = pwi_ref[:, cols]

        def tile_body(i, carry):
            pr, pi = carry
            base = pl.multiple_of(i * SUBLANES, SUBLANES)
            zr = xr_ref[pl.ds(base, SUBLANES), cols]
            zi = xi_ref[pl.ds(base, SUBLANES), cols]
            sh = bsz
            apr, api = a1r, a1i
            for _ in range(steps_per_tile.bit_length() - 1):
                sr = jnp.where(row >= sh, pltpu.roll(zr, sh, axis=0), 0.0)
                si = jnp.where(row >= sh, pltpu.roll(zi, sh, axis=0), 0.0)
                zr, zi = zr + apr * sr - api * si, zi + apr * si + api * sr
                apr, api = apr * apr - api * api, 2.0 * apr * api
                sh *= 2
            last = SUBLANES - bsz
            br_, bi_ = pr, pi
            sh = bsz
            while sh < SUBLANES:
                br_ = jnp.where(row >= last, br_, pltpu.roll(br_, SUBLANES - sh, axis=0))
                bi_ = jnp.where(row >= last, bi_, pltpu.roll(bi_, SUBLANES - sh, axis=0))
                last -= sh
                sh *= 2
            xr = zr + pwr * br_ - pwi * bi_
            xi = zi + pwr * bi_ + pwi * br_
            xr_ref[pl.ds(base, SUBLANES), cols] = xr
            xi_ref[pl.ds(base, SUBLANES), cols] = xi
            return xr, xi

        fr, fi = lax.fori_loop(0, ntiles, tile_body, (cr_ref[:, cols], ci_ref[:, cols]))
        cr_ref[:, cols] = fr
        ci_ref[:, cols] = fi

    row8 = lax.broadcasted_iota(jnp.int32, (SUBLANES, nstate), 0)

    def state_in(x, old):
        xs = pltpu.roll(x, bsz, axis=0)
        first = jnp.where(row8 < bsz, pltpu.roll(old, bsz, axis=0), xs[:SUBLANES])
        return jnp.concatenate([first, xs[SUBLANES:]], axis=0).astype(BF16)

    xin_r = state_in(xr_ref[...], old_r)
    xin_i = state_in(xi_ref[...], old_i)

    tw = 2 * gl

    def k_part(g):
        strips = []
        for s in range(S5_CHUNK // 2):
            kdim = (2 * s + 2) * gl
            strips.append(_dot(u_g[g][:, :kdim], kmat_ref[g, :kdim, s * tw:(s + 1) * tw]))
        return jnp.concatenate(strips, axis=1)

    y_g = [k_part(g)
           + _dot(xin_r[:, g * spart:(g + 1) * spart], mre_ref[g])
           + _dot(xin_i[:, g * spart:(g + 1) * spart], mim_ref[g]) for g in range(nblk)]
    y = jnp.concatenate(
        [jnp.concatenate([y_g[g][:, t * gl:(t + 1) * gl] for g in range(nblk)], axis=1)
         + dsk_ref[...] * u_j[t] for t in range(S5_CHUNK)], axis=0)
    y = jax.nn.gelu(y)
    gate = jax.nn.sigmoid(_dot(y.astype(BF16), wglu_ref[...]))
    out = (y * gate).astype(BF16)
    for s in range(nsub):
        o_s = jnp.concatenate([out[t * nrow + s * rps:t * nrow + (s + 1) * rps]
                               for t in range(S5_CHUNK)], axis=0)
        o_bt = _dot(permt_ref[...], o_s)
        for b in range(bsz):
            o_ref[s * sub:(s + 1) * sub, b * width:(b + 1) * width] = (
                o_bt[b * sub:(b + 1) * sub].astype(o_ref.dtype))


def _s5_mixer_chunked(u2, bsz, lam_re, lam_im, log_dt, b_re, b_im, c_re, c_im, d_skip, w_glu,
                      tchunk):
    seq = u2.shape[0]
    width = u2.shape[1] // bsz
    g, p = lam_re.shape
    hh = b_re.shape[-1]
    gh = g * hh
    nstate = g * p
    rr = S5_CHUNK
    ldt = jnp.broadcast_to(log_dt[:, None], (g, p))
    rep_c = lambda t: jnp.repeat(t, hh, axis=0)
    rep_t = lambda t: jnp.repeat(t.T, hh, axis=1)
    b_c = lambda b: b.transpose(0, 2, 1).reshape(gh, p)
    c_t = lambda c: c.transpose(2, 0, 1).reshape(p, gh)
    spt = SUBLANES // bsz
    gb = LANES // hh
    nblk = g // gb
    blk_w = rr * LANES
    bf = lambda *s: jax.ShapeDtypeStruct(s, BF16)
    f32 = lambda *s: jax.ShapeDtypeStruct(s, F32)
    kmat, w_re, w_im, m_re, m_im, a8_r, a8_i = pl.pallas_call(
        functools.partial(_s5_maps_kernel, hh=hh),
        out_shape=[bf(nblk, blk_w, blk_w), bf(nblk, blk_w, gb * p), bf(nblk, blk_w, gb * p),
                   bf(nblk, gb * p, blk_w), bf(nblk, gb * p, blk_w), f32(spt, g, p), f32(spt, g, p)],
        compiler_params=pltpu.CompilerParams(vmem_limit_bytes=VMEM_LIMIT),
        name="s5_chunk_maps",
    )(lam_re, lam_im, ldt, rep_c(lam_re), rep_c(lam_im), rep_c(ldt), b_c(b_re), b_c(b_im),
      rep_t(lam_re), rep_t(lam_im), rep_t(ldt), c_t(c_re), c_t(c_im))
    pw_r = jnp.repeat(a8_r.reshape(spt, nstate), bsz, axis=0)
    pw_i = jnp.repeat(a8_i.reshape(spt, nstate), bsz, axis=0)
    a1r = jnp.broadcast_to(pw_r[0:1], (SUBLANES, nstate))
    a1i = jnp.broadcast_to(pw_i[0:1], (SUBLANES, nstate))
    consts = [kmat, w_re, w_im, m_re, m_im, d_skip.reshape(1, width),
              a1r, a1i, pw_r, pw_i, w_glu.astype(BF16)]

    sub = min(tchunk, 128)
    rows = sub * bsz
    dst = jnp.arange(rows)
    rps = sub // rr * bsz
    j_, rem = dst // rps, dst % rps
    src_of = (rem % bsz) * sub + (rem // bsz) * rr + j_
    perm = (jnp.arange(rows)[None, :] == src_of[:, None]).astype(BF16)
    nrow = tchunk // rr * bsz
    kern = functools.partial(_s5_chunk_kernel, bsz=bsz, sub=sub, lane_chunk=512)
    return pl.pallas_call(
        kern,
        grid=(seq // tchunk,),
        in_specs=[pl.BlockSpec((tchunk, bsz * width), lambda i: (i, 0))]
                 + [_const_spec(t.shape) for t in consts]
                 + [_const_spec((rows, rows)), _const_spec((rows, rows))],
        out_specs=pl.BlockSpec((tchunk, bsz * width), lambda i: (i, 0)),
        out_shape=jax.ShapeDtypeStruct((seq, bsz * width), BF16),
        scratch_shapes=[
            pltpu.VMEM((nrow, nstate), F32), pltpu.VMEM((nrow, nstate), F32),
            pltpu.VMEM((SUBLANES, nstate), F32), pltpu.VMEM((SUBLANES, nstate), F32),
        ],
        compiler_params=_cparams("arbitrary"),
        name="s5_chunked",
    )(u2, *consts, perm, perm.T)


def _diff_attn_kernel(lam_ref, sub_ref, q_ref, k_ref, v_ref, o_ref, q2_ref, m_ref, acc_ref,
                      *, tq, lam_init):
    qi = pl.program_id(2)
    vdim = v_ref.shape[-1]
    q = q_ref[0]
    lane = lax.broadcasted_iota(jnp.int32, q.shape, 1)
    zero = jnp.zeros_like(q)
    q2_ref[:tq, :] = jnp.where(lane < DT_HEAD, q, zero)
    q2_ref[tq:, :] = jnp.where(lane >= DT_HEAD, q, zero)
    all_rows = ((0, 2 * tq),)

    def gather(ref, row_slices):
        parts = [ref[a:b, :] for a, b in row_slices]
        return parts[0] if len(parts) == 1 else jnp.concatenate(parts, axis=0)

    def scores(kstart, ksize, row_slices=all_rows):
        start = pl.multiple_of(kstart, ksize)
        return _dot_nt(gather(q2_ref, row_slices), k_ref[0, pl.ds(start, ksize), :])

    def update(kstart, ksize, s, row_slices=all_rows, mask=None, first=False):
        start = pl.multiple_of(kstart, ksize)
        v_ext = jnp.concatenate([v_ref[0, pl.ds(start, ksize), :], jnp.ones((ksize, vdim), BF16)],
                                axis=1)
        if mask is not None:
            s = jnp.where(mask, s, -jnp.inf)
        part = s[:, :LANES]
        for t in range(1, ksize // LANES):
            part = jnp.maximum(part, s[:, t * LANES:(t + 1) * LANES])
        row_max = jnp.max(part, axis=-1, keepdims=True)
        if first:
            assert row_slices == all_rows
            m_new = jnp.broadcast_to(row_max, part.shape)
        else:
            m_old = gather(m_ref, row_slices)
            m_new = jnp.maximum(m_old, row_max)
        p = jnp.concatenate([jnp.exp2((s[:, t * LANES:(t + 1) * LANES] - m_new).astype(BF16))
                             for t in range(ksize // LANES)], axis=1)
        acc = _dot(p, v_ext)
        if not first:
            alpha = jnp.exp2(m_old - m_new)
            acc = acc + jnp.concatenate([alpha] * (2 * vdim // LANES), axis=1) * gather(
                acc_ref, row_slices)
        off = 0
        for a, b in row_slices:
            m_ref[a:b, :] = m_new[off:off + b - a]
            acc_ref[a:b, :] = acc[off:off + b - a]
            off += b - a

    def diagonal(kstart, with_previous, first):
        hq = tq // 2
        if with_previous:
            s_prev = scores(kstart - tq, tq)
        row_l = lax.broadcasted_iota(jnp.int32, (2 * tq, hq), 0)
        col_l = lax.broadcasted_iota(jnp.int32, (2 * tq, hq), 1)
        row_r = lax.broadcasted_iota(jnp.int32, (tq, hq), 0)
        col_r = lax.broadcasted_iota(jnp.int32, (tq, hq), 1)
        late_rows = ((hq, tq), (tq + hq, 2 * tq))
        s_left = scores(kstart, hq)
        s_right = scores(kstart + hq, hq, late_rows)
        if with_previous:
            update(kstart - tq, tq, s_prev, first=first)
        update(kstart, hq, s_left, mask=col_l <= row_l % tq, first=first and not with_previous)
        update(kstart + hq, hq, s_right, late_rows, col_r <= row_r % hq)

    def pair(j0, first=False):
        s0 = scores(j0 * tq, tq)
        s1 = scores((j0 + 1) * tq, tq)
        update(j0 * tq, tq, s0, first=first)
        update((j0 + 1) * tq, tq, s1)

    npairs = qi // 2

    @pl.when(npairs > 0)
    def _():
        pair(0, first=True)

    lax.fori_loop(1, npairs, lambda jj, _: (pair(2 * jj), 0)[1], 0)
    for odd in (False, True):
        for is_first in (False, True):
            @pl.when((qi % 2 == int(odd)) & ((npairs == 0) if is_first else (npairs > 0)))
            def _():
                diagonal(qi * tq, odd, is_first)

    lv = lam_ref[...]
    lam = (jnp.exp(jnp.sum(lv[0:1] * lv[1:2], axis=-1, keepdims=True))
           - jnp.exp(jnp.sum(lv[2:3] * lv[3:4], axis=-1, keepdims=True)) + lam_init)
    acc = acc_ref[...]
    o = (acc[:tq, :vdim] / acc[:tq, vdim:]) - lam * (acc[tq:, :vdim] / acc[tq:, vdim:])
    ms = jnp.mean(o * o, axis=-1, keepdims=True)
    o = o * lax.rsqrt(ms + RMS_EPS) * sub_ref[...] * (1.0 - lam_init)
    o_ref[0] = o.astype(o_ref.dtype)


def _diff_attention(q, k, v, lq1, lk1, lq2, lk2, subln, lam_init, tq):
    bsz, seq, dfw = q.shape
    vdim = 2 * DT_HEAD
    heads = dfw // vdim
    lamv = jnp.zeros((SUBLANES, LANES), F32)
    for i, t in enumerate((lq1, lk1, lq2, lk2)):
        lamv = lamv.at[i, :t.shape[0]].set(t)
    kern = functools.partial(_diff_attn_kernel, tq=tq, lam_init=lam_init)
    kv_spec = pl.BlockSpec((1, seq, vdim), lambda b, h, i: (b, 0, h))
    return pl.pallas_call(
        kern,
        grid=(bsz, heads, seq // tq),
        in_specs=[
            _const_spec((SUBLANES, LANES)),
            _const_spec((1, vdim)),
            pl.BlockSpec((1, tq, vdim), lambda b, h, i: (b, i, h)),
            kv_spec, kv_spec,
        ],
        out_specs=pl.BlockSpec((1, tq, vdim), lambda b, h, i: (b, i, h)),
        out_shape=jax.ShapeDtypeStruct((bsz, seq, dfw), BF16),
        scratch_shapes=[pltpu.VMEM((2 * tq, vdim), BF16), pltpu.VMEM((2 * tq, LANES), F32),
                        pltpu.VMEM((2 * tq, 2 * vdim), F32)],
        compiler_params=_cparams("arbitrary", "arbitrary", "arbitrary"),
        name="diff_attention",
    )(lamv, subln.reshape(1, vdim), q, k, v)


def _mix_ffn_kernel(*refs, n_in, hid_chunk):
    x_ref, mod_ref, g_ref = refs[:3]
    act_refs = refs[3:3 + n_in]
    w_refs = refs[3 + n_in:3 + 2 * n_in]
    wg_ref, wu_ref, wd_ref, o_ref = refs[3 + 2 * n_in:]
    mix = None
    for a_ref, w_ref in zip(act_refs, w_refs):
        a = a_ref[0] if len(a_ref.shape) == 3 else a_ref[...]
        t = _dot(a, w_ref[...])
        mix = t if mix is None else mix + t
    x1 = x_ref[0] + mod_ref[0, 2:3, :] * mix
    h = _modnorm(x1, g_ref[...], mod_ref[0, 4:5, :], mod_ref[0, 3:4, :]).astype(BF16)
    acc = None
    for j in range(wg_ref.shape[2] // hid_chunk):
        sl = slice(j * hid_chunk, (j + 1) * hid_chunk)
        gate = _dot(h, wg_ref[0, :, sl])
        up = _dot(h, wu_ref[0, :, sl])
        act = (gate * jax.nn.sigmoid(gate) * up).astype(BF16)
        t = _dot(act, wd_ref[0, sl, :])
        acc = t if acc is None else acc + t
    o_ref[0] = x1 + mod_ref[0, 5:6, :] * acc


def _mix_ffn(x, mod, g, acts, weights, ffn, tm, hid_chunk, w_specs=None):
    bsz, seq, d = x.shape
    if w_specs is None:
        w_specs = [_const_spec(w.shape) for w in weights]
    wg, wu, wd, layer = ffn
    assert wg.shape[2] % hid_chunk == 0

    def layer_spec(w):
        return pl.BlockSpec((1,) + w.shape[1:], lambda i, b: (layer, 0, 0),
                            pipeline_mode=pl.Buffered(1))

    tok = pl.BlockSpec((1, tm, d), lambda i, b: (b, i, 0))
    return pl.pallas_call(
        functools.partial(_mix_ffn_kernel, n_in=len(acts), hid_chunk=hid_chunk),
        grid=(seq // tm, bsz),
        in_specs=[tok, pl.BlockSpec((1, SUBLANES, d), lambda i, b: (b, 0, 0)), _const_spec((1, d))]
                 + [spec for _, spec in acts]
                 + w_specs
                 + [layer_spec(wg), layer_spec(wu), layer_spec(wd)],
        out_specs=tok,
        out_shape=jax.ShapeDtypeStruct((bsz, seq, d), F32),
        compiler_params=_cparams("arbitrary", "arbitrary"),
        name="mix_ffn",
    )(x, mod, g.reshape(1, d), *[a for a, _ in acts], *weights, wg, wu, wd)


def _rwkv_in_kernel(x_ref, xp_ref, mod_ref, g_ref, mu_ref, wr_ref, wk_ref, wv_ref, w1_ref, w2_ref,
                    a1_ref, a2_ref, g1_ref, g2_ref, w0_ref, a0_ref, kkw_ref, kaw_ref, bd_ref,
                    r_out, lw_out, k_out, v_out, kk_out, a_out, g_out):
    i = pl.program_id(1)
    g = g_ref[...]
    scale = mod_ref[0, 1:2, :]
    shift = mod_ref[0, 0:1, :]
    h = _modnorm(x_ref[0], g, scale, shift)
    hp = _modnorm(xp_ref[0][SUBLANES - 1:SUBLANES, :], g, scale, shift)
    hp = jnp.where(i == 0, 0.0, hp)
    row = lax.broadcasted_iota(jnp.int32, h.shape, 0)
    h_prev = jnp.where(row == 0, hp, pltpu.roll(h, 1, axis=0))
    dx = h_prev - h

    def lerp(j):
        return (h + dx * mu_ref[j:j + 1, :]).astype(BF16)

    r_out[0] = _dot(lerp(0), wr_ref[...])
    wl = jnp.tanh(_dot(lerp(1), w1_ref[...]))
    wdec = w0_ref[...] + _dot(wl.astype(BF16), w2_ref[...])
    w = jnp.minimum(wdec, 0.0) - jnp.log1p(jnp.exp(-jnp.abs(wdec))) - 0.5
    lw_out[0] = -jnp.exp(w)
    k = _dot(lerp(2), wk_ref[...])
    v_out[0] = _dot(lerp(3), wv_ref[...])
    al = _dot(lerp(4), a1_ref[...])
    a = jax.nn.sigmoid(a0_ref[...] + _dot(al.astype(BF16), a2_ref[...]))
    a_out[0] = a
    gl = jax.nn.sigmoid(_dot(lerp(5), g1_ref[...]))
    g_out[0] = _dot(gl.astype(BF16), g2_ref[...])
    kk = k * kkw_ref[...]
    bd = bd_ref[...]
    for j in range(kk.shape[1] // LANES):
        kj = kk[:, j * LANES:(j + 1) * LANES]
        ss = _dot((kj * kj).astype(BF16), bd) * float(RWKV_HEAD)
        kk_out[0, :, j * LANES:(j + 1) * LANES] = kj * lax.rsqrt(jnp.maximum(ss, 1e-24))
    k_out[0] = k * (1.0 + (a - 1.0) * kaw_ref[...])


def _rwkv_in(x, mod, g, mu, w_r, w_k, w_v, w0, w1, w2, a0, a1, a2, g1, g2, k_k, k_a, tm):
    bsz, seq, d = x.shape
    tok = pl.BlockSpec((1, tm, d), lambda b, i: (b, i, 0))
    prev = pl.BlockSpec((1, SUBLANES, d),
                        lambda b, i: (b, jnp.maximum(i * (tm // SUBLANES) - 1, 0), 0))
    bf = lambda w: w.astype(BF16)
    vec = lambda t: t.reshape(1, d)
    consts = [vec(g), mu, bf(w_r), bf(w_k), bf(w_v), bf(w1), bf(w2), bf(a1), bf(a2), bf(g1), bf(g2),
              vec(w0), vec(a0), vec(k_k), vec(k_a), _group_mean_matrix(LANES, RWKV_HEAD)]
    out = jax.ShapeDtypeStruct((bsz, seq, d), F32)
    return pl.pallas_call(
        _rwkv_in_kernel,
        grid=(bsz, seq // tm),
        in_specs=[tok, prev, pl.BlockSpec((1, SUBLANES, d), lambda b, i: (b, 0, 0))]
                 + [_const_spec(t.shape) for t in consts],
        out_specs=[tok] * 7,
        out_shape=[out] * 7,
        compiler_params=_cparams("arbitrary", "arbitrary"),
        name="rwkv_in_proj",
    )(x, x, mod, *consts)


def _rwkv_rec_kernel(r_ref, lw_ref, k_ref, v_ref, kk_ref, a_ref, g_ref, rk_ref, lng_ref, lnb_ref,
                     tri_ref, bd_ref, o_ref, m_ref, y_ref, *, chunk):
    tstep = pl.program_id(2)

    @pl.when(tstep == 0)
    def _():
        m_ref[...] = jnp.zeros_like(m_ref)

    tb, width = r_ref.shape[1:]
    hd = RWKV_HEAD
    gw = m_ref.shape[-1]
    nh = gw // hd
    groups = range(width // gw)
    colsl = [slice(q * gw, (q + 1) * gw) for q in groups]
    lane = lax.broadcasted_iota(jnp.int32, (chunk, gw), 1)
    trow = lax.broadcasted_iota(jnp.int32, (chunk, gw), 0)
    jpos = lane % hd
    strict = jpos < trow
    incl = jpos <= trow
    eye = jnp.where(jpos == trow, 1.0, 0.0)
    head_of_lane = lane // hd
    in_head = [head_of_lane == h for h in range(nh)]
    sq_r = lax.broadcasted_iota(jnp.int32, (gw, gw), 0)
    sq_c = lax.broadcasted_iota(jnp.int32, (gw, gw), 1)
    same_head = (sq_r // hd) == (sq_c // hd)
    diag = sq_r == sq_c
    tri = tri_ref[...]
    bd = bd_ref[...]
    bf = lambda t: t.astype(BF16)

    def blockdiag(y):
        return bf(jnp.concatenate([jnp.where(in_head[h], y, 0.0) for h in range(nh)], axis=0))

    def blockdiag_t(x):
        xt = jnp.concatenate([x] * nh, axis=0).T
        return bf(jnp.where(same_head, xt, 0.0))

    inst = [(c, q) for c in range(tb // chunk) for q in groups]
    cls = []
    for c, q in inst:
        lw = lw_ref[0, c * chunk:(c + 1) * chunk, colsl[q]]
        hi = bf(lw)
        r1 = lw - hi.astype(F32)
        mid = bf(r1)
        lo = bf(r1 - mid.astype(F32))
        cls.append((lw, _dot(tri, hi) + _dot(tri, mid) + _dot(tri, lo)))
    opnd = []
    for (c, q), (lw, cl) in zip(inst, cls):
        rows = slice(c * chunk, (c + 1) * chunk)
        cols = colsl[q]
        k = k_ref[0, rows, cols]
        kk = kk_ref[0, rows, cols]
        b = kk * a_ref[0, rows, cols]
        v = v_ref[0, rows, cols]
        cl_end = cl[chunk - 1:chunk, :]
        p_inv = jnp.exp(-cl)
        p_tail = jnp.exp(cl_end - cl)
        ar = jnp.concatenate([bf(-kk * jnp.exp(cl - lw)), bf(r_ref[0, rows, cols] * jnp.exp(cl))],
                             axis=0)
        tail_t = bf(jnp.concatenate([b * p_tail, k * p_tail], axis=0).T)
        pl_full = jnp.broadcast_to(jnp.exp(cl_end), (gw, gw))
        pl_col = jnp.sum(jnp.where(diag, pl_full, 0.0), axis=1, keepdims=True)
        opnd.append((ar, blockdiag_t(b * p_inv), blockdiag_t(k * p_inv), bf(v), blockdiag(v),
                     tail_t, pl_col))
    g_b = [_dot(o[0], o[1]) for o in opnd]
    g_k = [_dot(o[0], o[2]) for o in opnd]
    nms = [jnp.where(strict, g[:chunk], 0.0) for g in g_b]
    tinv = [eye + nm for nm in nms]
    pw = [bf(_dot(bf(nm), blockdiag(nm))) for nm in nms]
    for _ in range(chunk.bit_length() - 3):
        both = [_dot(jnp.concatenate([bf(t), p2], axis=0), blockdiag(p2)) for t, p2 in zip(tinv, pw)]
        tinv = [t + bo[:chunk] for t, bo in zip(tinv, both)]
        pw = [bf(bo[chunk:]) for bo in both]
    tinv = [bf(t + _dot(bf(t), blockdiag(p2))) for t, p2 in zip(tinv, pw)]
    g_ak = [bf(jnp.where(strict, g[:chunk], 0.0)) for g in g_k]
    g_rb = [bf(jnp.where(incl, g[chunk:], 0.0)) for g in g_b]
    g_rk = [bf(jnp.where(incl, g[chunk:], 0.0)) for g in g_k]

    state = [m_ref[q] for q in groups]
    for c in range(tb // chunk):
        ids = [c * len(groups) + q for q in groups]
        m0_bf = [bf(state[q]) for q in groups]
        rhs = [_dot(jnp.concatenate([opnd[i][0][:chunk], g_ak[i]], axis=1),
                    jnp.concatenate([m0_bf[q], opnd[i][4]], axis=0)) for q, i in zip(groups, ids)]
        u = [_dot(tinv[i], blockdiag(rhs[q])) for q, i in zip(groups, ids)]
        for q, i in zip(groups, ids):
            y_ref[c * chunk:(c + 1) * chunk, colsl[q]] = _dot(
                jnp.concatenate([opnd[i][0][chunk:], g_rb[i], g_rk[i]], axis=1),
                jnp.concatenate([m0_bf[q], blockdiag(u[q]), opnd[i][4]], axis=0))
        for q, i in zip(groups, ids):
            upd = _dot(opnd[i][5], jnp.concatenate([bf(u[q]), opnd[i][3]], axis=0))
            state[q] = opnd[i][6] * state[q] + jnp.where(same_head, upd, 0.0)
    for q in groups:
        m_ref[q] = state[q]

    npair = width // LANES
    for p in range(npair):
        cols = slice(p * LANES, (p + 1) * LANES)
        y = y_ref[:, cols]
        mean = _dot_x2(y, bd)
        dlt = y - mean
        var = _dot(bf(dlt * dlt), bd)
        yn = dlt * lax.rsqrt(var + GN_EPS) * lng_ref[:, cols] + lnb_ref[:, cols]
        rk_sum = _dot(bf(r_ref[0, :, cols] * k_ref[0, :, cols] * rk_ref[:, cols]), bd) * float(hd)
        out = (yn + rk_sum * v_ref[0, :, cols]) * g_ref[0, :, cols]
        o_ref[0, :, cols] = out.astype(o_ref.dtype)


def _rwkv_recurrence(r, lw, k, v, kk, a, g, r_k, ln_g, ln_b, tb, chunk, wblk):
    bsz, seq, d = r.shape
    gw = 2 * RWKV_HEAD
    assert chunk == RWKV_HEAD and wblk % gw == 0
    tok = pl.BlockSpec((1, tb, wblk), lambda b, j, t: (b, t, j))
    vec = pl.BlockSpec((1, wblk), lambda b, j, t: (0, j))
    tri = jnp.tril(jnp.ones((chunk, chunk), F32)).astype(BF16)
    bd = _group_mean_matrix(LANES, RWKV_HEAD)
    return pl.pallas_call(
        functools.partial(_rwkv_rec_kernel, chunk=chunk),
        grid=(bsz, d // wblk, seq // tb),
        in_specs=[tok] * 7 + [vec] * 3 + [_const_spec(tri.shape), _const_spec(bd.shape)],
        out_specs=tok,
        out_shape=jax.ShapeDtypeStruct((bsz, seq, d), BF16),
        scratch_shapes=[pltpu.VMEM((wblk // gw, gw, gw), F32),
                        pltpu.VMEM((tb, wblk), F32)],
        compiler_params=_cparams("arbitrary", "arbitrary", "arbitrary"),
        name="rwkv_recurrence",
    )(r, lw, k, v, kk, a, g, r_k.reshape(1, d), ln_g.reshape(1, d), ln_b.reshape(1, d), tri, bd)


def _odd_layer(x, mod, norm_mix, norm_ffn, ffn, mu, w_r, w_k, w_v, w_o, w0, w1,
               w2, a0, a1, a2, g1, g2, k_k, k_a, r_k, ln_g, ln_b):
    bsz, seq, d = x.shape
    r, lw, k, v, kk, a, g = _rwkv_in(x, mod, norm_mix, mu, w_r, w_k, w_v, w0, w1, w2, a0, a1, a2,
                                     g1, g2, k_k, k_a, _pick_tile(seq, 512))
    yg = _rwkv_recurrence(r, lw, k, v, kk, a, g, r_k, ln_g, ln_b, _pick_tile(seq, 512), 64,
                          d)
    tm = _pick_tile(seq, FFN_TILE)
    acts = [(yg, pl.BlockSpec((1, tm, d), lambda i, b: (b, i, 0)))]
    return _mix_ffn(x, mod, norm_ffn, acts, [w_o.astype(BF16)], ffn, tm, 256)


def _pick_tile(n, pref):
    t = min(n, pref)
    assert n % t == 0, (n, t)
    return t


def _even_layer(x, mod, tables, lam_init, norm_mix, norm_ffn, ffn, w_in,
                lam_re, lam_im, log_dt, b_re, b_im, c_re, c_im, d_skip, w_glu, q_norm, k_norm,
                lq1, lk1, lq2, lk2, subln, w_out):
    bsz, seq, d = x.shape
    s5w = lam_re.shape[0] * b_re.shape[-1]
    dfw = (w_in.shape[1] - s5w) // 3
    u2, q, k, v = _even_in(x, mod, norm_mix, w_in, q_norm, k_norm, tables, s5w, dfw,
                           _pick_tile(seq, 1024))
    ys = _s5_mixer_chunked(u2, bsz, lam_re, lam_im, log_dt, b_re, b_im,
                           c_re, c_im, d_skip.reshape(-1), w_glu, _pick_tile(seq, 512))
    att = _diff_attention(q, k, v, lq1, lk1, lq2, lk2, subln, lam_init, _pick_tile(seq, 512))
    w_out_bf = w_out.astype(BF16)
    assert s5w == dfw
    tm = _pick_tile(seq, FFN_TILE)
    acts = [
        (ys, pl.BlockSpec((tm, s5w), lambda i, b: (i, b))),
        (att, pl.BlockSpec((1, tm, dfw), lambda i, b: (b, i, 0))),
    ]
    w_specs = [pl.BlockSpec((s5w, d), lambda i, b: (0, 0), pipeline_mode=pl.Buffered(1)),
               pl.BlockSpec((dfw, d), lambda i, b: (1, 0), pipeline_mode=pl.Buffered(1))]
    return _mix_ffn(x, mod, norm_ffn, acts, [w_out_bf, w_out_bf], ffn, tm, 256, w_specs)


def kernel(x, c, positions, w_ada, b_ada, norm_mix, norm_ffn, ffn_w_gate, ffn_w_up, ffn_w_down,
           ev_w_in, ev_s5_lam_re, ev_s5_lam_im, ev_s5_log_dt, ev_s5_b_re, ev_s5_b_im, ev_s5_c_re,
           ev_s5_c_im, ev_s5_d, ev_s5_w_glu, ev_q_norm, ev_k_norm, ev_lambda_q1, ev_lambda_k1,
           ev_lambda_q2, ev_lambda_k2, ev_subln, ev_w_out, od_mu, od_w_r, od_w_k, od_w_v, od_w_o,
           od_w0, od_w1, od_w2, od_a0, od_a1, od_a2, od_g1, od_g2, od_k_k, od_k_a, od_r_k,
           od_ln_g, od_ln_b):
    depth = w_ada.shape[0]
    mod = _ada_mod(c, w_ada, b_ada)
    tables = _rope_tables(positions)
    ffn_bf = (ffn_w_gate.astype(BF16), ffn_w_up.astype(BF16), ffn_w_down.astype(BF16))
    for l in range(depth):
        ffn = ffn_bf + (l,)
        if l % 2 == 0:
            e = l // 2
            lam_init = 0.8 - 0.6 * math.exp(-0.3 * l)
            x = _even_layer(x, mod[l], tables, lam_init, norm_mix[l], norm_ffn[l], ffn,
                            ev_w_in[e], ev_s5_lam_re[e],
                            ev_s5_lam_im[e], ev_s5_log_dt[e], ev_s5_b_re[e], ev_s5_b_im[e],
                            ev_s5_c_re[e], ev_s5_c_im[e], ev_s5_d[e], ev_s5_w_glu[e], ev_q_norm[e],
                            ev_k_norm[e], ev_lambda_q1[e], ev_lambda_k1[e], ev_lambda_q2[e],
                            ev_lambda_k2[e], ev_subln[e], ev_w_out[e])
        else:
            o = l // 2
            x = _odd_layer(x, mod[l], norm_mix[l], norm_ffn[l], ffn,
                           od_mu[o], od_w_r[o], od_w_k[o], od_w_v[o], od_w_o[o],
                           od_w0[o], od_w1[o], od_w2[o], od_a0[o], od_a1[o], od_a2[o], od_g1[o],
                           od_g2[o], od_k_k[o], od_k_a[o], od_r_k[o], od_ln_g[o], od_ln_b[o])
    return x
```

```python
import functools
import math

import jax
import jax.numpy as jnp
from jax import lax
from jax.experimental import pallas as pl
from jax.experimental.pallas import tpu as pltpu

F32 = jnp.float32
BF16 = jnp.bfloat16

RMS_EPS = 1e-6
GN_EPS = 64e-5
ROPE_THETA = 500000.0
DT_HEAD = 64
ROT_DIM = DT_HEAD // 4
RWKV_HEAD = 64
LANES = 128
SUBLANES = 8
VMEM_LIMIT = 56 * 1024 * 1024


def _cparams(*sem):
    return pltpu.CompilerParams(dimension_semantics=sem, vmem_limit_bytes=VMEM_LIMIT)


def _const_spec(shape):
    nd = len(shape)
    return pl.BlockSpec(shape, lambda *_: (0,) * nd, pipeline_mode=pl.Buffered(1))


def _dot(a, b):
    return jnp.dot(a, b, preferred_element_type=F32)


def _dot_nt(a, b):
    return lax.dot_general(a, b, (((1,), (1,)), ((), ())), preferred_element_type=F32)


def _split_bf16(x):
    hi = x.astype(BF16)
    lo = (x - hi.astype(F32)).astype(BF16)
    return hi, lo


def _dot_x2(x, w_bf16):
    hi, lo = _split_bf16(x)
    return _dot(hi, w_bf16) + _dot(lo, w_bf16)


def _dot_x2_rhs(w_bf16, x):
    hi, lo = _split_bf16(x)
    return _dot(w_bf16, hi) + _dot(w_bf16, lo)


def _modnorm(x, g, scale, shift):
    ms = jnp.mean(x * x, axis=-1, keepdims=True)
    return (x * lax.rsqrt(ms + RMS_EPS)) * (g * (1.0 + scale)) + shift


def _ada_kernel(c_ref, w_ref, b_ref, o_ref):
    c = c_ref[...]
    ca = c * jax.nn.sigmoid(c)
    hi, lo = _split_bf16(ca)
    w = w_ref[0]
    w_hi, w_lo = _split_bf16(w)
    o_ref[0] = _dot(hi, w_hi) + _dot(lo, w_hi) + _dot(hi, w_lo) + b_ref[0]


def _ada_mod(c, w_ada, b_ada):
    depth, d, n = w_ada.shape
    bsz = c.shape[0]
    rows = -(-bsz // SUBLANES) * SUBLANES
    c_pad = jnp.zeros((rows, d), F32).at[:bsz].set(c)
    tn = n // 4
    out = pl.pallas_call(
        _ada_kernel,
        grid=(depth, n // tn),
        in_specs=[
            pl.BlockSpec((rows, d), lambda l, j: (0, 0)),
            pl.BlockSpec((1, d, tn), lambda l, j: (l, 0, j)),
            pl.BlockSpec((1, 1, tn), lambda l, j: (l, 0, j)),
        ],
        out_specs=pl.BlockSpec((1, rows, tn), lambda l, j: (l, 0, j)),
        out_shape=jax.ShapeDtypeStruct((depth, rows, n), F32),
        compiler_params=_cparams("arbitrary", "arbitrary"),
        name="ada_mod",
    )(c_pad, w_ada, b_ada.reshape(depth, 1, n))
    mod = out[:, :bsz].reshape(depth, bsz, 6, d)
    return jnp.pad(mod, ((0, 0), (0, 0), (0, SUBLANES - 6), (0, 0)))


def _rope_trig_kernel(p_ref, f_ref, o_ref):
    half = f_ref.shape[0]
    pad = jnp.zeros((LANES - 2 * half, LANES), F32)
    for r in range(p_ref.shape[0]):
        a = p_ref[r:r + 1, :] * f_ref[...]
        o_ref[r * LANES:(r + 1) * LANES, :] = jnp.concatenate([jnp.cos(a), jnp.sin(a), pad], axis=0).T


def _rope_tables(positions):
    bsz, seq = positions.shape
    half = ROT_DIM // 2
    inv_freq = ROPE_THETA ** (-jnp.arange(0, ROT_DIM, 2, dtype=F32) / ROT_DIM)
    rows = bsz * seq // LANES
    tr = min(rows, SUBLANES)
    cs = pl.pallas_call(
        _rope_trig_kernel,
        grid=(rows // tr,),
        in_specs=[pl.BlockSpec((tr, LANES), lambda i: (i, 0)), _const_spec((half, LANES))],
        out_specs=pl.BlockSpec((tr * LANES, LANES), lambda i: (i, 0)),
        out_shape=jax.ShapeDtypeStruct((bsz * seq, LANES), F32),
        compiler_params=_cparams("arbitrary"),
        name="rope_trig",
    )(positions.astype(F32).reshape(rows, LANES), jnp.broadcast_to(inv_freq[:, None], (half, LANES)))
    return cs.reshape(bsz, seq, LANES)


def _expand_rope(cs):
    half = ROT_DIM // 2
    lane = lax.broadcasted_iota(jnp.int32, cs.shape, 1)
    c0 = jnp.where(lane < half, cs, jnp.where(lane < ROT_DIM, pltpu.roll(cs, half, axis=1), 1.0))
    sa0 = jnp.where(lane < half, -pltpu.roll(cs, LANES - half, axis=1), 0.0)
    sb0 = jnp.where((lane >= half) & (lane < ROT_DIM), cs, 0.0)
    second = lane >= DT_HEAD
    return (jnp.where(second, pltpu.roll(c0, DT_HEAD, axis=1), c0),
            jnp.where(second, pltpu.roll(sa0, DT_HEAD, axis=1), sa0),
            jnp.where(second, pltpu.roll(sb0, DT_HEAD, axis=1), sb0))


def _group_mean_matrix(width, group):
    idx = jnp.arange(width) // group
    return jnp.where(idx[:, None] == idx[None, :], 1.0 / group, 0.0).astype(BF16)


def _even_in_kernel(x_ref, mod_ref, g_ref, w_ref, qn_ref, kn_ref, cs_ref, bd_ref,
                    u_ref, q_ref, k_ref, v_ref, *, s5w, dfw):
    x = x_ref[0]
    h = _modnorm(x, g_ref[...], mod_ref[0, 1:2, :], mod_ref[0, 0:1, :])
    proj = _dot(h.astype(BF16), w_ref[...])
    u_ref[...] = proj[:, :s5w]
    cosv, sav, sbv = _expand_rope(cs_ref[0])
    half = ROT_DIM // 2

    def norm_rope(t, gn, out_ref, post_scale):
        ms = _dot((t * t).astype(BF16), bd_ref[...])
        t = t * lax.rsqrt(ms + RMS_EPS) * gn
        for j in range(dfw // LANES):
            tj = t[:, j * LANES:(j + 1) * LANES]
            up = pltpu.roll(tj, LANES - half, axis=1)
            dn = pltpu.roll(tj, half, axis=1)
            rj = tj * cosv + up * sav + dn * sbv
            out_ref[0, :, j * LANES:(j + 1) * LANES] = (rj * post_scale).astype(out_ref.dtype)

    norm_rope(proj[:, s5w:s5w + dfw], qn_ref[...], q_ref, DT_HEAD ** -0.5 * math.log2(math.e))
    norm_rope(proj[:, s5w + dfw:s5w + 2 * dfw], kn_ref[...], k_ref, 1.0)
    v_ref[0] = proj[:, s5w + 2 * dfw:].astype(v_ref.dtype)


def _even_in(x, mod, g, w_in, q_norm, k_norm, tables, s5w, dfw, tm):
    bsz, seq, d = x.shape
    ncol = w_in.shape[1]
    rep = dfw // DT_HEAD
    kern = functools.partial(_even_in_kernel, s5w=s5w, dfw=dfw)
    tok = lambda w: pl.BlockSpec((1, tm, w), lambda i, b: (b, i, 0))
    return pl.pallas_call(
        kern,
        grid=(seq // tm, bsz),
        in_specs=[
            tok(d),
            pl.BlockSpec((1, SUBLANES, d), lambda i, b: (b, 0, 0)),
            _const_spec((1, d)),
            _const_spec((d, ncol)),
            _const_spec((1, dfw)),
            _const_spec((1, dfw)),
            tok(LANES),
            _const_spec((dfw, dfw)),
        ],
        out_specs=[
            pl.BlockSpec((tm, s5w), lambda i, b: (i, b)),
            tok(dfw), tok(dfw), tok(dfw),
        ],
        out_shape=[
            jax.ShapeDtypeStruct((seq, bsz * s5w), F32),
            jax.ShapeDtypeStruct((bsz, seq, dfw), BF16),
            jax.ShapeDtypeStruct((bsz, seq, dfw), BF16),
            jax.ShapeDtypeStruct((bsz, seq, dfw), BF16),
        ],
        compiler_params=_cparams("arbitrary", "arbitrary"),
        name="even_in_proj",
    )(x, mod, g.reshape(1, d), w_in.astype(BF16),
      jnp.tile(q_norm, rep).reshape(1, dfw), jnp.tile(k_norm, rep).reshape(1, dfw),
      tables, _group_mean_matrix(dfw, DT_HEAD))


S5_CHUNK = 8
FFN_TILE = 1024


def _s5_maps_kernel(lr_ref, li_ref, ldt_ref, lrc_ref, lic_ref, ldtc_ref, br_ref, bi_ref,
                    lrt_ref, lit_ref, ldtt_ref, cr_ref, ci_ref, dsk_ref,
                    kmat_out, wre_out, wim_out, mre_out, mim_out, a8r_out, a8i_out, *, hh):
    rr = S5_CHUNK
    nblk = kmat_out.shape[0]
    pp = lrc_ref.shape[1]
    gb = LANES // hh

    def abar(lr, li, ldt):
        dt = jnp.exp(ldt)
        mag = jnp.exp(lr * dt)
        return mag * jnp.cos(li * dt), mag * jnp.sin(li * dt)

    def cmul(ar, ai, br, bi):
        return ar * br - ai * bi, ar * bi + ai * br

    def mm3(a, b):
        a_hi, a_lo = _split_bf16(a)
        b_hi, b_lo = _split_bf16(b)
        return _dot(a_hi, b_hi) + _dot(a_hi, b_lo) + _dot(a_lo, b_hi)

    def iota2(shape, axis):
        return lax.broadcasted_iota(jnp.int32, shape, axis)

    same_kk = (iota2((LANES, LANES), 0) // hh) == (iota2((LANES, LANES), 1) // hh)
    diag_kk = iota2((LANES, LANES), 0) == iota2((LANES, LANES), 1)
    same_w = (iota2((LANES, gb * pp), 0) // hh) == (iota2((LANES, gb * pp), 1) // pp)
    same_m = (iota2((gb * pp, LANES), 0) // pp) == (iota2((gb * pp, LANES), 1) // hh)
    lane_tile = jnp.where(iota2((pp, gb * pp), 1) % pp == iota2((pp, gb * pp), 0), 1.0, 0.0
                          ).astype(BF16)

    lr, li = lrc_ref[...], lic_ref[...]
    a_r, a_i = abar(lr, li, ldtc_ref[...])
    den = lr * lr + li * li
    num_r = a_r - 1.0
    q_r = (num_r * lr + a_i * li) / den
    q_i = (a_i * lr - num_r * li) / den
    bb_r, bb_i = cmul(q_r, q_i, br_ref[...], bi_ref[...])
    pr, pi = bb_r, bb_i
    for i in range(rr - 1, -1, -1):
        for out, val in ((wre_out, pr), (wim_out, pi)):
            tiled = _dot(val.astype(BF16), lane_tile)
            for b in range(nblk):
                out[b, i * LANES:(i + 1) * LANES, :] = jnp.where(
                    same_w, tiled[b * LANES:(b + 1) * LANES], 0.0).astype(BF16)
        if i:
            pr, pi = cmul(a_r, a_i, pr, pi)

    kmat_out[...] = jnp.zeros_like(kmat_out)
    t_ar, t_ai = abar(lrt_ref[...], lit_ref[...], ldtt_ref[...])
    car, cai = cr_ref[...], ci_ref[...]
    for k in range(rr + 1):
        if k:
            car, cai = cmul(car, cai, t_ar, t_ai)
            for out, val in ((mre_out, car), (mim_out, -cai)):
                for b in range(nblk):
                    rows = jnp.concatenate([val[:, b * LANES:(b + 1) * LANES]] * gb, axis=0)
                    out[b, :, (k - 1) * LANES:k * LANES] = jnp.where(same_m, rows, 0.0
                                                                     ).astype(BF16)
        if k < rr:
            for b in range(nblk):
                sl = slice(b * LANES, (b + 1) * LANES)
                kk = jnp.where(same_kk, mm3(bb_r[sl], car[:, sl]) - mm3(bb_i[sl], cai[:, sl]), 0.0)
                if k == 0:
                    kk = kk + jnp.where(diag_kk, jnp.broadcast_to(dsk_ref[:, sl], kk.shape), 0.0)
                kk = kk.astype(BF16)
                for j in range(rr - k):
                    kmat_out[b, j * LANES:(j + 1) * LANES, (j + k) * LANES:(j + k + 1) * LANES] = kk
    e_r, e_i = abar(lr_ref[...], li_ref[...], ldt_ref[...])
    for _ in range(S5_CHUNK.bit_length() - 1):
        e_r, e_i = cmul(e_r, e_i, e_r, e_i)
    pr, pi = e_r, e_i
    for s in range(a8r_out.shape[0]):
        if s:
            pr, pi = cmul(pr, pi, e_r, e_i)
        a8r_out[s] = pr
        a8i_out[s] = pi


def _s5_chunk_kernel(u_ref, kmat_ref, wre_ref, wim_ref, mre_ref, mim_ref, a1r_ref, a1i_ref,
                     pwr_ref, pwi_ref, wglu_ref, perm_ref, permt_ref, o_ref, xr_ref, xi_ref,
                     cr_ref, ci_ref, *, bsz, sub, lane_chunk):
    nrow, nstate = xr_ref.shape
    tchunk = u_ref.shape[0]
    width = wglu_ref.shape[0]
    nsub = tchunk // sub
    rps = sub // S5_CHUNK * bsz
    nblk, gl = kmat_ref.shape[0], kmat_ref.shape[1] // S5_CHUNK
    spart = wre_ref.shape[2]
    step = pl.program_id(0)

    @pl.when(step == 0)
    def _():
        cr_ref[...] = jnp.zeros_like(cr_ref)
        ci_ref[...] = jnp.zeros_like(ci_ref)

    blk = u_ref[...]
    parts = []
    for s in range(nsub):
        u_bt = jnp.concatenate([blk[s * sub:(s + 1) * sub, b * width:(b + 1) * width]
                                for b in range(bsz)], axis=0)
        parts.append(_dot(perm_ref[...], u_bt.astype(BF16)))
    u_jb = [jnp.concatenate([p[j * rps:(j + 1) * rps] for p in parts], axis=0).astype(BF16)
            for j in range(S5_CHUNK)]
    u_g = [jnp.concatenate([t[:, g * gl:(g + 1) * gl] for t in u_jb], axis=1)
           for g in range(nblk)]

    for g in range(nblk):
        xr_ref[:, g * spart:(g + 1) * spart] = _dot(u_g[g], wre_ref[g])
        xi_ref[:, g * spart:(g + 1) * spart] = _dot(u_g[g], wim_ref[g])

    old_r = cr_ref[...]
    old_i = ci_ref[...]
    row = lax.broadcasted_iota(jnp.int32, (SUBLANES, lane_chunk), 0)
    steps_per_tile = SUBLANES // bsz
    ntiles = nrow // SUBLANES
    for c in range(nstate // lane_chunk):
        cols = pl.ds(c * lane_chunk, lane_chunk)
        a1r = a1r_ref[:, cols]
        a1i = a1i_ref[:, cols]
        pwr = pwr_ref[:, cols]
        pwi = pwi_ref[:, cols]

        def tile_body(i, carry):
            pr, pi = carry
            base = pl.multiple_of(i * SUBLANES, SUBLANES)
            zr = xr_ref[pl.ds(base, SUBLANES), cols]
            zi = xi_ref[pl.ds(base, SUBLANES), cols]
            sh = bsz
            apr, api = a1r, a1i
            for _ in range(steps_per_tile.bit_length() - 1):
                sr = jnp.where(row >= sh, pltpu.roll(zr, sh, axis=0), 0.0)
                si = jnp.where(row >= sh, pltpu.roll(zi, sh, axis=0), 0.0)
                zr, zi = zr + apr * sr - api * si, zi + apr * si + api * sr
                apr, api = apr * apr - api * api, 2.0 * apr * api
                sh *= 2
            last = SUBLANES - bsz
            br_, bi_ = pr, pi
            sh = bsz
            while sh < SUBLANES:
                br_ = jnp.where(row >= last, br_, pltpu.roll(br_, SUBLANES - sh, axis=0))
                bi_ = jnp.where(row >= last, bi_, pltpu.roll(bi_, SUBLANES - sh, axis=0))
                last -= sh
                sh *= 2
            xr = zr + pwr * br_ - pwi * bi_
            xi = zi + pwr * bi_ + pwi * br_
            xr_ref[pl.ds(base, SUBLANES), cols] = xr
            xi_ref[pl.ds(base, SUBLANES), cols] = xi
            return xr, xi

        fr, fi = lax.fori_loop(0, ntiles, tile_body, (cr_ref[:, cols], ci_ref[:, cols]))
        cr_ref[:, cols] = fr
        ci_ref[:, cols] = fi

    row8 = lax.broadcasted_iota(jnp.int32, (SUBLANES, nstate), 0)

    def state_in(x, old):
        xs = pltpu.roll(x, bsz, axis=0)
        first = jnp.where(row8 < bsz, pltpu.roll(old, bsz, axis=0), xs[:SUBLANES])
        return jnp.concatenate([first, xs[SUBLANES:]], axis=0).astype(BF16)

    xin_r = state_in(xr_ref[...], old_r)
    xin_i = state_in(xi_ref[...], old_i)

    tw = 2 * gl

    def k_part(g):
        strips = []
        for s in range(S5_CHUNK // 2):
            kdim = (2 * s + 2) * gl
            strips.append(_dot(u_g[g][:, :kdim], kmat_ref[g, :kdim, s * tw:(s + 1) * tw]))
        return jnp.concatenate(strips, axis=1)

    y_g = [k_part(g)
           + _dot(xin_r[:, g * spart:(g + 1) * spart], mre_ref[g])
           + _dot(xin_i[:, g * spart:(g + 1) * spart], mim_ref[g]) for g in range(nblk)]
    y = jnp.concatenate(
        [jnp.concatenate([y_g[g][:, t * gl:(t + 1) * gl] for g in range(nblk)], axis=1)
         for t in range(S5_CHUNK)], axis=0)
    y = jax.nn.gelu(y)
    gate = jax.nn.sigmoid(_dot(y.astype(BF16), wglu_ref[...]))
    out = (y * gate).astype(BF16)
    for s in range(nsub):
        o_s = jnp.concatenate([out[t * nrow + s * rps:t * nrow + (s + 1) * rps]
                               for t in range(S5_CHUNK)], axis=0)
        o_bt = _dot(permt_ref[...], o_s)
        for b in range(bsz):
            o_ref[s * sub:(s + 1) * sub, b * width:(b + 1) * width] = (
                o_bt[b * sub:(b + 1) * sub].astype(o_ref.dtype))


def _s5_mixer_chunked(u2, bsz, lam_re, lam_im, log_dt, b_re, b_im, c_re, c_im, d_skip, w_glu,
                      tchunk):
    seq = u2.shape[0]
    width = u2.shape[1] // bsz
    g, p = lam_re.shape
    hh = b_re.shape[-1]
    gh = g * hh
    nstate = g * p
    rr = S5_CHUNK
    ldt = jnp.broadcast_to(log_dt[:, None], (g, p))
    rep_c = lambda t: jnp.repeat(t, hh, axis=0)
    rep_t = lambda t: jnp.repeat(t.T, hh, axis=1)
    b_c = lambda b: b.transpose(0, 2, 1).reshape(gh, p)
    c_t = lambda c: c.transpose(2, 0, 1).reshape(p, gh)
    spt = SUBLANES // bsz
    gb = LANES // hh
    nblk = g // gb
    blk_w = rr * LANES
    bf = lambda *s: jax.ShapeDtypeStruct(s, BF16)
    f32 = lambda *s: jax.ShapeDtypeStruct(s, F32)
    kmat, w_re, w_im, m_re, m_im, a8_r, a8_i = pl.pallas_call(
        functools.partial(_s5_maps_kernel, hh=hh),
        out_shape=[bf(nblk, blk_w, blk_w), bf(nblk, blk_w, gb * p), bf(nblk, blk_w, gb * p),
                   bf(nblk, gb * p, blk_w), bf(nblk, gb * p, blk_w), f32(spt, g, p), f32(spt, g, p)],
        compiler_params=pltpu.CompilerParams(vmem_limit_bytes=VMEM_LIMIT),
        name="s5_chunk_maps",
    )(lam_re, lam_im, ldt, rep_c(lam_re), rep_c(lam_im), rep_c(ldt), b_c(b_re), b_c(b_im),
      rep_t(lam_re), rep_t(lam_im), rep_t(ldt), c_t(c_re), c_t(c_im), d_skip.reshape(1, width))
    pw_r = jnp.repeat(a8_r.reshape(spt, nstate), bsz, axis=0)
    pw_i = jnp.repeat(a8_i.reshape(spt, nstate), bsz, axis=0)
    a1r = jnp.broadcast_to(pw_r[0:1], (SUBLANES, nstate))
    a1i = jnp.broadcast_to(pw_i[0:1], (SUBLANES, nstate))
    consts = [kmat, w_re, w_im, m_re, m_im,
              a1r, a1i, pw_r, pw_i, w_glu.astype(BF16)]

    sub = min(tchunk, 128)
    rows = sub * bsz
    dst = jnp.arange(rows)
    rps = sub // rr * bsz
    j_, rem = dst // rps, dst % rps
    src_of = (rem % bsz) * sub + (rem // bsz) * rr + j_
    perm = (jnp.arange(rows)[None, :] == src_of[:, None]).astype(BF16)
    nrow = tchunk // rr * bsz
    kern = functools.partial(_s5_chunk_kernel, bsz=bsz, sub=sub, lane_chunk=512)
    return pl.pallas_call(
        kern,
        grid=(seq // tchunk,),
        in_specs=[pl.BlockSpec((tchunk, bsz * width), lambda i: (i, 0))]
                 + [_const_spec(t.shape) for t in consts]
                 + [_const_spec((rows, rows)), _const_spec((rows, rows))],
        out_specs=pl.BlockSpec((tchunk, bsz * width), lambda i: (i, 0)),
        out_shape=jax.ShapeDtypeStruct((seq, bsz * width), BF16),
        scratch_shapes=[
            pltpu.VMEM((nrow, nstate), F32), pltpu.VMEM((nrow, nstate), F32),
            pltpu.VMEM((SUBLANES, nstate), F32), pltpu.VMEM((SUBLANES, nstate), F32),
        ],
        compiler_params=_cparams("arbitrary"),
        name="s5_chunked",
    )(u2, *consts, perm, perm.T)


def _diff_attn_kernel(lam_ref, sub_ref, q_ref, k_ref, v_ref, o_ref, q2_ref, m_ref, acc_ref,
                      *, tq, lam_init):
    qi = pl.program_id(2)
    vdim = v_ref.shape[-1]
    q = q_ref[0]
    lane = lax.broadcasted_iota(jnp.int32, q.shape, 1)
    zero = jnp.zeros_like(q)
    q2_ref[:tq, :] = jnp.where(lane < DT_HEAD, q, zero)
    q2_ref[tq:, :] = jnp.where(lane >= DT_HEAD, q, zero)
    all_rows = ((0, 2 * tq),)

    def gather(ref, row_slices):
        parts = [ref[a:b, :] for a, b in row_slices]
        return parts[0] if len(parts) == 1 else jnp.concatenate(parts, axis=0)

    def scores(kstart, ksize, row_slices=all_rows):
        start = pl.multiple_of(kstart, ksize)
        return _dot_nt(gather(q2_ref, row_slices), k_ref[0, pl.ds(start, ksize), :])

    def update(kstart, ksize, s, row_slices=all_rows, mask=None, first=False):
        start = pl.multiple_of(kstart, ksize)
        v_ext = jnp.concatenate([v_ref[0, pl.ds(start, ksize), :], jnp.ones((ksize, vdim), BF16)],
                                axis=1)
        if mask is not None:
            s = jnp.where(mask, s, -jnp.inf)
        part = s[:, :LANES]
        for t in range(1, ksize // LANES):
            part = jnp.maximum(part, s[:, t * LANES:(t + 1) * LANES])
        row_max = jnp.max(part, axis=-1, keepdims=True)
        if first:
            assert row_slices == all_rows
            m_new = jnp.broadcast_to(row_max, part.shape)
        else:
            m_old = gather(m_ref, row_slices)
            m_new = jnp.maximum(m_old, row_max)
        p = jnp.concatenate([jnp.exp2((s[:, t * LANES:(t + 1) * LANES] - m_new).astype(BF16))
                             for t in range(ksize // LANES)], axis=1)
        acc = _dot(p, v_ext)
        if not first:
            alpha = jnp.exp2(m_old - m_new)
            acc = acc + jnp.concatenate([alpha] * (2 * vdim // LANES), axis=1) * gather(
                acc_ref, row_slices)
        off = 0
        for a, b in row_slices:
            m_ref[a:b, :] = m_new[off:off + b - a]
            acc_ref[a:b, :] = acc[off:off + b - a]
            off += b - a

    def diagonal(kstart, with_previous, first):
        hq = tq // 2
        if with_previous:
            s_prev = scores(kstart - tq, tq)
        row_l = lax.broadcasted_iota(jnp.int32, (2 * tq, hq), 0)
        col_l = lax.broadcasted_iota(jnp.int32, (2 * tq, hq), 1)
        row_r = lax.broadcasted_iota(jnp.int32, (tq, hq), 0)
        col_r = lax.broadcasted_iota(jnp.int32, (tq, hq), 1)
        late_rows = ((hq, tq), (tq + hq, 2 * tq))
        s_left = scores(kstart, hq)
        s_right = scores(kstart + hq, hq, late_rows)
        if with_previous:
            update(kstart - tq, tq, s_prev, first=first)
        update(kstart, hq, s_left, mask=col_l <= row_l % tq, first=first and not with_previous)
        update(kstart + hq, hq, s_right, late_rows, col_r <= row_r % hq)

    def pair(j0, first=False):
        s0 = scores(j0 * tq, tq)
        s1 = scores((j0 + 1) * tq, tq)
        update(j0 * tq, tq, s0, first=first)
        update((j0 + 1) * tq, tq, s1)

    npairs = qi // 2

    @pl.when(npairs > 0)
    def _():
        pair(0, first=True)

    lax.fori_loop(1, npairs, lambda jj, _: (pair(2 * jj), 0)[1], 0)
    for odd in (False, True):
        for is_first in (False, True):
            @pl.when((qi % 2 == int(odd)) & ((npairs == 0) if is_first else (npairs > 0)))
            def _():
                diagonal(qi * tq, odd, is_first)

    lv = lam_ref[...]
    lam = (jnp.exp(jnp.sum(lv[0:1] * lv[1:2], axis=-1, keepdims=True))
           - jnp.exp(jnp.sum(lv[2:3] * lv[3:4], axis=-1, keepdims=True)) + lam_init)
    acc = acc_ref[...]
    o = (acc[:tq, :vdim] / acc[:tq, vdim:]) - lam * (acc[tq:, :vdim] / acc[tq:, vdim:])
    ms = jnp.mean(o * o, axis=-1, keepdims=True)
    o = o * lax.rsqrt(ms + RMS_EPS) * sub_ref[...] * (1.0 - lam_init)
    o_ref[0] = o.astype(o_ref.dtype)


def _diff_attention(q, k, v, lq1, lk1, lq2, lk2, subln, lam_init, tq):
    bsz, seq, dfw = q.shape
    vdim = 2 * DT_HEAD
    heads = dfw // vdim
    lamv = jnp.zeros((SUBLANES, LANES), F32)
    for i, t in enumerate((lq1, lk1, lq2, lk2)):
        lamv = lamv.at[i, :t.shape[0]].set(t)
    kern = functools.partial(_diff_attn_kernel, tq=tq, lam_init=lam_init)
    kv_spec = pl.BlockSpec((1, seq, vdim), lambda b, h, i: (b, 0, h))
    return pl.pallas_call(
        kern,
        grid=(bsz, heads, seq // tq),
        in_specs=[
            _const_spec((SUBLANES, LANES)),
            _const_spec((1, vdim)),
            pl.BlockSpec((1, tq, vdim), lambda b, h, i: (b, i, h)),
            kv_spec, kv_spec,
        ],
        out_specs=pl.BlockSpec((1, tq, vdim), lambda b, h, i: (b, i, h)),
        out_shape=jax.ShapeDtypeStruct((bsz, seq, dfw), BF16),
        scratch_shapes=[pltpu.VMEM((2 * tq, vdim), BF16), pltpu.VMEM((2 * tq, LANES), F32),
                        pltpu.VMEM((2 * tq, 2 * vdim), F32)],
        compiler_params=_cparams("arbitrary", "arbitrary", "arbitrary"),
        name="diff_attention",
    )(lamv, subln.reshape(1, vdim), q, k, v)


def _mix_ffn_kernel(*refs, n_in, hid_chunk):
    x_ref, mod_ref, g_ref = refs[:3]
    act_refs = refs[3:3 + n_in]
    w_refs = refs[3 + n_in:3 + 2 * n_in]
    wg_ref, wu_ref, wd_ref, o_ref = refs[3 + 2 * n_in:]
    mix = None
    for a_ref, w_ref in zip(act_refs, w_refs):
        a = a_ref[0] if len(a_ref.shape) == 3 else a_ref[...]
        t = _dot(a, w_ref[...])
        mix = t if mix is None else mix + t
    x1 = x_ref[0] + mod_ref[0, 2:3, :] * mix
    h = _modnorm(x1, g_ref[...], mod_ref[0, 4:5, :], mod_ref[0, 3:4, :]).astype(BF16)
    acc = None
    for j in range(wg_ref.shape[2] // hid_chunk):
        sl = slice(j * hid_chunk, (j + 1) * hid_chunk)
        gate = _dot(h, wg_ref[0, :, sl])
        up = _dot(h, wu_ref[0, :, sl])
        act = (gate * jax.nn.sigmoid(gate) * up).astype(BF16)
        t = _dot(act, wd_ref[0, sl, :])
        acc = t if acc is None else acc + t
    o_ref[0] = x1 + mod_ref[0, 5:6, :] * acc


def _mix_ffn(x, mod, g, acts, weights, ffn, tm, hid_chunk, w_specs=None):
    bsz, seq, d = x.shape
    if w_specs is None:
        w_specs = [_const_spec(w.shape) for w in weights]
    wg, wu, wd, layer = ffn
    assert wg.shape[2] % hid_chunk == 0

    def layer_spec(w):
        return pl.BlockSpec((1,) + w.shape[1:], lambda i, b: (layer, 0, 0),
                            pipeline_mode=pl.Buffered(1))

    tok = pl.BlockSpec((1, tm, d), lambda i, b: (b, i, 0))
    return pl.pallas_call(
        functools.partial(_mix_ffn_kernel, n_in=len(acts), hid_chunk=hid_chunk),
        grid=(seq // tm, bsz),
        in_specs=[tok, pl.BlockSpec((1, SUBLANES, d), lambda i, b: (b, 0, 0)), _const_spec((1, d))]
                 + [spec for _, spec in acts]
                 + w_specs
                 + [layer_spec(wg), layer_spec(wu), layer_spec(wd)],
        out_specs=tok,
        out_shape=jax.ShapeDtypeStruct((bsz, seq, d), F32),
        compiler_params=_cparams("arbitrary", "arbitrary"),
        name="mix_ffn",
    )(x, mod, g.reshape(1, d), *[a for a, _ in acts], *weights, wg, wu, wd)


def _rwkv_in_kernel(x_ref, xp_ref, mod_ref, g_ref, mu_ref, wr_ref, wk_ref, wv_ref, w1_ref, w2_ref,
                    a1_ref, a2_ref, g1_ref, g2_ref, w0_ref, a0_ref, kkw_ref, kaw_ref, bd_ref,
                    r_out, lw_out, k_out, v_out, kk_out, a_out, g_out):
    i = pl.program_id(1)
    g = g_ref[...]
    scale = mod_ref[0, 1:2, :]
    shift = mod_ref[0, 0:1, :]
    h = _modnorm(x_ref[0], g, scale, shift)
    hp = _modnorm(xp_ref[0][SUBLANES - 1:SUBLANES, :], g, scale, shift)
    hp = jnp.where(i == 0, 0.0, hp)
    row = lax.broadcasted_iota(jnp.int32, h.shape, 0)
    h_prev = jnp.where(row == 0, hp, pltpu.roll(h, 1, axis=0))
    dx = h_prev - h

    def lerp(j):
        return (h + dx * mu_ref[j:j + 1, :]).astype(BF16)

    r_out[0] = _dot(lerp(0), wr_ref[...])
    wl = jnp.tanh(_dot(lerp(1), w1_ref[...]))
    wdec = w0_ref[...] + _dot(wl.astype(BF16), w2_ref[...])
    w = jnp.minimum(wdec, 0.0) - jnp.log1p(jnp.exp(-jnp.abs(wdec))) - 0.5
    lw_out[0] = -jnp.exp(w)
    k = _dot(lerp(2), wk_ref[...])
    v_out[0] = _dot(lerp(3), wv_ref[...])
    al = _dot(lerp(4), a1_ref[...])
    a = jax.nn.sigmoid(a0_ref[...] + _dot(al.astype(BF16), a2_ref[...]))
    a_out[0] = a
    gl = jax.nn.sigmoid(_dot(lerp(5), g1_ref[...]))
    g_out[0] = _dot(gl.astype(BF16), g2_ref[...])
    kk = k * kkw_ref[...]
    bd = bd_ref[...]
    for j in range(kk.shape[1] // LANES):
        kj = kk[:, j * LANES:(j + 1) * LANES]
        ss = _dot((kj * kj).astype(BF16), bd) * float(RWKV_HEAD)
        kk_out[0, :, j * LANES:(j + 1) * LANES] = kj * lax.rsqrt(jnp.maximum(ss, 1e-24))
    k_out[0] = k * (1.0 + (a - 1.0) * kaw_ref[...])


def _rwkv_in(x, mod, g, mu, w_r, w_k, w_v, w0, w1, w2, a0, a1, a2, g1, g2, k_k, k_a, tm):
    bsz, seq, d = x.shape
    tok = pl.BlockSpec((1, tm, d), lambda b, i: (b, i, 0))
    prev = pl.BlockSpec((1, SUBLANES, d),
                        lambda b, i: (b, jnp.maximum(i * (tm // SUBLANES) - 1, 0), 0))
    bf = lambda w: w.astype(BF16)
    vec = lambda t: t.reshape(1, d)
    consts = [vec(g), mu, bf(w_r), bf(w_k), bf(w_v), bf(w1), bf(w2), bf(a1), bf(a2), bf(g1), bf(g2),
              vec(w0), vec(a0), vec(k_k), vec(k_a), _group_mean_matrix(LANES, RWKV_HEAD)]
    out = jax.ShapeDtypeStruct((bsz, seq, d), F32)
    return pl.pallas_call(
        _rwkv_in_kernel,
        grid=(bsz, seq // tm),
        in_specs=[tok, prev, pl.BlockSpec((1, SUBLANES, d), lambda b, i: (b, 0, 0))]
                 + [_const_spec(t.shape) for t in consts],
        out_specs=[tok] * 7,
        out_shape=[out] * 7,
        compiler_params=_cparams("arbitrary", "arbitrary"),
        name="rwkv_in_proj",
    )(x, x, mod, *consts)


def _rwkv_rec_kernel(r_ref, lw_ref, k_ref, v_ref, kk_ref, a_ref, g_ref, rk_ref, lng_ref, lnb_ref,
                     tri_ref, bd_ref, o_ref, m_ref, y_ref, *, chunk):
    tstep = pl.program_id(2)

    @pl.when(tstep == 0)
    def _():
        m_ref[...] = jnp.zeros_like(m_ref)

    tb, width = r_ref.shape[1:]
    hd = RWKV_HEAD
    gw = m_ref.shape[-1]
    nh = gw // hd
    groups = range(width // gw)
    colsl = [slice(q * gw, (q + 1) * gw) for q in groups]
    lane = lax.broadcasted_iota(jnp.int32, (chunk, gw), 1)
    trow = lax.broadcasted_iota(jnp.int32, (chunk, gw), 0)
    jpos = lane % hd
    strict = jpos < trow
    incl = jpos <= trow
    eye = jnp.where(jpos == trow, 1.0, 0.0)
    head_of_lane = lane // hd
    in_head = [head_of_lane == h for h in range(nh)]
    sq_r = lax.broadcasted_iota(jnp.int32, (gw, gw), 0)
    sq_c = lax.broadcasted_iota(jnp.int32, (gw, gw), 1)
    same_head = (sq_r // hd) == (sq_c // hd)
    diag = sq_r == sq_c
    tri = tri_ref[...]
    bd = bd_ref[...]
    bf = lambda t: t.astype(BF16)

    def blockdiag(y):
        return bf(jnp.concatenate([jnp.where(in_head[h], y, 0.0) for h in range(nh)], axis=0))

    def blockdiag_t(x):
        xt = jnp.concatenate([x] * nh, axis=0).T
        return bf(jnp.where(same_head, xt, 0.0))

    inst = [(c, q) for c in range(tb // chunk) for q in groups]
    cls = []
    for c, q in inst:
        lw = lw_ref[0, c * chunk:(c + 1) * chunk, colsl[q]]
        hi = bf(lw)
        r1 = lw - hi.astype(F32)
        mid = bf(r1)
        lo = bf(r1 - mid.astype(F32))
        cls.append((lw, _dot(tri, hi) + _dot(tri, mid) + _dot(tri, lo)))
    opnd = []
    for (c, q), (lw, cl) in zip(inst, cls):
        rows = slice(c * chunk, (c + 1) * chunk)
        cols = colsl[q]
        k = k_ref[0, rows, cols]
        kk = kk_ref[0, rows, cols]
        b = kk * a_ref[0, rows, cols]
        v = v_ref[0, rows, cols]
        cl_end = cl[chunk - 1:chunk, :]
        p_inv = jnp.exp(-cl)
        p_tail = jnp.exp(cl_end - cl)
        ar = jnp.concatenate([bf(-kk * jnp.exp(cl - lw)), bf(r_ref[0, rows, cols] * jnp.exp(cl))],
                             axis=0)
        tail_t = bf(jnp.concatenate([b * p_tail, k * p_tail], axis=0).T)
        pl_full = jnp.broadcast_to(jnp.exp(cl_end), (gw, gw))
        pl_col = jnp.sum(jnp.where(diag, pl_full, 0.0), axis=1, keepdims=True)
        opnd.append((ar, blockdiag_t(b * p_inv), blockdiag_t(k * p_inv), bf(v), blockdiag(v),
                     tail_t, pl_col))
    g_b = [_dot(o[0], o[1]) for o in opnd]
    g_k = [_dot(o[0], o[2]) for o in opnd]
    nms = [jnp.where(strict, g[:chunk], 0.0) for g in g_b]
    tinv = [eye + nm for nm in nms]
    pw = [bf(_dot(bf(nm), blockdiag(nm))) for nm in nms]
    for _ in range(chunk.bit_length() - 3):
        both = [_dot(jnp.concatenate([bf(t), p2], axis=0), blockdiag(p2)) for t, p2 in zip(tinv, pw)]
        tinv = [t + bo[:chunk] for t, bo in zip(tinv, both)]
        pw = [bf(bo[chunk:]) for bo in both]
    tinv = [bf(t + _dot(bf(t), blockdiag(p2))) for t, p2 in zip(tinv, pw)]
    g_ak = [bf(jnp.where(strict, g[:chunk], 0.0)) for g in g_k]
    g_rb = [bf(jnp.where(incl, g[chunk:], 0.0)) for g in g_b]
    g_rk = [bf(jnp.where(incl, g[chunk:], 0.0)) for g in g_k]

    state = [m_ref[q] for q in groups]
    for c in range(tb // chunk):
        ids = [c * len(groups) + q for q in groups]
        m0_bf = [bf(state[q]) for q in groups]
        rhs = [_dot(jnp.concatenate([opnd[i][0][:chunk], g_ak[i]], axis=1),
                    jnp.concatenate([m0_bf[q], opnd[i][4]], axis=0)) for q, i in zip(groups, ids)]
        u = [_dot(tinv[i], blockdiag(rhs[q])) for q, i in zip(groups, ids)]
        for q, i in zip(groups, ids):
            y_ref[c * chunk:(c + 1) * chunk, colsl[q]] = _dot(
                jnp.concatenate([opnd[i][0][chunk:], g_rb[i], g_rk[i]], axis=1),
                jnp.concatenate([m0_bf[q], blockdiag(u[q]), opnd[i][4]], axis=0))
        for q, i in zip(groups, ids):
            upd = _dot(opnd[i][5], jnp.concatenate([bf(u[q]), opnd[i][3]], axis=0))
            state[q] = opnd[i][6] * state[q] + jnp.where(same_head, upd, 0.0)
    for q in groups:
        m_ref[q] = state[q]

    npair = width // LANES
    for p in range(npair):
        cols = slice(p * LANES, (p + 1) * LANES)
        y = y_ref[:, cols]
        mean = _dot_x2(y, bd)
        dlt = y - mean
        var = _dot(bf(dlt * dlt), bd)
        yn = dlt * lax.rsqrt(var + GN_EPS) * lng_ref[:, cols] + lnb_ref[:, cols]
        rk_sum = _dot(bf(r_ref[0, :, cols] * k_ref[0, :, cols] * rk_ref[:, cols]), bd) * float(hd)
        out = (yn + rk_sum * v_ref[0, :, cols]) * g_ref[0, :, cols]
        o_ref[0, :, cols] = out.astype(o_ref.dtype)


def _rwkv_recurrence(r, lw, k, v, kk, a, g, r_k, ln_g, ln_b, tb, chunk, wblk):
    bsz, seq, d = r.shape
    gw = 2 * RWKV_HEAD
    assert chunk == RWKV_HEAD and wblk % gw == 0
    tok = pl.BlockSpec((1, tb, wblk), lambda b, j, t: (b, t, j))
    vec = pl.BlockSpec((1, wblk), lambda b, j, t: (0, j))
    tri = jnp.tril(jnp.ones((chunk, chunk), F32)).astype(BF16)
    bd = _group_mean_matrix(LANES, RWKV_HEAD)
    return pl.pallas_call(
        functools.partial(_rwkv_rec_kernel, chunk=chunk),
        grid=(bsz, d // wblk, seq // tb),
        in_specs=[tok] * 7 + [vec] * 3 + [_const_spec(tri.shape), _const_spec(bd.shape)],
        out_specs=tok,
        out_shape=jax.ShapeDtypeStruct((bsz, seq, d), BF16),
        scratch_shapes=[pltpu.VMEM((wblk // gw, gw, gw), F32),
                        pltpu.VMEM((tb, wblk), F32)],
        compiler_params=_cparams("arbitrary", "arbitrary", "arbitrary"),
        name="rwkv_recurrence",
    )(r, lw, k, v, kk, a, g, r_k.reshape(1, d), ln_g.reshape(1, d), ln_b.reshape(1, d), tri, bd)


def _odd_layer(x, mod, norm_mix, norm_ffn, ffn, mu, w_r, w_k, w_v, w_o, w0, w1,
               w2, a0, a1, a2, g1, g2, k_k, k_a, r_k, ln_g, ln_b):
    bsz, seq, d = x.shape
    r, lw, k, v, kk, a, g = _rwkv_in(x, mod, norm_mix, mu, w_r, w_k, w_v, w0, w1, w2, a0, a1, a2,
                                     g1, g2, k_k, k_a, _pick_tile(seq, 512))
    yg = _rwkv_recurrence(r, lw, k, v, kk, a, g, r_k, ln_g, ln_b, _pick_tile(seq, 512), 64,
                          d)
    tm = _pick_tile(seq, FFN_TILE)
    acts = [(yg, pl.BlockSpec((1, tm, d), lambda i, b: (b, i, 0)))]
    return _mix_ffn(x, mod, norm_ffn, acts, [w_o.astype(BF16)], ffn, tm, 256)


def _pick_tile(n, pref):
    t = min(n, pref)
    assert n % t == 0, (n, t)
    return t


def _even_layer(x, mod, tables, lam_init, norm_mix, norm_ffn, ffn, w_in,
                lam_re, lam_im, log_dt, b_re, b_im, c_re, c_im, d_skip, w_glu, q_norm, k_norm,
                lq1, lk1, lq2, lk2, subln, w_out):
    bsz, seq, d = x.shape
    s5w = lam_re.shape[0] * b_re.shape[-1]
    dfw = (w_in.shape[1] - s5w) // 3
    u2, q, k, v = _even_in(x, mod, norm_mix, w_in, q_norm, k_norm, tables, s5w, dfw,
                           _pick_tile(seq, 512))
    ys = _s5_mixer_chunked(u2, bsz, lam_re, lam_im, log_dt, b_re, b_im,
                           c_re, c_im, d_skip.reshape(-1), w_glu, _pick_tile(seq, 512))
    att = _diff_attention(q, k, v, lq1, lk1, lq2, lk2, subln, lam_init, _pick_tile(seq, 512))
    w_out_bf = w_out.astype(BF16)
    assert s5w == dfw
    tm = _pick_tile(seq, FFN_TILE)
    acts = [
        (ys, pl.BlockSpec((tm, s5w), lambda i, b: (i, b))),
        (att, pl.BlockSpec((1, tm, dfw), lambda i, b: (b, i, 0))),
    ]
    w_specs = [pl.BlockSpec((s5w, d), lambda i, b: (0, 0), pipeline_mode=pl.Buffered(1)),
               pl.BlockSpec((dfw, d), lambda i, b: (1, 0), pipeline_mode=pl.Buffered(1))]
    return _mix_ffn(x, mod, norm_ffn, acts, [w_out_bf, w_out_bf], ffn, tm, 256, w_specs)


def kernel(x, c, positions, w_ada, b_ada, norm_mix, norm_ffn, ffn_w_gate, ffn_w_up, ffn_w_down,
           ev_w_in, ev_s5_lam_re, ev_s5_lam_im, ev_s5_log_dt, ev_s5_b_re, ev_s5_b_im, ev_s5_c_re,
           ev_s5_c_im, ev_s5_d, ev_s5_w_glu, ev_q_norm, ev_k_norm, ev_lambda_q1, ev_lambda_k1,
           ev_lambda_q2, ev_lambda_k2, ev_subln, ev_w_out, od_mu, od_w_r, od_w_k, od_w_v, od_w_o,
           od_w0, od_w1, od_w2, od_a0, od_a1, od_a2, od_g1, od_g2, od_k_k, od_k_a, od_r_k,
           od_ln_g, od_ln_b):
    depth = w_ada.shape[0]
    mod = _ada_mod(c, w_ada, b_ada)
    tables = _rope_tables(positions)
    ffn_bf = (ffn_w_gate.astype(BF16), ffn_w_up.astype(BF16), ffn_w_down.astype(BF16))
    for l in range(depth):
        ffn = ffn_bf + (l,)
        if l % 2 == 0:
            e = l // 2
            lam_init = 0.8 - 0.6 * math.exp(-0.3 * l)
            x = _even_layer(x, mod[l], tables, lam_init, norm_mix[l], norm_ffn[l], ffn,
                            ev_w_in[e], ev_s5_lam_re[e],
                            ev_s5_lam_im[e], ev_s5_log_dt[e], ev_s5_b_re[e], ev_s5_b_im[e],
                            ev_s5_c_re[e], ev_s5_c_im[e], ev_s5_d[e], ev_s5_w_glu[e], ev_q_norm[e],
                            ev_k_norm[e], ev_lambda_q1[e], ev_lambda_k1[e], ev_lambda_q2[e],
                            ev_lambda_k2[e], ev_subln[e], ev_w_out[e])
        else:
            o = l // 2
            x = _odd_layer(x, mod[l], norm_mix[l], norm_ffn[l], ffn,
                           od_mu[o], od_w_r[o], od_w_k[o], od_w_v[o], od_w_o[o],
                           od_w0[o], od_w1[o], od_w2[o], od_a0[o], od_a1[o], od_a2[o], od_g1[o],
                           od_g2[o], od_k_k[o], od_k_a[o], od_r_k[o], od_ln_g[o], od_ln_b[o])
    return x
```
